```python
import math
import jax, jax.numpy as jnp
from jax import lax
import numpy as np

D_MODEL = 2048
BATCH = 4
SEQ = 4096
DEPTH = 2

MIX_WIDTH = D_MODEL
DN_WIDTH = MIX_WIDTH // 2
CV_WIDTH = MIX_WIDTH - DN_WIDTH
DN_HEAD_DIM = 128
DN_HEADS = DN_WIDTH // DN_HEAD_DIM
SHORT_CONV = 3
CHUNK = 64
CV_KERNEL = 31
CV_GROUPS = 8
N_GROUPS = 8
EXPERTS_PER_GROUP = 8
N_EXPERTS = N_GROUPS * EXPERTS_PER_GROUP
TOP_K = 2
EXPERT_FF = D_MODEL // 4
MOE_BLOCK = 128
DEEPNORM_ALPHA = (2 * DEPTH) ** 0.25
DEEPNORM_BETA = (8 * DEPTH) ** -0.25
LN_EPS = 1e-5
RMS_EPS = 1e-6
IN_SIZES = (DN_WIDTH, DN_WIDTH, DN_WIDTH, DN_WIDTH, 2 * DN_HEADS, 2 * DN_HEADS, 2 * CV_WIDTH)
IN_WIDTH = sum(IN_SIZES)
IN_OFFSETS = tuple(int(v) for v in np.cumsum(IN_SIZES)[:-1])

kernel_name = 'hybrid_deltanet_conformer_hmoe_encoder'


def _layer_norm(x, g, b):
    xf = x.astype(jnp.float32)
    mu = jnp.mean(xf, -1, keepdims=True)
    var = jnp.mean(jnp.square(xf - mu), -1, keepdims=True)
    y = (xf - mu) * lax.rsqrt(var + LN_EPS)
    return (y * g.astype(jnp.float32) + b.astype(jnp.float32)).astype(x.dtype)


def _centred_depthwise_conv(x, w):
    pad = w.shape[0] // 2
    return lax.conv_general_dilated(
        x, w[:, None, :].astype(x.dtype), window_strides=(1,), padding=[(pad, pad)],
        dimension_numbers=('NWC', 'WIO', 'NWC'), feature_group_count=x.shape[-1])


def _chunk_gated_delta(q, k, v, g, beta):
    B, H, L, Dk = q.shape
    Dv = v.shape[-1]
    N = L // CHUNK
    q = q * (Dk ** -0.5)
    rs = lambda t: t.reshape((B, H, N, CHUNK) + t.shape[3:])
    q, k, v, g, beta = rs(q), rs(k), rs(v), rs(g), rs(beta)
    gc = jnp.cumsum(g, axis=-1)
    idx = jnp.arange(CHUNK)
    incl = idx[:, None] >= idx[None, :]
    strict = idx[:, None] > idx[None, :]
    diff = gc[..., :, None] - gc[..., None, :]
    decay = jnp.where(incl, jnp.exp(jnp.where(incl, diff, 0.0)), 0.0)
    kb = k * beta[..., None]
    lmat = jnp.where(strict, jnp.einsum('bhncd,bhnsd->bhncs', kb, k) * decay, 0.0)
    eye = jnp.eye(CHUNK, dtype=q.dtype)
    rhs = jnp.concatenate([v * beta[..., None], kb * jnp.exp(gc)[..., None]], axis=-1)
    sol = lax.linalg.triangular_solve(eye + lmat, rhs, left_side=True, lower=True, unit_diagonal=True)
    u, w = sol[..., :Dv], sol[..., Dv:]
    attn = jnp.einsum('bhncd,bhnsd->bhncs', q, k) * decay
    qg = q * jnp.exp(gc)[..., None]
    kd = k * jnp.exp(gc[..., -1:] - gc)[..., None]
    glast = jnp.exp(gc[..., -1])
    xs = tuple(jnp.moveaxis(t, 2, 0) for t in (qg, kd, u, w, attn, glast))

    def step(S, inp):
        qg_n, kd_n, u_n, w_n, a_n, gl_n = inp
        v_new = u_n - jnp.einsum('bhck,bhkv->bhcv', w_n, S)
        o_n = jnp.einsum('bhck,bhkv->bhcv', qg_n, S) + jnp.einsum('bhcs,bhsv->bhcv', a_n, v_new)
        S = S * gl_n[..., None, None] + jnp.einsum('bhck,bhcv->bhkv', kd_n, v_new)
        return S, o_n

    S0 = jnp.zeros((B, H, Dk, Dv), q.dtype)
    _, o = lax.scan(step, S0, xs)
    return jnp.moveaxis(o, 0, 2).reshape(B, H, L, Dv)


def _decay_and_beta(a, b, a_log, dt_bias):
    g = -jnp.exp(a_log.astype(jnp.float32)) * jax.nn.softplus(a + dt_bias.astype(jnp.float32))
    beta = jax.nn.sigmoid(b)
    return jnp.transpose(g, (0, 2, 1)), jnp.transpose(beta, (0, 2, 1))


def _deltanet_group(q, k, v, z, a, b, conv_w, a_log, dt_bias, norm_w):
    B, L, _ = q.shape
    H, Dh = DN_HEADS, DN_HEAD_DIM
    out_dtype = q.dtype
    qkv = jax.nn.silu(_centred_depthwise_conv(jnp.concatenate([q, k, v], axis=-1), conv_w))
    heads = lambda t: jnp.transpose(t.reshape(B, L, H, Dh).astype(jnp.float32), (0, 2, 1, 3))
    qh, kh, vh = (heads(t) for t in jnp.split(qkv, 3, axis=-1))
    qh = qh * lax.rsqrt(jnp.sum(qh * qh, -1, keepdims=True) + RMS_EPS)
    kh = kh * lax.rsqrt(jnp.sum(kh * kh, -1, keepdims=True) + RMS_EPS)
    af = a.astype(jnp.float32)
    bf = b.astype(jnp.float32)
    g_f, beta_f = _decay_and_beta(af[..., :H], bf[..., :H], a_log[0], dt_bias[0])
    g_b, beta_b = _decay_and_beta(af[..., H:], bf[..., H:], a_log[1], dt_bias[1])
    rev = lambda t: jnp.flip(t, axis=2)
    o = _chunk_gated_delta(qh, kh, vh, g_f, beta_f) + rev(
        _chunk_gated_delta(rev(qh), rev(kh), rev(vh), rev(g_b), rev(beta_b)))
    o = jnp.transpose(o, (0, 2, 1, 3))
    o = o * lax.rsqrt(jnp.mean(o * o, -1, keepdims=True) + RMS_EPS) * norm_w.astype(jnp.float32)
    o = o * jax.nn.silu(z.astype(jnp.float32).reshape(B, L, H, Dh))
    return o.reshape(B, L, H * Dh).astype(out_dtype)


def _conformer_conv_group(u, dw_w, dw_b, ln_g, ln_b):
    val, gate = jnp.split(u, 2, axis=-1)
    y = val * jax.nn.sigmoid(gate)
    y = _centred_depthwise_conv(y, dw_w) + dw_b
    B, L, C = y.shape
    yf = y.astype(jnp.float32).reshape(B, L, CV_GROUPS, C // CV_GROUPS)
    mu = jnp.mean(yf, -1, keepdims=True)
    var = jnp.mean(jnp.square(yf - mu), -1, keepdims=True)
    yf = ((yf - mu) * lax.rsqrt(var + LN_EPS)).reshape(B, L, C)
    yf = yf * ln_g.astype(jnp.float32) + ln_b.astype(jnp.float32)
    return jax.nn.silu(yf).astype(u.dtype)


def _hier_moe(h, w_group, b_group, w_expert, b_expert, w_gu, w_dn):
    B, L, D = h.shape
    T = B * L
    xt = h.reshape(T, D)
    gl = (xt @ w_group + b_group).astype(jnp.float32)
    gp = jax.nn.softmax(gl, axis=-1)
    gsel = jnp.argmax(gl, axis=-1).astype(jnp.int32)
    pg = jnp.take_along_axis(gp, gsel[:, None], axis=1)[:, 0]
    el = (xt @ w_expert + b_expert).astype(jnp.float32).reshape(T, N_GROUPS, EXPERTS_PER_GROUP)
    el = jnp.take_along_axis(el, gsel[:, None, None], axis=1)[:, 0]
    ep = jax.nn.softmax(el, axis=-1)
    top_p, top_i = lax.top_k(ep, TOP_K)
    top_p = top_p / jnp.sum(top_p, -1, keepdims=True)
    eid = (gsel[:, None] * EXPERTS_PER_GROUP + top_i).reshape(-1).astype(jnp.int32)
    gate = (pg[:, None] * top_p).reshape(-1)
    A = T * TOP_K
    tok = jnp.arange(A, dtype=jnp.int32) // TOP_K
    order = jnp.argsort(eid)
    se = eid[order]
    counts = jnp.zeros((N_EXPERTS,), jnp.int32).at[eid].add(1)
    pc = ((counts + MOE_BLOCK - 1) // MOE_BLOCK) * MOE_BLOCK
    pend = jnp.cumsum(pc)
    pstart = pend - pc
    cstart = jnp.cumsum(counts) - counts
    dest = pstart[se] + jnp.arange(A, dtype=jnp.int32) - cstart[se]
    NB = (A + N_EXPERTS * (MOE_BLOCK - 1) + MOE_BLOCK - 1) // MOE_BLOCK
    P = NB * MOE_BLOCK
    tok_buf = jnp.full((P,), T, jnp.int32).at[dest].set(tok[order])
    gate_buf = jnp.zeros((P,), gate.dtype).at[dest].set(gate[order])
    blk_start = jnp.arange(NB, dtype=jnp.int32) * MOE_BLOCK
    blk_e = jnp.minimum(jnp.searchsorted(pend, blk_start, side='right'), N_EXPERTS - 1).astype(jnp.int32)
    xpad = jnp.concatenate([xt, jnp.zeros((1, D), xt.dtype)], axis=0)

    def run_block(args):
        tb, e = args
        xb = xpad[tb]
        gt, up = jnp.split(xb @ w_gu[e], 2, axis=-1)
        return (jax.nn.silu(gt) * up) @ w_dn[e]

    yb = lax.map(run_block, (tok_buf.reshape(NB, MOE_BLOCK), blk_e))
    y = yb.reshape(P, D) * gate_buf.astype(yb.dtype)[:, None]
    out = jax.ops.segment_sum(y, tok_buf, num_segments=T + 1)[:T]
    return out.reshape(B, L, D)


def setup_inputs(seed: int = 0) -> dict:
    key = jax.random.key(seed)
    ks = jax.random.split(key, 24)
    f32 = jnp.float32
    nrm = lambda k, shape, s: jax.random.normal(k, shape, f32) * s
    D = D_MODEL
    dt = jnp.exp(jax.random.uniform(ks[6], (DEPTH, 2, DN_HEADS), f32, math.log(1e-3), math.log(1e-1)))
    return {
        'x': nrm(ks[0], (BATCH, SEQ, D), 1.0),
        'emb_ln_g': 1.0 + nrm(ks[1], (D,), 0.02),
        'emb_ln_b': nrm(ks[2], (D,), 0.02),
        'w_in': nrm(ks[3], (DEPTH, D, IN_WIDTH), D ** -0.5),
        'short_conv_w': nrm(ks[4], (DEPTH, SHORT_CONV, 3 * DN_WIDTH), SHORT_CONV ** -0.5),
        'a_log': jnp.log(jax.random.uniform(ks[5], (DEPTH, 2, DN_HEADS), f32, 1.0, 16.0)),
        'dt_bias': dt + jnp.log(-jnp.expm1(-dt)),
        'dn_norm_w': 1.0 + nrm(ks[7], (DEPTH, DN_HEAD_DIM), 0.02),
        'dw_conv_w': nrm(ks[8], (DEPTH, CV_KERNEL, CV_WIDTH), CV_KERNEL ** -0.5),
        'dw_conv_b': nrm(ks[9], (DEPTH, CV_WIDTH), 0.02),
        'conv_ln_g': 1.0 + nrm(ks[10], (DEPTH, CV_WIDTH), 0.02),
        'conv_ln_b': nrm(ks[11], (DEPTH, CV_WIDTH), 0.02),
        'w_out': nrm(ks[12], (DEPTH, MIX_WIDTH, D), MIX_WIDTH ** -0.5 * DEEPNORM_BETA),
        'ln1_g': 1.0 + nrm(ks[13], (DEPTH, D), 0.02),
        'ln1_b': nrm(ks[14], (DEPTH, D), 0.02),
        'w_group': nrm(ks[15], (DEPTH, D, N_GROUPS), D ** -0.5),
        'b_group': nrm(ks[16], (DEPTH, N_GROUPS), 0.01),
        'w_expert': nrm(ks[17], (DEPTH, D, N_EXPERTS), D ** -0.5),
        'b_expert': nrm(ks[18], (DEPTH, N_EXPERTS), 0.01),
        'w_gate_up': nrm(ks[19], (DEPTH, N_EXPERTS, D, 2 * EXPERT_FF), D ** -0.5),
        'w_down': nrm(ks[20], (DEPTH, N_EXPERTS, EXPERT_FF, D), EXPERT_FF ** -0.5 * DEEPNORM_BETA),
        'ln2_g': 1.0 + nrm(ks[21], (DEPTH, D), 0.02),
        'ln2_b': nrm(ks[22], (DEPTH, D), 0.02),
    }


def reference(x, emb_ln_g, emb_ln_b, w_in, short_conv_w, a_log, dt_bias, dn_norm_w, dw_conv_w,
              dw_conv_b, conv_ln_g, conv_ln_b, w_out, ln1_g, ln1_b, w_group, b_group, w_expert,
              b_expert, w_gate_up, w_down, ln2_g, ln2_b):
    h = _layer_norm(x, emb_ln_g, emb_ln_b)
    for l in range(DEPTH):
        proj = h @ w_in[l]
        q, k, v, z, a, b, glu = jnp.split(proj, IN_OFFSETS, axis=-1)
        dn = _deltanet_group(q, k, v, z, a, b, short_conv_w[l], a_log[l], dt_bias[l], dn_norm_w[l])
        cv = _conformer_conv_group(glu, dw_conv_w[l], dw_conv_b[l], conv_ln_g[l], conv_ln_b[l])
        mix = jnp.concatenate([dn, cv], axis=-1) @ w_out[l]
        h = _layer_norm(DEEPNORM_ALPHA * h + mix, ln1_g[l], ln1_b[l])
        ffn = _hier_moe(h, w_group[l], b_group[l], w_expert[l], b_expert[l], w_gate_up[l], w_down[l])
        h = _layer_norm(DEEPNORM_ALPHA * h + ffn, ln2_g[l], ln2_b[l])
    return h
```

```python
import functools

import jax
import jax.numpy as jnp
from jax import lax
from jax.experimental import pallas as pl
from jax.experimental.pallas import tpu as pltpu

F32 = jnp.float32
BF16 = jnp.bfloat16
U32 = jnp.uint32
I32 = jnp.int32

LANES = 128
DN_HEAD_DIM = 128
DN_CHUNK = 128
CV_GROUP = 128
N_GROUPS = 8
EXPERTS_PER_GROUP = 8
N_EXPERTS = N_GROUPS * EXPERTS_PER_GROUP
TOP_K = 2
FFN_BLOCK = 128
LN_EPS = 1e-5
RMS_EPS = 1e-6
VMEM_LIMIT = 56 * 1024 * 1024


def _cparams(*sem):
    return pltpu.CompilerParams(dimension_semantics=sem, vmem_limit_bytes=VMEM_LIMIT)


def _layer_norm(x, g, b):
    mu = jnp.mean(x, -1, keepdims=True)
    xc = x - mu
    var = jnp.mean(xc * xc, -1, keepdims=True)
    return xc * lax.rsqrt(var + LN_EPS) * g + b


def _sigmoid(x):
    return 1.0 / (1.0 + jnp.exp(-x))


def _silu(x):
    return x * _sigmoid(x)


def _pack_halves(y):
    n = y.shape[1] // 2
    bits = lax.bitcast_convert_type(y.astype(BF16).astype(F32), U32)
    return (bits[:, :n] >> 16) | bits[:, n:]


def _unpack_halves(u):
    lo = lax.bitcast_convert_type(u << 16, F32)
    hi = lax.bitcast_convert_type(u & jnp.uint32(0xFFFF0000), F32)
    return lo, hi


def _emb_ln_kernel(x_ref, g_ref, b_ref, h_ref, hb_ref):
    h = _layer_norm(x_ref[...], g_ref[...], b_ref[...])
    h_ref[...] = h
    hb_ref[...] = h.astype(BF16)


def _emb_ln(x, g, b, tm=512):
    T, D = x.shape
    row = pl.BlockSpec((tm, D), lambda i: (i, 0))
    vec = pl.BlockSpec((1, D), lambda i: (0, 0))
    return pl.pallas_call(
        _emb_ln_kernel, grid=(T // tm,), in_specs=[row, vec, vec], out_specs=[row, row],
        out_shape=[jax.ShapeDtypeStruct((T, D), F32), jax.ShapeDtypeStruct((T, D), BF16)],
        compiler_params=_cparams("parallel"), name="emb_ln")(x, g.reshape(1, D), b.reshape(1, D))


def _matmul_kernel(x_ref, w_ref, o_ref):
    o_ref[...] = jnp.dot(x_ref[...], w_ref[...], preferred_element_type=F32)


def _in_proj(hb, w, tm=1024, tn=1024):
    T, D = hb.shape
    N = w.shape[1]
    return pl.pallas_call(
        _matmul_kernel, grid=(T // tm, N // tn),
        in_specs=[pl.BlockSpec((tm, D), lambda i, j: (i, 0)), pl.BlockSpec((D, tn), lambda i, j: (0, j))],
        out_specs=pl.BlockSpec((tm, tn), lambda i, j: (i, j)),
        out_shape=jax.ShapeDtypeStruct((T, N), F32),
        compiler_params=_cparams("parallel", "arbitrary"), name="in_proj")(hb, w)


def _decay_beta_kernel(x_ref, w_ref, alog_ref, dtb_ref, o_ref, *, n_decay):
    ab = jnp.dot(x_ref[...], w_ref[...], preferred_element_type=F32)
    lane = lax.broadcasted_iota(I32, ab.shape, 1)
    s = ab + dtb_ref[...]
    softplus = jnp.maximum(s, 0.0) + jnp.log(1.0 + jnp.exp(-jnp.abs(s)))
    g = -jnp.exp(alog_ref[...]) * softplus
    o_ref[...] = jnp.where(lane < n_decay, g, _sigmoid(ab))


def _decay_beta(hb, w_ab, a_log, dt_bias, tm=1024):
    T, D = hb.shape
    n = a_log.size
    pad = lambda v: jnp.pad(v.reshape(1, n).astype(F32), ((0, 0), (0, LANES - n)))
    w = jnp.pad(w_ab, ((0, 0), (0, LANES - w_ab.shape[1]))).astype(BF16)
    vec = pl.BlockSpec((1, LANES), lambda i: (0, 0))
    return pl.pallas_call(
        functools.partial(_decay_beta_kernel, n_decay=n), grid=(T // tm,),
        in_specs=[pl.BlockSpec((tm, D), lambda i: (i, 0)), pl.BlockSpec((D, LANES), lambda i: (0, 0)), vec, vec],
        out_specs=pl.BlockSpec((tm, LANES), lambda i: (i, 0)),
        out_shape=jax.ShapeDtypeStruct((T, LANES), F32),
        compiler_params=_cparams("parallel"), name="decay_beta")(hb, w, pad(a_log), pad(dt_bias))


def _dot(a, b):
    return jnp.dot(a.astype(BF16), b.astype(BF16), preferred_element_type=F32)


def _dot_nt(a, b):
    return lax.dot_general(a.astype(BF16), b.astype(BF16), (((1,), (1,)), ((), ())),
                           preferred_element_type=F32)


def _dot_tn(a, b):
    return lax.dot_general(a.astype(BF16), b.astype(BF16), (((0,), (0,)), ((), ())),
                           preferred_element_type=F32)


def _deltanet_kernel(q_ref, k_ref, v_ref, z_ref, cq_ref, ck_ref, cv_ref, gt_ref, nw_ref, o_ref,
                     qs, ks, vs, ob, wq_s, at_s, kd_s, u_s, gl_s, s_s, *, L, C, H, R):
    DH = DN_HEAD_DIM
    h = pl.program_id(1)
    NC = L // C

    def conv_rows(i, _):
        r0 = pl.multiple_of(i * R, R)
        rows = pl.ds(r0, R)
        rid = lax.broadcasted_iota(I32, (R, DH), 0)
        has_prev = (r0 > 0).astype(F32)
        has_next = (r0 + R < L).astype(F32)

        def conv_silu(ref, w_ref):
            x = ref[0, rows, :]
            xm = ref[0, pl.ds(jnp.maximum(r0 - 1, 0), 1), :] * has_prev
            xp = ref[0, pl.ds(jnp.minimum(r0 + R, L - 1), 1), :] * has_next
            x_prev = jnp.where(rid == 0, xm, pltpu.roll(x, 1, 0))
            x_next = jnp.where(rid == R - 1, xp, pltpu.roll(x, R - 1, 0))
            return _silu(w_ref[0:1, :] * x_prev + w_ref[1:2, :] * x + w_ref[2:3, :] * x_next)

        q = conv_silu(q_ref, cq_ref)
        k = conv_silu(k_ref, ck_ref)
        qs[rows, :] = q * (lax.rsqrt(jnp.sum(q * q, -1, keepdims=True) + RMS_EPS) * (DH ** -0.5))
        ks[rows, :] = k * lax.rsqrt(jnp.sum(k * k, -1, keepdims=True) + RMS_EPS)
        vs[rows, :] = conv_silu(v_ref, cv_ref)
        return 0

    lax.fori_loop(0, L // R, conv_rows, 0)

    ri = lax.broadcasted_iota(I32, (C, C), 0)
    ci = lax.broadcasted_iota(I32, (C, C), 1)
    eye = ri == ci
    eye_f = eye.astype(F32)
    incl = (ri >= ci, ri <= ci)
    strict = (ri > ci, ri < ci)
    n_sq = C.bit_length() - 2

    def prep(c, _):
        rows = pl.ds(pl.multiple_of(c * C, C), C)
        q = qs[rows, :]
        k = ks[rows, :]
        v = vs[rows, :]
        kq = _dot_nt(jnp.concatenate([k, q], axis=0), k)
        kk, qk = kq[:C], kq[C:]
        for d in range(2):
            g_row = gt_ref[0, d * H + h, pl.ds(c, 1), :]
            b_row = gt_ref[0, (2 + d) * H + h, pl.ds(c, 1), :]
            m, ms, mt = incl[d], strict[d], incl[1 - d]
            g_col = jnp.sum(jnp.where(eye, g_row, 0.0), axis=1, keepdims=True)
            b_col = jnp.sum(jnp.where(eye, b_row, 0.0), axis=1, keepdims=True)
            gc_col = jnp.sum(jnp.where(m, g_row, 0.0), axis=1, keepdims=True)
            gc_row = jnp.sum(jnp.where(mt, g_col, 0.0), axis=0, keepdims=True)
            tot = jnp.sum(g_row, axis=1, keepdims=True)
            decay = jnp.where(m, jnp.exp(jnp.where(m, gc_col - gc_row, 0.0)), 0.0)
            a = jnp.where(ms, -(kk * b_col * decay), 0.0)
            x = eye_f + a
            p = a
            for _ in range(n_sq):
                p = _dot(p, p)
                x = x + _dot(x, p)
            egc = jnp.exp(gc_col)
            sol = _dot(x, jnp.concatenate([v * b_col, k * (b_col * egc)], axis=1))
            u_s[d, c] = sol[:, :DH]
            wq_s[d, c] = jnp.concatenate([sol[:, DH:], q * egc], axis=0).astype(BF16)
            at_s[d, c] = (qk * decay).astype(BF16)
            kd_s[d, c] = (k * jnp.exp(tot - gc_col)).astype(BF16)
            gl_s[d, pl.ds(c, 1), :] = jnp.broadcast_to(jnp.exp(tot), (1, DH))
        return 0

    lax.fori_loop(0, NC, prep, 0)

    s_s[...] = jnp.zeros_like(s_s)

    def scan(n, _):
        for d in range(2):
            c = n if d == 0 else NC - 1 - n
            rows = pl.ds(pl.multiple_of(c * C, C), C)
            s = s_s[d]
            r1 = jnp.dot(wq_s[d, c], s.astype(BF16), preferred_element_type=F32)
            v_new = (u_s[d, c] - r1[:C]).astype(BF16)
            o = r1[C:] + jnp.dot(at_s[d, c], v_new, preferred_element_type=F32)
            s_s[d] = s * gl_s[d, pl.ds(c, 1), :] + _dot_tn(kd_s[d, c], v_new)
            if d == 0:
                o_ref[0, rows, :] = o
            else:
                ob[rows, :] = o
        return 0

    lax.fori_loop(0, NC, scan, 0)

    def gate_rows(i, _):
        rows = pl.ds(pl.multiple_of(i * R, R), R)
        o = o_ref[0, rows, :] + ob[rows, :]
        o = o * lax.rsqrt(jnp.mean(o * o, -1, keepdims=True) + RMS_EPS) * nw_ref[...]
        o_ref[0, rows, :] = o * _silu(z_ref[0, rows, :])
        return 0

    lax.fori_loop(0, L // R, gate_rows, 0)


def _deltanet(proj, gt, conv_w, norm_w, *, B, L, H):
    DH, C, R = DN_HEAD_DIM, DN_CHUNK, 256
    NC = L // C
    col = lambda off: pl.BlockSpec((1, L, DH), lambda b, h: (b, 0, off + h))
    cw = lambda off: pl.BlockSpec((3, DH), lambda b, h: (0, off + h))
    kern = functools.partial(_deltanet_kernel, L=L, C=C, H=H, R=R)
    return pl.pallas_call(
        kern, grid=(B, H),
        in_specs=[col(0), col(H), col(2 * H), col(3 * H), cw(0), cw(H), cw(2 * H),
                  pl.BlockSpec((1, 4 * H, NC, C), lambda b, h: (b, 0, 0, 0)),
                  pl.BlockSpec((1, DH), lambda b, h: (0, 0))],
        out_specs=pl.BlockSpec((1, L, DH), lambda b, h: (b, 0, h)),
        out_shape=jax.ShapeDtypeStruct((B, L, H * DH), F32),
        scratch_shapes=[pltpu.VMEM((L, DH), F32), pltpu.VMEM((L, DH), F32), pltpu.VMEM((L, DH), F32),
                        pltpu.VMEM((L, DH), F32),
                        pltpu.VMEM((2, NC, 2 * C, DH), BF16), pltpu.VMEM((2, NC, C, C), BF16),
                        pltpu.VMEM((2, NC, C, DH), BF16), pltpu.VMEM((2, NC, C, DH), F32),
                        pltpu.VMEM((2, NC, DH), F32), pltpu.VMEM((2, DH, DH), F32)],
        compiler_params=_cparams("parallel", "arbitrary"), name="deltanet",
    )(proj, proj, proj, proj, conv_w, conv_w, conv_w, gt, norm_w.reshape(1, DH))


def _conformer_kernel(val_ref, gate_ref, w_ref, b_ref, g_ref, beta_ref, o_ref, ypad, *, L, K, R, HALO):
    pad = K // 2
    zeros = jnp.zeros((HALO, CV_GROUP), F32)
    ypad[pl.ds(0, HALO), :] = zeros
    ypad[pl.ds(HALO + L, HALO), :] = zeros

    def glu_rows(i, _):
        r0 = pl.multiple_of(i * R, R)
        ypad[pl.ds(HALO + r0, R), :] = val_ref[0, pl.ds(r0, R), :] * _sigmoid(gate_ref[0, pl.ds(r0, R), :])
        return 0

    lax.fori_loop(0, L // R, glu_rows, 0)

    def conv_rows(i, _):
        r0 = pl.multiple_of(i * R, R)
        acc = jnp.zeros((R, CV_GROUP), F32)
        for t in range(K):
            acc = acc + w_ref[t:t + 1, :] * ypad[pl.ds(r0 + (HALO - pad + t), R), :]
        y = _layer_norm(acc + b_ref[...], g_ref[...], beta_ref[...])
        o_ref[0, pl.ds(r0, R), :] = _silu(y)
        return 0

    lax.fori_loop(0, L // R, conv_rows, 0)


def _conformer(proj, w, b, g, beta, *, B, L, col0, width):
    K = w.shape[0]
    n = width // CV_GROUP
    HALO, R = 16, 64
    c0 = col0 // CV_GROUP
    vec = pl.BlockSpec((1, CV_GROUP), lambda bb, j: (0, j))
    kern = functools.partial(_conformer_kernel, L=L, K=K, R=R, HALO=HALO)
    return pl.pallas_call(
        kern, grid=(B, n),
        in_specs=[pl.BlockSpec((1, L, CV_GROUP), lambda bb, j: (bb, 0, c0 + j)),
                  pl.BlockSpec((1, L, CV_GROUP), lambda bb, j: (bb, 0, c0 + n + j)),
                  pl.BlockSpec((K, CV_GROUP), lambda bb, j: (0, j)), vec, vec, vec],
        out_specs=pl.BlockSpec((1, L, CV_GROUP), lambda bb, j: (bb, 0, j)),
        out_shape=jax.ShapeDtypeStruct((B, L, width), F32),
        scratch_shapes=[pltpu.VMEM((L + 2 * HALO, CV_GROUP), F32)],
        compiler_params=_cparams("parallel", "parallel"), name="conformer",
    )(proj, proj, w, b.reshape(1, width), g.reshape(1, width), beta.reshape(1, width))


def _out_router_kernel(dn_ref, cv_ref, h_ref, w_ref, g_ref, b_ref, wr_ref, br_ref,
                       h1_ref, h1p_ref, route_ref, routet_ref, *, alpha, n_dn):
    mix = jnp.dot(dn_ref[...].astype(BF16), w_ref[pl.ds(0, n_dn), :], preferred_element_type=F32)
    mix = mix + jnp.dot(cv_ref[...].astype(BF16), w_ref[pl.ds(n_dn, w_ref.shape[0] - n_dn), :],
                        preferred_element_type=F32)
    h1 = _layer_norm(alpha * h_ref[...] + mix, g_ref[...], b_ref[...])
    h1_ref[...] = h1
    h1p_ref[...] = _pack_halves(h1)

    logits = jnp.dot(h1.astype(BF16), wr_ref[...], preferred_element_type=F32) + br_ref[...]
    lane = lax.broadcasted_iota(I32, logits.shape, 1)
    neg = jnp.float32(-jnp.inf)
    big = jnp.int32(LANES)
    gl = jnp.where(lane < N_GROUPS, logits, neg)
    gmax = jnp.max(gl, -1, keepdims=True)
    gsel = jnp.min(jnp.where(gl == gmax, lane, big), -1, keepdims=True)
    pg = 1.0 / jnp.sum(jnp.exp(gl - gmax), -1, keepdims=True)
    lo = N_GROUPS + gsel * EXPERTS_PER_GROUP
    el = jnp.where((lane >= lo) & (lane < lo + EXPERTS_PER_GROUP), logits, neg)
    e1 = jnp.max(el, -1, keepdims=True)
    i1 = jnp.min(jnp.where(el == e1, lane, big), -1, keepdims=True)
    el2 = jnp.where(lane == i1, neg, el)
    e2 = jnp.max(el2, -1, keepdims=True)
    i2 = jnp.min(jnp.where(el2 == e2, lane, big), -1, keepdims=True)
    r = jnp.exp(e2 - e1)
    p1 = 1.0 / (1.0 + r)
    p2 = r * p1
    route = jnp.where(lane == 0, (i1 - N_GROUPS).astype(F32),
                      jnp.where(lane == 1, (i2 - N_GROUPS).astype(F32),
                                jnp.where(lane == 2, pg * p1, jnp.where(lane == 3, pg * p2, 0.0))))
    route_ref[...] = route
    routet_ref[...] = route.T[:8, :]


def _out_router(dn, cv, h, w_out, g, b, w_router, b_router, *, alpha, tm=256):
    T, D = h.shape
    n_dn, n_cv = dn.shape[1], cv.shape[1]
    vec = pl.BlockSpec((1, D), lambda i: (0, 0))
    row = pl.BlockSpec((tm, D), lambda i: (i, 0))
    kern = functools.partial(_out_router_kernel, alpha=alpha, n_dn=n_dn)
    return pl.pallas_call(
        kern, grid=(T // tm,),
        in_specs=[pl.BlockSpec((tm, n_dn), lambda i: (i, 0)), pl.BlockSpec((tm, n_cv), lambda i: (i, 0)), row,
                  pl.BlockSpec((n_dn + n_cv, D), lambda i: (0, 0)), vec, vec,
                  pl.BlockSpec((D, LANES), lambda i: (0, 0)), pl.BlockSpec((1, LANES), lambda i: (0, 0))],
        out_specs=[row, pl.BlockSpec((tm, D // 2), lambda i: (i, 0)),
                   pl.BlockSpec((tm, LANES), lambda i: (i, 0)), pl.BlockSpec((8, tm), lambda i: (0, i))],
        out_shape=[jax.ShapeDtypeStruct((T, D), F32), jax.ShapeDtypeStruct((T, D // 2), U32),
                   jax.ShapeDtypeStruct((T, LANES), F32), jax.ShapeDtypeStruct((8, T), F32)],
        compiler_params=_cparams("parallel"), name="out_router",
    )(dn, cv, h, w_out, g.reshape(1, D), b.reshape(1, D), w_router, b_router)


def _onehot_t(routet_ref, tr):
    sub = lax.broadcasted_iota(I32, (2 * N_EXPERTS, tr), 0)
    e1 = routet_ref[0:1, :].astype(I32)
    e2 = routet_ref[1:2, :].astype(I32)
    return sub == jnp.where(sub < N_EXPERTS, e1, e2 + N_EXPERTS)


def _count_kernel(routet_ref, cnt_ref, *, tr):
    @pl.when(pl.program_id(0) == 0)
    def _():
        cnt_ref[...] = jnp.zeros_like(cnt_ref)
    oh = _onehot_t(routet_ref, tr).astype(F32)
    cnt_ref[...] += jnp.sum(oh, axis=1, keepdims=True)


def _place_kernel(routet_ref, base_ref, dest_ref, carry, *, tr):
    @pl.when(pl.program_id(0) == 0)
    def _():
        carry[...] = jnp.zeros_like(carry)
    oh = _onehot_t(routet_ref, tr)
    ohb = oh.astype(F32).astype(BF16)
    ri = lax.broadcasted_iota(I32, (tr, tr), 0)
    ci = lax.broadcasted_iota(I32, (tr, tr), 1)
    before = (ri < ci).astype(F32).astype(BF16)
    excl = jnp.dot(ohb, before, preferred_element_type=F32)
    pos = jnp.where(oh, excl + carry[...] + base_ref[...], 0.0)
    d1 = jnp.sum(pos[:N_EXPERTS], axis=0, keepdims=True)
    d2 = jnp.sum(pos[N_EXPERTS:], axis=0, keepdims=True)
    sub = lax.broadcasted_iota(I32, (8, tr), 0)
    dest_ref[...] = jnp.where(sub == 0, d1, jnp.where(sub == 1, d2, 0.0)).astype(I32)
    carry[...] += jnp.sum(oh.astype(F32), axis=1, keepdims=True)


def _placement(routet, tr=512):
    T = routet.shape[1]
    E = N_EXPERTS
    rt = pl.BlockSpec((8, tr), lambda i: (0, i))
    col = pl.BlockSpec((2 * E, 1), lambda i: (0, 0))
    cnt = pl.pallas_call(
        functools.partial(_count_kernel, tr=tr), grid=(T // tr,), in_specs=[rt], out_specs=col,
        out_shape=jax.ShapeDtypeStruct((2 * E, 1), F32),
        compiler_params=_cparams("arbitrary"), name="moe_count")(routet)
    c1 = cnt[:E, 0].astype(I32)
    c2 = cnt[E:, 0].astype(I32)
    padded = ((c1 + c2 + FFN_BLOCK - 1) // FFN_BLOCK) * FFN_BLOCK
    pend = jnp.cumsum(padded)
    pstart = pend - padded
    base = jnp.concatenate([pstart, pstart + c1]).astype(F32).reshape(2 * E, 1)
    dest = pl.pallas_call(
        functools.partial(_place_kernel, tr=tr), grid=(T // tr,), in_specs=[rt, col], out_specs=rt,
        out_shape=jax.ShapeDtypeStruct((8, T), I32),
        scratch_shapes=[pltpu.VMEM((2 * E, 1), F32)],
        compiler_params=_cparams("arbitrary"), name="moe_place")(routet, base)
    n_blocks = (T * TOP_K + E * (FFN_BLOCK - 1) + FFN_BLOCK - 1) // FFN_BLOCK
    starts = jnp.arange(n_blocks, dtype=I32) * FFN_BLOCK
    blk_e = jnp.minimum(jnp.searchsorted(pend, starts, side='right'), E - 1).astype(I32)
    blk_first = jnp.concatenate([jnp.ones((1,), I32), (blk_e[1:] != blk_e[:-1]).astype(I32)])
    n_used = (pend[-1] // FFN_BLOCK).astype(I32).reshape(1)
    return dest[0], dest[1], blk_e, blk_first, n_used, n_blocks


def _dispatch_kernel(d1_ref, d2_ref, src_ref, init_ref, xs_ref, sem, *, tb):
    del init_ref
    base = pl.program_id(0) * tb

    def issue(t, _):
        row = src_ref.at[pl.ds(base + t, 1), :]
        pltpu.make_async_copy(row, xs_ref.at[pl.ds(d1_ref[t], 1), :], sem).start()
        pltpu.make_async_copy(row, xs_ref.at[pl.ds(d2_ref[t], 1), :], sem).start()
        return 0

    lax.fori_loop(0, tb, issue, 0)
    blk = src_ref.at[pl.ds(base, tb), :]
    pltpu.make_async_copy(blk, xs_ref.at[pl.ds(0, tb), :], sem).wait()
    pltpu.make_async_copy(blk, xs_ref.at[pl.ds(0, tb), :], sem).wait()


def _dispatch(h1p, d1, d2, n_rows, tb=1024):
    T, W = h1p.shape
    idx = pl.BlockSpec((tb,), lambda i: (i,), memory_space=pltpu.SMEM)
    any_spec = pl.BlockSpec(memory_space=pl.ANY)
    return pl.pallas_call(
        functools.partial(_dispatch_kernel, tb=tb), grid=(T // tb,),
        in_specs=[idx, idx, any_spec, any_spec], out_specs=any_spec,
        out_shape=jax.ShapeDtypeStruct((n_rows, W), U32),
        scratch_shapes=[pltpu.SemaphoreType.DMA(())],
        input_output_aliases={3: 0},
        compiler_params=_cparams("arbitrary"), name="moe_dispatch",
    )(d1, d2, h1p, jnp.zeros((n_rows, W), U32))


def _ffn_kernel(blk_e_ref, blk_first_ref, n_used_ref, xs_ref, wgu_ref, wdn_ref, y_ref, wgu_s, wdn_s, *, ff):
    i = pl.program_id(0)

    @pl.when(blk_first_ref[i] == 1)
    def _():
        wgu_s[...] = wgu_ref[0].astype(BF16)
        wdn_s[...] = wdn_ref[0].astype(BF16)

    @pl.when(i < n_used_ref[0])
    def _():
        lo, hi = _unpack_halves(xs_ref[...])
        x = jnp.concatenate([lo, hi], axis=1).astype(BF16)
        gu = jnp.dot(x, wgu_s[...], preferred_element_type=F32)
        act = (_silu(gu[:, :ff]) * gu[:, ff:]).astype(BF16)
        y_ref[...] = _pack_halves(jnp.dot(act, wdn_s[...], preferred_element_type=F32))

    @pl.when(i >= n_used_ref[0])
    def _():
        y_ref[...] = jnp.zeros_like(y_ref)


def _ffn(xs, w_gu, w_dn, blk_e, blk_first, n_used, n_blocks):
    E, D, FF2 = w_gu.shape
    ff = FF2 // 2
    W = xs.shape[1]
    grid_spec = pltpu.PrefetchScalarGridSpec(
        num_scalar_prefetch=3, grid=(n_blocks,),
        in_specs=[pl.BlockSpec((FFN_BLOCK, W), lambda i, be, bf, nu: (i, 0)),
                  pl.BlockSpec((1, D, FF2), lambda i, be, bf, nu: (be[i], 0, 0)),
                  pl.BlockSpec((1, ff, D), lambda i, be, bf, nu: (be[i], 0, 0))],
        out_specs=pl.BlockSpec((FFN_BLOCK, W), lambda i, be, bf, nu: (i, 0)),
        scratch_shapes=[pltpu.VMEM((D, FF2), BF16), pltpu.VMEM((ff, D), BF16)])
    return pl.pallas_call(
        functools.partial(_ffn_kernel, ff=ff), grid_spec=grid_spec,
        out_shape=jax.ShapeDtypeStruct(xs.shape, U32),
        compiler_params=_cparams("arbitrary"), name="moe_ffn",
    )(blk_e, blk_first, n_used, xs, w_gu, w_dn)


def _combine_kernel(d1_ref, d2_ref, y_ref, h1_ref, route_ref, g_ref, b_ref, h2_ref, h2b_ref, ybuf, sem,
                    *, tc, alpha):
    def issue(t, _):
        pltpu.make_async_copy(y_ref.at[pl.ds(d1_ref[t], 1), :], ybuf.at[0, pl.ds(t, 1), :], sem).start()
        pltpu.make_async_copy(y_ref.at[pl.ds(d2_ref[t], 1), :], ybuf.at[1, pl.ds(t, 1), :], sem).start()
        return 0

    lax.fori_loop(0, tc, issue, 0)
    pltpu.make_async_copy(y_ref.at[pl.ds(0, tc), :], ybuf.at[0], sem).wait()
    pltpu.make_async_copy(y_ref.at[pl.ds(0, tc), :], ybuf.at[1], sem).wait()

    route = route_ref[...]
    g1 = route[:, 2:3]
    g2 = route[:, 3:4]
    lo1, hi1 = _unpack_halves(ybuf[0])
    lo2, hi2 = _unpack_halves(ybuf[1])
    ffn = jnp.concatenate([g1 * lo1 + g2 * lo2, g1 * hi1 + g2 * hi2], axis=1)
    h2 = _layer_norm(alpha * h1_ref[...] + ffn, g_ref[...], b_ref[...])
    h2_ref[...] = h2
    h2b_ref[...] = h2.astype(BF16)


def _combine(y, d1, d2, h1, route, g, b, *, alpha, tc=256):
    T, D = h1.shape
    W = y.shape[1]
    idx = pl.BlockSpec((tc,), lambda i: (i,), memory_space=pltpu.SMEM)
    row = pl.BlockSpec((tc, D), lambda i: (i, 0))
    vec = pl.BlockSpec((1, D), lambda i: (0, 0))
    return pl.pallas_call(
        functools.partial(_combine_kernel, tc=tc, alpha=alpha), grid=(T // tc,),
        in_specs=[idx, idx, pl.BlockSpec(memory_space=pl.ANY), row,
                  pl.BlockSpec((tc, LANES), lambda i: (i, 0)), vec, vec],
        out_specs=[row, row],
        out_shape=[jax.ShapeDtypeStruct((T, D), F32), jax.ShapeDtypeStruct((T, D), BF16)],
        scratch_shapes=[pltpu.VMEM((2, tc, W), U32), pltpu.SemaphoreType.DMA(())],
        compiler_params=_cparams("arbitrary"), name="moe_combine",
    )(d1, d2, y, h1, route, g.reshape(1, D), b.reshape(1, D))


def kernel(x, emb_ln_g, emb_ln_b, w_in, short_conv_w, a_log, dt_bias, dn_norm_w, dw_conv_w, dw_conv_b,
           conv_ln_g, conv_ln_b, w_out, ln1_g, ln1_b, w_group, b_group, w_expert, b_expert, w_gate_up,
           w_down, ln2_g, ln2_b):
    B, L, D = x.shape
    T = B * L
    depth = w_in.shape[0]
    H = a_log.shape[2]
    dn_w = H * DN_HEAD_DIM
    cv_w = dw_conv_w.shape[2]
    n_ab = 4 * H
    alpha = (2 * depth) ** 0.25
    C = DN_CHUNK

    h, hb = _emb_ln(x.reshape(T, D), emb_ln_g, emb_ln_b)
    for l in range(depth):
        w = w_in[l]
        w_main = jnp.concatenate([w[:, :4 * dn_w], w[:, 4 * dn_w + n_ab:]], axis=1).astype(BF16)
        proj = _in_proj(hb, w_main).reshape(B, L, -1)
        gb = _decay_beta(hb, w[:, 4 * dn_w:4 * dn_w + n_ab], a_log[l], dt_bias[l])
        gt = gb[:, :n_ab].reshape(B, L // C, C, n_ab).transpose(0, 3, 1, 2)
        dn = _deltanet(proj, gt, short_conv_w[l], dn_norm_w[l], B=B, L=L, H=H)
        cv = _conformer(proj, dw_conv_w[l], dw_conv_b[l], conv_ln_g[l], conv_ln_b[l],
                        B=B, L=L, col0=4 * dn_w, width=cv_w)
        n_r = N_GROUPS + N_EXPERTS
        w_router = jnp.pad(jnp.concatenate([w_group[l], w_expert[l]], axis=1),
                           ((0, 0), (0, LANES - n_r))).astype(BF16)
        b_router = jnp.pad(jnp.concatenate([b_group[l], b_expert[l]]), (0, LANES - n_r)).reshape(1, LANES)
        h1, h1p, route, routet = _out_router(
            dn.reshape(T, dn_w), cv.reshape(T, cv_w), h, w_out[l].astype(BF16), ln1_g[l], ln1_b[l],
            w_router, b_router, alpha=alpha)
        d1, d2, blk_e, blk_first, n_used, n_blocks = _placement(routet)
        xs = _dispatch(h1p, d1, d2, n_blocks * FFN_BLOCK)
        y = _ffn(xs, w_gate_up[l], w_down[l], blk_e, blk_first, n_used, n_blocks)
        h, hb = _combine(y, d1, d2, h1, route, ln2_g[l], ln2_b[l], alpha=alpha)
    return h.reshape(B, L, D)
```

```python
import functools

import jax
import jax.numpy as jnp
from jax import lax
from jax.experimental import pallas as pl
from jax.experimental.pallas import tpu as pltpu

F32 = jnp.float32
BF16 = jnp.bfloat16
U32 = jnp.uint32
I32 = jnp.int32

LANES = 128
DN_HEAD_DIM = 128
DN_CHUNK = 128
PREP_UNROLL = 8
CV_GROUP = 128
N_GROUPS = 8
EXPERTS_PER_GROUP = 8
N_EXPERTS = N_GROUPS * EXPERTS_PER_GROUP
TOP_K = 2
FFN_BLOCK = 128
LN_EPS = 1e-5
RMS_EPS = 1e-6
VMEM_LIMIT = 56 * 1024 * 1024


def _cparams(*sem):
    return pltpu.CompilerParams(dimension_semantics=sem, vmem_limit_bytes=VMEM_LIMIT)


def _layer_norm(x, g, b):
    mu = jnp.mean(x, -1, keepdims=True)
    xc = x - mu
    var = jnp.mean(xc * xc, -1, keepdims=True)
    return xc * lax.rsqrt(var + LN_EPS) * g + b


def _sigmoid(x):
    return 1.0 / (1.0 + jnp.exp(-x))


def _silu(x):
    return x * _sigmoid(x)


def _pack_halves(y):
    n = y.shape[1] // 2
    bits = lax.bitcast_convert_type(y.astype(BF16).astype(F32), U32)
    return (bits[:, :n] >> 16) | bits[:, n:]


def _unpack_halves(u):
    lo = lax.bitcast_convert_type(u << 16, F32)
    hi = lax.bitcast_convert_type(u & jnp.uint32(0xFFFF0000), F32)
    return lo, hi


def _emb_ln_kernel(x_ref, g_ref, b_ref, h_ref, hb_ref):
    h = _layer_norm(x_ref[...], g_ref[...], b_ref[...])
    h_ref[...] = h
    hb_ref[...] = h.astype(BF16)


def _emb_ln(x, g, b, tm=512):
    T, D = x.shape
    row = pl.BlockSpec((tm, D), lambda i: (i, 0))
    vec = pl.BlockSpec((1, D), lambda i: (0, 0))
    return pl.pallas_call(
        _emb_ln_kernel, grid=(T // tm,), in_specs=[row, vec, vec], out_specs=[row, row],
        out_shape=[jax.ShapeDtypeStruct((T, D), F32), jax.ShapeDtypeStruct((T, D), BF16)],
        compiler_params=_cparams("parallel"), name="emb_ln")(x, g.reshape(1, D), b.reshape(1, D))


def _matmul_kernel(x_ref, w_ref, o_ref):
    o_ref[...] = jnp.dot(x_ref[...], w_ref[...], preferred_element_type=F32)


def _in_proj(hb, w, tm=1024, tn=1024):
    T, D = hb.shape
    N = w.shape[1]
    return pl.pallas_call(
        _matmul_kernel, grid=(T // tm, N // tn),
        in_specs=[pl.BlockSpec((tm, D), lambda i, j: (i, 0)), pl.BlockSpec((D, tn), lambda i, j: (0, j))],
        out_specs=pl.BlockSpec((tm, tn), lambda i, j: (i, j)),
        out_shape=jax.ShapeDtypeStruct((T, N), F32),
        compiler_params=_cparams("parallel", "arbitrary"), name="in_proj")(hb, w)


def _decay_beta_kernel(x_ref, w_ref, alog_ref, dtb_ref, o_ref, *, n_decay):
    ab = jnp.dot(x_ref[...], w_ref[...], preferred_element_type=F32)
    lane = lax.broadcasted_iota(I32, ab.shape, 1)
    s = ab + dtb_ref[...]
    softplus = jnp.maximum(s, 0.0) + jnp.log(1.0 + jnp.exp(-jnp.abs(s)))
    g = -jnp.exp(alog_ref[...]) * softplus
    o_ref[...] = jnp.where(lane < n_decay, g, _sigmoid(ab))


def _decay_beta(hb, w_ab, a_log, dt_bias, tm=1024):
    T, D = hb.shape
    n = a_log.size
    pad = lambda v: jnp.pad(v.reshape(1, n).astype(F32), ((0, 0), (0, LANES - n)))
    w = jnp.pad(w_ab, ((0, 0), (0, LANES - w_ab.shape[1]))).astype(BF16)
    vec = pl.BlockSpec((1, LANES), lambda i: (0, 0))
    return pl.pallas_call(
        functools.partial(_decay_beta_kernel, n_decay=n), grid=(T // tm,),
        in_specs=[pl.BlockSpec((tm, D), lambda i: (i, 0)), pl.BlockSpec((D, LANES), lambda i: (0, 0)), vec, vec],
        out_specs=pl.BlockSpec((tm, LANES), lambda i: (i, 0)),
        out_shape=jax.ShapeDtypeStruct((T, LANES), F32),
        compiler_params=_cparams("parallel"), name="decay_beta")(hb, w, pad(a_log), pad(dt_bias))


def _dot(a, b):
    return jnp.dot(a.astype(BF16), b.astype(BF16), preferred_element_type=F32)


def _dot_nt(a, b):
    return lax.dot_general(a.astype(BF16), b.astype(BF16), (((1,), (1,)), ((), ())),
                           preferred_element_type=F32)


def _dot_tn(a, b):
    return lax.dot_general(a.astype(BF16), b.astype(BF16), (((0,), (0,)), ((), ())),
                           preferred_element_type=F32)


def _deltanet_kernel(q_ref, k_ref, v_ref, z_ref, cq_ref, ck_ref, cv_ref, gt_ref, nw_ref, o_ref,
                     qs, ks, vs, ob, wq_s, at_s, kd_s, u_s, gl_s, s_s, rhs_s, *, L, C, H, R):
    DH = DN_HEAD_DIM
    h = pl.program_id(1)
    NC = L // C

    def conv_rows(i, _):
        r0 = pl.multiple_of(i * R, R)
        rows = pl.ds(r0, R)
        rid = lax.broadcasted_iota(I32, (R, DH), 0)
        has_prev = (r0 > 0).astype(F32)
        has_next = (r0 + R < L).astype(F32)

        def conv_silu(ref, w_ref):
            x = ref[0, rows, :]
            xm = ref[0, pl.ds(jnp.maximum(r0 - 1, 0), 1), :] * has_prev
            xp = ref[0, pl.ds(jnp.minimum(r0 + R, L - 1), 1), :] * has_next
            x_prev = jnp.where(rid == 0, xm, pltpu.roll(x, 1, 0))
            x_next = jnp.where(rid == R - 1, xp, pltpu.roll(x, R - 1, 0))
            return _silu(w_ref[0:1, :] * x_prev + w_ref[1:2, :] * x + w_ref[2:3, :] * x_next)

        q = conv_silu(q_ref, cq_ref)
        k = conv_silu(k_ref, ck_ref)
        qs[rows, :] = q * (lax.rsqrt(jnp.sum(q * q, -1, keepdims=True) + RMS_EPS) * (DH ** -0.5))
        ks[rows, :] = k * lax.rsqrt(jnp.sum(k * k, -1, keepdims=True) + RMS_EPS)
        vs[rows, :] = conv_silu(v_ref, cv_ref)
        return 0

    lax.fori_loop(0, L // R, conv_rows, 0)

    ri = lax.broadcasted_iota(I32, (C, C), 0)
    ci = lax.broadcasted_iota(I32, (C, C), 1)
    eye = ri == ci
    eye_f = eye.astype(F32)
    incl = (ri >= ci, ri <= ci)
    strict = (ri > ci, ri < ci)
    n_sq = C.bit_length() - 2
    zero_b = jnp.zeros((C, C), BF16)

    def block_diag(a, b):
        return jnp.concatenate([jnp.concatenate([a, zero_b], axis=1),
                                jnp.concatenate([zero_b, b], axis=1)], axis=0)

    def prep_start(c, j):
        rows = pl.ds(pl.multiple_of(c * C, C), C)
        q = qs[rows, :]
        k = ks[rows, :]
        v = vs[rows, :]
        kq = _dot_nt(jnp.concatenate([k, q], axis=0), k)
        kk, qk = kq[:C], kq[C:]
        a2 = []
        for d in range(2):
            g_row = gt_ref[0, d * H + h, pl.ds(c, 1), :]
            b_row = gt_ref[0, (2 + d) * H + h, pl.ds(c, 1), :]
            m, ms, mt = incl[d], strict[d], incl[1 - d]
            g_col = jnp.sum(jnp.where(eye, g_row, 0.0), axis=1, keepdims=True)
            b_col = jnp.sum(jnp.where(eye, b_row, 0.0), axis=1, keepdims=True)
            gc_col = jnp.sum(jnp.where(m, g_row, 0.0), axis=1, keepdims=True)
            gc_row = jnp.sum(jnp.where(mt, g_col, 0.0), axis=0, keepdims=True)
            tot = jnp.sum(g_row, axis=1, keepdims=True)
            decay = jnp.where(m, jnp.exp(jnp.where(m, gc_col - gc_row, 0.0)), 0.0)
            a2.append(jnp.where(ms, -(kk * b_col * decay), 0.0))
            egc = jnp.exp(gc_col)
            rhs_s[d, j] = jnp.concatenate([v * b_col, k * (b_col * egc)], axis=1).astype(BF16)
            wq_s[d, c, pl.ds(C, C), :] = (q * egc).astype(BF16)
            at_s[d, c] = (qk * decay).astype(BF16)
            kd_s[d, c] = (k * jnp.exp(tot - gc_col)).astype(BF16)
            gl_s[d, pl.ds(c, 1), :] = jnp.broadcast_to(jnp.exp(tot), (1, DH))
        return jnp.concatenate(a2, axis=1)

    def prep(i, _):
        cs = [i * PREP_UNROLL + j for j in range(PREP_UNROLL)]
        ps = [prep_start(c, j) for j, c in enumerate(cs)]
        eye2 = jnp.concatenate([eye_f, eye_f], axis=1)
        xs_ = [eye2 + p for p in ps]
        for _ in range(n_sq):
            pbs = [p.astype(BF16) for p in ps]
            ps = [jnp.dot(pb, block_diag(pb[:, :C], pb[:, C:]), preferred_element_type=F32) for pb in pbs]
            pbs = [p.astype(BF16) for p in ps]
            xs_ = [x + jnp.dot(x.astype(BF16), block_diag(pb[:, :C], pb[:, C:]), preferred_element_type=F32)
                   for x, pb in zip(xs_, pbs)]
        for j, (c, x) in enumerate(zip(cs, xs_)):
            for d in range(2):
                sol = jnp.dot(x[:, d * C:(d + 1) * C].astype(BF16), rhs_s[d, j], preferred_element_type=F32)
                u_s[d, c] = sol[:, :DH]
                wq_s[d, c, pl.ds(0, C), :] = sol[:, DH:].astype(BF16)
        return 0

    lax.fori_loop(0, NC // PREP_UNROLL, prep, 0)

    s_s[...] = jnp.zeros_like(s_s)

    def scan(n, _):
        cf, cb = n, NC - 1 - n
        s_f, s_b = s_s[0], s_s[1]
        wq = jnp.concatenate([wq_s[0, cf], wq_s[1, cb]], axis=1)
        r1 = jnp.dot(wq, block_diag(s_f.astype(BF16), s_b.astype(BF16)), preferred_element_type=F32)
        u = jnp.concatenate([u_s[0, cf], u_s[1, cb]], axis=1)
        v_new = (u - r1[:C]).astype(BF16)
        vd = block_diag(v_new[:, :DH], v_new[:, DH:])
        at = jnp.concatenate([at_s[0, cf], at_s[1, cb]], axis=1)
        o = r1[C:] + jnp.dot(at, vd, preferred_element_type=F32)
        kd = jnp.concatenate([kd_s[0, cf], kd_s[1, cb]], axis=0)
        ds = lax.dot_general(kd, vd, (((0,), (0,)), ((), ())), preferred_element_type=F32)
        s_s[0] = s_f * gl_s[0, pl.ds(cf, 1), :] + ds[:, :DH]
        s_s[1] = s_b * gl_s[1, pl.ds(cb, 1), :] + ds[:, DH:]
        o_ref[0, pl.ds(pl.multiple_of(cf * C, C), C), :] = o[:, :DH]
        ob[pl.ds(pl.multiple_of(cb * C, C), C), :] = o[:, DH:]
        return 0

    lax.fori_loop(0, NC, scan, 0)

    def gate_rows(i, _):
        rows = pl.ds(pl.multiple_of(i * R, R), R)
        o = o_ref[0, rows, :] + ob[rows, :]
        o = o * lax.rsqrt(jnp.mean(o * o, -1, keepdims=True) + RMS_EPS) * nw_ref[...]
        o_ref[0, rows, :] = o * _silu(z_ref[0, rows, :])
        return 0

    lax.fori_loop(0, L // R, gate_rows, 0)


def _deltanet(proj, gt, conv_w, norm_w, *, B, L, H):
    DH, C, R = DN_HEAD_DIM, DN_CHUNK, 256
    assert C == DH and L % (C * PREP_UNROLL) == 0
    NC = L // C
    col = lambda off: pl.BlockSpec((1, L, DH), lambda b, h: (b, 0, off + h))
    cw = lambda off: pl.BlockSpec((3, DH), lambda b, h: (0, off + h))
    kern = functools.partial(_deltanet_kernel, L=L, C=C, H=H, R=R)
    return pl.pallas_call(
        kern, grid=(B, H),
        in_specs=[col(0), col(H), col(2 * H), col(3 * H), cw(0), cw(H), cw(2 * H),
                  pl.BlockSpec((1, 4 * H, NC, C), lambda b, h: (b, 0, 0, 0)),
                  pl.BlockSpec((1, DH), lambda b, h: (0, 0))],
        out_specs=pl.BlockSpec((1, L, DH), lambda b, h: (b, 0, h)),
        out_shape=jax.ShapeDtypeStruct((B, L, H * DH), F32),
        scratch_shapes=[pltpu.VMEM((L, DH), F32), pltpu.VMEM((L, DH), F32), pltpu.VMEM((L, DH), F32),
                        pltpu.VMEM((L, DH), F32),
                        pltpu.VMEM((2, NC, 2 * C, DH), BF16), pltpu.VMEM((2, NC, C, C), BF16),
                        pltpu.VMEM((2, NC, C, DH), BF16), pltpu.VMEM((2, NC, C, DH), F32),
                        pltpu.VMEM((2, NC, DH), F32), pltpu.VMEM((2, DH, DH), F32),
                        pltpu.VMEM((2, PREP_UNROLL, C, 2 * DH), BF16)],
        compiler_params=_cparams("parallel", "arbitrary"), name="deltanet",
    )(proj, proj, proj, proj, conv_w, conv_w, conv_w, gt, norm_w.reshape(1, DH))


def _conformer_kernel(val_ref, gate_ref, w_ref, b_ref, g_ref, beta_ref, o_ref, ypad, *, L, K, R, HALO):
    pad = K // 2
    zeros = jnp.zeros((HALO, CV_GROUP), F32)
    ypad[pl.ds(0, HALO), :] = zeros
    ypad[pl.ds(HALO + L, HALO), :] = zeros

    def glu_rows(i, _):
        r0 = pl.multiple_of(i * R, R)
        ypad[pl.ds(HALO + r0, R), :] = val_ref[0, pl.ds(r0, R), :] * _sigmoid(gate_ref[0, pl.ds(r0, R), :])
        return 0

    lax.fori_loop(0, L // R, glu_rows, 0)

    def conv_rows(i, _):
        r0 = pl.multiple_of(i * R, R)
        acc = jnp.zeros((R, CV_GROUP), F32)
        for t in range(K):
            acc = acc + w_ref[t:t + 1, :] * ypad[pl.ds(r0 + (HALO - pad + t), R), :]
        y = _layer_norm(acc + b_ref[...], g_ref[...], beta_ref[...])
        o_ref[0, pl.ds(r0, R), :] = _silu(y)
        return 0

    lax.fori_loop(0, L // R, conv_rows, 0)


def _conformer(proj, w, b, g, beta, *, B, L, col0, width):
    K = w.shape[0]
    n = width // CV_GROUP
    HALO, R = 16, 64
    c0 = col0 // CV_GROUP
    vec = pl.BlockSpec((1, CV_GROUP), lambda bb, j: (0, j))
    kern = functools.partial(_conformer_kernel, L=L, K=K, R=R, HALO=HALO)
    return pl.pallas_call(
        kern, grid=(B, n),
        in_specs=[pl.BlockSpec((1, L, CV_GROUP), lambda bb, j: (bb, 0, c0 + j)),
                  pl.BlockSpec((1, L, CV_GROUP), lambda bb, j: (bb, 0, c0 + n + j)),
                  pl.BlockSpec((K, CV_GROUP), lambda bb, j: (0, j)), vec, vec, vec],
        out_specs=pl.BlockSpec((1, L, CV_GROUP), lambda bb, j: (bb, 0, j)),
        out_shape=jax.ShapeDtypeStruct((B, L, width), F32),
        scratch_shapes=[pltpu.VMEM((L + 2 * HALO, CV_GROUP), F32)],
        compiler_params=_cparams("parallel", "parallel"), name="conformer",
    )(proj, proj, w, b.reshape(1, width), g.reshape(1, width), beta.reshape(1, width))


def _out_router_kernel(dn_ref, cv_ref, h_ref, w_ref, g_ref, b_ref, wr_ref, br_ref,
                       h1_ref, h1p_ref, route_ref, routet_ref, *, alpha, n_dn):
    mix = jnp.dot(dn_ref[...].astype(BF16), w_ref[pl.ds(0, n_dn), :], preferred_element_type=F32)
    mix = mix + jnp.dot(cv_ref[...].astype(BF16), w_ref[pl.ds(n_dn, w_ref.shape[0] - n_dn), :],
                        preferred_element_type=F32)
    h1 = _layer_norm(alpha * h_ref[...] + mix, g_ref[...], b_ref[...])
    h1_ref[...] = h1
    h1p_ref[...] = _pack_halves(h1)

    logits = jnp.dot(h1.astype(BF16), wr_ref[...], preferred_element_type=F32) + br_ref[...]
    lane = lax.broadcasted_iota(I32, logits.shape, 1)
    neg = jnp.float32(-jnp.inf)
    big = jnp.int32(LANES)
    gl = jnp.where(lane < N_GROUPS, logits, neg)
    gmax = jnp.max(gl, -1, keepdims=True)
    gsel = jnp.min(jnp.where(gl == gmax, lane, big), -1, keepdims=True)
    pg = 1.0 / jnp.sum(jnp.exp(gl - gmax), -1, keepdims=True)
    lo = N_GROUPS + gsel * EXPERTS_PER_GROUP
    el = jnp.where((lane >= lo) & (lane < lo + EXPERTS_PER_GROUP), logits, neg)
    e1 = jnp.max(el, -1, keepdims=True)
    i1 = jnp.min(jnp.where(el == e1, lane, big), -1, keepdims=True)
    el2 = jnp.where(lane == i1, neg, el)
    e2 = jnp.max(el2, -1, keepdims=True)
    i2 = jnp.min(jnp.where(el2 == e2, lane, big), -1, keepdims=True)
    r = jnp.exp(e2 - e1)
    p1 = 1.0 / (1.0 + r)
    p2 = r * p1
    route = jnp.where(lane == 0, (i1 - N_GROUPS).astype(F32),
                      jnp.where(lane == 1, (i2 - N_GROUPS).astype(F32),
                                jnp.where(lane == 2, pg * p1, jnp.where(lane == 3, pg * p2, 0.0))))
    route_ref[...] = route
    routet_ref[...] = route.T[:8, :]


def _out_router(dn, cv, h, w_out, g, b, w_router, b_router, *, alpha, tm=256):
    T, D = h.shape
    n_dn, n_cv = dn.shape[1], cv.shape[1]
    vec = pl.BlockSpec((1, D), lambda i: (0, 0))
    row = pl.BlockSpec((tm, D), lambda i: (i, 0))
    kern = functools.partial(_out_router_kernel, alpha=alpha, n_dn=n_dn)
    return pl.pallas_call(
        kern, grid=(T // tm,),
        in_specs=[pl.BlockSpec((tm, n_dn), lambda i: (i, 0)), pl.BlockSpec((tm, n_cv), lambda i: (i, 0)), row,
                  pl.BlockSpec((n_dn + n_cv, D), lambda i: (0, 0)), vec, vec,
                  pl.BlockSpec((D, LANES), lambda i: (0, 0)), pl.BlockSpec((1, LANES), lambda i: (0, 0))],
        out_specs=[row, pl.BlockSpec((tm, D // 2), lambda i: (i, 0)),
                   pl.BlockSpec((tm, LANES), lambda i: (i, 0)), pl.BlockSpec((8, tm), lambda i: (0, i))],
        out_shape=[jax.ShapeDtypeStruct((T, D), F32), jax.ShapeDtypeStruct((T, D // 2), U32),
                   jax.ShapeDtypeStruct((T, LANES), F32), jax.ShapeDtypeStruct((8, T), F32)],
        compiler_params=_cparams("parallel"), name="out_router",
    )(dn, cv, h, w_out, g.reshape(1, D), b.reshape(1, D), w_router, b_router)


def _onehot_t(routet_ref, tr):
    sub = lax.broadcasted_iota(I32, (2 * N_EXPERTS, tr), 0)
    e1 = routet_ref[0:1, :].astype(I32)
    e2 = routet_ref[1:2, :].astype(I32)
    return sub == jnp.where(sub < N_EXPERTS, e1, e2 + N_EXPERTS)


def _count_kernel(routet_ref, cnt_ref, *, tr):
    @pl.when(pl.program_id(0) == 0)
    def _():
        cnt_ref[...] = jnp.zeros_like(cnt_ref)
    oh = _onehot_t(routet_ref, tr).astype(F32)
    cnt_ref[...] += jnp.sum(oh, axis=1, keepdims=True)


def _place_kernel(routet_ref, base_ref, dest_ref, carry, *, tr):
    @pl.when(pl.program_id(0) == 0)
    def _():
        carry[...] = jnp.zeros_like(carry)
    oh = _onehot_t(routet_ref, tr)
    ohb = oh.astype(F32).astype(BF16)
    ri = lax.broadcasted_iota(I32, (tr, tr), 0)
    ci = lax.broadcasted_iota(I32, (tr, tr), 1)
    before = (ri < ci).astype(F32).astype(BF16)
    excl = jnp.dot(ohb, before, preferred_element_type=F32)
    pos = jnp.where(oh, excl + carry[...] + base_ref[...], 0.0)
    d1 = jnp.sum(pos[:N_EXPERTS], axis=0, keepdims=True)
    d2 = jnp.sum(pos[N_EXPERTS:], axis=0, keepdims=True)
    sub = lax.broadcasted_iota(I32, (8, tr), 0)
    dest_ref[...] = jnp.where(sub == 0, d1, jnp.where(sub == 1, d2, 0.0)).astype(I32)
    carry[...] += jnp.sum(oh.astype(F32), axis=1, keepdims=True)


def _placement(routet, tr=512):
    T = routet.shape[1]
    E = N_EXPERTS
    rt = pl.BlockSpec((8, tr), lambda i: (0, i))
    col = pl.BlockSpec((2 * E, 1), lambda i: (0, 0))
    cnt = pl.pallas_call(
        functools.partial(_count_kernel, tr=tr), grid=(T // tr,), in_specs=[rt], out_specs=col,
        out_shape=jax.ShapeDtypeStruct((2 * E, 1), F32),
        compiler_params=_cparams("arbitrary"), name="moe_count")(routet)
    c1 = cnt[:E, 0].astype(I32)
    c2 = cnt[E:, 0].astype(I32)
    padded = ((c1 + c2 + FFN_BLOCK - 1) // FFN_BLOCK) * FFN_BLOCK
    pend = jnp.cumsum(padded)
    pstart = pend - padded
    base = jnp.concatenate([pstart, pstart + c1]).astype(F32).reshape(2 * E, 1)
    dest = pl.pallas_call(
        functools.partial(_place_kernel, tr=tr), grid=(T // tr,), in_specs=[rt, col], out_specs=rt,
        out_shape=jax.ShapeDtypeStruct((8, T), I32),
        scratch_shapes=[pltpu.VMEM((2 * E, 1), F32)],
        compiler_params=_cparams("arbitrary"), name="moe_place")(routet, base)
    n_blocks = (T * TOP_K + E * (FFN_BLOCK - 1) + FFN_BLOCK - 1) // FFN_BLOCK
    starts = jnp.arange(n_blocks, dtype=I32) * FFN_BLOCK
    blk_e = jnp.minimum(jnp.sum((pend[None, :] <= starts[:, None]).astype(I32), axis=1), E - 1)
    blk_first = jnp.concatenate([jnp.ones((1,), I32), (blk_e[1:] != blk_e[:-1]).astype(I32)])
    n_used = (pend[-1] // FFN_BLOCK).astype(I32).reshape(1)
    return dest[0], dest[1], blk_e, blk_first, n_used, n_blocks


def _dispatch_kernel(d1_ref, d2_ref, src_ref, init_ref, xs_ref, sem, *, tb):
    del init_ref

    def issue(t, _):
        row = src_ref.at[pl.ds(t, 1), :]
        pltpu.make_async_copy(row, xs_ref.at[pl.ds(d1_ref[t], 1), :], sem).start()
        pltpu.make_async_copy(row, xs_ref.at[pl.ds(d2_ref[t], 1), :], sem).start()
        return 0

    lax.fori_loop(0, tb, issue, 0, unroll=8)
    pltpu.make_async_copy(src_ref, xs_ref.at[pl.ds(0, tb), :], sem).wait()
    pltpu.make_async_copy(src_ref, xs_ref.at[pl.ds(0, tb), :], sem).wait()


def _dispatch(h1p, d1, d2, n_rows, tb=1024):
    T, W = h1p.shape
    idx = pl.BlockSpec((tb,), lambda i: (i,), memory_space=pltpu.SMEM)
    any_spec = pl.BlockSpec(memory_space=pl.ANY)
    return pl.pallas_call(
        functools.partial(_dispatch_kernel, tb=tb), grid=(T // tb,),
        in_specs=[idx, idx, pl.BlockSpec((tb, W), lambda i: (i, 0)), any_spec], out_specs=any_spec,
        out_shape=jax.ShapeDtypeStruct((n_rows, W), U32),
        scratch_shapes=[pltpu.SemaphoreType.DMA(())],
        input_output_aliases={3: 0},
        compiler_params=_cparams("arbitrary"), name="moe_dispatch",
    )(d1, d2, h1p, jnp.zeros((n_rows, W), U32))


def _ffn_kernel(blk_e_ref, blk_first_ref, n_used_ref, xs_ref, wgu_ref, wdn_ref, y_ref, wgu_s, wdn_s, *, ff):
    i = pl.program_id(0)

    @pl.when(blk_first_ref[i] == 1)
    def _():
        wgu_s[...] = wgu_ref[0, 0].astype(BF16)
        wdn_s[...] = wdn_ref[0, 0].astype(BF16)

    @pl.when(i < n_used_ref[0])
    def _():
        lo, hi = _unpack_halves(xs_ref[...])
        x = jnp.concatenate([lo, hi], axis=1).astype(BF16)
        gu = jnp.dot(x, wgu_s[...], preferred_element_type=F32)
        act = (_silu(gu[:, :ff]) * gu[:, ff:]).astype(BF16)
        y_ref[...] = _pack_halves(jnp.dot(act, wdn_s[...], preferred_element_type=F32))

    @pl.when(i >= n_used_ref[0])
    def _():
        y_ref[...] = jnp.zeros_like(y_ref)


def _ffn(xs, w_gu, w_dn, layer, blk_e, blk_first, n_used, n_blocks):
    _, E, D, FF2 = w_gu.shape
    ff = FF2 // 2
    W = xs.shape[1]
    grid_spec = pltpu.PrefetchScalarGridSpec(
        num_scalar_prefetch=3, grid=(n_blocks,),
        in_specs=[pl.BlockSpec((FFN_BLOCK, W), lambda i, be, bf, nu: (i, 0)),
                  pl.BlockSpec((1, 1, D, FF2), lambda i, be, bf, nu: (layer, be[i], 0, 0)),
                  pl.BlockSpec((1, 1, ff, D), lambda i, be, bf, nu: (layer, be[i], 0, 0))],
        out_specs=pl.BlockSpec((FFN_BLOCK, W), lambda i, be, bf, nu: (i, 0)),
        scratch_shapes=[pltpu.VMEM((D, FF2), BF16), pltpu.VMEM((ff, D), BF16)])
    return pl.pallas_call(
        functools.partial(_ffn_kernel, ff=ff), grid_spec=grid_spec,
        out_shape=jax.ShapeDtypeStruct(xs.shape, U32),
        compiler_params=_cparams("arbitrary"), name="moe_ffn",
    )(blk_e, blk_first, n_used, xs, w_gu, w_dn)


def _combine_kernel(d1_ref, d2_ref, y_ref, h1_ref, route_ref, g_ref, b_ref, h2_ref, h2b_ref, ybuf, sem,
                    *, tc, alpha):
    def issue(t, _):
        pltpu.make_async_copy(y_ref.at[pl.ds(d1_ref[t], 1), :], ybuf.at[0, pl.ds(t, 1), :], sem).start()
        pltpu.make_async_copy(y_ref.at[pl.ds(d2_ref[t], 1), :], ybuf.at[1, pl.ds(t, 1), :], sem).start()
        return 0

    lax.fori_loop(0, tc, issue, 0)
    pltpu.make_async_copy(y_ref.at[pl.ds(0, tc), :], ybuf.at[0], sem).wait()
    pltpu.make_async_copy(y_ref.at[pl.ds(0, tc), :], ybuf.at[1], sem).wait()

    route = route_ref[...]
    g1 = route[:, 2:3]
    g2 = route[:, 3:4]
    lo1, hi1 = _unpack_halves(ybuf[0])
    lo2, hi2 = _unpack_halves(ybuf[1])
    ffn = jnp.concatenate([g1 * lo1 + g2 * lo2, g1 * hi1 + g2 * hi2], axis=1)
    h2 = _layer_norm(alpha * h1_ref[...] + ffn, g_ref[...], b_ref[...])
    h2_ref[...] = h2
    h2b_ref[...] = h2.astype(BF16)


def _combine(y, d1, d2, h1, route, g, b, *, alpha, tc=256):
    T, D = h1.shape
    W = y.shape[1]
    idx = pl.BlockSpec((tc,), lambda i: (i,), memory_space=pltpu.SMEM)
    row = pl.BlockSpec((tc, D), lambda i: (i, 0))
    vec = pl.BlockSpec((1, D), lambda i: (0, 0))
    return pl.pallas_call(
        functools.partial(_combine_kernel, tc=tc, alpha=alpha), grid=(T // tc,),
        in_specs=[idx, idx, pl.BlockSpec(memory_space=pl.ANY), row,
                  pl.BlockSpec((tc, LANES), lambda i: (i, 0)), vec, vec],
        out_specs=[row, row],
        out_shape=[jax.ShapeDtypeStruct((T, D), F32), jax.ShapeDtypeStruct((T, D), BF16)],
        scratch_shapes=[pltpu.VMEM((2, tc, W), U32), pltpu.SemaphoreType.DMA(())],
        compiler_params=_cparams("arbitrary"), name="moe_combine",
    )(d1, d2, y, h1, route, g.reshape(1, D), b.reshape(1, D))


def kernel(x, emb_ln_g, emb_ln_b, w_in, short_conv_w, a_log, dt_bias, dn_norm_w, dw_conv_w, dw_conv_b,
           conv_ln_g, conv_ln_b, w_out, ln1_g, ln1_b, w_group, b_group, w_expert, b_expert, w_gate_up,
           w_down, ln2_g, ln2_b):
    B, L, D = x.shape
    T = B * L
    depth = w_in.shape[0]
    H = a_log.shape[2]
    dn_w = H * DN_HEAD_DIM
    cv_w = dw_conv_w.shape[2]
    n_ab = 4 * H
    alpha = (2 * depth) ** 0.25
    C = DN_CHUNK

    h, hb = _emb_ln(x.reshape(T, D), emb_ln_g, emb_ln_b)
    for l in range(depth):
        w = w_in[l]
        w_main = jnp.concatenate([w[:, :4 * dn_w], w[:, 4 * dn_w + n_ab:]], axis=1).astype(BF16)
        proj = _in_proj(hb, w_main).reshape(B, L, -1)
        gb = _decay_beta(hb, w[:, 4 * dn_w:4 * dn_w + n_ab], a_log[l], dt_bias[l])
        gt = gb[:, :n_ab].reshape(B, L // C, C, n_ab).transpose(0, 3, 1, 2)
        dn = _deltanet(proj, gt, short_conv_w[l], dn_norm_w[l], B=B, L=L, H=H)
        cv = _conformer(proj, dw_conv_w[l], dw_conv_b[l], conv_ln_g[l], conv_ln_b[l],
                        B=B, L=L, col0=4 * dn_w, width=cv_w)
        n_r = N_GROUPS + N_EXPERTS
        w_router = jnp.pad(jnp.concatenate([w_group[l], w_expert[l]], axis=1),
                           ((0, 0), (0, LANES - n_r))).astype(BF16)
        b_router = jnp.pad(jnp.concatenate([b_group[l], b_expert[l]]), (0, LANES - n_r)).reshape(1, LANES)
        h1, h1p, route, routet = _out_router(
            dn.reshape(T, dn_w), cv.reshape(T, cv_w), h, w_out[l].astype(BF16), ln1_g[l], ln1_b[l],
            w_router, b_router, alpha=alpha)
        d1, d2, blk_e, blk_first, n_used, n_blocks = _placement(routet)
        xs = _dispatch(h1p, d1, d2, n_blocks * FFN_BLOCK)
        y = _ffn(xs, w_gate_up, w_down, l, blk_e, blk_first, n_used, n_blocks)
        h, hb = _combine(y, d1, d2, h1, route, ln2_g[l], ln2_b[l], alpha=alpha)
    return h.reshape(B, L, D)
```

```python
import functools

import jax
import jax.numpy as jnp
from jax import lax
from jax.experimental import pallas as pl
from jax.experimental.pallas import tpu as pltpu

F32 = jnp.float32
BF16 = jnp.bfloat16
U32 = jnp.uint32
I32 = jnp.int32

LANES = 128
DN_HEAD_DIM = 128
DN_CHUNK = 128
PREP_UNROLL = 8
CV_GROUP = 128
N_GROUPS = 8
EXPERTS_PER_GROUP = 8
N_EXPERTS = N_GROUPS * EXPERTS_PER_GROUP
TOP_K = 2
FFN_BLOCK = 128
LN_EPS = 1e-5
RMS_EPS = 1e-6
VMEM_LIMIT = 56 * 1024 * 1024


def _cparams(*sem):
    return pltpu.CompilerParams(dimension_semantics=sem, vmem_limit_bytes=VMEM_LIMIT)


def _layer_norm(x, g, b):
    mu = jnp.mean(x, -1, keepdims=True)
    xc = x - mu
    var = jnp.mean(xc * xc, -1, keepdims=True)
    return xc * lax.rsqrt(var + LN_EPS) * g + b


def _sigmoid(x):
    return 1.0 / (1.0 + jnp.exp(-x))


def _silu(x):
    return x * _sigmoid(x)


def _pack_halves(y):
    n = y.shape[1] // 2
    bits = lax.bitcast_convert_type(y.astype(BF16).astype(F32), U32)
    return (bits[:, :n] >> 16) | bits[:, n:]


def _unpack_halves(u):
    lo = lax.bitcast_convert_type(u << 16, F32)
    hi = lax.bitcast_convert_type(u & jnp.uint32(0xFFFF0000), F32)
    return lo, hi


def _emb_ln_kernel(x_ref, g_ref, b_ref, h_ref, hb_ref):
    h = _layer_norm(x_ref[...], g_ref[...], b_ref[...])
    h_ref[...] = h
    hb_ref[...] = h.astype(BF16)


def _emb_ln(x, g, b, tm=512):
    T, D = x.shape
    row = pl.BlockSpec((tm, D), lambda i: (i, 0))
    vec = pl.BlockSpec((1, D), lambda i: (0, 0))
    return pl.pallas_call(
        _emb_ln_kernel, grid=(T // tm,), in_specs=[row, vec, vec], out_specs=[row, row],
        out_shape=[jax.ShapeDtypeStruct((T, D), F32), jax.ShapeDtypeStruct((T, D), BF16)],
        compiler_params=_cparams("parallel"), name="emb_ln")(x, g.reshape(1, D), b.reshape(1, D))


def _matmul_kernel(x_ref, w_ref, o_ref):
    o_ref[...] = jnp.dot(x_ref[...], w_ref[...], preferred_element_type=F32)


def _in_proj(hb, w, tm=1024, tn=1024):
    T, D = hb.shape
    N = w.shape[1]
    return pl.pallas_call(
        _matmul_kernel, grid=(T // tm, N // tn),
        in_specs=[pl.BlockSpec((tm, D), lambda i, j: (i, 0)), pl.BlockSpec((D, tn), lambda i, j: (0, j))],
        out_specs=pl.BlockSpec((tm, tn), lambda i, j: (i, j)),
        out_shape=jax.ShapeDtypeStruct((T, N), F32),
        compiler_params=_cparams("parallel", "arbitrary"), name="in_proj")(hb, w)


def _decay_beta_kernel(x_ref, w_ref, alog_ref, dtb_ref, o_ref, *, n_decay):
    ab = jnp.dot(x_ref[...], w_ref[...], preferred_element_type=F32)
    lane = lax.broadcasted_iota(I32, ab.shape, 1)
    s = ab + dtb_ref[...]
    softplus = jnp.maximum(s, 0.0) + jnp.log(1.0 + jnp.exp(-jnp.abs(s)))
    g = -jnp.exp(alog_ref[...]) * softplus
    o_ref[...] = jnp.where(lane < n_decay, g, _sigmoid(ab))


def _decay_beta(hb, w_ab, a_log, dt_bias, tm=1024):
    T, D = hb.shape
    n = a_log.size
    pad = lambda v: jnp.pad(v.reshape(1, n).astype(F32), ((0, 0), (0, LANES - n)))
    w = jnp.pad(w_ab, ((0, 0), (0, LANES - w_ab.shape[1]))).astype(BF16)
    vec = pl.BlockSpec((1, LANES), lambda i: (0, 0))
    return pl.pallas_call(
        functools.partial(_decay_beta_kernel, n_decay=n), grid=(T // tm,),
        in_specs=[pl.BlockSpec((tm, D), lambda i: (i, 0)), pl.BlockSpec((D, LANES), lambda i: (0, 0)), vec, vec],
        out_specs=pl.BlockSpec((tm, LANES), lambda i: (i, 0)),
        out_shape=jax.ShapeDtypeStruct((T, LANES), F32),
        compiler_params=_cparams("parallel"), name="decay_beta")(hb, w, pad(a_log), pad(dt_bias))


def _dot(a, b):
    return jnp.dot(a.astype(BF16), b.astype(BF16), preferred_element_type=F32)


def _dot_nt(a, b):
    return lax.dot_general(a.astype(BF16), b.astype(BF16), (((1,), (1,)), ((), ())),
                           preferred_element_type=F32)


def _dot_tn(a, b):
    return lax.dot_general(a.astype(BF16), b.astype(BF16), (((0,), (0,)), ((), ())),
                           preferred_element_type=F32)


def _deltanet_kernel(q_ref, k_ref, v_ref, z_ref, cq_ref, ck_ref, cv_ref, gt_ref, nw_ref, o_ref,
                     qs, ks, vs, ob, wq_s, at_s, kd_s, u_s, gl_s, s_s, rhs_s, *, L, C, H, R):
    DH = DN_HEAD_DIM
    h = pl.program_id(1)
    NC = L // C

    def conv_rows(i, _):
        r0 = pl.multiple_of(i * R, R)
        rows = pl.ds(r0, R)
        rid = lax.broadcasted_iota(I32, (R, DH), 0)
        has_prev = (r0 > 0).astype(F32)
        has_next = (r0 + R < L).astype(F32)

        def conv_silu(ref, w_ref):
            x = ref[0, rows, :]
            xm = ref[0, pl.ds(jnp.maximum(r0 - 1, 0), 1), :] * has_prev
            xp = ref[0, pl.ds(jnp.minimum(r0 + R, L - 1), 1), :] * has_next
            x_prev = jnp.where(rid == 0, xm, pltpu.roll(x, 1, 0))
            x_next = jnp.where(rid == R - 1, xp, pltpu.roll(x, R - 1, 0))
            return _silu(w_ref[0:1, :] * x_prev + w_ref[1:2, :] * x + w_ref[2:3, :] * x_next)

        q = conv_silu(q_ref, cq_ref)
        k = conv_silu(k_ref, ck_ref)
        qs[rows, :] = q * (lax.rsqrt(jnp.sum(q * q, -1, keepdims=True) + RMS_EPS) * (DH ** -0.5))
        ks[rows, :] = k * lax.rsqrt(jnp.sum(k * k, -1, keepdims=True) + RMS_EPS)
        vs[rows, :] = conv_silu(v_ref, cv_ref)
        return 0

    lax.fori_loop(0, L // R, conv_rows, 0)

    ri = lax.broadcasted_iota(I32, (C, C), 0)
    ci = lax.broadcasted_iota(I32, (C, C), 1)
    eye = ri == ci
    eye_f = eye.astype(F32)
    incl = (ri >= ci, ri <= ci)
    strict = (ri > ci, ri < ci)
    n_sq = C.bit_length() - 2
    zero_b = jnp.zeros((C, C), BF16)

    def block_diag(a, b):
        return jnp.concatenate([jnp.concatenate([a, zero_b], axis=1),
                                jnp.concatenate([zero_b, b], axis=1)], axis=0)

    def prep_start(c, j):
        rows = pl.ds(pl.multiple_of(c * C, C), C)
        q = qs[rows, :]
        k = ks[rows, :]
        v = vs[rows, :]
        kq = _dot_nt(jnp.concatenate([k, q], axis=0), k)
        kk, qk = kq[:C], kq[C:]
        a2 = []
        for d in range(2):
            g_row = gt_ref[0, d * H + h, pl.ds(c, 1), :]
            b_row = gt_ref[0, (2 + d) * H + h, pl.ds(c, 1), :]
            m, ms, mt = incl[d], strict[d], incl[1 - d]
            g_col = jnp.sum(jnp.where(eye, g_row, 0.0), axis=1, keepdims=True)
            b_col = jnp.sum(jnp.where(eye, b_row, 0.0), axis=1, keepdims=True)
            gc_col = jnp.sum(jnp.where(m, g_row, 0.0), axis=1, keepdims=True)
            gc_row = jnp.sum(jnp.where(mt, g_col, 0.0), axis=0, keepdims=True)
            tot = jnp.sum(g_row, axis=1, keepdims=True)
            decay = jnp.where(m, jnp.exp(jnp.where(m, gc_col - gc_row, 0.0)), 0.0)
            a2.append(jnp.where(ms, -(kk * b_col * decay), 0.0))
            egc = jnp.exp(gc_col)
            rhs_s[d, j] = jnp.concatenate([v * b_col, k * (b_col * egc)], axis=1).astype(BF16)
            wq_s[d, c, pl.ds(C, C), :] = (q * egc).astype(BF16)
            at_s[d, c] = (qk * decay).astype(BF16)
            kd_s[d, c] = (k * jnp.exp(tot - gc_col)).astype(BF16)
            gl_s[d, pl.ds(c, 1), :] = jnp.broadcast_to(jnp.exp(tot), (1, DH))
        return jnp.concatenate(a2, axis=1)

    def prep(i, _):
        cs = [i * PREP_UNROLL + j for j in range(PREP_UNROLL)]
        ps = [prep_start(c, j) for j, c in enumerate(cs)]
        eye2 = jnp.concatenate([eye_f, eye_f], axis=1)
        xs_ = [eye2 + p for p in ps]
        for _ in range(n_sq):
            pbs = [p.astype(BF16) for p in ps]
            ps = [jnp.dot(pb, block_diag(pb[:, :C], pb[:, C:]), preferred_element_type=F32) for pb in pbs]
            pbs = [p.astype(BF16) for p in ps]
            xs_ = [x + jnp.dot(x.astype(BF16), block_diag(pb[:, :C], pb[:, C:]), preferred_element_type=F32)
                   for x, pb in zip(xs_, pbs)]
        for j, (c, x) in enumerate(zip(cs, xs_)):
            for d in range(2):
                sol = jnp.dot(x[:, d * C:(d + 1) * C].astype(BF16), rhs_s[d, j], preferred_element_type=F32)
                u_s[d, c] = sol[:, :DH]
                wq_s[d, c, pl.ds(0, C), :] = sol[:, DH:].astype(BF16)
        return 0

    lax.fori_loop(0, NC // PREP_UNROLL, prep, 0)

    s_s[...] = jnp.zeros_like(s_s)

    def scan(n, _):
        cf, cb = n, NC - 1 - n
        s_f, s_b = s_s[0], s_s[1]
        wq = jnp.concatenate([wq_s[0, cf], wq_s[1, cb]], axis=1)
        r1 = jnp.dot(wq, block_diag(s_f.astype(BF16), s_b.astype(BF16)), preferred_element_type=F32)
        u = jnp.concatenate([u_s[0, cf], u_s[1, cb]], axis=1)
        v_new = (u - r1[:C]).astype(BF16)
        vd = block_diag(v_new[:, :DH], v_new[:, DH:])
        at = jnp.concatenate([at_s[0, cf], at_s[1, cb]], axis=1)
        o = r1[C:] + jnp.dot(at, vd, preferred_element_type=F32)
        kd = jnp.concatenate([kd_s[0, cf], kd_s[1, cb]], axis=0)
        ds = lax.dot_general(kd, vd, (((0,), (0,)), ((), ())), preferred_element_type=F32)
        s_s[0] = s_f * gl_s[0, pl.ds(cf, 1), :] + ds[:, :DH]
        s_s[1] = s_b * gl_s[1, pl.ds(cb, 1), :] + ds[:, DH:]
        o_ref[0, pl.ds(pl.multiple_of(cf * C, C), C), :] = o[:, :DH]
        ob[pl.ds(pl.multiple_of(cb * C, C), C), :] = o[:, DH:]
        return 0

    lax.fori_loop(0, NC, scan, 0)

    def gate_rows(i, _):
        rows = pl.ds(pl.multiple_of(i * R, R), R)
        o = o_ref[0, rows, :] + ob[rows, :]
        o = o * lax.rsqrt(jnp.mean(o * o, -1, keepdims=True) + RMS_EPS) * nw_ref[...]
        o_ref[0, rows, :] = o * _silu(z_ref[0, rows, :])
        return 0

    lax.fori_loop(0, L // R, gate_rows, 0)


def _deltanet(proj, gt, conv_w, norm_w, *, B, L, H):
    DH, C, R = DN_HEAD_DIM, DN_CHUNK, 256
    assert C == DH and L % (C * PREP_UNROLL) == 0
    NC = L // C
    col = lambda off: pl.BlockSpec((1, L, DH), lambda b, h: (b, 0, off + h))
    cw = lambda off: pl.BlockSpec((3, DH), lambda b, h: (0, off + h))
    kern = functools.partial(_deltanet_kernel, L=L, C=C, H=H, R=R)
    return pl.pallas_call(
        kern, grid=(B, H),
        in_specs=[col(0), col(H), col(2 * H), col(3 * H), cw(0), cw(H), cw(2 * H),
                  pl.BlockSpec((1, 4 * H, NC, C), lambda b, h: (b, 0, 0, 0)),
                  pl.BlockSpec((1, DH), lambda b, h: (0, 0))],
        out_specs=pl.BlockSpec((1, L, DH), lambda b, h: (b, 0, h)),
        out_shape=jax.ShapeDtypeStruct((B, L, H * DH), F32),
        scratch_shapes=[pltpu.VMEM((L, DH), F32), pltpu.VMEM((L, DH), F32), pltpu.VMEM((L, DH), F32),
                        pltpu.VMEM((L, DH), F32),
                        pltpu.VMEM((2, NC, 2 * C, DH), BF16), pltpu.VMEM((2, NC, C, C), BF16),
                        pltpu.VMEM((2, NC, C, DH), BF16), pltpu.VMEM((2, NC, C, DH), F32),
                        pltpu.VMEM((2, NC, DH), F32), pltpu.VMEM((2, DH, DH), F32),
                        pltpu.VMEM((2, PREP_UNROLL, C, 2 * DH), BF16)],
        compiler_params=_cparams("parallel", "arbitrary"), name="deltanet",
    )(proj, proj, proj, proj, conv_w, conv_w, conv_w, gt, norm_w.reshape(1, DH))


def _conformer_kernel(val_ref, gate_ref, w_ref, b_ref, g_ref, beta_ref, o_ref, ypad, *, L, K, R, HALO):
    pad = K // 2
    zeros = jnp.zeros((HALO, CV_GROUP), F32)
    ypad[pl.ds(0, HALO), :] = zeros
    ypad[pl.ds(HALO + L, HALO), :] = zeros

    def glu_rows(i, _):
        r0 = pl.multiple_of(i * R, R)
        ypad[pl.ds(HALO + r0, R), :] = val_ref[0, pl.ds(r0, R), :] * _sigmoid(gate_ref[0, pl.ds(r0, R), :])
        return 0

    lax.fori_loop(0, L // R, glu_rows, 0)

    def conv_rows(i, _):
        r0 = pl.multiple_of(i * R, R)
        acc = jnp.zeros((R, CV_GROUP), F32)
        for t in range(K):
            acc = acc + w_ref[t:t + 1, :] * ypad[pl.ds(r0 + (HALO - pad + t), R), :]
        y = _layer_norm(acc + b_ref[...], g_ref[...], beta_ref[...])
        o_ref[0, pl.ds(r0, R), :] = _silu(y)
        return 0

    lax.fori_loop(0, L // R, conv_rows, 0)


def _conformer(proj, w, b, g, beta, *, B, L, col0, width):
    K = w.shape[0]
    n = width // CV_GROUP
    HALO, R = 16, 256
    c0 = col0 // CV_GROUP
    vec = pl.BlockSpec((1, CV_GROUP), lambda bb, j: (0, j))
    kern = functools.partial(_conformer_kernel, L=L, K=K, R=R, HALO=HALO)
    return pl.pallas_call(
        kern, grid=(B, n),
        in_specs=[pl.BlockSpec((1, L, CV_GROUP), lambda bb, j: (bb, 0, c0 + j)),
                  pl.BlockSpec((1, L, CV_GROUP), lambda bb, j: (bb, 0, c0 + n + j)),
                  pl.BlockSpec((K, CV_GROUP), lambda bb, j: (0, j)), vec, vec, vec],
        out_specs=pl.BlockSpec((1, L, CV_GROUP), lambda bb, j: (bb, 0, j)),
        out_shape=jax.ShapeDtypeStruct((B, L, width), F32),
        scratch_shapes=[pltpu.VMEM((L + 2 * HALO, CV_GROUP), F32)],
        compiler_params=_cparams("parallel", "parallel"), name="conformer",
    )(proj, proj, w, b.reshape(1, width), g.reshape(1, width), beta.reshape(1, width))


def _out_router_kernel(dn_ref, cv_ref, h_ref, w_ref, g_ref, b_ref, wr_ref, br_ref,
                       h1_ref, h1p_ref, route_ref, routet_ref, *, alpha, n_dn, sub):
    n_sub = h_ref.shape[0] // sub

    def matmul(s):
        rows = pl.ds(s * sub, sub)
        mix = jnp.dot(dn_ref[rows, :].astype(BF16), w_ref[pl.ds(0, n_dn), :], preferred_element_type=F32)
        return mix + jnp.dot(cv_ref[rows, :].astype(BF16), w_ref[pl.ds(n_dn, w_ref.shape[0] - n_dn), :],
                             preferred_element_type=F32)

    def finish(s, mix):
        rows = pl.ds(s * sub, sub)
        h1 = _layer_norm(alpha * h_ref[rows, :] + mix, g_ref[...], b_ref[...])
        h1_ref[rows, :] = h1
        h1p_ref[rows, :] = _pack_halves(h1)
        logits = jnp.dot(h1.astype(BF16), wr_ref[...], preferred_element_type=F32) + br_ref[...]
        route = _route(logits)
        route_ref[rows, :] = route
        routet_ref[:, rows] = route.T[:8, :]

    mix = matmul(0)
    for s in range(1, n_sub):
        nxt = matmul(s)
        finish(s - 1, mix)
        mix = nxt
    finish(n_sub - 1, mix)


def _route(logits):
    lane = lax.broadcasted_iota(I32, logits.shape, 1)
    neg = jnp.float32(-jnp.inf)
    big = jnp.int32(LANES)
    gl = jnp.where(lane < N_GROUPS, logits, neg)
    gmax = jnp.max(gl, -1, keepdims=True)
    gsel = jnp.min(jnp.where(gl == gmax, lane, big), -1, keepdims=True)
    pg = 1.0 / jnp.sum(jnp.exp(gl - gmax), -1, keepdims=True)
    lo = N_GROUPS + gsel * EXPERTS_PER_GROUP
    el = jnp.where((lane >= lo) & (lane < lo + EXPERTS_PER_GROUP), logits, neg)
    e1 = jnp.max(el, -1, keepdims=True)
    i1 = jnp.min(jnp.where(el == e1, lane, big), -1, keepdims=True)
    el2 = jnp.where(lane == i1, neg, el)
    e2 = jnp.max(el2, -1, keepdims=True)
    i2 = jnp.min(jnp.where(el2 == e2, lane, big), -1, keepdims=True)
    r = jnp.exp(e2 - e1)
    p1 = 1.0 / (1.0 + r)
    p2 = r * p1
    route = jnp.where(lane == 0, (i1 - N_GROUPS).astype(F32),
                      jnp.where(lane == 1, (i2 - N_GROUPS).astype(F32),
                                jnp.where(lane == 2, pg * p1, jnp.where(lane == 3, pg * p2, 0.0))))
    return route


def _out_router(dn, cv, h, w_out, g, b, w_router, b_router, *, alpha, tm=256, sub=128):
    T, D = h.shape
    n_dn, n_cv = dn.shape[1], cv.shape[1]
    vec = pl.BlockSpec((1, D), lambda i: (0, 0))
    row = pl.BlockSpec((tm, D), lambda i: (i, 0))
    kern = functools.partial(_out_router_kernel, alpha=alpha, n_dn=n_dn, sub=sub)
    return pl.pallas_call(
        kern, grid=(T // tm,),
        in_specs=[pl.BlockSpec((tm, n_dn), lambda i: (i, 0)), pl.BlockSpec((tm, n_cv), lambda i: (i, 0)), row,
                  pl.BlockSpec((n_dn + n_cv, D), lambda i: (0, 0)), vec, vec,
                  pl.BlockSpec((D, LANES), lambda i: (0, 0)), pl.BlockSpec((1, LANES), lambda i: (0, 0))],
        out_specs=[row, pl.BlockSpec((tm, D // 2), lambda i: (i, 0)),
                   pl.BlockSpec((tm, LANES), lambda i: (i, 0)), pl.BlockSpec((8, tm), lambda i: (0, i))],
        out_shape=[jax.ShapeDtypeStruct((T, D), F32), jax.ShapeDtypeStruct((T, D // 2), U32),
                   jax.ShapeDtypeStruct((T, LANES), F32), jax.ShapeDtypeStruct((8, T), F32)],
        compiler_params=_cparams("parallel"), name="out_router",
    )(dn, cv, h, w_out, g.reshape(1, D), b.reshape(1, D), w_router, b_router)


def _onehot_t(routet_ref, tr):
    sub = lax.broadcasted_iota(I32, (2 * N_EXPERTS, tr), 0)
    e1 = routet_ref[0:1, :].astype(I32)
    e2 = routet_ref[1:2, :].astype(I32)
    return sub == jnp.where(sub < N_EXPERTS, e1, e2 + N_EXPERTS)


def _count_kernel(routet_ref, cnt_ref, *, tr):
    @pl.when(pl.program_id(0) == 0)
    def _():
        cnt_ref[...] = jnp.zeros_like(cnt_ref)
    oh = _onehot_t(routet_ref, tr).astype(F32)
    cnt_ref[...] += jnp.sum(oh, axis=1, keepdims=True)


def _place_kernel(routet_ref, base_ref, dest_ref, carry, *, tr):
    @pl.when(pl.program_id(0) == 0)
    def _():
        carry[...] = jnp.zeros_like(carry)
    oh = _onehot_t(routet_ref, tr)
    ohb = oh.astype(F32).astype(BF16)
    ri = lax.broadcasted_iota(I32, (tr, tr), 0)
    ci = lax.broadcasted_iota(I32, (tr, tr), 1)
    before = (ri < ci).astype(F32).astype(BF16)
    excl = jnp.dot(ohb, before, preferred_element_type=F32)
    pos = jnp.where(oh, excl + carry[...] + base_ref[...], 0.0)
    d1 = jnp.sum(pos[:N_EXPERTS], axis=0, keepdims=True)
    d2 = jnp.sum(pos[N_EXPERTS:], axis=0, keepdims=True)
    sub = lax.broadcasted_iota(I32, (8, tr), 0)
    dest_ref[...] = jnp.where(sub == 0, d1, jnp.where(sub == 1, d2, 0.0)).astype(I32)
    carry[...] += jnp.sum(oh.astype(F32), axis=1, keepdims=True)


def _placement(routet, tr=512):
    T = routet.shape[1]
    E = N_EXPERTS
    rt = pl.BlockSpec((8, tr), lambda i: (0, i))
    col = pl.BlockSpec((2 * E, 1), lambda i: (0, 0))
    cnt = pl.pallas_call(
        functools.partial(_count_kernel, tr=tr), grid=(T // tr,), in_specs=[rt], out_specs=col,
        out_shape=jax.ShapeDtypeStruct((2 * E, 1), F32),
        compiler_params=_cparams("arbitrary"), name="moe_count")(routet)
    c1 = cnt[:E, 0].astype(I32)
    c2 = cnt[E:, 0].astype(I32)
    padded = ((c1 + c2 + FFN_BLOCK - 1) // FFN_BLOCK) * FFN_BLOCK
    pend = jnp.cumsum(padded)
    pstart = pend - padded
    base = jnp.concatenate([pstart, pstart + c1]).astype(F32).reshape(2 * E, 1)
    dest = pl.pallas_call(
        functools.partial(_place_kernel, tr=tr), grid=(T // tr,), in_specs=[rt, col], out_specs=rt,
        out_shape=jax.ShapeDtypeStruct((8, T), I32),
        scratch_shapes=[pltpu.VMEM((2 * E, 1), F32)],
        compiler_params=_cparams("arbitrary"), name="moe_place")(routet, base)
    n_blocks = (T * TOP_K + E * (FFN_BLOCK - 1) + FFN_BLOCK - 1) // FFN_BLOCK
    starts = jnp.arange(n_blocks, dtype=I32) * FFN_BLOCK
    blk_e = jnp.minimum(jnp.sum((pend[None, :] <= starts[:, None]).astype(I32), axis=1), E - 1)
    blk_first = jnp.concatenate([jnp.ones((1,), I32), (blk_e[1:] != blk_e[:-1]).astype(I32)])
    n_used = (pend[-1] // FFN_BLOCK).astype(I32).reshape(1)
    ids = jnp.arange(E, dtype=I32)
    later = (padded > 0)[None, :] & (ids[None, :] > ids[:, None])
    next_tab = jnp.min(jnp.where(later, ids[None, :], E), axis=1)
    blk_next = jnp.where(next_tab == E, -1, next_tab)[blk_e]
    return dest[0], dest[1], blk_e, blk_first, blk_next, n_used, n_blocks


def _dispatch_kernel(d1_ref, d2_ref, src_ref, init_ref, xs_ref, sem, *, tb):
    del init_ref

    def issue(t, _):
        row = src_ref.at[pl.ds(t, 1), :]
        pltpu.make_async_copy(row, xs_ref.at[pl.ds(d1_ref[t], 1), :], sem).start()
        pltpu.make_async_copy(row, xs_ref.at[pl.ds(d2_ref[t], 1), :], sem).start()
        return 0

    lax.fori_loop(0, tb, issue, 0, unroll=8)
    pltpu.make_async_copy(src_ref, xs_ref.at[pl.ds(0, tb), :], sem).wait()
    pltpu.make_async_copy(src_ref, xs_ref.at[pl.ds(0, tb), :], sem).wait()


def _dispatch(h1p, d1, d2, n_rows, tb=1024):
    T, W = h1p.shape
    idx = pl.BlockSpec((tb,), lambda i: (i,), memory_space=pltpu.SMEM)
    any_spec = pl.BlockSpec(memory_space=pl.ANY)
    return pl.pallas_call(
        functools.partial(_dispatch_kernel, tb=tb), grid=(T // tb,),
        in_specs=[idx, idx, pl.BlockSpec((tb, W), lambda i: (i, 0)), any_spec], out_specs=any_spec,
        out_shape=jax.ShapeDtypeStruct((n_rows, W), U32),
        scratch_shapes=[pltpu.SemaphoreType.DMA(())],
        input_output_aliases={3: 0},
        compiler_params=_cparams("arbitrary"), name="moe_dispatch",
    )(d1, d2, h1p, jnp.zeros((n_rows, W), U32))


def _ffn_kernel(blk_e_ref, blk_first_ref, blk_next_ref, n_used_ref, xs_ref, wgu_hbm, wdn_hbm, y_ref,
                wgu_f, wdn_f, wgu_s, wdn_s, sem, *, ff, layer):
    i = pl.program_id(0)
    used = i < n_used_ref[0]

    def weight_copies(e):
        return (pltpu.make_async_copy(wgu_hbm.at[layer, e], wgu_f, sem.at[0]),
                pltpu.make_async_copy(wdn_hbm.at[layer, e], wdn_f, sem.at[1]))

    @pl.when(i == 0)
    def _():
        for c in weight_copies(blk_e_ref[0]):
            c.start()

    @pl.when(used & (blk_first_ref[i] == 1))
    def _():
        for c in weight_copies(blk_e_ref[i]):
            c.wait()
        wgu_s[...] = wgu_f[...].astype(BF16)
        wdn_s[...] = wdn_f[...].astype(BF16)

        @pl.when(blk_next_ref[i] >= 0)
        def _():
            for c in weight_copies(blk_next_ref[i]):
                c.start()

    @pl.when(used)
    def _():
        lo, hi = _unpack_halves(xs_ref[...])
        x = jnp.concatenate([lo, hi], axis=1).astype(BF16)
        gu = jnp.dot(x, wgu_s[...], preferred_element_type=F32)
        act = (_silu(gu[:, :ff]) * gu[:, ff:]).astype(BF16)
        y_ref[...] = _pack_halves(jnp.dot(act, wdn_s[...], preferred_element_type=F32))

    @pl.when(i >= n_used_ref[0])
    def _():
        y_ref[...] = jnp.zeros_like(y_ref)


def _ffn(xs, w_gu, w_dn, layer, blk_e, blk_first, blk_next, n_used, n_blocks):
    _, E, D, FF2 = w_gu.shape
    ff = FF2 // 2
    W = xs.shape[1]
    any_spec = pl.BlockSpec(memory_space=pl.ANY)
    grid_spec = pltpu.PrefetchScalarGridSpec(
        num_scalar_prefetch=4, grid=(n_blocks,),
        in_specs=[pl.BlockSpec((FFN_BLOCK, W), lambda i, *_: (i, 0)), any_spec, any_spec],
        out_specs=pl.BlockSpec((FFN_BLOCK, W), lambda i, *_: (i, 0)),
        scratch_shapes=[pltpu.VMEM((D, FF2), F32), pltpu.VMEM((ff, D), F32),
                        pltpu.VMEM((D, FF2), BF16), pltpu.VMEM((ff, D), BF16),
                        pltpu.SemaphoreType.DMA((2,))])
    return pl.pallas_call(
        functools.partial(_ffn_kernel, ff=ff, layer=layer), grid_spec=grid_spec,
        out_shape=jax.ShapeDtypeStruct(xs.shape, U32),
        compiler_params=_cparams("arbitrary"), name="moe_ffn",
    )(blk_e, blk_first, blk_next, n_used, xs, w_gu, w_dn)


def _combine_kernel(d1_ref, d2_ref, d1n_ref, d2n_ref, y_ref, h1_ref, route_ref, g_ref, b_ref, h2_ref, h2b_ref,
                    ybuf, sem, *, tc, alpha):
    i = pl.program_id(0)
    slot = i % 2

    def gather(r1, r2, s):
        def issue(t, _):
            pltpu.make_async_copy(y_ref.at[pl.ds(r1[t], 1), :], ybuf.at[s, 0, pl.ds(t, 1), :], sem.at[s]).start()
            pltpu.make_async_copy(y_ref.at[pl.ds(r2[t], 1), :], ybuf.at[s, 1, pl.ds(t, 1), :], sem.at[s]).start()
            return 0
        lax.fori_loop(0, tc, issue, 0, unroll=8)

    @pl.when(i == 0)
    def _():
        gather(d1_ref, d2_ref, 0)

    @pl.when(i + 1 < pl.num_programs(0))
    def _():
        gather(d1n_ref, d2n_ref, 1 - slot)

    for j in range(TOP_K):
        pltpu.make_async_copy(y_ref.at[pl.ds(0, tc), :], ybuf.at[slot, j], sem.at[slot]).wait()

    route = route_ref[...]
    g1 = route[:, 2:3]
    g2 = route[:, 3:4]
    lo1, hi1 = _unpack_halves(ybuf[slot, 0])
    lo2, hi2 = _unpack_halves(ybuf[slot, 1])
    ffn = jnp.concatenate([g1 * lo1 + g2 * lo2, g1 * hi1 + g2 * hi2], axis=1)
    h2 = _layer_norm(alpha * h1_ref[...] + ffn, g_ref[...], b_ref[...])
    h2_ref[...] = h2
    h2b_ref[...] = h2.astype(BF16)


def _combine(y, d1, d2, h1, route, g, b, *, alpha, tc=256):
    T, D = h1.shape
    W = y.shape[1]
    n = T // tc
    idx = pl.BlockSpec((tc,), lambda i: (i,), memory_space=pltpu.SMEM)
    idx_next = pl.BlockSpec((tc,), lambda i: (jnp.minimum(i + 1, n - 1),), memory_space=pltpu.SMEM)
    row = pl.BlockSpec((tc, D), lambda i: (i, 0))
    vec = pl.BlockSpec((1, D), lambda i: (0, 0))
    return pl.pallas_call(
        functools.partial(_combine_kernel, tc=tc, alpha=alpha), grid=(n,),
        in_specs=[idx, idx, idx_next, idx_next, pl.BlockSpec(memory_space=pl.ANY), row,
                  pl.BlockSpec((tc, LANES), lambda i: (i, 0)), vec, vec],
        out_specs=[row, row],
        out_shape=[jax.ShapeDtypeStruct((T, D), F32), jax.ShapeDtypeStruct((T, D), BF16)],
        scratch_shapes=[pltpu.VMEM((2, TOP_K, tc, W), U32), pltpu.SemaphoreType.DMA((2,))],
        compiler_params=_cparams("arbitrary"), name="moe_combine",
    )(d1, d2, d1, d2, y, h1, route, g.reshape(1, D), b.reshape(1, D))


def kernel(x, emb_ln_g, emb_ln_b, w_in, short_conv_w, a_log, dt_bias, dn_norm_w, dw_conv_w, dw_conv_b,
           conv_ln_g, conv_ln_b, w_out, ln1_g, ln1_b, w_group, b_group, w_expert, b_expert, w_gate_up,
           w_down, ln2_g, ln2_b):
    B, L, D = x.shape
    T = B * L
    depth = w_in.shape[0]
    H = a_log.shape[2]
    dn_w = H * DN_HEAD_DIM
    cv_w = dw_conv_w.shape[2]
    n_ab = 4 * H
    alpha = (2 * depth) ** 0.25
    C = DN_CHUNK

    h, hb = _emb_ln(x.reshape(T, D), emb_ln_g, emb_ln_b)
    for l in range(depth):
        w = w_in[l]
        w_main = jnp.concatenate([w[:, :4 * dn_w], w[:, 4 * dn_w + n_ab:]], axis=1).astype(BF16)
        proj = _in_proj(hb, w_main).reshape(B, L, -1)
        gb = _decay_beta(hb, w[:, 4 * dn_w:4 * dn_w + n_ab], a_log[l], dt_bias[l])
        gt = gb[:, :n_ab].reshape(B, L // C, C, n_ab).transpose(0, 3, 1, 2)
        dn = _deltanet(proj, gt, short_conv_w[l], dn_norm_w[l], B=B, L=L, H=H)
        cv = _conformer(proj, dw_conv_w[l], dw_conv_b[l], conv_ln_g[l], conv_ln_b[l],
                        B=B, L=L, col0=4 * dn_w, width=cv_w)
        n_r = N_GROUPS + N_EXPERTS
        w_router = jnp.pad(jnp.concatenate([w_group[l], w_expert[l]], axis=1),
                           ((0, 0), (0, LANES - n_r))).astype(BF16)
        b_router = jnp.pad(jnp.concatenate([b_group[l], b_expert[l]]), (0, LANES - n_r)).reshape(1, LANES)
        h1, h1p, route, routet = _out_router(
            dn.reshape(T, dn_w), cv.reshape(T, cv_w), h, w_out[l].astype(BF16), ln1_g[l], ln1_b[l],
            w_router, b_router, alpha=alpha)
        d1, d2, blk_e, blk_first, blk_next, n_used, n_blocks = _placement(routet)
        xs = _dispatch(h1p, d1, d2, n_blocks * FFN_BLOCK)
        y = _ffn(xs, w_gate_up, w_down, l, blk_e, blk_first, blk_next, n_used, n_blocks)
        h, hb = _combine(y, d1, d2, h1, route, ln2_g[l], ln2_b[l], alpha=alpha)
    return h.reshape(B, L, D)
```

```python
import functools

import jax
import jax.numpy as jnp
from jax import lax
from jax.experimental import pallas as pl
from jax.experimental.pallas import tpu as pltpu

F32 = jnp.float32
BF16 = jnp.bfloat16
U32 = jnp.uint32
I32 = jnp.int32

LANES = 128
DN_HEAD_DIM = 128
DN_CHUNK = 128
PREP_UNROLL = 8
CV_GROUP = 128
N_GROUPS = 8
EXPERTS_PER_GROUP = 8
N_EXPERTS = N_GROUPS * EXPERTS_PER_GROUP
TOP_K = 2
FFN_BLOCK = 128
FFN_WEIGHT_CHUNKS = 4
LN_EPS = 1e-5
RMS_EPS = 1e-6
VMEM_LIMIT = 56 * 1024 * 1024


def _cparams(*sem):
    return pltpu.CompilerParams(dimension_semantics=sem, vmem_limit_bytes=VMEM_LIMIT)


def _layer_norm(x, g, b):
    mu = jnp.mean(x, -1, keepdims=True)
    xc = x - mu
    var = jnp.mean(xc * xc, -1, keepdims=True)
    return xc * lax.rsqrt(var + LN_EPS) * g + b


def _sigmoid(x):
    return 1.0 / (1.0 + jnp.exp(-x))


def _silu(x):
    return x * _sigmoid(x)


def _pack_halves(y):
    n = y.shape[1] // 2
    bits = lax.bitcast_convert_type(y.astype(BF16).astype(F32), U32)
    return (bits[:, :n] >> 16) | bits[:, n:]


def _unpack_halves(u):
    lo = lax.bitcast_convert_type(u << 16, F32)
    hi = lax.bitcast_convert_type(u & jnp.uint32(0xFFFF0000), F32)
    return lo, hi


def _emb_ln_kernel(x_ref, g_ref, b_ref, h_ref, hb_ref):
    h = _layer_norm(x_ref[...], g_ref[...], b_ref[...])
    h_ref[...] = h
    hb_ref[...] = h.astype(BF16)


def _emb_ln(x, g, b, tm=512):
    T, D = x.shape
    row = pl.BlockSpec((tm, D), lambda i: (i, 0))
    vec = pl.BlockSpec((1, D), lambda i: (0, 0))
    return pl.pallas_call(
        _emb_ln_kernel, grid=(T // tm,), in_specs=[row, vec, vec], out_specs=[row, row],
        out_shape=[jax.ShapeDtypeStruct((T, D), F32), jax.ShapeDtypeStruct((T, D), BF16)],
        compiler_params=_cparams("parallel"), name="emb_ln")(x, g.reshape(1, D), b.reshape(1, D))


def _matmul_kernel(x_ref, w_ref, o_ref):
    o_ref[...] = jnp.dot(x_ref[...], w_ref[...], preferred_element_type=F32)


def _in_proj(hb, w, tm=1024, tn=1024):
    T, D = hb.shape
    N = w.shape[1]
    return pl.pallas_call(
        _matmul_kernel, grid=(T // tm, N // tn),
        in_specs=[pl.BlockSpec((tm, D), lambda i, j: (i, 0)), pl.BlockSpec((D, tn), lambda i, j: (0, j))],
        out_specs=pl.BlockSpec((tm, tn), lambda i, j: (i, j)),
        out_shape=jax.ShapeDtypeStruct((T, N), F32),
        compiler_params=_cparams("parallel", "arbitrary"), name="in_proj")(hb, w)


def _decay_beta_kernel(x_ref, w_ref, alog_ref, dtb_ref, o_ref, *, n_decay):
    ab = jnp.dot(x_ref[...], w_ref[...], preferred_element_type=F32)
    lane = lax.broadcasted_iota(I32, ab.shape, 1)
    s = ab + dtb_ref[...]
    softplus = jnp.maximum(s, 0.0) + jnp.log(1.0 + jnp.exp(-jnp.abs(s)))
    g = -jnp.exp(alog_ref[...]) * softplus
    o_ref[...] = jnp.where(lane < n_decay, g, _sigmoid(ab))


def _decay_beta(hb, w_ab, a_log, dt_bias, tm=1024):
    T, D = hb.shape
    n = a_log.size
    pad = lambda v: jnp.pad(v.reshape(1, n).astype(F32), ((0, 0), (0, LANES - n)))
    w = jnp.pad(w_ab, ((0, 0), (0, LANES - w_ab.shape[1]))).astype(BF16)
    vec = pl.BlockSpec((1, LANES), lambda i: (0, 0))
    return pl.pallas_call(
        functools.partial(_decay_beta_kernel, n_decay=n), grid=(T // tm,),
        in_specs=[pl.BlockSpec((tm, D), lambda i: (i, 0)), pl.BlockSpec((D, LANES), lambda i: (0, 0)), vec, vec],
        out_specs=pl.BlockSpec((tm, LANES), lambda i: (i, 0)),
        out_shape=jax.ShapeDtypeStruct((T, LANES), F32),
        compiler_params=_cparams("parallel"), name="decay_beta")(hb, w, pad(a_log), pad(dt_bias))


def _dot(a, b):
    return jnp.dot(a.astype(BF16), b.astype(BF16), preferred_element_type=F32)


def _dot_nt(a, b):
    return lax.dot_general(a.astype(BF16), b.astype(BF16), (((1,), (1,)), ((), ())),
                           preferred_element_type=F32)


def _dot_tn(a, b):
    return lax.dot_general(a.astype(BF16), b.astype(BF16), (((0,), (0,)), ((), ())),
                           preferred_element_type=F32)


def _deltanet_kernel(q_ref, k_ref, v_ref, z_ref, cq_ref, ck_ref, cv_ref, gt_ref, nw_ref, o_ref,
                     qs, ks, vs, ob, wq_s, at_s, kd_s, u_s, gl_s, s_s, rhs_s, *, L, C, H, R):
    DH = DN_HEAD_DIM
    h = pl.program_id(1)
    NC = L // C

    def conv_rows(i, _):
        r0 = pl.multiple_of(i * R, R)
        rows = pl.ds(r0, R)
        rid = lax.broadcasted_iota(I32, (R, DH), 0)
        has_prev = (r0 > 0).astype(F32)
        has_next = (r0 + R < L).astype(F32)

        def conv_silu(ref, w_ref):
            x = ref[0, rows, :]
            xm = ref[0, pl.ds(jnp.maximum(r0 - 1, 0), 1), :] * has_prev
            xp = ref[0, pl.ds(jnp.minimum(r0 + R, L - 1), 1), :] * has_next
            x_prev = jnp.where(rid == 0, xm, pltpu.roll(x, 1, 0))
            x_next = jnp.where(rid == R - 1, xp, pltpu.roll(x, R - 1, 0))
            return _silu(w_ref[0:1, :] * x_prev + w_ref[1:2, :] * x + w_ref[2:3, :] * x_next)

        q = conv_silu(q_ref, cq_ref)
        k = conv_silu(k_ref, ck_ref)
        qs[rows, :] = q * (lax.rsqrt(jnp.sum(q * q, -1, keepdims=True) + RMS_EPS) * (DH ** -0.5))
        ks[rows, :] = k * lax.rsqrt(jnp.sum(k * k, -1, keepdims=True) + RMS_EPS)
        vs[rows, :] = conv_silu(v_ref, cv_ref)
        return 0

    lax.fori_loop(0, L // R, conv_rows, 0)

    ri = lax.broadcasted_iota(I32, (C, C), 0)
    ci = lax.broadcasted_iota(I32, (C, C), 1)
    eye = ri == ci
    eye_f = eye.astype(F32)
    incl = (ri >= ci, ri <= ci)
    strict = (ri > ci, ri < ci)
    n_sq = C.bit_length() - 2
    zero_b = jnp.zeros((C, C), BF16)

    def block_diag(a, b):
        return jnp.concatenate([jnp.concatenate([a, zero_b], axis=1),
                                jnp.concatenate([zero_b, b], axis=1)], axis=0)

    def prep_start(c, j):
        rows = pl.ds(pl.multiple_of(c * C, C), C)
        q = qs[rows, :]
        k = ks[rows, :]
        v = vs[rows, :]
        kq = _dot_nt(jnp.concatenate([k, q], axis=0), k)
        kk, qk = kq[:C], kq[C:]
        a2 = []
        for d in range(2):
            g_row = gt_ref[0, d * H + h, pl.ds(c, 1), :]
            b_row = gt_ref[0, (2 + d) * H + h, pl.ds(c, 1), :]
            m, ms, mt = incl[d], strict[d], incl[1 - d]
            g_col = jnp.sum(jnp.where(eye, g_row, 0.0), axis=1, keepdims=True)
            b_col = jnp.sum(jnp.where(eye, b_row, 0.0), axis=1, keepdims=True)
            gc_col = jnp.sum(jnp.where(m, g_row, 0.0), axis=1, keepdims=True)
            gc_row = jnp.sum(jnp.where(mt, g_col, 0.0), axis=0, keepdims=True)
            tot = jnp.sum(g_row, axis=1, keepdims=True)
            decay = jnp.where(m, jnp.exp(jnp.where(m, gc_col - gc_row, 0.0)), 0.0)
            a2.append(jnp.where(ms, -(kk * b_col * decay), 0.0))
            egc = jnp.exp(gc_col)
            rhs_s[d, j] = jnp.concatenate([v * b_col, k * (b_col * egc)], axis=1).astype(BF16)
            wq_s[d, c, pl.ds(C, C), :] = (q * egc).astype(BF16)
            at_s[d, c] = (qk * decay).astype(BF16)
            kd_s[d, c] = (k * jnp.exp(tot - gc_col)).astype(BF16)
            gl_s[d, pl.ds(c, 1), :] = jnp.broadcast_to(jnp.exp(tot), (1, DH))
        return jnp.concatenate(a2, axis=1)

    def prep(i, _):
        cs = [i * PREP_UNROLL + j for j in range(PREP_UNROLL)]
        ps = [prep_start(c, j) for j, c in enumerate(cs)]
        eye2 = jnp.concatenate([eye_f, eye_f], axis=1)
        xs_ = [eye2 + p for p in ps]
        for _ in range(n_sq):
            pbs = [p.astype(BF16) for p in ps]
            ps = [jnp.dot(pb, block_diag(pb[:, :C], pb[:, C:]), preferred_element_type=F32) for pb in pbs]
            pbs = [p.astype(BF16) for p in ps]
            xs_ = [x + jnp.dot(x.astype(BF16), block_diag(pb[:, :C], pb[:, C:]), preferred_element_type=F32)
                   for x, pb in zip(xs_, pbs)]
        for j, (c, x) in enumerate(zip(cs, xs_)):
            for d in range(2):
                sol = jnp.dot(x[:, d * C:(d + 1) * C].astype(BF16), rhs_s[d, j], preferred_element_type=F32)
                u_s[d, c] = sol[:, :DH]
                wq_s[d, c, pl.ds(0, C), :] = sol[:, DH:].astype(BF16)
        return 0

    lax.fori_loop(0, NC // PREP_UNROLL, prep, 0)

    s_s[...] = jnp.zeros_like(s_s)

    def scan(n, _):
        cf, cb = n, NC - 1 - n
        s_f, s_b = s_s[0], s_s[1]
        wq = jnp.concatenate([wq_s[0, cf], wq_s[1, cb]], axis=1)
        r1 = jnp.dot(wq, block_diag(s_f.astype(BF16), s_b.astype(BF16)), preferred_element_type=F32)
        u = jnp.concatenate([u_s[0, cf], u_s[1, cb]], axis=1)
        v_new = (u - r1[:C]).astype(BF16)
        vd = block_diag(v_new[:, :DH], v_new[:, DH:])
        at = jnp.concatenate([at_s[0, cf], at_s[1, cb]], axis=1)
        o = r1[C:] + jnp.dot(at, vd, preferred_element_type=F32)
        kd = jnp.concatenate([kd_s[0, cf], kd_s[1, cb]], axis=0)
        ds = lax.dot_general(kd, vd, (((0,), (0,)), ((), ())), preferred_element_type=F32)
        s_s[0] = s_f * gl_s[0, pl.ds(cf, 1), :] + ds[:, :DH]
        s_s[1] = s_b * gl_s[1, pl.ds(cb, 1), :] + ds[:, DH:]
        o_ref[0, pl.ds(pl.multiple_of(cf * C, C), C), :] = o[:, :DH]
        ob[pl.ds(pl.multiple_of(cb * C, C), C), :] = o[:, DH:]
        return 0

    lax.fori_loop(0, NC, scan, 0)

    def gate_rows(i, _):
        rows = pl.ds(pl.multiple_of(i * R, R), R)
        o = o_ref[0, rows, :] + ob[rows, :]
        o = o * lax.rsqrt(jnp.mean(o * o, -1, keepdims=True) + RMS_EPS) * nw_ref[...]
        o_ref[0, rows, :] = o * _silu(z_ref[0, rows, :])
        return 0

    lax.fori_loop(0, L // R, gate_rows, 0)


def _deltanet(proj, gt, conv_w, norm_w, *, B, L, H):
    DH, C, R = DN_HEAD_DIM, DN_CHUNK, 256
    assert C == DH and L % (C * PREP_UNROLL) == 0
    NC = L // C
    col = lambda off: pl.BlockSpec((1, L, DH), lambda b, h: (b, 0, off + h))
    cw = lambda off: pl.BlockSpec((3, DH), lambda b, h: (0, off + h))
    kern = functools.partial(_deltanet_kernel, L=L, C=C, H=H, R=R)
    return pl.pallas_call(
        kern, grid=(B, H),
        in_specs=[col(0), col(H), col(2 * H), col(3 * H), cw(0), cw(H), cw(2 * H),
                  pl.BlockSpec((1, 4 * H, NC, C), lambda b, h: (b, 0, 0, 0)),
                  pl.BlockSpec((1, DH), lambda b, h: (0, 0))],
        out_specs=pl.BlockSpec((1, L, DH), lambda b, h: (b, 0, h)),
        out_shape=jax.ShapeDtypeStruct((B, L, H * DH), F32),
        scratch_shapes=[pltpu.VMEM((L, DH), F32), pltpu.VMEM((L, DH), F32), pltpu.VMEM((L, DH), F32),
                        pltpu.VMEM((L, DH), F32),
                        pltpu.VMEM((2, NC, 2 * C, DH), BF16), pltpu.VMEM((2, NC, C, C), BF16),
                        pltpu.VMEM((2, NC, C, DH), BF16), pltpu.VMEM((2, NC, C, DH), F32),
                        pltpu.VMEM((2, NC, DH), F32), pltpu.VMEM((2, DH, DH), F32),
                        pltpu.VMEM((2, PREP_UNROLL, C, 2 * DH), BF16)],
        compiler_params=_cparams("parallel", "arbitrary"), name="deltanet",
    )(proj, proj, proj, proj, conv_w, conv_w, conv_w, gt, norm_w.reshape(1, DH))


def _conformer_kernel(val_ref, gate_ref, w_ref, b_ref, g_ref, beta_ref, o_ref, ypad, *, L, K, R, HALO):
    pad = K // 2
    zeros = jnp.zeros((HALO, CV_GROUP), F32)
    ypad[pl.ds(0, HALO), :] = zeros
    ypad[pl.ds(HALO + L, HALO), :] = zeros

    def glu_rows(i, _):
        r0 = pl.multiple_of(i * R, R)
        ypad[pl.ds(HALO + r0, R), :] = val_ref[0, pl.ds(r0, R), :] * _sigmoid(gate_ref[0, pl.ds(r0, R), :])
        return 0

    lax.fori_loop(0, L // R, glu_rows, 0)

    def conv_rows(i, _):
        r0 = pl.multiple_of(i * R, R)
        acc = jnp.zeros((R, CV_GROUP), F32)
        for t in range(K):
            acc = acc + w_ref[t:t + 1, :] * ypad[pl.ds(r0 + (HALO - pad + t), R), :]
        y = _layer_norm(acc + b_ref[...], g_ref[...], beta_ref[...])
        o_ref[0, pl.ds(r0, R), :] = _silu(y)
        return 0

    lax.fori_loop(0, L // R, conv_rows, 0)


def _conformer(proj, w, b, g, beta, *, B, L, col0, width):
    K = w.shape[0]
    n = width // CV_GROUP
    HALO, R = 16, 256
    c0 = col0 // CV_GROUP
    vec = pl.BlockSpec((1, CV_GROUP), lambda bb, j: (0, j))
    kern = functools.partial(_conformer_kernel, L=L, K=K, R=R, HALO=HALO)
    return pl.pallas_call(
        kern, grid=(B, n),
        in_specs=[pl.BlockSpec((1, L, CV_GROUP), lambda bb, j: (bb, 0, c0 + j)),
                  pl.BlockSpec((1, L, CV_GROUP), lambda bb, j: (bb, 0, c0 + n + j)),
                  pl.BlockSpec((K, CV_GROUP), lambda bb, j: (0, j)), vec, vec, vec],
        out_specs=pl.BlockSpec((1, L, CV_GROUP), lambda bb, j: (bb, 0, j)),
        out_shape=jax.ShapeDtypeStruct((B, L, width), F32),
        scratch_shapes=[pltpu.VMEM((L + 2 * HALO, CV_GROUP), F32)],
        compiler_params=_cparams("parallel", "parallel"), name="conformer",
    )(proj, proj, w, b.reshape(1, width), g.reshape(1, width), beta.reshape(1, width))


def _out_router_kernel(dn_ref, cv_ref, h_ref, w_ref, g_ref, b_ref, wr_ref, br_ref,
                       h1_ref, h1p_ref, route_ref, routet_ref, *, alpha, n_dn, sub):
    n_sub = h_ref.shape[0] // sub

    def matmul(s):
        rows = pl.ds(s * sub, sub)
        mix = jnp.dot(dn_ref[rows, :].astype(BF16), w_ref[pl.ds(0, n_dn), :], preferred_element_type=F32)
        return mix + jnp.dot(cv_ref[rows, :].astype(BF16), w_ref[pl.ds(n_dn, w_ref.shape[0] - n_dn), :],
                             preferred_element_type=F32)

    def finish(s, mix):
        rows = pl.ds(s * sub, sub)
        h1 = _layer_norm(alpha * h_ref[rows, :] + mix, g_ref[...], b_ref[...])
        h1_ref[rows, :] = h1
        h1p_ref[rows, :] = _pack_halves(h1)
        logits = jnp.dot(h1.astype(BF16), wr_ref[...], preferred_element_type=F32) + br_ref[...]
        route = _route(logits)
        route_ref[rows, :] = route
        routet_ref[:, rows] = route.T[:8, :]

    mix = matmul(0)
    for s in range(1, n_sub):
        nxt = matmul(s)
        finish(s - 1, mix)
        mix = nxt
    finish(n_sub - 1, mix)


def _route(logits):
    lane = lax.broadcasted_iota(I32, logits.shape, 1)
    neg = jnp.float32(-jnp.inf)
    big = jnp.int32(LANES)
    gl = jnp.where(lane < N_GROUPS, logits, neg)
    gmax = jnp.max(gl, -1, keepdims=True)
    gsel = jnp.min(jnp.where(gl == gmax, lane, big), -1, keepdims=True)
    pg = 1.0 / jnp.sum(jnp.exp(gl - gmax), -1, keepdims=True)
    lo = N_GROUPS + gsel * EXPERTS_PER_GROUP
    el = jnp.where((lane >= lo) & (lane < lo + EXPERTS_PER_GROUP), logits, neg)
    e1 = jnp.max(el, -1, keepdims=True)
    i1 = jnp.min(jnp.where(el == e1, lane, big), -1, keepdims=True)
    el2 = jnp.where(lane == i1, neg, el)
    e2 = jnp.max(el2, -1, keepdims=True)
    i2 = jnp.min(jnp.where(el2 == e2, lane, big), -1, keepdims=True)
    r = jnp.exp(e2 - e1)
    p1 = 1.0 / (1.0 + r)
    p2 = r * p1
    route = jnp.where(lane == 0, (i1 - N_GROUPS).astype(F32),
                      jnp.where(lane == 1, (i2 - N_GROUPS).astype(F32),
                                jnp.where(lane == 2, pg * p1, jnp.where(lane == 3, pg * p2, 0.0))))
    return route


def _out_router(dn, cv, h, w_out, g, b, w_router, b_router, *, alpha, tm=256, sub=128):
    T, D = h.shape
    n_dn, n_cv = dn.shape[1], cv.shape[1]
    vec = pl.BlockSpec((1, D), lambda i: (0, 0))
    row = pl.BlockSpec((tm, D), lambda i: (i, 0))
    kern = functools.partial(_out_router_kernel, alpha=alpha, n_dn=n_dn, sub=sub)
    return pl.pallas_call(
        kern, grid=(T // tm,),
        in_specs=[pl.BlockSpec((tm, n_dn), lambda i: (i, 0)), pl.BlockSpec((tm, n_cv), lambda i: (i, 0)), row,
                  pl.BlockSpec((n_dn + n_cv, D), lambda i: (0, 0)), vec, vec,
                  pl.BlockSpec((D, LANES), lambda i: (0, 0)), pl.BlockSpec((1, LANES), lambda i: (0, 0))],
        out_specs=[row, pl.BlockSpec((tm, D // 2), lambda i: (i, 0)),
                   pl.BlockSpec((tm, LANES), lambda i: (i, 0)), pl.BlockSpec((8, tm), lambda i: (0, i))],
        out_shape=[jax.ShapeDtypeStruct((T, D), F32), jax.ShapeDtypeStruct((T, D // 2), U32),
                   jax.ShapeDtypeStruct((T, LANES), F32), jax.ShapeDtypeStruct((8, T), F32)],
        compiler_params=_cparams("parallel"), name="out_router",
    )(dn, cv, h, w_out, g.reshape(1, D), b.reshape(1, D), w_router, b_router)


def _onehot_t(routet_ref, tr):
    sub = lax.broadcasted_iota(I32, (2 * N_EXPERTS, tr), 0)
    e1 = routet_ref[0:1, :].astype(I32)
    e2 = routet_ref[1:2, :].astype(I32)
    return sub == jnp.where(sub < N_EXPERTS, e1, e2 + N_EXPERTS)


def _count_kernel(routet_ref, cnt_ref, *, tr):
    @pl.when(pl.program_id(0) == 0)
    def _():
        cnt_ref[...] = jnp.zeros_like(cnt_ref)
    oh = _onehot_t(routet_ref, tr).astype(F32)
    cnt_ref[...] += jnp.sum(oh, axis=1, keepdims=True)


def _place_kernel(routet_ref, base_ref, dest_ref, carry, *, tr):
    @pl.when(pl.program_id(0) == 0)
    def _():
        carry[...] = jnp.zeros_like(carry)
    oh = _onehot_t(routet_ref, tr)
    ohb = oh.astype(F32).astype(BF16)
    ri = lax.broadcasted_iota(I32, (tr, tr), 0)
    ci = lax.broadcasted_iota(I32, (tr, tr), 1)
    before = (ri < ci).astype(F32).astype(BF16)
    excl = jnp.dot(ohb, before, preferred_element_type=F32)
    pos = jnp.where(oh, excl + carry[...] + base_ref[...], 0.0)
    d1 = jnp.sum(pos[:N_EXPERTS], axis=0, keepdims=True)
    d2 = jnp.sum(pos[N_EXPERTS:], axis=0, keepdims=True)
    sub = lax.broadcasted_iota(I32, (8, tr), 0)
    dest_ref[...] = jnp.where(sub == 0, d1, jnp.where(sub == 1, d2, 0.0)).astype(I32)
    carry[...] += jnp.sum(oh.astype(F32), axis=1, keepdims=True)


def _placement(routet, tr=512):
    T = routet.shape[1]
    E = N_EXPERTS
    rt = pl.BlockSpec((8, tr), lambda i: (0, i))
    col = pl.BlockSpec((2 * E, 1), lambda i: (0, 0))
    cnt = pl.pallas_call(
        functools.partial(_count_kernel, tr=tr), grid=(T // tr,), in_specs=[rt], out_specs=col,
        out_shape=jax.ShapeDtypeStruct((2 * E, 1), F32),
        compiler_params=_cparams("arbitrary"), name="moe_count")(routet)
    c1 = cnt[:E, 0].astype(I32)
    c2 = cnt[E:, 0].astype(I32)
    padded = ((c1 + c2 + FFN_BLOCK - 1) // FFN_BLOCK) * FFN_BLOCK
    pend = jnp.cumsum(padded)
    pstart = pend - padded
    base = jnp.concatenate([pstart, pstart + c1]).astype(F32).reshape(2 * E, 1)
    dest = pl.pallas_call(
        functools.partial(_place_kernel, tr=tr), grid=(T // tr,), in_specs=[rt, col], out_specs=rt,
        out_shape=jax.ShapeDtypeStruct((8, T), I32),
        scratch_shapes=[pltpu.VMEM((2 * E, 1), F32)],
        compiler_params=_cparams("arbitrary"), name="moe_place")(routet, base)
    n_blocks = (T * TOP_K + E * (FFN_BLOCK - 1) + FFN_BLOCK - 1) // FFN_BLOCK
    starts = jnp.arange(n_blocks, dtype=I32) * FFN_BLOCK
    blk_e = jnp.minimum(jnp.sum((pend[None, :] <= starts[:, None]).astype(I32), axis=1), E - 1)
    blk_first = jnp.concatenate([jnp.ones((1,), I32), (blk_e[1:] != blk_e[:-1]).astype(I32)])
    n_used = (pend[-1] // FFN_BLOCK).astype(I32).reshape(1)
    ids = jnp.arange(E, dtype=I32)
    later = (padded > 0)[None, :] & (ids[None, :] > ids[:, None])
    next_tab = jnp.min(jnp.where(later, ids[None, :], E), axis=1)
    blk_next = jnp.where(next_tab == E, -1, next_tab)[blk_e]
    return dest[0], dest[1], blk_e, blk_first, blk_next, n_used, n_blocks


def _dispatch_kernel(d1_ref, d2_ref, src_ref, init_ref, xs_ref, sem, *, tb):
    del init_ref

    def issue(t, _):
        row = src_ref.at[pl.ds(t, 1), :]
        pltpu.make_async_copy(row, xs_ref.at[pl.ds(d1_ref[t], 1), :], sem).start(priority=0)
        pltpu.make_async_copy(row, xs_ref.at[pl.ds(d2_ref[t], 1), :], sem).start(priority=1)
        return 0

    lax.fori_loop(0, tb, issue, 0, unroll=8)
    pltpu.make_async_copy(src_ref, xs_ref.at[pl.ds(0, tb), :], sem).wait()
    pltpu.make_async_copy(src_ref, xs_ref.at[pl.ds(0, tb), :], sem).wait()


def _dispatch(h1p, d1, d2, n_rows, tb=1024):
    T, W = h1p.shape
    idx = pl.BlockSpec((tb,), lambda i: (i,), memory_space=pltpu.SMEM)
    any_spec = pl.BlockSpec(memory_space=pl.ANY)
    return pl.pallas_call(
        functools.partial(_dispatch_kernel, tb=tb), grid=(T // tb,),
        in_specs=[idx, idx, pl.BlockSpec((tb, W), lambda i: (i, 0)), any_spec], out_specs=any_spec,
        out_shape=jax.ShapeDtypeStruct((n_rows, W), U32),
        scratch_shapes=[pltpu.SemaphoreType.DMA(())],
        input_output_aliases={3: 0},
        compiler_params=_cparams("arbitrary"), name="moe_dispatch",
    )(d1, d2, h1p, jnp.zeros((n_rows, W), U32))


def _ffn_kernel(blk_e_ref, blk_first_ref, blk_next_ref, n_used_ref, xs_ref, wgu_hbm, wdn_hbm, y_ref,
                wgu_f, wdn_f, wgu_s, wdn_s, sem, *, ff, layer):
    i = pl.program_id(0)
    used = i < n_used_ref[0]

    def weight_copies(e):
        copies = []
        for k, (src, dst) in enumerate(((wgu_hbm, wgu_f), (wdn_hbm, wdn_f))):
            rows = dst.shape[0] // FFN_WEIGHT_CHUNKS
            for c in range(FFN_WEIGHT_CHUNKS):
                r = pl.ds(c * rows, rows)
                copies.append(pltpu.make_async_copy(src.at[layer, e, r, :], dst.at[r, :],
                                                    sem.at[k * FFN_WEIGHT_CHUNKS + c]))
        return copies

    @pl.when(i == 0)
    def _():
        for c in weight_copies(blk_e_ref[0]):
            c.start(priority=1)

    @pl.when(used & (blk_first_ref[i] == 1))
    def _():
        for c in weight_copies(blk_e_ref[i]):
            c.wait()
        wgu_s[...] = wgu_f[...].astype(BF16)
        wdn_s[...] = wdn_f[...].astype(BF16)

        @pl.when(blk_next_ref[i] >= 0)
        def _():
            for c in weight_copies(blk_next_ref[i]):
                c.start(priority=1)

    @pl.when(used)
    def _():
        lo, hi = _unpack_halves(xs_ref[...])
        x = jnp.concatenate([lo, hi], axis=1).astype(BF16)
        gu = jnp.dot(x, wgu_s[...], preferred_element_type=F32)
        act = (_silu(gu[:, :ff]) * gu[:, ff:]).astype(BF16)
        y_ref[...] = _pack_halves(jnp.dot(act, wdn_s[...], preferred_element_type=F32))

    @pl.when(i >= n_used_ref[0])
    def _():
        y_ref[...] = jnp.zeros_like(y_ref)


def _ffn(xs, w_gu, w_dn, layer, blk_e, blk_first, blk_next, n_used, n_blocks):
    _, E, D, FF2 = w_gu.shape
    ff = FF2 // 2
    W = xs.shape[1]
    any_spec = pl.BlockSpec(memory_space=pl.ANY)
    grid_spec = pltpu.PrefetchScalarGridSpec(
        num_scalar_prefetch=4, grid=(n_blocks,),
        in_specs=[pl.BlockSpec((FFN_BLOCK, W), lambda i, *_: (i, 0)), any_spec, any_spec],
        out_specs=pl.BlockSpec((FFN_BLOCK, W), lambda i, *_: (i, 0)),
        scratch_shapes=[pltpu.VMEM((D, FF2), F32), pltpu.VMEM((ff, D), F32),
                        pltpu.VMEM((D, FF2), BF16), pltpu.VMEM((ff, D), BF16),
                        pltpu.SemaphoreType.DMA((2 * FFN_WEIGHT_CHUNKS,))])
    return pl.pallas_call(
        functools.partial(_ffn_kernel, ff=ff, layer=layer), grid_spec=grid_spec,
        out_shape=jax.ShapeDtypeStruct(xs.shape, U32),
        compiler_params=_cparams("arbitrary"), name="moe_ffn",
    )(blk_e, blk_first, blk_next, n_used, xs, w_gu, w_dn)


def _combine_kernel(d1_ref, d2_ref, d1n_ref, d2n_ref, y_ref, h1_ref, route_ref, g_ref, b_ref, h2_ref, h2b_ref,
                    ybuf, sem, *, tc, alpha):
    i = pl.program_id(0)
    slot = i % 2

    def gather(r1, r2, s):
        def issue(t, _):
            pltpu.make_async_copy(y_ref.at[pl.ds(r1[t], 1), :], ybuf.at[s, 0, pl.ds(t, 1), :],
                                  sem.at[s]).start(priority=0)
            pltpu.make_async_copy(y_ref.at[pl.ds(r2[t], 1), :], ybuf.at[s, 1, pl.ds(t, 1), :],
                                  sem.at[s]).start(priority=1)
            return 0
        lax.fori_loop(0, tc, issue, 0, unroll=8)

    @pl.when(i == 0)
    def _():
        gather(d1_ref, d2_ref, 0)

    @pl.when(i + 1 < pl.num_programs(0))
    def _():
        gather(d1n_ref, d2n_ref, 1 - slot)

    for j in range(TOP_K):
        pltpu.make_async_copy(y_ref.at[pl.ds(0, tc), :], ybuf.at[slot, j], sem.at[slot]).wait()

    route = route_ref[...]
    g1 = route[:, 2:3]
    g2 = route[:, 3:4]
    lo1, hi1 = _unpack_halves(ybuf[slot, 0])
    lo2, hi2 = _unpack_halves(ybuf[slot, 1])
    ffn = jnp.concatenate([g1 * lo1 + g2 * lo2, g1 * hi1 + g2 * hi2], axis=1)
    h2 = _layer_norm(alpha * h1_ref[...] + ffn, g_ref[...], b_ref[...])
    h2_ref[...] = h2
    h2b_ref[...] = h2.astype(BF16)


def _combine(y, d1, d2, h1, route, g, b, *, alpha, tc=256):
    T, D = h1.shape
    W = y.shape[1]
    n = T // tc
    idx = pl.BlockSpec((tc,), lambda i: (i,), memory_space=pltpu.SMEM)
    idx_next = pl.BlockSpec((tc,), lambda i: (jnp.minimum(i + 1, n - 1),), memory_space=pltpu.SMEM)
    row = pl.BlockSpec((tc, D), lambda i: (i, 0))
    vec = pl.BlockSpec((1, D), lambda i: (0, 0))
    return pl.pallas_call(
        functools.partial(_combine_kernel, tc=tc, alpha=alpha), grid=(n,),
        in_specs=[idx, idx, idx_next, idx_next, pl.BlockSpec(memory_space=pl.ANY), row,
                  pl.BlockSpec((tc, LANES), lambda i: (i, 0)), vec, vec],
        out_specs=[row, row],
        out_shape=[jax.ShapeDtypeStruct((T, D), F32), jax.ShapeDtypeStruct((T, D), BF16)],
        scratch_shapes=[pltpu.VMEM((2, TOP_K, tc, W), U32), pltpu.SemaphoreType.DMA((2,))],
        compiler_params=_cparams("arbitrary"), name="moe_combine",
    )(d1, d2, d1, d2, y, h1, route, g.reshape(1, D), b.reshape(1, D))


def kernel(x, emb_ln_g, emb_ln_b, w_in, short_conv_w, a_log, dt_bias, dn_norm_w, dw_conv_w, dw_conv_b,
           conv_ln_g, conv_ln_b, w_out, ln1_g, ln1_b, w_group, b_group, w_expert, b_expert, w_gate_up,
           w_down, ln2_g, ln2_b):
    B, L, D = x.shape
    T = B * L
    depth = w_in.shape[0]
    H = a_log.shape[2]
    dn_w = H * DN_HEAD_DIM
    cv_w = dw_conv_w.shape[2]
    n_ab = 4 * H
    alpha = (2 * depth) ** 0.25
    C = DN_CHUNK

    h, hb = _emb_ln(x.reshape(T, D), emb_ln_g, emb_ln_b)
    for l in range(depth):
        w = w_in[l]
        w_main = jnp.concatenate([w[:, :4 * dn_w], w[:, 4 * dn_w + n_ab:]], axis=1).astype(BF16)
        proj = _in_proj(hb, w_main).reshape(B, L, -1)
        gb = _decay_beta(hb, w[:, 4 * dn_w:4 * dn_w + n_ab], a_log[l], dt_bias[l])
        gt = gb[:, :n_ab].reshape(B, L // C, C, n_ab).transpose(0, 3, 1, 2)
        dn = _deltanet(proj, gt, short_conv_w[l], dn_norm_w[l], B=B, L=L, H=H)
        cv = _conformer(proj, dw_conv_w[l], dw_conv_b[l], conv_ln_g[l], conv_ln_b[l],
                        B=B, L=L, col0=4 * dn_w, width=cv_w)
        n_r = N_GROUPS + N_EXPERTS
        w_router = jnp.pad(jnp.concatenate([w_group[l], w_expert[l]], axis=1),
                           ((0, 0), (0, LANES - n_r))).astype(BF16)
        b_router = jnp.pad(jnp.concatenate([b_group[l], b_expert[l]]), (0, LANES - n_r)).reshape(1, LANES)
        h1, h1p, route, routet = _out_router(
            dn.reshape(T, dn_w), cv.reshape(T, cv_w), h, w_out[l].astype(BF16), ln1_g[l], ln1_b[l],
            w_router, b_router, alpha=alpha)
        d1, d2, blk_e, blk_first, blk_next, n_used, n_blocks = _placement(routet)
        xs = _dispatch(h1p, d1, d2, n_blocks * FFN_BLOCK)
        y = _ffn(xs, w_gate_up, w_down, l, blk_e, blk_first, blk_next, n_used, n_blocks)
        h, hb = _combine(y, d1, d2, h1, route, ln2_g[l], ln2_b[l], alpha=alpha)
    return h.reshape(B, L, D)
```

```python
import functools
import itertools

import jax
import jax.numpy as jnp
from jax import lax
from jax.experimental import pallas as pl
from jax.experimental.pallas import tpu as pltpu

F32 = jnp.float32
BF16 = jnp.bfloat16
U32 = jnp.uint32
I32 = jnp.int32

LANES = 128
DN_HEAD_DIM = 128
DN_CHUNK = 128
PREP_UNROLL = 8
CV_GROUP = 128
N_GROUPS = 8
EXPERTS_PER_GROUP = 8
N_EXPERTS = N_GROUPS * EXPERTS_PER_GROUP
TOP_K = 2
FFN_BLOCK = 256
FFN_WEIGHT_CHUNKS = 4
LN_EPS = 1e-5
RMS_EPS = 1e-6
VMEM_LIMIT = 56 * 1024 * 1024


def _cparams(*sem):
    return pltpu.CompilerParams(dimension_semantics=sem, vmem_limit_bytes=VMEM_LIMIT)


def _layer_norm(x, g, b):
    mu = jnp.mean(x, -1, keepdims=True)
    xc = x - mu
    var = jnp.mean(xc * xc, -1, keepdims=True)
    return xc * lax.rsqrt(var + LN_EPS) * g + b


def _sigmoid(x):
    return 1.0 / (1.0 + jnp.exp(-x))


def _silu(x):
    return x * _sigmoid(x)


def _pack_halves(y):
    n = y.shape[1] // 2
    bits = lax.bitcast_convert_type(y.astype(BF16).astype(F32), U32)
    return (bits[:, :n] >> 16) | bits[:, n:]


def _unpack_halves(u):
    lo = lax.bitcast_convert_type(u << 16, F32)
    hi = lax.bitcast_convert_type(u & jnp.uint32(0xFFFF0000), F32)
    return lo, hi


def _emb_ln_kernel(x_ref, g_ref, b_ref, h_ref, hb_ref):
    h = _layer_norm(x_ref[...], g_ref[...], b_ref[...])
    h_ref[...] = h
    hb_ref[...] = h.astype(BF16)


def _emb_ln(x, g, b, tm=512):
    T, D = x.shape
    row = pl.BlockSpec((tm, D), lambda i: (i, 0))
    vec = pl.BlockSpec((1, D), lambda i: (0, 0))
    return pl.pallas_call(
        _emb_ln_kernel, grid=(T // tm,), in_specs=[row, vec, vec], out_specs=[row, row],
        out_shape=[jax.ShapeDtypeStruct((T, D), F32), jax.ShapeDtypeStruct((T, D), BF16)],
        compiler_params=_cparams("parallel"), name="emb_ln")(x, g.reshape(1, D), b.reshape(1, D))


def _matmul_kernel(x_ref, w_ref, o_ref):
    o_ref[...] = jnp.dot(x_ref[...], w_ref[...], preferred_element_type=F32)


def _in_proj(hb, w, tm=1024, tn=1024):
    T, D = hb.shape
    N = w.shape[1]
    return pl.pallas_call(
        _matmul_kernel, grid=(T // tm, N // tn),
        in_specs=[pl.BlockSpec((tm, D), lambda i, j: (i, 0)), pl.BlockSpec((D, tn), lambda i, j: (0, j))],
        out_specs=pl.BlockSpec((tm, tn), lambda i, j: (i, j)),
        out_shape=jax.ShapeDtypeStruct((T, N), F32),
        compiler_params=_cparams("parallel", "arbitrary"), name="in_proj")(hb, w)


def _decay_beta_kernel(x_ref, w_ref, alog_ref, dtb_ref, o_ref, *, n_decay):
    ab = jnp.dot(x_ref[...], w_ref[...], preferred_element_type=F32)
    lane = lax.broadcasted_iota(I32, ab.shape, 1)
    s = ab + dtb_ref[...]
    softplus = jnp.maximum(s, 0.0) + jnp.log(1.0 + jnp.exp(-jnp.abs(s)))
    g = -jnp.exp(alog_ref[...]) * softplus
    o_ref[...] = jnp.where(lane < n_decay, g, _sigmoid(ab))


def _decay_beta(hb, w_ab, a_log, dt_bias, tm=1024):
    T, D = hb.shape
    n = a_log.size
    pad = lambda v: jnp.pad(v.reshape(1, n).astype(F32), ((0, 0), (0, LANES - n)))
    w = jnp.pad(w_ab, ((0, 0), (0, LANES - w_ab.shape[1]))).astype(BF16)
    vec = pl.BlockSpec((1, LANES), lambda i: (0, 0))
    return pl.pallas_call(
        functools.partial(_decay_beta_kernel, n_decay=n), grid=(T // tm,),
        in_specs=[pl.BlockSpec((tm, D), lambda i: (i, 0)), pl.BlockSpec((D, LANES), lambda i: (0, 0)), vec, vec],
        out_specs=pl.BlockSpec((tm, LANES), lambda i: (i, 0)),
        out_shape=jax.ShapeDtypeStruct((T, LANES), F32),
        compiler_params=_cparams("parallel"), name="decay_beta")(hb, w, pad(a_log), pad(dt_bias))


def _dot(a, b):
    return jnp.dot(a.astype(BF16), b.astype(BF16), preferred_element_type=F32)


def _dot_nt(a, b):
    return lax.dot_general(a.astype(BF16), b.astype(BF16), (((1,), (1,)), ((), ())),
                           preferred_element_type=F32)


def _dot_tn(a, b):
    return lax.dot_general(a.astype(BF16), b.astype(BF16), (((0,), (0,)), ((), ())),
                           preferred_element_type=F32)


def _deltanet_kernel(q_ref, k_ref, v_ref, z_ref, cq_ref, ck_ref, cv_ref, gt_ref, nw_ref, o_ref,
                     qs, ks, vs, ob, wq_s, at_s, kd_s, u_s, gl_s, s_s, rhs_s, *, L, C, H, R, n_items):
    DH = DN_HEAD_DIM
    s = pl.program_id(0)
    h = jnp.minimum(s, n_items - 1) % H
    slot_p = s % 2
    slot_s = 1 - slot_p
    NC = L // C
    U = PREP_UNROLL

    def conv_rows(i, _):
        r0 = pl.multiple_of(i * R, R)
        rows = pl.ds(r0, R)
        rid = lax.broadcasted_iota(I32, (R, DH), 0)
        has_prev = (r0 > 0).astype(F32)
        has_next = (r0 + R < L).astype(F32)

        def conv_silu(ref, w_ref):
            x = ref[0, rows, :]
            xm = ref[0, pl.ds(jnp.maximum(r0 - 1, 0), 1), :] * has_prev
            xp = ref[0, pl.ds(jnp.minimum(r0 + R, L - 1), 1), :] * has_next
            x_prev = jnp.where(rid == 0, xm, pltpu.roll(x, 1, 0))
            x_next = jnp.where(rid == R - 1, xp, pltpu.roll(x, R - 1, 0))
            return _silu(w_ref[0:1, :] * x_prev + w_ref[1:2, :] * x + w_ref[2:3, :] * x_next)

        q = conv_silu(q_ref, cq_ref)
        k = conv_silu(k_ref, ck_ref)
        qs[rows, :] = q * (lax.rsqrt(jnp.sum(q * q, -1, keepdims=True) + RMS_EPS) * (DH ** -0.5))
        ks[rows, :] = k * lax.rsqrt(jnp.sum(k * k, -1, keepdims=True) + RMS_EPS)
        vs[rows, :] = conv_silu(v_ref, cv_ref)
        return 0

    @pl.when(s < n_items)
    def _():
        lax.fori_loop(0, L // R, conv_rows, 0)

    ri = lax.broadcasted_iota(I32, (C, C), 0)
    ci = lax.broadcasted_iota(I32, (C, C), 1)
    eye = ri == ci
    eye_f = eye.astype(F32)
    incl = (ri >= ci, ri <= ci)
    strict = (ri > ci, ri < ci)
    n_sq = C.bit_length() - 2
    zero_b = jnp.zeros((C, C), BF16)

    def block_diag(a, b):
        return jnp.concatenate([jnp.concatenate([a, zero_b], axis=1),
                                jnp.concatenate([zero_b, b], axis=1)], axis=0)

    def prep_start(c, j):
        rows = pl.ds(pl.multiple_of(c * C, C), C)
        q = qs[rows, :]
        k = ks[rows, :]
        v = vs[rows, :]
        kq = _dot_nt(jnp.concatenate([k, q], axis=0), k)
        kk, qk = kq[:C], kq[C:]
        a2 = []
        for d in range(2):
            g_row = gt_ref[0, d * H + h, pl.ds(c, 1), :]
            b_row = gt_ref[0, (2 + d) * H + h, pl.ds(c, 1), :]
            m, ms, mt = incl[d], strict[d], incl[1 - d]
            g_col = jnp.sum(jnp.where(eye, g_row, 0.0), axis=1, keepdims=True)
            b_col = jnp.sum(jnp.where(eye, b_row, 0.0), axis=1, keepdims=True)
            gc_col = jnp.sum(jnp.where(m, g_row, 0.0), axis=1, keepdims=True)
            gc_row = jnp.sum(jnp.where(mt, g_col, 0.0), axis=0, keepdims=True)
            tot = jnp.sum(g_row, axis=1, keepdims=True)
            decay = jnp.where(m, jnp.exp(jnp.where(m, gc_col - gc_row, 0.0)), 0.0)
            a2.append(jnp.where(ms, -(kk * b_col * decay), 0.0))
            egc = jnp.exp(gc_col)
            rhs_s[d, j] = jnp.concatenate([v * b_col, k * (b_col * egc)], axis=1).astype(BF16)
            wq_s[slot_p, d, c, pl.ds(C, C), :] = (q * egc).astype(BF16)
            at_s[slot_p, d, c] = (qk * decay).astype(BF16)
            kd_s[slot_p, d, c] = (k * jnp.exp(tot - gc_col)).astype(BF16)
            gl_s[slot_p, d, pl.ds(c, 1), :] = jnp.broadcast_to(jnp.exp(tot), (1, DH))
        return jnp.concatenate(a2, axis=1)

    def prep_phases(i):
        cs = [i * U + j for j in range(U)]
        ps = []
        for j, c in enumerate(cs):
            ps.append(prep_start(c, j))
            if j == U // 2 - 1:
                yield
        eye2 = jnp.concatenate([eye_f, eye_f], axis=1)
        xs_ = [eye2 + p for p in ps]
        yield
        for _ in range(n_sq):
            pbs = [p.astype(BF16) for p in ps]
            ps = [jnp.dot(pb, block_diag(pb[:, :C], pb[:, C:]), preferred_element_type=F32) for pb in pbs]
            yield
            pbs = [p.astype(BF16) for p in ps]
            xs_ = [x + jnp.dot(x.astype(BF16), block_diag(pb[:, :C], pb[:, C:]), preferred_element_type=F32)
                   for x, pb in zip(xs_, pbs)]
            yield
        for j, (c, x) in enumerate(zip(cs, xs_)):
            for d in range(2):
                sol = jnp.dot(x[:, d * C:(d + 1) * C].astype(BF16), rhs_s[d, j], preferred_element_type=F32)
                u_s[slot_p, d, c] = sol[:, :DH]
                wq_s[slot_p, d, c, pl.ds(0, C), :] = sol[:, DH:].astype(BF16)
            if j == U // 2 - 1:
                yield
        yield

    def scan_stages(n):
        cf, cb = n, NC - 1 - n
        s_f, s_b = s_s[0], s_s[1]
        wq = jnp.concatenate([wq_s[slot_s, 0, cf], wq_s[slot_s, 1, cb]], axis=1)
        r1 = jnp.dot(wq, block_diag(s_f.astype(BF16), s_b.astype(BF16)), preferred_element_type=F32)
        yield
        u = jnp.concatenate([u_s[slot_s, 0, cf], u_s[slot_s, 1, cb]], axis=1)
        v_new = (u - r1[:C]).astype(BF16)
        vd = block_diag(v_new[:, :DH], v_new[:, DH:])
        at = jnp.concatenate([at_s[slot_s, 0, cf], at_s[slot_s, 1, cb]], axis=1)
        o = r1[C:] + jnp.dot(at, vd, preferred_element_type=F32)
        kd = jnp.concatenate([kd_s[slot_s, 0, cf], kd_s[slot_s, 1, cb]], axis=0)
        ds = lax.dot_general(kd, vd, (((0,), (0,)), ((), ())), preferred_element_type=F32)
        s_s[0] = s_f * gl_s[slot_s, 0, pl.ds(cf, 1), :] + ds[:, :DH]
        s_s[1] = s_b * gl_s[slot_s, 1, pl.ds(cb, 1), :] + ds[:, DH:]
        o_ref[0, pl.ds(pl.multiple_of(cf * C, C), C), :] = o[:, :DH]
        ob[pl.ds(pl.multiple_of(cb * C, C), C), :] = o[:, DH:]
        yield

    def run(prepare, scan):
        def trip(i, _):
            prep_it = prep_phases(i) if prepare else iter(())
            scan_it = itertools.chain.from_iterable(scan_stages(i * U + j) for j in range(U)) if scan else iter(())
            for _ in itertools.zip_longest(scan_it, prep_it):
                pass
            return 0
        if scan:
            s_s[...] = jnp.zeros_like(s_s)
        lax.fori_loop(0, NC // U, trip, 0)

    @pl.when(s == 0)
    def _():
        run(True, False)

    @pl.when((s > 0) & (s < n_items))
    def _():
        run(True, True)

    @pl.when(s == n_items)
    def _():
        run(False, True)

    def gate_rows(i, _):
        rows = pl.ds(pl.multiple_of(i * R, R), R)
        o = o_ref[0, rows, :] + ob[rows, :]
        o = o * lax.rsqrt(jnp.mean(o * o, -1, keepdims=True) + RMS_EPS) * nw_ref[...]
        o_ref[0, rows, :] = o * _silu(z_ref[0, rows, :])
        return 0

    @pl.when(s > 0)
    def _():
        lax.fori_loop(0, L // R, gate_rows, 0)


def _deltanet(proj, gt, conv_w, norm_w, *, B, L, H):
    DH, C, R = DN_HEAD_DIM, DN_CHUNK, 256
    assert C == DH and L % (C * PREP_UNROLL) == 0
    NC = L // C
    n_items = B * H
    cur = lambda s: jnp.minimum(s, n_items - 1)
    prev = lambda s: jnp.maximum(s - 1, 0)
    once = pl.Buffered(1)
    col = lambda off: pl.BlockSpec((1, L, DH), lambda s: (cur(s) // H, 0, off + cur(s) % H), pipeline_mode=once)
    cw = lambda off: pl.BlockSpec((3, DH), lambda s: (0, off + cur(s) % H))
    kern = functools.partial(_deltanet_kernel, L=L, C=C, H=H, R=R, n_items=n_items)
    return pl.pallas_call(
        kern, grid=(n_items + 1,),
        in_specs=[col(0), col(H), col(2 * H),
                  pl.BlockSpec((1, L, DH), lambda s: (prev(s) // H, 0, 3 * H + prev(s) % H), pipeline_mode=once),
                  cw(0), cw(H), cw(2 * H),
                  pl.BlockSpec((1, 4 * H, NC, C), lambda s: (cur(s) // H, 0, 0, 0)),
                  pl.BlockSpec((1, DH), lambda s: (0, 0))],
        out_specs=pl.BlockSpec((1, L, DH), lambda s: (prev(s) // H, 0, prev(s) % H)),
        out_shape=jax.ShapeDtypeStruct((B, L, H * DH), F32),
        scratch_shapes=[pltpu.VMEM((L, DH), F32), pltpu.VMEM((L, DH), F32), pltpu.VMEM((L, DH), F32),
                        pltpu.VMEM((L, DH), F32),
                        pltpu.VMEM((2, 2, NC, 2 * C, DH), BF16), pltpu.VMEM((2, 2, NC, C, C), BF16),
                        pltpu.VMEM((2, 2, NC, C, DH), BF16), pltpu.VMEM((2, 2, NC, C, DH), F32),
                        pltpu.VMEM((2, 2, NC, DH), F32), pltpu.VMEM((2, DH, DH), F32),
                        pltpu.VMEM((2, PREP_UNROLL, C, 2 * DH), BF16)],
        compiler_params=_cparams("arbitrary"), name="deltanet",
    )(proj, proj, proj, proj, conv_w, conv_w, conv_w, gt, norm_w.reshape(1, DH))


def _conformer_kernel(val_ref, gate_ref, w_ref, b_ref, g_ref, beta_ref, o_ref, ypad, *, L, K, R, HALO):
    pad = K // 2
    zeros = jnp.zeros((HALO, CV_GROUP), F32)
    ypad[pl.ds(0, HALO), :] = zeros
    ypad[pl.ds(HALO + L, HALO), :] = zeros

    def glu_rows(i, _):
        r0 = pl.multiple_of(i * R, R)
        ypad[pl.ds(HALO + r0, R), :] = val_ref[0, pl.ds(r0, R), :] * _sigmoid(gate_ref[0, pl.ds(r0, R), :])
        return 0

    lax.fori_loop(0, L // R, glu_rows, 0)

    def conv_rows(i, _):
        r0 = pl.multiple_of(i * R, R)
        acc = jnp.zeros((R, CV_GROUP), F32)
        for t in range(K):
            acc = acc + w_ref[t:t + 1, :] * ypad[pl.ds(r0 + (HALO - pad + t), R), :]
        y = _layer_norm(acc + b_ref[...], g_ref[...], beta_ref[...])
        o_ref[0, pl.ds(r0, R), :] = _silu(y)
        return 0

    lax.fori_loop(0, L // R, conv_rows, 0)


def _conformer(proj, w, b, g, beta, *, B, L, col0, width):
    K = w.shape[0]
    n = width // CV_GROUP
    HALO, R = 16, 256
    c0 = col0 // CV_GROUP
    vec = pl.BlockSpec((1, CV_GROUP), lambda bb, j: (0, j))
    kern = functools.partial(_conformer_kernel, L=L, K=K, R=R, HALO=HALO)
    return pl.pallas_call(
        kern, grid=(B, n),
        in_specs=[pl.BlockSpec((1, L, CV_GROUP), lambda bb, j: (bb, 0, c0 + j)),
                  pl.BlockSpec((1, L, CV_GROUP), lambda bb, j: (bb, 0, c0 + n + j)),
                  pl.BlockSpec((K, CV_GROUP), lambda bb, j: (0, j)), vec, vec, vec],
        out_specs=pl.BlockSpec((1, L, CV_GROUP), lambda bb, j: (bb, 0, j)),
        out_shape=jax.ShapeDtypeStruct((B, L, width), F32),
        scratch_shapes=[pltpu.VMEM((L + 2 * HALO, CV_GROUP), F32)],
        compiler_params=_cparams("parallel", "parallel"), name="conformer",
    )(proj, proj, w, b.reshape(1, width), g.reshape(1, width), beta.reshape(1, width))


def _out_router_kernel(dn_ref, cv_ref, h_ref, w_ref, g_ref, b_ref, wr_ref, br_ref,
                       h1_ref, h1p_ref, route_ref, routet_ref, *, alpha, n_dn, sub):
    n_sub = h_ref.shape[0] // sub

    def matmul(s):
        rows = pl.ds(s * sub, sub)
        mix = jnp.dot(dn_ref[rows, :].astype(BF16), w_ref[pl.ds(0, n_dn), :], preferred_element_type=F32)
        return mix + jnp.dot(cv_ref[rows, :].astype(BF16), w_ref[pl.ds(n_dn, w_ref.shape[0] - n_dn), :],
                             preferred_element_type=F32)

    def finish(s, mix):
        rows = pl.ds(s * sub, sub)
        h1 = _layer_norm(alpha * h_ref[rows, :] + mix, g_ref[...], b_ref[...])
        h1_ref[rows, :] = h1
        h1p_ref[rows, :] = _pack_halves(h1)
        logits = jnp.dot(h1.astype(BF16), wr_ref[...], preferred_element_type=F32) + br_ref[...]
        route = _route(logits)
        route_ref[rows, :] = route
        routet_ref[:, rows] = route.T[:8, :]

    mix = matmul(0)
    for s in range(1, n_sub):
        nxt = matmul(s)
        finish(s - 1, mix)
        mix = nxt
    finish(n_sub - 1, mix)


def _route(logits):
    lane = lax.broadcasted_iota(I32, logits.shape, 1)
    neg = jnp.float32(-jnp.inf)
    big = jnp.int32(LANES)
    gl = jnp.where(lane < N_GROUPS, logits, neg)
    gmax = jnp.max(gl, -1, keepdims=True)
    gsel = jnp.min(jnp.where(gl == gmax, lane, big), -1, keepdims=True)
    pg = 1.0 / jnp.sum(jnp.exp(gl - gmax), -1, keepdims=True)
    lo = N_GROUPS + gsel * EXPERTS_PER_GROUP
    el = jnp.where((lane >= lo) & (lane < lo + EXPERTS_PER_GROUP), logits, neg)
    e1 = jnp.max(el, -1, keepdims=True)
    i1 = jnp.min(jnp.where(el == e1, lane, big), -1, keepdims=True)
    el2 = jnp.where(lane == i1, neg, el)
    e2 = jnp.max(el2, -1, keepdims=True)
    i2 = jnp.min(jnp.where(el2 == e2, lane, big), -1, keepdims=True)
    r = jnp.exp(e2 - e1)
    p1 = 1.0 / (1.0 + r)
    p2 = r * p1
    route = jnp.where(lane == 0, (i1 - N_GROUPS).astype(F32),
                      jnp.where(lane == 1, (i2 - N_GROUPS).astype(F32),
                                jnp.where(lane == 2, pg * p1, jnp.where(lane == 3, pg * p2, 0.0))))
    return route


def _out_router(dn, cv, h, w_out, g, b, w_router, b_router, *, alpha, tm=256, sub=128):
    T, D = h.shape
    n_dn, n_cv = dn.shape[1], cv.shape[1]
    vec = pl.BlockSpec((1, D), lambda i: (0, 0))
    row = pl.BlockSpec((tm, D), lambda i: (i, 0))
    kern = functools.partial(_out_router_kernel, alpha=alpha, n_dn=n_dn, sub=sub)
    return pl.pallas_call(
        kern, grid=(T // tm,),
        in_specs=[pl.BlockSpec((tm, n_dn), lambda i: (i, 0)), pl.BlockSpec((tm, n_cv), lambda i: (i, 0)), row,
                  pl.BlockSpec((n_dn + n_cv, D), lambda i: (0, 0)), vec, vec,
                  pl.BlockSpec((D, LANES), lambda i: (0, 0)), pl.BlockSpec((1, LANES), lambda i: (0, 0))],
        out_specs=[row, pl.BlockSpec((tm, D // 2), lambda i: (i, 0)),
                   pl.BlockSpec((tm, LANES), lambda i: (i, 0)), pl.BlockSpec((8, tm), lambda i: (0, i))],
        out_shape=[jax.ShapeDtypeStruct((T, D), F32), jax.ShapeDtypeStruct((T, D // 2), U32),
                   jax.ShapeDtypeStruct((T, LANES), F32), jax.ShapeDtypeStruct((8, T), F32)],
        compiler_params=_cparams("parallel"), name="out_router",
    )(dn, cv, h, w_out, g.reshape(1, D), b.reshape(1, D), w_router, b_router)


def _onehot_t(routet_ref, tr):
    sub = lax.broadcasted_iota(I32, (2 * N_EXPERTS, tr), 0)
    e1 = routet_ref[0:1, :].astype(I32)
    e2 = routet_ref[1:2, :].astype(I32)
    return sub == jnp.where(sub < N_EXPERTS, e1, e2 + N_EXPERTS)


def _count_kernel(routet_ref, cnt_ref, *, tr):
    @pl.when(pl.program_id(0) == 0)
    def _():
        cnt_ref[...] = jnp.zeros_like(cnt_ref)
    oh = _onehot_t(routet_ref, tr).astype(F32)
    cnt_ref[...] += jnp.sum(oh, axis=1, keepdims=True)


def _place_kernel(routet_ref, base_ref, dest_ref, carry, *, tr):
    @pl.when(pl.program_id(0) == 0)
    def _():
        carry[...] = jnp.zeros_like(carry)
    oh = _onehot_t(routet_ref, tr)
    ohb = oh.astype(F32).astype(BF16)
    ri = lax.broadcasted_iota(I32, (tr, tr), 0)
    ci = lax.broadcasted_iota(I32, (tr, tr), 1)
    before = (ri < ci).astype(F32).astype(BF16)
    excl = jnp.dot(ohb, before, preferred_element_type=F32)
    pos = jnp.where(oh, excl + carry[...] + base_ref[...], 0.0)
    d1 = jnp.sum(pos[:N_EXPERTS], axis=0, keepdims=True)
    d2 = jnp.sum(pos[N_EXPERTS:], axis=0, keepdims=True)
    sub = lax.broadcasted_iota(I32, (8, tr), 0)
    dest_ref[...] = jnp.where(sub == 0, d1, jnp.where(sub == 1, d2, 0.0)).astype(I32)
    carry[...] += jnp.sum(oh.astype(F32), axis=1, keepdims=True)


def _placement(routet, tr=512):
    T = routet.shape[1]
    E = N_EXPERTS
    rt = pl.BlockSpec((8, tr), lambda i: (0, i))
    col = pl.BlockSpec((2 * E, 1), lambda i: (0, 0))
    cnt = pl.pallas_call(
        functools.partial(_count_kernel, tr=tr), grid=(T // tr,), in_specs=[rt], out_specs=col,
        out_shape=jax.ShapeDtypeStruct((2 * E, 1), F32),
        compiler_params=_cparams("arbitrary"), name="moe_count")(routet)
    c1 = cnt[:E, 0].astype(I32)
    c2 = cnt[E:, 0].astype(I32)
    padded = ((c1 + c2 + FFN_BLOCK - 1) // FFN_BLOCK) * FFN_BLOCK
    pend = jnp.cumsum(padded)
    pstart = pend - padded
    base = jnp.concatenate([pstart, pstart + c1]).astype(F32).reshape(2 * E, 1)
    dest = pl.pallas_call(
        functools.partial(_place_kernel, tr=tr), grid=(T // tr,), in_specs=[rt, col], out_specs=rt,
        out_shape=jax.ShapeDtypeStruct((8, T), I32),
        scratch_shapes=[pltpu.VMEM((2 * E, 1), F32)],
        compiler_params=_cparams("arbitrary"), name="moe_place")(routet, base)
    n_blocks = (T * TOP_K + E * (FFN_BLOCK - 1) + FFN_BLOCK - 1) // FFN_BLOCK
    starts = jnp.arange(n_blocks, dtype=I32) * FFN_BLOCK
    blk_e = jnp.minimum(jnp.sum((pend[None, :] <= starts[:, None]).astype(I32), axis=1), E - 1)
    blk_first = jnp.concatenate([jnp.ones((1,), I32), (blk_e[1:] != blk_e[:-1]).astype(I32)])
    n_used = (pend[-1] // FFN_BLOCK).astype(I32).reshape(1)
    ids = jnp.arange(E, dtype=I32)
    later = (padded > 0)[None, :] & (ids[None, :] > ids[:, None])
    next_tab = jnp.min(jnp.where(later, ids[None, :], E), axis=1)
    blk_next = jnp.where(next_tab == E, -1, next_tab)[blk_e]
    return dest[0], dest[1], blk_e, blk_first, blk_next, n_used, n_blocks


def _dispatch_kernel(d1_ref, d2_ref, src_ref, init_ref, xs_ref, sem, *, tb):
    del init_ref

    def issue(t, _):
        row = src_ref.at[pl.ds(t, 1), :]
        pltpu.make_async_copy(row, xs_ref.at[pl.ds(d1_ref[t], 1), :], sem).start(priority=0)
        pltpu.make_async_copy(row, xs_ref.at[pl.ds(d2_ref[t], 1), :], sem).start(priority=1)
        return 0

    lax.fori_loop(0, tb, issue, 0, unroll=8)
    pltpu.make_async_copy(src_ref, xs_ref.at[pl.ds(0, tb), :], sem).wait()
    pltpu.make_async_copy(src_ref, xs_ref.at[pl.ds(0, tb), :], sem).wait()


def _dispatch(h1p, d1, d2, n_rows, tb=1024):
    T, W = h1p.shape
    idx = pl.BlockSpec((tb,), lambda i: (i,), memory_space=pltpu.SMEM)
    any_spec = pl.BlockSpec(memory_space=pl.ANY)
    return pl.pallas_call(
        functools.partial(_dispatch_kernel, tb=tb), grid=(T // tb,),
        in_specs=[idx, idx, pl.BlockSpec((tb, W), lambda i: (i, 0)), any_spec], out_specs=any_spec,
        out_shape=jax.ShapeDtypeStruct((n_rows, W), U32),
        scratch_shapes=[pltpu.SemaphoreType.DMA(())],
        input_output_aliases={3: 0},
        compiler_params=_cparams("arbitrary"), name="moe_dispatch",
    )(d1, d2, h1p, jnp.zeros((n_rows, W), U32))


def _ffn_kernel(blk_e_ref, blk_first_ref, blk_next_ref, n_used_ref, xs_ref, wgu_hbm, wdn_hbm, y_ref,
                wgu_f, wdn_f, wgu_s, wdn_s, sem, *, ff, layer):
    i = pl.program_id(0)
    used = i < n_used_ref[0]

    def weight_copies(e):
        copies = []
        for k, (src, dst) in enumerate(((wgu_hbm, wgu_f), (wdn_hbm, wdn_f))):
            rows = dst.shape[0] // FFN_WEIGHT_CHUNKS
            for c in range(FFN_WEIGHT_CHUNKS):
                r = pl.ds(c * rows, rows)
                copies.append(pltpu.make_async_copy(src.at[layer, e, r, :], dst.at[r, :],
                                                    sem.at[k * FFN_WEIGHT_CHUNKS + c]))
        return copies

    @pl.when(i == 0)
    def _():
        for c in weight_copies(blk_e_ref[0]):
            c.start(priority=1)

    @pl.when(used & (blk_first_ref[i] == 1))
    def _():
        for c in weight_copies(blk_e_ref[i]):
            c.wait()
        wgu_s[...] = wgu_f[...].astype(BF16)
        wdn_s[...] = wdn_f[...].astype(BF16)

        @pl.when(blk_next_ref[i] >= 0)
        def _():
            for c in weight_copies(blk_next_ref[i]):
                c.start(priority=1)

    @pl.when(used)
    def _():
        lo, hi = _unpack_halves(xs_ref[...])
        x = jnp.concatenate([lo, hi], axis=1).astype(BF16)
        gu = jnp.dot(x, wgu_s[...], preferred_element_type=F32)
        act = (_silu(gu[:, :ff]) * gu[:, ff:]).astype(BF16)
        y_ref[...] = _pack_halves(jnp.dot(act, wdn_s[...], preferred_element_type=F32))

    @pl.when(i >= n_used_ref[0])
    def _():
        y_ref[...] = jnp.zeros_like(y_ref)


def _ffn(xs, w_gu, w_dn, layer, blk_e, blk_first, blk_next, n_used, n_blocks):
    _, E, D, FF2 = w_gu.shape
    ff = FF2 // 2
    W = xs.shape[1]
    any_spec = pl.BlockSpec(memory_space=pl.ANY)
    grid_spec = pltpu.PrefetchScalarGridSpec(
        num_scalar_prefetch=4, grid=(n_blocks,),
        in_specs=[pl.BlockSpec((FFN_BLOCK, W), lambda i, *_: (i, 0)), any_spec, any_spec],
        out_specs=pl.BlockSpec((FFN_BLOCK, W), lambda i, *_: (i, 0)),
        scratch_shapes=[pltpu.VMEM((D, FF2), F32), pltpu.VMEM((ff, D), F32),
                        pltpu.VMEM((D, FF2), BF16), pltpu.VMEM((ff, D), BF16),
                        pltpu.SemaphoreType.DMA((2 * FFN_WEIGHT_CHUNKS,))])
    return pl.pallas_call(
        functools.partial(_ffn_kernel, ff=ff, layer=layer), grid_spec=grid_spec,
        out_shape=jax.ShapeDtypeStruct(xs.shape, U32),
        compiler_params=_cparams("arbitrary"), name="moe_ffn",
    )(blk_e, blk_first, blk_next, n_used, xs, w_gu, w_dn)


def _combine_kernel(d1_ref, d2_ref, d1n_ref, d2n_ref, y_ref, h1_ref, route_ref, g_ref, b_ref, h2_ref, h2b_ref,
                    ybuf, sem, *, tc, alpha):
    i = pl.program_id(0)
    slot = i % 2

    def gather(r1, r2, s):
        def issue(t, _):
            pltpu.make_async_copy(y_ref.at[pl.ds(r1[t], 1), :], ybuf.at[s, 0, pl.ds(t, 1), :],
                                  sem.at[s]).start(priority=0)
            pltpu.make_async_copy(y_ref.at[pl.ds(r2[t], 1), :], ybuf.at[s, 1, pl.ds(t, 1), :],
                                  sem.at[s]).start(priority=1)
            return 0
        lax.fori_loop(0, tc, issue, 0, unroll=8)

    @pl.when(i == 0)
    def _():
        gather(d1_ref, d2_ref, 0)

    @pl.when(i + 1 < pl.num_programs(0))
    def _():
        gather(d1n_ref, d2n_ref, 1 - slot)

    for j in range(TOP_K):
        pltpu.make_async_copy(y_ref.at[pl.ds(0, tc), :], ybuf.at[slot, j], sem.at[slot]).wait()

    route = route_ref[...]
    g1 = route[:, 2:3]
    g2 = route[:, 3:4]
    lo1, hi1 = _unpack_halves(ybuf[slot, 0])
    lo2, hi2 = _unpack_halves(ybuf[slot, 1])
    ffn = jnp.concatenate([g1 * lo1 + g2 * lo2, g1 * hi1 + g2 * hi2], axis=1)
    h2 = _layer_norm(alpha * h1_ref[...] + ffn, g_ref[...], b_ref[...])
    h2_ref[...] = h2
    h2b_ref[...] = h2.astype(BF16)


def _combine(y, d1, d2, h1, route, g, b, *, alpha, tc=256):
    T, D = h1.shape
    W = y.shape[1]
    n = T // tc
    idx = pl.BlockSpec((tc,), lambda i: (i,), memory_space=pltpu.SMEM)
    idx_next = pl.BlockSpec((tc,), lambda i: (jnp.minimum(i + 1, n - 1),), memory_space=pltpu.SMEM)
    row = pl.BlockSpec((tc, D), lambda i: (i, 0))
    vec = pl.BlockSpec((1, D), lambda i: (0, 0))
    return pl.pallas_call(
        functools.partial(_combine_kernel, tc=tc, alpha=alpha), grid=(n,),
        in_specs=[idx, idx, idx_next, idx_next, pl.BlockSpec(memory_space=pl.ANY), row,
                  pl.BlockSpec((tc, LANES), lambda i: (i, 0)), vec, vec],
        out_specs=[row, row],
        out_shape=[jax.ShapeDtypeStruct((T, D), F32), jax.ShapeDtypeStruct((T, D), BF16)],
        scratch_shapes=[pltpu.VMEM((2, TOP_K, tc, W), U32), pltpu.SemaphoreType.DMA((2,))],
        compiler_params=_cparams("arbitrary"), name="moe_combine",
    )(d1, d2, d1, d2, y, h1, route, g.reshape(1, D), b.reshape(1, D))


def kernel(x, emb_ln_g, emb_ln_b, w_in, short_conv_w, a_log, dt_bias, dn_norm_w, dw_conv_w, dw_conv_b,
           conv_ln_g, conv_ln_b, w_out, ln1_g, ln1_b, w_group, b_group, w_expert, b_expert, w_gate_up,
           w_down, ln2_g, ln2_b):
    B, L, D = x.shape
    T = B * L
    depth = w_in.shape[0]
    H = a_log.shape[2]
    dn_w = H * DN_HEAD_DIM
    cv_w = dw_conv_w.shape[2]
    n_ab = 4 * H
    alpha = (2 * depth) ** 0.25
    C = DN_CHUNK

    h, hb = _emb_ln(x.reshape(T, D), emb_ln_g, emb_ln_b)
    for l in range(depth):
        w = w_in[l]
        w_main = jnp.concatenate([w[:, :4 * dn_w], w[:, 4 * dn_w + n_ab:]], axis=1).astype(BF16)
        proj = _in_proj(hb, w_main).reshape(B, L, -1)
        gb = _decay_beta(hb, w[:, 4 * dn_w:4 * dn_w + n_ab], a_log[l], dt_bias[l])
        gt = gb[:, :n_ab].reshape(B, L // C, C, n_ab).transpose(0, 3, 1, 2)
        dn = _deltanet(proj, gt, short_conv_w[l], dn_norm_w[l], B=B, L=L, H=H)
        cv = _conformer(proj, dw_conv_w[l], dw_conv_b[l], conv_ln_g[l], conv_ln_b[l],
                        B=B, L=L, col0=4 * dn_w, width=cv_w)
        n_r = N_GROUPS + N_EXPERTS
        w_router = jnp.pad(jnp.concatenate([w_group[l], w_expert[l]], axis=1),
                           ((0, 0), (0, LANES - n_r))).astype(BF16)
        b_router = jnp.pad(jnp.concatenate([b_group[l], b_expert[l]]), (0, LANES - n_r)).reshape(1, LANES)
        h1, h1p, route, routet = _out_router(
            dn.reshape(T, dn_w), cv.reshape(T, cv_w), h, w_out[l].astype(BF16), ln1_g[l], ln1_b[l],
            w_router, b_router, alpha=alpha)
        d1, d2, blk_e, blk_first, blk_next, n_used, n_blocks = _placement(routet)
        xs = _dispatch(h1p, d1, d2, n_blocks * FFN_BLOCK)
        y = _ffn(xs, w_gate_up, w_down, l, blk_e, blk_first, blk_next, n_used, n_blocks)
        h, hb = _combine(y, d1, d2, h1, route, ln2_g[l], ln2_b[l], alpha=alpha)
    return h.reshape(B, L, D)
```

```python
import functools
import itertools

import jax
import jax.numpy as jnp
from jax import lax
from jax.experimental import pallas as pl
from jax.experimental.pallas import tpu as pltpu

F32 = jnp.float32
BF16 = jnp.bfloat16
U32 = jnp.uint32
I32 = jnp.int32

LANES = 128
DN_HEAD_DIM = 128
DN_CHUNK = 128
PREP_UNROLL = 8
CV_GROUP = 128
N_GROUPS = 8
EXPERTS_PER_GROUP = 8
N_EXPERTS = N_GROUPS * EXPERTS_PER_GROUP
TOP_K = 2
FFN_BLOCK = 256
FFN_WEIGHT_CHUNKS = 4
LN_EPS = 1e-5
RMS_EPS = 1e-6
VMEM_LIMIT = 56 * 1024 * 1024


def _cparams(*sem):
    return pltpu.CompilerParams(dimension_semantics=sem, vmem_limit_bytes=VMEM_LIMIT)


def _layer_norm(x, g, b):
    mu = jnp.mean(x, -1, keepdims=True)
    xc = x - mu
    var = jnp.mean(xc * xc, -1, keepdims=True)
    return xc * lax.rsqrt(var + LN_EPS) * g + b


def _sigmoid(x):
    return 1.0 / (1.0 + jnp.exp(-x))


def _silu(x):
    return x * _sigmoid(x)


def _pack_halves(y):
    n = y.shape[1] // 2
    bits = lax.bitcast_convert_type(y.astype(BF16).astype(F32), U32)
    return (bits[:, :n] >> 16) | bits[:, n:]


def _unpack_halves(u):
    lo = lax.bitcast_convert_type(u << 16, F32)
    hi = lax.bitcast_convert_type(u & jnp.uint32(0xFFFF0000), F32)
    return lo, hi


def _emb_ln_kernel(x_ref, g_ref, b_ref, h_ref, hb_ref):
    h = _layer_norm(x_ref[...], g_ref[...], b_ref[...])
    h_ref[...] = h
    hb_ref[...] = h.astype(BF16)


def _emb_ln(x, g, b, tm=512):
    T, D = x.shape
    row = pl.BlockSpec((tm, D), lambda i: (i, 0))
    vec = pl.BlockSpec((1, D), lambda i: (0, 0))
    return pl.pallas_call(
        _emb_ln_kernel, grid=(T // tm,), in_specs=[row, vec, vec], out_specs=[row, row],
        out_shape=[jax.ShapeDtypeStruct((T, D), F32), jax.ShapeDtypeStruct((T, D), BF16)],
        compiler_params=_cparams("parallel"), name="emb_ln")(x, g.reshape(1, D), b.reshape(1, D))


def _in_proj_kernel(x_ref, w_ref, o_ref):
    acc = jnp.dot(x_ref[...], w_ref[...], preferred_element_type=F32)
    for j in range(o_ref.shape[0]):
        o_ref[j] = acc[:, j * LANES:(j + 1) * LANES]


def _in_proj(hb, w, tm=1024, tn=1024):
    T, D = hb.shape
    N = w.shape[1]
    return pl.pallas_call(
        _in_proj_kernel, grid=(T // tm, N // tn),
        in_specs=[pl.BlockSpec((tm, D), lambda i, j: (i, 0)), pl.BlockSpec((D, tn), lambda i, j: (0, j))],
        out_specs=pl.BlockSpec((tn // LANES, tm, LANES), lambda i, j: (j, i, 0)),
        out_shape=jax.ShapeDtypeStruct((N // LANES, T, LANES), F32),
        compiler_params=_cparams("parallel", "arbitrary"), name="in_proj")(hb, w)


def _decay_beta_kernel(x_ref, w_ref, alog_ref, dtb_ref, o_ref, *, n_decay):
    ab = jnp.dot(x_ref[...], w_ref[...], preferred_element_type=F32)
    lane = lax.broadcasted_iota(I32, ab.shape, 1)
    s = ab + dtb_ref[...]
    softplus = jnp.maximum(s, 0.0) + jnp.log(1.0 + jnp.exp(-jnp.abs(s)))
    g = -jnp.exp(alog_ref[...]) * softplus
    o_ref[...] = jnp.where(lane < n_decay, g, _sigmoid(ab))


def _decay_beta(hb, w_ab, a_log, dt_bias, tm=1024):
    T, D = hb.shape
    n = a_log.size
    pad = lambda v: jnp.pad(v.reshape(1, n).astype(F32), ((0, 0), (0, LANES - n)))
    w = jnp.pad(w_ab, ((0, 0), (0, LANES - w_ab.shape[1]))).astype(BF16)
    vec = pl.BlockSpec((1, LANES), lambda i: (0, 0))
    return pl.pallas_call(
        functools.partial(_decay_beta_kernel, n_decay=n), grid=(T // tm,),
        in_specs=[pl.BlockSpec((tm, D), lambda i: (i, 0)), pl.BlockSpec((D, LANES), lambda i: (0, 0)), vec, vec],
        out_specs=pl.BlockSpec((tm, LANES), lambda i: (i, 0)),
        out_shape=jax.ShapeDtypeStruct((T, LANES), F32),
        compiler_params=_cparams("parallel"), name="decay_beta")(hb, w, pad(a_log), pad(dt_bias))


def _dot(a, b):
    return jnp.dot(a.astype(BF16), b.astype(BF16), preferred_element_type=F32)


def _dot_nt(a, b):
    return lax.dot_general(a.astype(BF16), b.astype(BF16), (((1,), (1,)), ((), ())),
                           preferred_element_type=F32)


def _dot_tn(a, b):
    return lax.dot_general(a.astype(BF16), b.astype(BF16), (((0,), (0,)), ((), ())),
                           preferred_element_type=F32)


def _deltanet_kernel(q_ref, k_ref, v_ref, z_ref, cq_ref, ck_ref, cv_ref, gt_ref, nw_ref, o_ref,
                     qs, ks, vs, ob, wq_s, at_s, kd_s, u_s, gl_s, s_s, rhs_s, *, L, C, H, R, n_items):
    DH = DN_HEAD_DIM
    s = pl.program_id(0)
    h = jnp.minimum(s, n_items - 1) % H
    slot_p = s % 2
    slot_s = 1 - slot_p
    NC = L // C
    U = PREP_UNROLL

    def conv_rows(i, _):
        r0 = pl.multiple_of(i * R, R)
        rows = pl.ds(r0, R)
        rid = lax.broadcasted_iota(I32, (R, DH), 0)
        has_prev = (r0 > 0).astype(F32)
        has_next = (r0 + R < L).astype(F32)

        def conv_silu(ref, w_ref):
            x = ref[0, rows, :]
            xm = ref[0, pl.ds(jnp.maximum(r0 - 1, 0), 1), :] * has_prev
            xp = ref[0, pl.ds(jnp.minimum(r0 + R, L - 1), 1), :] * has_next
            x_prev = jnp.where(rid == 0, xm, pltpu.roll(x, 1, 0))
            x_next = jnp.where(rid == R - 1, xp, pltpu.roll(x, R - 1, 0))
            return _silu(w_ref[0:1, :] * x_prev + w_ref[1:2, :] * x + w_ref[2:3, :] * x_next)

        q = conv_silu(q_ref, cq_ref)
        k = conv_silu(k_ref, ck_ref)
        qs[rows, :] = q * (lax.rsqrt(jnp.sum(q * q, -1, keepdims=True) + RMS_EPS) * (DH ** -0.5))
        ks[rows, :] = k * lax.rsqrt(jnp.sum(k * k, -1, keepdims=True) + RMS_EPS)
        vs[rows, :] = conv_silu(v_ref, cv_ref)
        return 0

    @pl.when(s < n_items)
    def _():
        lax.fori_loop(0, L // R, conv_rows, 0)

    ri = lax.broadcasted_iota(I32, (C, C), 0)
    ci = lax.broadcasted_iota(I32, (C, C), 1)
    eye = ri == ci
    eye_f = eye.astype(F32)
    incl = (ri >= ci, ri <= ci)
    strict = (ri > ci, ri < ci)
    n_sq = C.bit_length() - 2
    zero_b = jnp.zeros((C, C), BF16)

    def block_diag(a, b):
        return jnp.concatenate([jnp.concatenate([a, zero_b], axis=1),
                                jnp.concatenate([zero_b, b], axis=1)], axis=0)

    def prep_start(c, j):
        rows = pl.ds(pl.multiple_of(c * C, C), C)
        q = qs[rows, :]
        k = ks[rows, :]
        v = vs[rows, :]
        kq = _dot_nt(jnp.concatenate([k, q], axis=0), k)
        kk, qk = kq[:C], kq[C:]
        a2 = []
        for d in range(2):
            g_row = gt_ref[0, d * H + h, pl.ds(c, 1), :]
            b_row = gt_ref[0, (2 + d) * H + h, pl.ds(c, 1), :]
            m, ms, mt = incl[d], strict[d], incl[1 - d]
            g_col = jnp.sum(jnp.where(eye, g_row, 0.0), axis=1, keepdims=True)
            b_col = jnp.sum(jnp.where(eye, b_row, 0.0), axis=1, keepdims=True)
            gc_col = jnp.sum(jnp.where(m, g_row, 0.0), axis=1, keepdims=True)
            gc_row = jnp.sum(jnp.where(mt, g_col, 0.0), axis=0, keepdims=True)
            tot = jnp.sum(g_row, axis=1, keepdims=True)
            decay = jnp.where(m, jnp.exp(jnp.where(m, gc_col - gc_row, 0.0)), 0.0)
            a2.append(jnp.where(ms, -(kk * b_col * decay), 0.0))
            egc = jnp.exp(gc_col)
            rhs_s[d, j] = jnp.concatenate([v * b_col, k * (b_col * egc)], axis=1).astype(BF16)
            wq_s[slot_p, d, c, pl.ds(C, C), :] = (q * egc).astype(BF16)
            at_s[slot_p, d, c] = (qk * decay).astype(BF16)
            kd_s[slot_p, d, c] = (k * jnp.exp(tot - gc_col)).astype(BF16)
            gl_s[slot_p, d, pl.ds(c, 1), :] = jnp.broadcast_to(jnp.exp(tot), (1, DH))
        return jnp.concatenate(a2, axis=1)

    def prep_phases(i):
        cs = [i * U + j for j in range(U)]
        ps = []
        for j, c in enumerate(cs):
            ps.append(prep_start(c, j))
            if j == U // 2 - 1:
                yield
        eye2 = jnp.concatenate([eye_f, eye_f], axis=1)
        xs_ = [eye2 + p for p in ps]
        yield
        for _ in range(n_sq):
            pbs = [p.astype(BF16) for p in ps]
            ps = [jnp.dot(pb, block_diag(pb[:, :C], pb[:, C:]), preferred_element_type=F32) for pb in pbs]
            yield
            pbs = [p.astype(BF16) for p in ps]
            xs_ = [x + jnp.dot(x.astype(BF16), block_diag(pb[:, :C], pb[:, C:]), preferred_element_type=F32)
                   for x, pb in zip(xs_, pbs)]
            yield
        for j, (c, x) in enumerate(zip(cs, xs_)):
            for d in range(2):
                sol = jnp.dot(x[:, d * C:(d + 1) * C].astype(BF16), rhs_s[d, j], preferred_element_type=F32)
                u_s[slot_p, d, c] = sol[:, :DH]
                wq_s[slot_p, d, c, pl.ds(0, C), :] = sol[:, DH:].astype(BF16)
            if j == U // 2 - 1:
                yield
        yield

    def scan_stages(n):
        cf, cb = n, NC - 1 - n
        s_f, s_b = s_s[0], s_s[1]
        wq = jnp.concatenate([wq_s[slot_s, 0, cf], wq_s[slot_s, 1, cb]], axis=1)
        r1 = jnp.dot(wq, block_diag(s_f.astype(BF16), s_b.astype(BF16)), preferred_element_type=F32)
        yield
        u = jnp.concatenate([u_s[slot_s, 0, cf], u_s[slot_s, 1, cb]], axis=1)
        v_new = (u - r1[:C]).astype(BF16)
        vd = block_diag(v_new[:, :DH], v_new[:, DH:])
        at = jnp.concatenate([at_s[slot_s, 0, cf], at_s[slot_s, 1, cb]], axis=1)
        o = r1[C:] + jnp.dot(at, vd, preferred_element_type=F32)
        kd = jnp.concatenate([kd_s[slot_s, 0, cf], kd_s[slot_s, 1, cb]], axis=0)
        ds = lax.dot_general(kd, vd, (((0,), (0,)), ((), ())), preferred_element_type=F32)
        s_s[0] = s_f * gl_s[slot_s, 0, pl.ds(cf, 1), :] + ds[:, :DH]
        s_s[1] = s_b * gl_s[slot_s, 1, pl.ds(cb, 1), :] + ds[:, DH:]
        o_ref[0, pl.ds(pl.multiple_of(cf * C, C), C), :] = o[:, :DH]
        ob[pl.ds(pl.multiple_of(cb * C, C), C), :] = o[:, DH:]
        yield

    def run(prepare, scan):
        def trip(i, _):
            prep_it = prep_phases(i) if prepare else iter(())
            scan_it = itertools.chain.from_iterable(scan_stages(i * U + j) for j in range(U)) if scan else iter(())
            for _ in itertools.zip_longest(scan_it, prep_it):
                pass
            return 0
        if scan:
            s_s[...] = jnp.zeros_like(s_s)
        lax.fori_loop(0, NC // U, trip, 0)

    @pl.when(s == 0)
    def _():
        run(True, False)

    @pl.when((s > 0) & (s < n_items))
    def _():
        run(True, True)

    @pl.when(s == n_items)
    def _():
        run(False, True)

    def gate_rows(i, _):
        rows = pl.ds(pl.multiple_of(i * R, R), R)
        o = o_ref[0, rows, :] + ob[rows, :]
        o = o * lax.rsqrt(jnp.mean(o * o, -1, keepdims=True) + RMS_EPS) * nw_ref[...]
        o_ref[0, rows, :] = o * _silu(z_ref[0, rows, :])
        return 0

    @pl.when(s > 0)
    def _():
        lax.fori_loop(0, L // R, gate_rows, 0)


def _deltanet(proj, gt, conv_w, norm_w, *, B, L, H):
    DH, C, R = DN_HEAD_DIM, DN_CHUNK, 256
    assert C == DH and L % (C * PREP_UNROLL) == 0
    NC = L // C
    n_items = B * H
    cur = lambda s: jnp.minimum(s, n_items - 1)
    prev = lambda s: jnp.maximum(s - 1, 0)
    once = pl.Buffered(1)
    col = lambda off: pl.BlockSpec((1, L, DH), lambda s: (off + cur(s) % H, cur(s) // H, 0))
    cw = lambda off: pl.BlockSpec((3, DH), lambda s: (0, off + cur(s) % H))
    kern = functools.partial(_deltanet_kernel, L=L, C=C, H=H, R=R, n_items=n_items)
    return pl.pallas_call(
        kern, grid=(n_items + 1,),
        in_specs=[col(0), col(H), col(2 * H),
                  pl.BlockSpec((1, L, DH), lambda s: (3 * H + prev(s) % H, prev(s) // H, 0), pipeline_mode=once),
                  cw(0), cw(H), cw(2 * H),
                  pl.BlockSpec((1, 4 * H, NC, C), lambda s: (cur(s) // H, 0, 0, 0)),
                  pl.BlockSpec((1, DH), lambda s: (0, 0))],
        out_specs=pl.BlockSpec((1, L, DH), lambda s: (prev(s) // H, 0, prev(s) % H)),
        out_shape=jax.ShapeDtypeStruct((B, L, H * DH), F32),
        scratch_shapes=[pltpu.VMEM((L, DH), F32), pltpu.VMEM((L, DH), F32), pltpu.VMEM((L, DH), F32),
                        pltpu.VMEM((L, DH), F32),
                        pltpu.VMEM((2, 2, NC, 2 * C, DH), BF16), pltpu.VMEM((2, 2, NC, C, C), BF16),
                        pltpu.VMEM((2, 2, NC, C, DH), BF16), pltpu.VMEM((2, 2, NC, C, DH), F32),
                        pltpu.VMEM((2, 2, NC, DH), F32), pltpu.VMEM((2, DH, DH), F32),
                        pltpu.VMEM((2, PREP_UNROLL, C, 2 * DH), BF16)],
        compiler_params=_cparams("arbitrary"), name="deltanet",
    )(proj, proj, proj, proj, conv_w, conv_w, conv_w, gt, norm_w.reshape(1, DH))


def _conformer_kernel(val_ref, gate_ref, w_ref, b_ref, g_ref, beta_ref, o_ref, ypad, *, L, K, R, HALO):
    pad = K // 2
    zeros = jnp.zeros((HALO, CV_GROUP), F32)
    ypad[pl.ds(0, HALO), :] = zeros
    ypad[pl.ds(HALO + L, HALO), :] = zeros

    def glu_rows(i, _):
        r0 = pl.multiple_of(i * R, R)
        ypad[pl.ds(HALO + r0, R), :] = val_ref[0, pl.ds(r0, R), :] * _sigmoid(gate_ref[0, pl.ds(r0, R), :])
        return 0

    lax.fori_loop(0, L // R, glu_rows, 0)

    def conv_rows(i, _):
        r0 = pl.multiple_of(i * R, R)
        acc = jnp.zeros((R, CV_GROUP), F32)
        for t in range(K):
            acc = acc + w_ref[t:t + 1, :] * ypad[pl.ds(r0 + (HALO - pad + t), R), :]
        y = _layer_norm(acc + b_ref[...], g_ref[...], beta_ref[...])
        o_ref[0, pl.ds(r0, R), :] = _silu(y)
        return 0

    lax.fori_loop(0, L // R, conv_rows, 0)


def _conformer(proj, w, b, g, beta, *, B, L, col0, width):
    K = w.shape[0]
    n = width // CV_GROUP
    HALO, R = 16, 256
    c0 = col0 // CV_GROUP
    vec = pl.BlockSpec((1, CV_GROUP), lambda bb, j: (0, j))
    kern = functools.partial(_conformer_kernel, L=L, K=K, R=R, HALO=HALO)
    return pl.pallas_call(
        kern, grid=(B, n),
        in_specs=[pl.BlockSpec((1, L, CV_GROUP), lambda bb, j: (c0 + j, bb, 0)),
                  pl.BlockSpec((1, L, CV_GROUP), lambda bb, j: (c0 + n + j, bb, 0)),
                  pl.BlockSpec((K, CV_GROUP), lambda bb, j: (0, j)), vec, vec, vec],
        out_specs=pl.BlockSpec((1, L, CV_GROUP), lambda bb, j: (bb, 0, j)),
        out_shape=jax.ShapeDtypeStruct((B, L, width), F32),
        scratch_shapes=[pltpu.VMEM((L + 2 * HALO, CV_GROUP), F32)],
        compiler_params=_cparams("parallel", "parallel"), name="conformer",
    )(proj, proj, w, b.reshape(1, width), g.reshape(1, width), beta.reshape(1, width))


def _out_router_kernel(dn_ref, cv_ref, h_ref, w_ref, g_ref, b_ref, wr_ref, br_ref,
                       h1_ref, h1p_ref, route_ref, routet_ref, *, alpha, n_dn, sub):
    n_sub = h_ref.shape[0] // sub

    def stages(s):
        rows = pl.ds(s * sub, sub)
        mix = jnp.dot(dn_ref[rows, :].astype(BF16), w_ref[pl.ds(0, n_dn), :], preferred_element_type=F32)
        mix = mix + jnp.dot(cv_ref[rows, :].astype(BF16), w_ref[pl.ds(n_dn, w_ref.shape[0] - n_dn), :],
                            preferred_element_type=F32)
        yield
        h1 = _layer_norm(alpha * h_ref[rows, :] + mix, g_ref[...], b_ref[...])
        h1_ref[rows, :] = h1
        h1p_ref[rows, :] = _pack_halves(h1)
        h1b = h1.astype(BF16)
        yield
        logits = jnp.dot(h1b, wr_ref[...], preferred_element_type=F32) + br_ref[...]
        yield
        route = _route(logits)
        route_ref[rows, :] = route
        routet_ref[:, rows] = route.T[:8, :]
        yield

    n_stage = 4
    pipes = [stages(s) for s in range(n_sub)]
    for t in range(n_sub + n_stage - 1):
        for s in range(t, t - n_stage, -1):
            if 0 <= s < n_sub:
                next(pipes[s])


def _route(logits):
    lane = lax.broadcasted_iota(I32, logits.shape, 1)
    neg = jnp.float32(-jnp.inf)
    big = jnp.int32(LANES)
    gl = jnp.where(lane < N_GROUPS, logits, neg)
    gmax = jnp.max(gl, -1, keepdims=True)
    gsel = jnp.min(jnp.where(gl == gmax, lane, big), -1, keepdims=True)
    pg = 1.0 / jnp.sum(jnp.exp(gl - gmax), -1, keepdims=True)
    lo = N_GROUPS + gsel * EXPERTS_PER_GROUP
    el = jnp.where((lane >= lo) & (lane < lo + EXPERTS_PER_GROUP), logits, neg)
    e1 = jnp.max(el, -1, keepdims=True)
    i1 = jnp.min(jnp.where(el == e1, lane, big), -1, keepdims=True)
    el2 = jnp.where(lane == i1, neg, el)
    e2 = jnp.max(el2, -1, keepdims=True)
    i2 = jnp.min(jnp.where(el2 == e2, lane, big), -1, keepdims=True)
    r = jnp.exp(e2 - e1)
    p1 = 1.0 / (1.0 + r)
    p2 = r * p1
    route = jnp.where(lane == 0, (i1 - N_GROUPS).astype(F32),
                      jnp.where(lane == 1, (i2 - N_GROUPS).astype(F32),
                                jnp.where(lane == 2, pg * p1, jnp.where(lane == 3, pg * p2, 0.0))))
    return route


def _out_router(dn, cv, h, w_out, g, b, w_router, b_router, *, alpha, tm=512, sub=128):
    T, D = h.shape
    n_dn, n_cv = dn.shape[1], cv.shape[1]
    vec = pl.BlockSpec((1, D), lambda i: (0, 0))
    row = pl.BlockSpec((tm, D), lambda i: (i, 0))
    kern = functools.partial(_out_router_kernel, alpha=alpha, n_dn=n_dn, sub=sub)
    return pl.pallas_call(
        kern, grid=(T // tm,),
        in_specs=[pl.BlockSpec((tm, n_dn), lambda i: (i, 0)), pl.BlockSpec((tm, n_cv), lambda i: (i, 0)), row,
                  pl.BlockSpec((n_dn + n_cv, D), lambda i: (0, 0)), vec, vec,
                  pl.BlockSpec((D, LANES), lambda i: (0, 0)), pl.BlockSpec((1, LANES), lambda i: (0, 0))],
        out_specs=[row, pl.BlockSpec((tm, D // 2), lambda i: (i, 0)),
                   pl.BlockSpec((tm, LANES), lambda i: (i, 0)), pl.BlockSpec((8, tm), lambda i: (0, i))],
        out_shape=[jax.ShapeDtypeStruct((T, D), F32), jax.ShapeDtypeStruct((T, D // 2), U32),
                   jax.ShapeDtypeStruct((T, LANES), F32), jax.ShapeDtypeStruct((8, T), F32)],
        compiler_params=_cparams("parallel"), name="out_router",
    )(dn, cv, h, w_out, g.reshape(1, D), b.reshape(1, D), w_router, b_router)


def _onehot_t(routet_ref, tr):
    sub = lax.broadcasted_iota(I32, (2 * N_EXPERTS, tr), 0)
    e1 = routet_ref[0:1, :].astype(I32)
    e2 = routet_ref[1:2, :].astype(I32)
    return sub == jnp.where(sub < N_EXPERTS, e1, e2 + N_EXPERTS)


def _count_kernel(routet_ref, cnt_ref, *, tr):
    @pl.when(pl.program_id(0) == 0)
    def _():
        cnt_ref[...] = jnp.zeros_like(cnt_ref)
    oh = _onehot_t(routet_ref, tr).astype(F32)
    cnt_ref[...] += jnp.sum(oh, axis=1, keepdims=True)


def _place_kernel(routet_ref, base_ref, dest_ref, carry, *, tr):
    @pl.when(pl.program_id(0) == 0)
    def _():
        carry[...] = jnp.zeros_like(carry)
    oh = _onehot_t(routet_ref, tr)
    ohb = oh.astype(F32).astype(BF16)
    ri = lax.broadcasted_iota(I32, (tr, tr), 0)
    ci = lax.broadcasted_iota(I32, (tr, tr), 1)
    before = (ri < ci).astype(F32).astype(BF16)
    excl = jnp.dot(ohb, before, preferred_element_type=F32)
    pos = jnp.where(oh, excl + carry[...] + base_ref[...], 0.0)
    d1 = jnp.sum(pos[:N_EXPERTS], axis=0, keepdims=True)
    d2 = jnp.sum(pos[N_EXPERTS:], axis=0, keepdims=True)
    sub = lax.broadcasted_iota(I32, (8, tr), 0)
    dest_ref[...] = jnp.where(sub == 0, d1, jnp.where(sub == 1, d2, 0.0)).astype(I32)
    carry[...] += jnp.sum(oh.astype(F32), axis=1, keepdims=True)


def _placement(routet, tr=512):
    T = routet.shape[1]
    E = N_EXPERTS
    rt = pl.BlockSpec((8, tr), lambda i: (0, i))
    col = pl.BlockSpec((2 * E, 1), lambda i: (0, 0))
    cnt = pl.pallas_call(
        functools.partial(_count_kernel, tr=tr), grid=(T // tr,), in_specs=[rt], out_specs=col,
        out_shape=jax.ShapeDtypeStruct((2 * E, 1), F32),
        compiler_params=_cparams("arbitrary"), name="moe_count")(routet)
    c1 = cnt[:E, 0].astype(I32)
    c2 = cnt[E:, 0].astype(I32)
    padded = ((c1 + c2 + FFN_BLOCK - 1) // FFN_BLOCK) * FFN_BLOCK
    pend = jnp.cumsum(padded)
    pstart = pend - padded
    base = jnp.concatenate([pstart, pstart + c1]).astype(F32).reshape(2 * E, 1)
    dest = pl.pallas_call(
        functools.partial(_place_kernel, tr=tr), grid=(T // tr,), in_specs=[rt, col], out_specs=rt,
        out_shape=jax.ShapeDtypeStruct((8, T), I32),
        scratch_shapes=[pltpu.VMEM((2 * E, 1), F32)],
        compiler_params=_cparams("arbitrary"), name="moe_place")(routet, base)
    n_blocks = (T * TOP_K + E * (FFN_BLOCK - 1) + FFN_BLOCK - 1) // FFN_BLOCK
    starts = jnp.arange(n_blocks, dtype=I32) * FFN_BLOCK
    blk_e = jnp.minimum(jnp.sum((pend[None, :] <= starts[:, None]).astype(I32), axis=1), E - 1)
    blk_first = jnp.concatenate([jnp.ones((1,), I32), (blk_e[1:] != blk_e[:-1]).astype(I32)])
    n_used = (pend[-1] // FFN_BLOCK).astype(I32).reshape(1)
    ids = jnp.arange(E, dtype=I32)
    later = (padded > 0)[None, :] & (ids[None, :] > ids[:, None])
    next_tab = jnp.min(jnp.where(later, ids[None, :], E), axis=1)
    blk_next = jnp.where(next_tab == E, -1, next_tab)[blk_e]
    return dest[0], dest[1], pend.astype(I32), blk_e, blk_first, blk_next, n_used, n_blocks


def _dispatch_kernel(d1_ref, d2_ref, pend_ref, src_ref, xs_ref, zbuf, sem, zsem, *, tb):
    @pl.when(pl.program_id(0) == 0)
    def _():
        zbuf[...] = jnp.zeros_like(zbuf)
        tails = []
        for e in range(N_EXPERTS):
            end = pend_ref[e]
            nonempty = end > (pend_ref[e - 1] if e else 0)
            start = pl.multiple_of(jnp.maximum(end - FFN_BLOCK, 0), FFN_BLOCK)
            tails.append((nonempty, pltpu.make_async_copy(zbuf, xs_ref.at[pl.ds(start, FFN_BLOCK), :], zsem)))
        for nonempty, copy in tails:
            pl.when(nonempty)(copy.start)

        def spare(b):
            return pltpu.make_async_copy(
                zbuf, xs_ref.at[pl.ds(pl.multiple_of(b * FFN_BLOCK, FFN_BLOCK), FFN_BLOCK), :], zsem)
        first_spare = pend_ref[N_EXPERTS - 1] // FFN_BLOCK
        n_blocks = xs_ref.shape[0] // FFN_BLOCK
        lax.fori_loop(first_spare, n_blocks, lambda b, c: (spare(b).start(), c)[1], 0)
        for nonempty, copy in tails:
            pl.when(nonempty)(copy.wait)
        lax.fori_loop(first_spare, n_blocks, lambda b, c: (spare(b).wait(), c)[1], 0)

    def issue(t, _):
        row = src_ref.at[pl.ds(t, 1), :]
        pltpu.make_async_copy(row, xs_ref.at[pl.ds(d1_ref[t], 1), :], sem).start(priority=0)
        pltpu.make_async_copy(row, xs_ref.at[pl.ds(d2_ref[t], 1), :], sem).start(priority=1)
        return 0

    lax.fori_loop(0, tb, issue, 0, unroll=8)
    pltpu.make_async_copy(src_ref, xs_ref.at[pl.ds(0, tb), :], sem).wait()
    pltpu.make_async_copy(src_ref, xs_ref.at[pl.ds(0, tb), :], sem).wait()


def _dispatch(h1p, d1, d2, pend, n_rows, tb=1024):
    T, W = h1p.shape
    idx = pl.BlockSpec((tb,), lambda i: (i,), memory_space=pltpu.SMEM)
    return pl.pallas_call(
        functools.partial(_dispatch_kernel, tb=tb), grid=(T // tb,),
        in_specs=[idx, idx, pl.BlockSpec((N_EXPERTS,), lambda i: (0,), memory_space=pltpu.SMEM),
                  pl.BlockSpec((tb, W), lambda i: (i, 0))],
        out_specs=pl.BlockSpec(memory_space=pl.ANY),
        out_shape=jax.ShapeDtypeStruct((n_rows, W), U32),
        scratch_shapes=[pltpu.VMEM((FFN_BLOCK, W), U32), pltpu.SemaphoreType.DMA(()), pltpu.SemaphoreType.DMA(())],
        compiler_params=_cparams("arbitrary"), name="moe_dispatch",
    )(d1, d2, pend, h1p)


def _ffn_kernel(blk_e_ref, blk_first_ref, blk_next_ref, n_used_ref, xs_ref, wgu_hbm, wdn_hbm, y_ref,
                wgu_f, wdn_f, wgu_s, wdn_s, sem, *, ff, layer):
    i = pl.program_id(0)
    used = i < n_used_ref[0]

    def weight_copies(e):
        copies = []
        for k, (src, dst) in enumerate(((wgu_hbm, wgu_f), (wdn_hbm, wdn_f))):
            rows = dst.shape[0] // FFN_WEIGHT_CHUNKS
            for c in range(FFN_WEIGHT_CHUNKS):
                r = pl.ds(c * rows, rows)
                copies.append(pltpu.make_async_copy(src.at[layer, e, r, :], dst.at[r, :],
                                                    sem.at[k * FFN_WEIGHT_CHUNKS + c]))
        return copies

    @pl.when(i == 0)
    def _():
        for c in weight_copies(blk_e_ref[0]):
            c.start(priority=1)

    @pl.when(used & (blk_first_ref[i] == 1))
    def _():
        for c in weight_copies(blk_e_ref[i]):
            c.wait()
        wgu_s[...] = wgu_f[...].astype(BF16)
        wdn_s[...] = wdn_f[...].astype(BF16)

        @pl.when(blk_next_ref[i] >= 0)
        def _():
            for c in weight_copies(blk_next_ref[i]):
                c.start(priority=1)

    @pl.when(used)
    def _():
        lo, hi = _unpack_halves(xs_ref[...])
        x = jnp.concatenate([lo, hi], axis=1).astype(BF16)
        gu = jnp.dot(x, wgu_s[...], preferred_element_type=F32)
        act = (_silu(gu[:, :ff]) * gu[:, ff:]).astype(BF16)
        y_ref[...] = _pack_halves(jnp.dot(act, wdn_s[...], preferred_element_type=F32))

    @pl.when(i >= n_used_ref[0])
    def _():
        y_ref[...] = jnp.zeros_like(y_ref)


def _ffn(xs, w_gu, w_dn, layer, blk_e, blk_first, blk_next, n_used, n_blocks):
    _, E, D, FF2 = w_gu.shape
    ff = FF2 // 2
    W = xs.shape[1]
    any_spec = pl.BlockSpec(memory_space=pl.ANY)
    grid_spec = pltpu.PrefetchScalarGridSpec(
        num_scalar_prefetch=4, grid=(n_blocks,),
        in_specs=[pl.BlockSpec((FFN_BLOCK, W), lambda i, be, bf, bn, nu: (jnp.minimum(i, nu[0] - 1), 0)),
                  any_spec, any_spec],
        out_specs=pl.BlockSpec((FFN_BLOCK, W), lambda i, *_: (i, 0)),
        scratch_shapes=[pltpu.VMEM((D, FF2), F32), pltpu.VMEM((ff, D), F32),
                        pltpu.VMEM((D, FF2), BF16), pltpu.VMEM((ff, D), BF16),
                        pltpu.SemaphoreType.DMA((2 * FFN_WEIGHT_CHUNKS,))])
    return pl.pallas_call(
        functools.partial(_ffn_kernel, ff=ff, layer=layer), grid_spec=grid_spec,
        out_shape=jax.ShapeDtypeStruct(xs.shape, U32),
        compiler_params=_cparams("arbitrary"), name="moe_ffn",
    )(blk_e, blk_first, blk_next, n_used, xs, w_gu, w_dn)


def _combine_kernel(d1_ref, d2_ref, d1n_ref, d2n_ref, y_ref, h1_ref, route_ref, g_ref, b_ref, h2_ref, h2b_ref,
                    ybuf, sem, *, tc, alpha):
    i = pl.program_id(0)
    slot = i % 2

    def row_copies(r1, r2, s, t):
        return (pltpu.make_async_copy(y_ref.at[pl.ds(r1[t], 1), :], ybuf.at[s, 0, pl.ds(t, 1), :], sem.at[s]),
                pltpu.make_async_copy(y_ref.at[pl.ds(r2[t], 1), :], ybuf.at[s, 1, pl.ds(t, 1), :], sem.at[s]))

    def wait_slot(s):
        for j in range(TOP_K):
            pltpu.make_async_copy(y_ref.at[pl.ds(0, tc), :], ybuf.at[s, j], sem.at[s]).wait()

    @pl.when(i == 0)
    def _():
        def issue(t, _):
            for j, c in enumerate(row_copies(d1_ref, d2_ref, 0, t)):
                c.start(priority=j)
            return 0
        lax.fori_loop(0, tc, issue, 0, unroll=8)

    wait_slot(slot)
    for t in range(tc):
        for j, c in enumerate(row_copies(d1n_ref, d2n_ref, 1 - slot, t)):
            c.start(priority=j)

    route = route_ref[...]
    g1 = route[:, 2:3]
    g2 = route[:, 3:4]
    lo1, hi1 = _unpack_halves(ybuf[slot, 0])
    lo2, hi2 = _unpack_halves(ybuf[slot, 1])
    ffn = jnp.concatenate([g1 * lo1 + g2 * lo2, g1 * hi1 + g2 * hi2], axis=1)
    h2 = _layer_norm(alpha * h1_ref[...] + ffn, g_ref[...], b_ref[...])
    h2_ref[...] = h2
    h2b_ref[...] = h2.astype(BF16)

    @pl.when(i == pl.num_programs(0) - 1)
    def _():
        wait_slot(1 - slot)


def _combine(y, d1, d2, h1, route, g, b, *, alpha, tc=256):
    T, D = h1.shape
    W = y.shape[1]
    n = T // tc
    idx = pl.BlockSpec((tc,), lambda i: (i,), memory_space=pltpu.SMEM)
    idx_next = pl.BlockSpec((tc,), lambda i: (jnp.minimum(i + 1, n - 1),), memory_space=pltpu.SMEM)
    row = pl.BlockSpec((tc, D), lambda i: (i, 0))
    vec = pl.BlockSpec((1, D), lambda i: (0, 0))
    return pl.pallas_call(
        functools.partial(_combine_kernel, tc=tc, alpha=alpha), grid=(n,),
        in_specs=[idx, idx, idx_next, idx_next, pl.BlockSpec(memory_space=pl.ANY), row,
                  pl.BlockSpec((tc, LANES), lambda i: (i, 0)), vec, vec],
        out_specs=[row, row],
        out_shape=[jax.ShapeDtypeStruct((T, D), F32), jax.ShapeDtypeStruct((T, D), BF16)],
        scratch_shapes=[pltpu.VMEM((2, TOP_K, tc, W), U32), pltpu.SemaphoreType.DMA((2,))],
        compiler_params=_cparams("arbitrary"), name="moe_combine",
    )(d1, d2, d1, d2, y, h1, route, g.reshape(1, D), b.reshape(1, D))


def kernel(x, emb_ln_g, emb_ln_b, w_in, short_conv_w, a_log, dt_bias, dn_norm_w, dw_conv_w, dw_conv_b,
           conv_ln_g, conv_ln_b, w_out, ln1_g, ln1_b, w_group, b_group, w_expert, b_expert, w_gate_up,
           w_down, ln2_g, ln2_b):
    B, L, D = x.shape
    T = B * L
    depth = w_in.shape[0]
    H = a_log.shape[2]
    dn_w = H * DN_HEAD_DIM
    cv_w = dw_conv_w.shape[2]
    n_ab = 4 * H
    alpha = (2 * depth) ** 0.25
    C = DN_CHUNK

    h, hb = _emb_ln(x.reshape(T, D), emb_ln_g, emb_ln_b)
    for l in range(depth):
        w = w_in[l]
        w_main = jnp.concatenate([w[:, :4 * dn_w], w[:, 4 * dn_w + n_ab:]], axis=1).astype(BF16)
        proj = _in_proj(hb, w_main)
        gb = _decay_beta(hb, w[:, 4 * dn_w:4 * dn_w + n_ab], a_log[l], dt_bias[l])
        gt = gb[:, :n_ab].reshape(B, L // C, C, n_ab).transpose(0, 3, 1, 2)
        dn = _deltanet(proj, gt, short_conv_w[l], dn_norm_w[l], B=B, L=L, H=H)
        cv = _conformer(proj, dw_conv_w[l], dw_conv_b[l], conv_ln_g[l], conv_ln_b[l],
                        B=B, L=L, col0=4 * dn_w, width=cv_w)
        n_r = N_GROUPS + N_EXPERTS
        w_router = jnp.pad(jnp.concatenate([w_group[l], w_expert[l]], axis=1),
                           ((0, 0), (0, LANES - n_r))).astype(BF16)
        b_router = jnp.pad(jnp.concatenate([b_group[l], b_expert[l]]), (0, LANES - n_r)).reshape(1, LANES)
        h1, h1p, route, routet = _out_router(
            dn.reshape(T, dn_w), cv.reshape(T, cv_w), h, w_out[l].astype(BF16), ln1_g[l], ln1_b[l],
            w_router, b_router, alpha=alpha)
        d1, d2, pend, blk_e, blk_first, blk_next, n_used, n_blocks = _placement(routet)
        xs = _dispatch(h1p, d1, d2, pend, n_blocks * FFN_BLOCK)
        y = _ffn(xs, w_gate_up, w_down, l, blk_e, blk_first, blk_next, n_used, n_blocks)
        h, hb = _combine(y, d1, d2, h1, route, ln2_g[l], ln2_b[l], alpha=alpha)
    return h.reshape(B, L, D)
```

```python
import functools
import itertools

import jax
import jax.numpy as jnp
from jax import lax
from jax.experimental import pallas as pl
from jax.experimental.pallas import tpu as pltpu

F32 = jnp.float32
BF16 = jnp.bfloat16
U32 = jnp.uint32
I32 = jnp.int32

LANES = 128
DN_HEAD_DIM = 128
DN_CHUNK = 128
PREP_UNROLL = 8
CV_GROUP = 128
N_GROUPS = 8
EXPERTS_PER_GROUP = 8
N_EXPERTS = N_GROUPS * EXPERTS_PER_GROUP
TOP_K = 2
FFN_BLOCK = 256
FFN_WEIGHT_CHUNKS = 4
LN_EPS = 1e-5
RMS_EPS = 1e-6
VMEM_LIMIT = 56 * 1024 * 1024


def _cparams(*sem):
    return pltpu.CompilerParams(dimension_semantics=sem, vmem_limit_bytes=VMEM_LIMIT)


def _layer_norm(x, g, b):
    mu = jnp.mean(x, -1, keepdims=True)
    xc = x - mu
    var = jnp.mean(xc * xc, -1, keepdims=True)
    return xc * lax.rsqrt(var + LN_EPS) * g + b


def _sigmoid(x):
    return 1.0 / (1.0 + jnp.exp(-x))


def _silu(x):
    return x * _sigmoid(x)


def _pack_halves(y):
    n = y.shape[1] // 2
    bits = lax.bitcast_convert_type(y.astype(BF16).astype(F32), U32)
    return (bits[:, :n] >> 16) | bits[:, n:]


def _unpack_halves(u):
    lo = lax.bitcast_convert_type(u << 16, F32)
    hi = lax.bitcast_convert_type(u & jnp.uint32(0xFFFF0000), F32)
    return lo, hi


def _in_proj_kernel(*refs, ln, n_decay, n_gate_rows):
    if ln:
        x_ref, g_ref, b_ref, w_ref, wab_ref, alog_ref, dtb_ref, o_ref, gt_ref, h_ref, hb_ref, xb = refs
    else:
        xb, w_ref, wab_ref, alog_ref, dtb_ref, o_ref, gt_ref = refs

    @pl.when(pl.program_id(1) == 0)
    def _():
        if ln:
            h = _layer_norm(x_ref[...], g_ref[...], b_ref[...])
            h_ref[...] = h
            xb[...] = h.astype(BF16)
            hb_ref[...] = xb[...]
        ab = jnp.dot(xb[...], wab_ref[...], preferred_element_type=F32)
        lane = lax.broadcasted_iota(I32, ab.shape, 1)
        sp = ab + dtb_ref[...]
        softplus = jnp.maximum(sp, 0.0) + jnp.log(1.0 + jnp.exp(-jnp.abs(sp)))
        decay = -jnp.exp(alog_ref[...]) * softplus
        gt_ref[0] = jnp.where(lane < n_decay, decay, _sigmoid(ab)).T[:n_gate_rows, :]

    acc = jnp.dot(xb[...], w_ref[...], preferred_element_type=F32)
    for j in range(o_ref.shape[0]):
        o_ref[j] = acc[:, j * LANES:(j + 1) * LANES]


def _in_proj(x, w, w_ab, a_log, dt_bias, *, B, L, ln_params=None, tn=1024):
    T, D = x.shape
    N = w.shape[1]
    ln = ln_params is not None
    tm = 512 if ln else 1024
    n = a_log.size
    n_gate_rows = 2 * n
    pad = lambda v: jnp.pad(v.reshape(1, n).astype(F32), ((0, 0), (0, LANES - n)))
    wab = jnp.pad(w_ab, ((0, 0), (0, LANES - w_ab.shape[1]))).astype(BF16)
    row = pl.BlockSpec((tm, D), lambda i, j: (i, 0))
    vec_d = pl.BlockSpec((1, D), lambda i, j: (0, 0))
    vec = pl.BlockSpec((1, LANES), lambda i, j: (0, 0))
    per_batch = L // tm
    in_specs = [row] + ([vec_d, vec_d] if ln else []) + [
        pl.BlockSpec((D, tn), lambda i, j: (0, j)), pl.BlockSpec((D, LANES), lambda i, j: (0, 0)), vec, vec]
    out_specs = [pl.BlockSpec((tn // LANES, tm, LANES), lambda i, j: (j, i, 0)),
                 pl.BlockSpec((1, n_gate_rows, tm), lambda i, j: (i // per_batch, 0, i % per_batch))]
    out_shape = [jax.ShapeDtypeStruct((N // LANES, T, LANES), F32),
                 jax.ShapeDtypeStruct((B, n_gate_rows, L), F32)]
    args = (x,) + ((ln_params[0].reshape(1, D), ln_params[1].reshape(1, D)) if ln else ()) + (
        w, wab, pad(a_log), pad(dt_bias))
    if ln:
        out_specs += [row, row]
        out_shape += [jax.ShapeDtypeStruct((T, D), F32), jax.ShapeDtypeStruct((T, D), BF16)]
    return pl.pallas_call(
        functools.partial(_in_proj_kernel, ln=ln, n_decay=n, n_gate_rows=n_gate_rows),
        grid=(T // tm, N // tn), in_specs=in_specs, out_specs=out_specs, out_shape=out_shape,
        scratch_shapes=[pltpu.VMEM((tm, D), BF16)] if ln else [],
        compiler_params=_cparams("parallel", "arbitrary"), name="in_proj")(*args)


def _dot(a, b):
    return jnp.dot(a.astype(BF16), b.astype(BF16), preferred_element_type=F32)


def _dot_nt(a, b):
    return lax.dot_general(a.astype(BF16), b.astype(BF16), (((1,), (1,)), ((), ())),
                           preferred_element_type=F32)


def _dot_tn(a, b):
    return lax.dot_general(a.astype(BF16), b.astype(BF16), (((0,), (0,)), ((), ())),
                           preferred_element_type=F32)


def _deltanet_kernel(q_ref, k_ref, v_ref, z_ref, cq_ref, ck_ref, cv_ref, gt_ref, nw_ref, o_ref,
                     qs, ks, vs, ob, wq_s, at_s, kd_s, u_s, gl_s, s_s, rhs_s, *, L, C, H, R, n_items):
    DH = DN_HEAD_DIM
    s = pl.program_id(0)
    h = jnp.minimum(s, n_items - 1) % H
    slot_p = s % 2
    slot_s = 1 - slot_p
    NC = L // C
    U = PREP_UNROLL

    def conv_rows(r0, first, last):
        rows = pl.ds(r0, R)
        rid = lax.broadcasted_iota(I32, (R, DH), 0)

        def conv_silu(ref, w_ref):
            x = ref[0, rows, :]
            if first:
                x_prev = jnp.where(rid == 0, 0.0, pltpu.roll(x, 1, 0))
            else:
                x_prev = ref[0, pl.ds(r0 - 1, R), :]
            if last:
                x_next = jnp.where(rid == R - 1, 0.0, pltpu.roll(x, R - 1, 0))
            else:
                x_next = ref[0, pl.ds(r0 + 1, R), :]
            return _silu(w_ref[0:1, :] * x_prev + w_ref[1:2, :] * x + w_ref[2:3, :] * x_next)

        q = conv_silu(q_ref, cq_ref)
        k = conv_silu(k_ref, ck_ref)
        qs[rows, :] = q * (lax.rsqrt(jnp.sum(q * q, -1, keepdims=True) + RMS_EPS) * (DH ** -0.5))
        ks[rows, :] = k * lax.rsqrt(jnp.sum(k * k, -1, keepdims=True) + RMS_EPS)
        vs[rows, :] = conv_silu(v_ref, cv_ref)

    @pl.when(s < n_items)
    def _():
        conv_rows(0, True, False)
        lax.fori_loop(1, L // R - 1, lambda i, c: (conv_rows(pl.multiple_of(i * R, R), False, False), c)[1], 0)
        conv_rows(L - R, False, True)

    ri = lax.broadcasted_iota(I32, (C, C), 0)
    ci = lax.broadcasted_iota(I32, (C, C), 1)
    eye = ri == ci
    eye_f = eye.astype(F32)
    incl = (ri >= ci, ri <= ci)
    strict = (ri > ci, ri < ci)
    n_sq = C.bit_length() - 2
    zero_b = jnp.zeros((C, C), BF16)

    def block_diag(a, b):
        return jnp.concatenate([jnp.concatenate([a, zero_b], axis=1),
                                jnp.concatenate([zero_b, b], axis=1)], axis=0)

    def prep_start(c, j):
        rows = pl.ds(pl.multiple_of(c * C, C), C)
        q = qs[rows, :]
        k = ks[rows, :]
        v = vs[rows, :]
        kq = _dot_nt(jnp.concatenate([k, q], axis=0), k)
        kk, qk = kq[:C], kq[C:]
        a2 = []
        for d in range(2):
            g_row = gt_ref[0, d * H + h, pl.ds(c, 1), :]
            b_row = gt_ref[0, (2 + d) * H + h, pl.ds(c, 1), :]
            m, ms, mt = incl[d], strict[d], incl[1 - d]
            g_col = jnp.sum(jnp.where(eye, g_row, 0.0), axis=1, keepdims=True)
            b_col = jnp.sum(jnp.where(eye, b_row, 0.0), axis=1, keepdims=True)
            gc_col = jnp.sum(jnp.where(m, g_row, 0.0), axis=1, keepdims=True)
            gc_row = jnp.sum(jnp.where(mt, g_col, 0.0), axis=0, keepdims=True)
            tot = jnp.sum(g_row, axis=1, keepdims=True)
            decay = jnp.where(m, jnp.exp(jnp.where(m, gc_col - gc_row, 0.0)), 0.0)
            a2.append(jnp.where(ms, -(kk * b_col * decay), 0.0))
            egc = jnp.exp(gc_col)
            rhs_s[d, j] = jnp.concatenate([v * b_col, k * (b_col * egc)], axis=1).astype(BF16)
            wq_s[slot_p, d, c, pl.ds(C, C), :] = (q * egc).astype(BF16)
            at_s[slot_p, d, c] = (qk * decay).astype(BF16)
            kd_s[slot_p, d, c] = (k * jnp.exp(tot - gc_col)).astype(BF16)
            gl_s[slot_p, d, pl.ds(c, 1), :] = jnp.broadcast_to(jnp.exp(tot), (1, DH))
        return jnp.concatenate(a2, axis=1)

    def prep_phases(i):
        cs = [i * U + j for j in range(U)]
        ps = []
        for j, c in enumerate(cs):
            ps.append(prep_start(c, j))
            if j == U // 2 - 1:
                yield
        eye2 = jnp.concatenate([eye_f, eye_f], axis=1)
        xs_ = [eye2 + p for p in ps]
        yield
        for _ in range(n_sq):
            pbs = [p.astype(BF16) for p in ps]
            ps = [jnp.dot(pb, block_diag(pb[:, :C], pb[:, C:]), preferred_element_type=F32) for pb in pbs]
            yield
            pbs = [p.astype(BF16) for p in ps]
            xs_ = [x + jnp.dot(x.astype(BF16), block_diag(pb[:, :C], pb[:, C:]), preferred_element_type=F32)
                   for x, pb in zip(xs_, pbs)]
            yield
        for j, (c, x) in enumerate(zip(cs, xs_)):
            for d in range(2):
                sol = jnp.dot(x[:, d * C:(d + 1) * C].astype(BF16), rhs_s[d, j], preferred_element_type=F32)
                u_s[slot_p, d, c] = sol[:, :DH]
                wq_s[slot_p, d, c, pl.ds(0, C), :] = sol[:, DH:].astype(BF16)
            if j == U // 2 - 1:
                yield
        yield

    def scan_stages(n):
        cf, cb = n, NC - 1 - n
        s_f, s_b = s_s[0], s_s[1]
        wq = jnp.concatenate([wq_s[slot_s, 0, cf], wq_s[slot_s, 1, cb]], axis=1)
        r1 = jnp.dot(wq, block_diag(s_f.astype(BF16), s_b.astype(BF16)), preferred_element_type=F32)
        yield
        u = jnp.concatenate([u_s[slot_s, 0, cf], u_s[slot_s, 1, cb]], axis=1)
        v_new = (u - r1[:C]).astype(BF16)
        vd = block_diag(v_new[:, :DH], v_new[:, DH:])
        at = jnp.concatenate([at_s[slot_s, 0, cf], at_s[slot_s, 1, cb]], axis=1)
        o = r1[C:] + jnp.dot(at, vd, preferred_element_type=F32)
        kd = jnp.concatenate([kd_s[slot_s, 0, cf], kd_s[slot_s, 1, cb]], axis=0)
        ds = lax.dot_general(kd, vd, (((0,), (0,)), ((), ())), preferred_element_type=F32)
        s_s[0] = s_f * gl_s[slot_s, 0, pl.ds(cf, 1), :] + ds[:, :DH]
        s_s[1] = s_b * gl_s[slot_s, 1, pl.ds(cb, 1), :] + ds[:, DH:]
        o_ref[0, pl.ds(pl.multiple_of(cf * C, C), C), :] = o[:, :DH]
        ob[pl.ds(pl.multiple_of(cb * C, C), C), :] = o[:, DH:]
        yield

    def run(prepare, scan):
        def trip(i, _):
            prep_it = prep_phases(i) if prepare else iter(())
            scan_it = itertools.chain.from_iterable(scan_stages(i * U + j) for j in range(U)) if scan else iter(())
            for _ in itertools.zip_longest(scan_it, prep_it):
                pass
            return 0
        if scan:
            s_s[...] = jnp.zeros_like(s_s)
        lax.fori_loop(0, NC // U, trip, 0)

    @pl.when(s == 0)
    def _():
        run(True, False)

    @pl.when((s > 0) & (s < n_items))
    def _():
        run(True, True)

    @pl.when(s == n_items)
    def _():
        run(False, True)

    def gate_rows(i, _):
        rows = pl.ds(pl.multiple_of(i * R, R), R)
        o = o_ref[0, rows, :] + ob[rows, :]
        o = o * lax.rsqrt(jnp.mean(o * o, -1, keepdims=True) + RMS_EPS) * nw_ref[...]
        o_ref[0, rows, :] = o * _silu(z_ref[0, rows, :])
        return 0

    @pl.when(s > 0)
    def _():
        lax.fori_loop(0, L // R, gate_rows, 0)


def _deltanet(proj, gt, conv_w, norm_w, *, B, L, H):
    DH, C, R = DN_HEAD_DIM, DN_CHUNK, 256
    assert C == DH and L % (C * PREP_UNROLL) == 0
    NC = L // C
    n_items = B * H
    cur = lambda s: jnp.minimum(s, n_items - 1)
    prev = lambda s: jnp.maximum(s - 1, 0)
    once = pl.Buffered(1)
    col = lambda off: pl.BlockSpec((1, L, DH), lambda s: (off + cur(s) % H, cur(s) // H, 0))
    cw = lambda off: pl.BlockSpec((3, DH), lambda s: (0, off + cur(s) % H))
    kern = functools.partial(_deltanet_kernel, L=L, C=C, H=H, R=R, n_items=n_items)
    return pl.pallas_call(
        kern, grid=(n_items + 1,),
        in_specs=[col(0), col(H), col(2 * H),
                  pl.BlockSpec((1, L, DH), lambda s: (3 * H + prev(s) % H, prev(s) // H, 0), pipeline_mode=once),
                  cw(0), cw(H), cw(2 * H),
                  pl.BlockSpec((1, 4 * H, NC, C), lambda s: (cur(s) // H, 0, 0, 0)),
                  pl.BlockSpec((1, DH), lambda s: (0, 0))],
        out_specs=pl.BlockSpec((1, L, DH), lambda s: (prev(s) // H, 0, prev(s) % H)),
        out_shape=jax.ShapeDtypeStruct((B, L, H * DH), F32),
        scratch_shapes=[pltpu.VMEM((L, DH), F32), pltpu.VMEM((L, DH), F32), pltpu.VMEM((L, DH), F32),
                        pltpu.VMEM((L, DH), F32),
                        pltpu.VMEM((2, 2, NC, 2 * C, DH), BF16), pltpu.VMEM((2, 2, NC, C, C), BF16),
                        pltpu.VMEM((2, 2, NC, C, DH), BF16), pltpu.VMEM((2, 2, NC, C, DH), F32),
                        pltpu.VMEM((2, 2, NC, DH), F32), pltpu.VMEM((2, DH, DH), F32),
                        pltpu.VMEM((2, PREP_UNROLL, C, 2 * DH), BF16)],
        compiler_params=_cparams("arbitrary"), name="deltanet",
    )(proj, proj, proj, proj, conv_w, conv_w, conv_w, gt, norm_w.reshape(1, DH))


def _conformer_kernel(val_ref, gate_ref, w_ref, b_ref, g_ref, beta_ref, o_ref, ypad, *, L, K, R, HALO):
    pad = K // 2
    zeros = jnp.zeros((HALO, CV_GROUP), F32)
    ypad[pl.ds(0, HALO), :] = zeros
    ypad[pl.ds(HALO + L, HALO), :] = zeros

    def glu_rows(i, _):
        r0 = pl.multiple_of(i * R, R)
        ypad[pl.ds(HALO + r0, R), :] = val_ref[0, pl.ds(r0, R), :] * _sigmoid(gate_ref[0, pl.ds(r0, R), :])
        return 0

    lax.fori_loop(0, L // R, glu_rows, 0)

    def conv_rows(i, _):
        r0 = pl.multiple_of(i * R, R)
        acc = jnp.zeros((R, CV_GROUP), F32)
        for t in range(K):
            acc = acc + w_ref[t:t + 1, :] * ypad[pl.ds(r0 + (HALO - pad + t), R), :]
        y = _layer_norm(acc + b_ref[...], g_ref[...], beta_ref[...])
        o_ref[0, pl.ds(r0, R), :] = _silu(y)
        return 0

    lax.fori_loop(0, L // R, conv_rows, 0)


def _conformer(proj, w, b, g, beta, *, B, L, col0, width):
    K = w.shape[0]
    n = width // CV_GROUP
    HALO, R = 16, 256
    c0 = col0 // CV_GROUP
    vec = pl.BlockSpec((1, CV_GROUP), lambda bb, j: (0, j))
    kern = functools.partial(_conformer_kernel, L=L, K=K, R=R, HALO=HALO)
    return pl.pallas_call(
        kern, grid=(B, n),
        in_specs=[pl.BlockSpec((1, L, CV_GROUP), lambda bb, j: (c0 + j, bb, 0)),
                  pl.BlockSpec((1, L, CV_GROUP), lambda bb, j: (c0 + n + j, bb, 0)),
                  pl.BlockSpec((K, CV_GROUP), lambda bb, j: (0, j)), vec, vec, vec],
        out_specs=pl.BlockSpec((1, L, CV_GROUP), lambda bb, j: (bb, 0, j)),
        out_shape=jax.ShapeDtypeStruct((B, L, width), F32),
        scratch_shapes=[pltpu.VMEM((L + 2 * HALO, CV_GROUP), F32)],
        compiler_params=_cparams("parallel", "parallel"), name="conformer",
    )(proj, proj, w, b.reshape(1, width), g.reshape(1, width), beta.reshape(1, width))


def _out_router_kernel(dn_ref, cv_ref, h_ref, w_ref, g_ref, b_ref, wr_ref, br_ref,
                       h1_ref, h1p_ref, route_ref, routet_ref, *, alpha, n_dn, sub):
    n_sub = h_ref.shape[0] // sub

    def stages(s):
        rows = pl.ds(s * sub, sub)
        mix = jnp.dot(dn_ref[rows, :].astype(BF16), w_ref[pl.ds(0, n_dn), :], preferred_element_type=F32)
        mix = mix + jnp.dot(cv_ref[rows, :].astype(BF16), w_ref[pl.ds(n_dn, w_ref.shape[0] - n_dn), :],
                            preferred_element_type=F32)
        yield
        h1 = _layer_norm(alpha * h_ref[rows, :] + mix, g_ref[...], b_ref[...])
        h1_ref[rows, :] = h1
        h1p_ref[rows, :] = _pack_halves(h1)
        h1b = h1.astype(BF16)
        yield
        logits = jnp.dot(h1b, wr_ref[...], preferred_element_type=F32) + br_ref[...]
        yield
        route = _route(logits)
        route_ref[rows, :] = route
        routet_ref[:, rows] = route.T[:8, :]
        yield

    n_stage = 4
    pipes = [stages(s) for s in range(n_sub)]
    for t in range(n_sub + n_stage - 1):
        for s in range(t, t - n_stage, -1):
            if 0 <= s < n_sub:
                next(pipes[s])


def _route(logits):
    lane = lax.broadcasted_iota(I32, logits.shape, 1)
    neg = jnp.float32(-jnp.inf)
    big = jnp.int32(LANES)
    gl = jnp.where(lane < N_GROUPS, logits, neg)
    gmax = jnp.max(gl, -1, keepdims=True)
    gsel = jnp.min(jnp.where(gl == gmax, lane, big), -1, keepdims=True)
    pg = 1.0 / jnp.sum(jnp.exp(gl - gmax), -1, keepdims=True)
    lo = N_GROUPS + gsel * EXPERTS_PER_GROUP
    el = jnp.where((lane >= lo) & (lane < lo + EXPERTS_PER_GROUP), logits, neg)
    e1 = jnp.max(el, -1, keepdims=True)
    i1 = jnp.min(jnp.where(el == e1, lane, big), -1, keepdims=True)
    el2 = jnp.where(lane == i1, neg, el)
    e2 = jnp.max(el2, -1, keepdims=True)
    i2 = jnp.min(jnp.where(el2 == e2, lane, big), -1, keepdims=True)
    r = jnp.exp(e2 - e1)
    p1 = 1.0 / (1.0 + r)
    p2 = r * p1
    route = jnp.where(lane == 0, (i1 - N_GROUPS).astype(F32),
                      jnp.where(lane == 1, (i2 - N_GROUPS).astype(F32),
                                jnp.where(lane == 2, pg * p1, jnp.where(lane == 3, pg * p2, 0.0))))
    return route


def _out_router(dn, cv, h, w_out, g, b, w_router, b_router, *, alpha, tm=512, sub=128):
    T, D = h.shape
    n_dn, n_cv = dn.shape[1], cv.shape[1]
    vec = pl.BlockSpec((1, D), lambda i: (0, 0))
    row = pl.BlockSpec((tm, D), lambda i: (i, 0))
    kern = functools.partial(_out_router_kernel, alpha=alpha, n_dn=n_dn, sub=sub)
    return pl.pallas_call(
        kern, grid=(T // tm,),
        in_specs=[pl.BlockSpec((tm, n_dn), lambda i: (i, 0)), pl.BlockSpec((tm, n_cv), lambda i: (i, 0)), row,
                  pl.BlockSpec((n_dn + n_cv, D), lambda i: (0, 0)), vec, vec,
                  pl.BlockSpec((D, LANES), lambda i: (0, 0)), pl.BlockSpec((1, LANES), lambda i: (0, 0))],
        out_specs=[row, pl.BlockSpec((tm, D // 2), lambda i: (i, 0)),
                   pl.BlockSpec((tm, LANES), lambda i: (i, 0)), pl.BlockSpec((8, tm), lambda i: (0, i))],
        out_shape=[jax.ShapeDtypeStruct((T, D), F32), jax.ShapeDtypeStruct((T, D // 2), U32),
                   jax.ShapeDtypeStruct((T, LANES), F32), jax.ShapeDtypeStruct((8, T), F32)],
        compiler_params=_cparams("parallel"), name="out_router",
    )(dn, cv, h, w_out, g.reshape(1, D), b.reshape(1, D), w_router, b_router)


def _onehot_t(routet_ref, tr):
    sub = lax.broadcasted_iota(I32, (2 * N_EXPERTS, tr), 0)
    e1 = routet_ref[0:1, :].astype(I32)
    e2 = routet_ref[1:2, :].astype(I32)
    return sub == jnp.where(sub < N_EXPERTS, e1, e2 + N_EXPERTS)


def _count_kernel(routet_ref, cnt_ref, *, tr):
    @pl.when(pl.program_id(0) == 0)
    def _():
        cnt_ref[...] = jnp.zeros_like(cnt_ref)
    oh = _onehot_t(routet_ref, tr).astype(F32)
    cnt_ref[...] += jnp.sum(oh, axis=1, keepdims=True)


def _place_kernel(routet_ref, base_ref, d1_ref, d2_ref, carry, *, tr):
    @pl.when(pl.program_id(0) == 0)
    def _():
        carry[...] = jnp.zeros_like(carry)
    oh = _onehot_t(routet_ref, tr)
    ohb = oh.astype(F32).astype(BF16)
    ri = lax.broadcasted_iota(I32, (tr, tr), 0)
    ci = lax.broadcasted_iota(I32, (tr, tr), 1)
    before = (ri < ci).astype(F32).astype(BF16)
    excl = jnp.dot(ohb, before, preferred_element_type=F32)
    pos = jnp.where(oh, excl + carry[...] + base_ref[...], 0.0)
    d1 = jnp.sum(pos[:N_EXPERTS], axis=0, keepdims=True)
    d2 = jnp.sum(pos[N_EXPERTS:], axis=0, keepdims=True)
    d1_ref[...] = d1.astype(I32)
    d2_ref[...] = d2.astype(I32)
    carry[...] += jnp.sum(oh.astype(F32), axis=1, keepdims=True)


def _placement(routet, tr=512):
    T = routet.shape[1]
    E = N_EXPERTS
    rt = pl.BlockSpec((8, tr), lambda i: (0, i))
    col = pl.BlockSpec((2 * E, 1), lambda i: (0, 0))
    cnt = pl.pallas_call(
        functools.partial(_count_kernel, tr=tr), grid=(T // tr,), in_specs=[rt], out_specs=col,
        out_shape=jax.ShapeDtypeStruct((2 * E, 1), F32),
        compiler_params=_cparams("arbitrary"), name="moe_count")(routet)
    c1 = cnt[:E, 0].astype(I32)
    c2 = cnt[E:, 0].astype(I32)
    padded = ((c1 + c2 + FFN_BLOCK - 1) // FFN_BLOCK) * FFN_BLOCK
    pend = jnp.cumsum(padded)
    pstart = pend - padded
    base = jnp.concatenate([pstart, pstart + c1]).astype(F32).reshape(2 * E, 1)
    lane_row = pl.BlockSpec((1, tr), lambda i: (0, i))
    d1, d2 = pl.pallas_call(
        functools.partial(_place_kernel, tr=tr), grid=(T // tr,), in_specs=[rt, col], out_specs=[lane_row, lane_row],
        out_shape=[jax.ShapeDtypeStruct((1, T), I32)] * 2,
        scratch_shapes=[pltpu.VMEM((2 * E, 1), F32)],
        compiler_params=_cparams("arbitrary"), name="moe_place")(routet, base)
    n_blocks = (T * TOP_K + E * (FFN_BLOCK - 1) + FFN_BLOCK - 1) // FFN_BLOCK
    starts = jnp.arange(n_blocks, dtype=I32) * FFN_BLOCK
    blk_e = jnp.minimum(jnp.sum((pend[None, :] <= starts[:, None]).astype(I32), axis=1), E - 1)
    blk_first = jnp.concatenate([jnp.ones((1,), I32), (blk_e[1:] != blk_e[:-1]).astype(I32)])
    n_used = (pend[-1] // FFN_BLOCK).astype(I32).reshape(1)
    ids = jnp.arange(E, dtype=I32)
    later = (padded > 0)[None, :] & (ids[None, :] > ids[:, None])
    next_tab = jnp.min(jnp.where(later, ids[None, :], E), axis=1)
    blk_next = jnp.where(next_tab == E, -1, next_tab)[blk_e]
    return d1.reshape(T), d2.reshape(T), pend.astype(I32), blk_e, blk_first, blk_next, n_used, n_blocks


def _dispatch_kernel(d1_ref, d2_ref, pend_ref, src_ref, xs_ref, zbuf, sem, zsem, *, tb):
    @pl.when(pl.program_id(0) == 0)
    def _():
        zbuf[...] = jnp.zeros_like(zbuf)
        tails = []
        for e in range(N_EXPERTS):
            end = pend_ref[e]
            nonempty = end > (pend_ref[e - 1] if e else 0)
            start = pl.multiple_of(jnp.maximum(end - FFN_BLOCK, 0), FFN_BLOCK)
            tails.append((nonempty, pltpu.make_async_copy(zbuf, xs_ref.at[pl.ds(start, FFN_BLOCK), :], zsem)))
        for nonempty, copy in tails:
            pl.when(nonempty)(copy.start)

        def spare(b):
            return pltpu.make_async_copy(
                zbuf, xs_ref.at[pl.ds(pl.multiple_of(b * FFN_BLOCK, FFN_BLOCK), FFN_BLOCK), :], zsem)
        first_spare = pend_ref[N_EXPERTS - 1] // FFN_BLOCK
        n_blocks = xs_ref.shape[0] // FFN_BLOCK
        lax.fori_loop(first_spare, n_blocks, lambda b, c: (spare(b).start(), c)[1], 0)
        for nonempty, copy in tails:
            pl.when(nonempty)(copy.wait)
        lax.fori_loop(first_spare, n_blocks, lambda b, c: (spare(b).wait(), c)[1], 0)

    def issue(t, _):
        row = src_ref.at[pl.ds(t, 1), :]
        pltpu.make_async_copy(row, xs_ref.at[pl.ds(d1_ref[t], 1), :], sem).start(priority=0)
        pltpu.make_async_copy(row, xs_ref.at[pl.ds(d2_ref[t], 1), :], sem).start(priority=1)
        return 0

    lax.fori_loop(0, tb, issue, 0, unroll=8)
    pltpu.make_async_copy(src_ref, xs_ref.at[pl.ds(0, tb), :], sem).wait()
    pltpu.make_async_copy(src_ref, xs_ref.at[pl.ds(0, tb), :], sem).wait()


def _dispatch(h1p, d1, d2, pend, n_rows, tb=1024):
    T, W = h1p.shape
    idx = pl.BlockSpec((tb,), lambda i: (i,), memory_space=pltpu.SMEM)
    return pl.pallas_call(
        functools.partial(_dispatch_kernel, tb=tb), grid=(T // tb,),
        in_specs=[idx, idx, pl.BlockSpec((N_EXPERTS,), lambda i: (0,), memory_space=pltpu.SMEM),
                  pl.BlockSpec((tb, W), lambda i: (i, 0))],
        out_specs=pl.BlockSpec(memory_space=pl.ANY),
        out_shape=jax.ShapeDtypeStruct((n_rows, W), U32),
        scratch_shapes=[pltpu.VMEM((FFN_BLOCK, W), U32), pltpu.SemaphoreType.DMA(()), pltpu.SemaphoreType.DMA(())],
        compiler_params=_cparams("arbitrary"), name="moe_dispatch",
    )(d1, d2, pend, h1p)


def _ffn_kernel(blk_e_ref, blk_first_ref, blk_next_ref, n_used_ref, xs_ref, wgu_hbm, wdn_hbm, y_ref,
                wgu_f, wdn_f, wgu_s, wdn_s, sem, *, ff, layer):
    i = pl.program_id(0)
    used = i < n_used_ref[0]

    def weight_copies(e):
        copies = []
        for k, (src, dst) in enumerate(((wgu_hbm, wgu_f), (wdn_hbm, wdn_f))):
            rows = dst.shape[0] // FFN_WEIGHT_CHUNKS
            for c in range(FFN_WEIGHT_CHUNKS):
                r = pl.ds(c * rows, rows)
                copies.append(pltpu.make_async_copy(src.at[layer, e, r, :], dst.at[r, :],
                                                    sem.at[k * FFN_WEIGHT_CHUNKS + c]))
        return copies

    @pl.when(i == 0)
    def _():
        for c in weight_copies(blk_e_ref[0]):
            c.start(priority=1)

    @pl.when(used & (blk_first_ref[i] == 1))
    def _():
        for c in weight_copies(blk_e_ref[i]):
            c.wait()
        wgu_s[...] = wgu_f[...].astype(BF16)
        wdn_s[...] = wdn_f[...].astype(BF16)

        @pl.when(blk_next_ref[i] >= 0)
        def _():
            for c in weight_copies(blk_next_ref[i]):
                c.start(priority=1)

    @pl.when(used)
    def _():
        lo, hi = _unpack_halves(xs_ref[...])
        x = jnp.concatenate([lo, hi], axis=1).astype(BF16)
        gu = jnp.dot(x, wgu_s[...], preferred_element_type=F32)
        act = (_silu(gu[:, :ff]) * gu[:, ff:]).astype(BF16)
        y_ref[...] = _pack_halves(jnp.dot(act, wdn_s[...], preferred_element_type=F32))

    @pl.when(i >= n_used_ref[0])
    def _():
        y_ref[...] = jnp.zeros_like(y_ref)


def _ffn(xs, w_gu, w_dn, layer, blk_e, blk_first, blk_next, n_used, n_blocks):
    _, E, D, FF2 = w_gu.shape
    ff = FF2 // 2
    W = xs.shape[1]
    any_spec = pl.BlockSpec(memory_space=pl.ANY)
    grid_spec = pltpu.PrefetchScalarGridSpec(
        num_scalar_prefetch=4, grid=(n_blocks,),
        in_specs=[pl.BlockSpec((FFN_BLOCK, W), lambda i, be, bf, bn, nu: (jnp.minimum(i, nu[0] - 1), 0)),
                  any_spec, any_spec],
        out_specs=pl.BlockSpec((FFN_BLOCK, W), lambda i, *_: (i, 0)),
        scratch_shapes=[pltpu.VMEM((D, FF2), F32), pltpu.VMEM((ff, D), F32),
                        pltpu.VMEM((D, FF2), BF16), pltpu.VMEM((ff, D), BF16),
                        pltpu.SemaphoreType.DMA((2 * FFN_WEIGHT_CHUNKS,))])
    return pl.pallas_call(
        functools.partial(_ffn_kernel, ff=ff, layer=layer), grid_spec=grid_spec,
        out_shape=jax.ShapeDtypeStruct(xs.shape, U32),
        compiler_params=_cparams("arbitrary"), name="moe_ffn",
    )(blk_e, blk_first, blk_next, n_used, xs, w_gu, w_dn)


def _combine_kernel(d1_ref, d2_ref, d1n_ref, d2n_ref, y_ref, h1_ref, route_ref, g_ref, b_ref, h2_ref, h2b_ref,
                    ybuf, sem, *, tc, alpha):
    i = pl.program_id(0)
    slot = i % 2

    def row_copies(r1, r2, s, t):
        return (pltpu.make_async_copy(y_ref.at[pl.ds(r1[t], 1), :], ybuf.at[s, 0, pl.ds(t, 1), :], sem.at[s]),
                pltpu.make_async_copy(y_ref.at[pl.ds(r2[t], 1), :], ybuf.at[s, 1, pl.ds(t, 1), :], sem.at[s]))

    def wait_slot(s):
        for j in range(TOP_K):
            pltpu.make_async_copy(y_ref.at[pl.ds(0, tc), :], ybuf.at[s, j], sem.at[s]).wait()

    @pl.when(i == 0)
    def _():
        def issue(t, _):
            for j, c in enumerate(row_copies(d1_ref, d2_ref, 0, t)):
                c.start(priority=j)
            return 0
        lax.fori_loop(0, tc, issue, 0, unroll=8)

    wait_slot(slot)
    for t in range(tc):
        for j, c in enumerate(row_copies(d1n_ref, d2n_ref, 1 - slot, t)):
            c.start(priority=j)

    route = route_ref[...]
    g1 = route[:, 2:3]
    g2 = route[:, 3:4]
    lo1, hi1 = _unpack_halves(ybuf[slot, 0])
    lo2, hi2 = _unpack_halves(ybuf[slot, 1])
    ffn = jnp.concatenate([g1 * lo1 + g2 * lo2, g1 * hi1 + g2 * hi2], axis=1)
    h2 = _layer_norm(alpha * h1_ref[...] + ffn, g_ref[...], b_ref[...])
    h2_ref[...] = h2
    h2b_ref[...] = h2.astype(BF16)

    @pl.when(i == pl.num_programs(0) - 1)
    def _():
        wait_slot(1 - slot)


def _combine(y, d1, d2, h1, route, g, b, *, alpha, tc=256):
    T, D = h1.shape
    W = y.shape[1]
    n = T // tc
    idx = pl.BlockSpec((tc,), lambda i: (i,), memory_space=pltpu.SMEM)
    idx_next = pl.BlockSpec((tc,), lambda i: (jnp.minimum(i + 1, n - 1),), memory_space=pltpu.SMEM)
    row = pl.BlockSpec((tc, D), lambda i: (i, 0))
    vec = pl.BlockSpec((1, D), lambda i: (0, 0))
    return pl.pallas_call(
        functools.partial(_combine_kernel, tc=tc, alpha=alpha), grid=(n,),
        in_specs=[idx, idx, idx_next, idx_next, pl.BlockSpec(memory_space=pl.ANY), row,
                  pl.BlockSpec((tc, LANES), lambda i: (i, 0)), vec, vec],
        out_specs=[row, row],
        out_shape=[jax.ShapeDtypeStruct((T, D), F32), jax.ShapeDtypeStruct((T, D), BF16)],
        scratch_shapes=[pltpu.VMEM((2, TOP_K, tc, W), U32), pltpu.SemaphoreType.DMA((2,))],
        compiler_params=_cparams("arbitrary"), name="moe_combine",
    )(d1, d2, d1, d2, y, h1, route, g.reshape(1, D), b.reshape(1, D))


def kernel(x, emb_ln_g, emb_ln_b, w_in, short_conv_w, a_log, dt_bias, dn_norm_w, dw_conv_w, dw_conv_b,
           conv_ln_g, conv_ln_b, w_out, ln1_g, ln1_b, w_group, b_group, w_expert, b_expert, w_gate_up,
           w_down, ln2_g, ln2_b):
    B, L, D = x.shape
    T = B * L
    depth = w_in.shape[0]
    H = a_log.shape[2]
    dn_w = H * DN_HEAD_DIM
    cv_w = dw_conv_w.shape[2]
    n_ab = 4 * H
    alpha = (2 * depth) ** 0.25
    C = DN_CHUNK

    h = hb = None
    for l in range(depth):
        w = w_in[l]
        w_main = jnp.concatenate([w[:, :4 * dn_w], w[:, 4 * dn_w + n_ab:]], axis=1).astype(BF16)
        w_ab = w[:, 4 * dn_w:4 * dn_w + n_ab]
        if l == 0:
            proj, gt, h, hb = _in_proj(x.reshape(T, D), w_main, w_ab, a_log[l], dt_bias[l], B=B, L=L,
                                       ln_params=(emb_ln_g, emb_ln_b))
        else:
            proj, gt = _in_proj(hb, w_main, w_ab, a_log[l], dt_bias[l], B=B, L=L)
        gt = gt.reshape(B, n_ab, L // C, C)
        dn = _deltanet(proj, gt, short_conv_w[l], dn_norm_w[l], B=B, L=L, H=H)
        cv = _conformer(proj, dw_conv_w[l], dw_conv_b[l], conv_ln_g[l], conv_ln_b[l],
                        B=B, L=L, col0=4 * dn_w, width=cv_w)
        n_r = N_GROUPS + N_EXPERTS
        w_router = jnp.pad(jnp.concatenate([w_group[l], w_expert[l]], axis=1),
                           ((0, 0), (0, LANES - n_r))).astype(BF16)
        b_router = jnp.pad(jnp.concatenate([b_group[l], b_expert[l]]), (0, LANES - n_r)).reshape(1, LANES)
        h1, h1p, route, routet = _out_router(
            dn.reshape(T, dn_w), cv.reshape(T, cv_w), h, w_out[l].astype(BF16), ln1_g[l], ln1_b[l],
            w_router, b_router, alpha=alpha)
        d1, d2, pend, blk_e, blk_first, blk_next, n_used, n_blocks = _placement(routet)
        xs = _dispatch(h1p, d1, d2, pend, n_blocks * FFN_BLOCK)
        y = _ffn(xs, w_gate_up, w_down, l, blk_e, blk_first, blk_next, n_used, n_blocks)
        h, hb = _combine(y, d1, d2, h1, route, ln2_g[l], ln2_b[l], alpha=alpha)
    return h.reshape(B, L, D)
```

```python
import functools
import itertools

import jax
import jax.numpy as jnp
from jax import lax
from jax.experimental import pallas as pl
from jax.experimental.pallas import tpu as pltpu

F32 = jnp.float32
BF16 = jnp.bfloat16
U32 = jnp.uint32
I32 = jnp.int32

LANES = 128
DN_HEAD_DIM = 128
DN_CHUNK = 128
PREP_UNROLL = 8
CV_GROUP = 128
N_GROUPS = 8
EXPERTS_PER_GROUP = 8
N_EXPERTS = N_GROUPS * EXPERTS_PER_GROUP
TOP_K = 2
FFN_BLOCK = 256
FFN_WEIGHT_CHUNKS = 4
LN_EPS = 1e-5
RMS_EPS = 1e-6
VMEM_LIMIT = 56 * 1024 * 1024


def _cparams(*sem):
    return pltpu.CompilerParams(dimension_semantics=sem, vmem_limit_bytes=VMEM_LIMIT)


def _layer_norm(x, g, b):
    mu = jnp.mean(x, -1, keepdims=True)
    xc = x - mu
    var = jnp.mean(xc * xc, -1, keepdims=True)
    return xc * lax.rsqrt(var + LN_EPS) * g + b


def _sigmoid(x):
    return 1.0 / (1.0 + jnp.exp(-x))


def _silu(x):
    return x * _sigmoid(x)


def _pack_halves(y):
    n = y.shape[1] // 2
    bits = lax.bitcast_convert_type(y.astype(BF16).astype(F32), U32)
    return (bits[:, :n] >> 16) | bits[:, n:]


def _unpack_halves(u):
    lo = lax.bitcast_convert_type(u << 16, F32)
    hi = lax.bitcast_convert_type(u & jnp.uint32(0xFFFF0000), F32)
    return lo, hi


def _emb_ln_kernel(x_ref, g_ref, b_ref, h_ref, hb_ref):
    h = _layer_norm(x_ref[...], g_ref[...], b_ref[...])
    h_ref[...] = h
    hb_ref[...] = h.astype(BF16)


def _emb_ln(x, g, b, tm=512):
    T, D = x.shape
    row = pl.BlockSpec((tm, D), lambda i: (i, 0))
    vec = pl.BlockSpec((1, D), lambda i: (0, 0))
    return pl.pallas_call(
        _emb_ln_kernel, grid=(T // tm,), in_specs=[row, vec, vec], out_specs=[row, row],
        out_shape=[jax.ShapeDtypeStruct((T, D), F32), jax.ShapeDtypeStruct((T, D), BF16)],
        compiler_params=_cparams("parallel"), name="emb_ln")(x, g.reshape(1, D), b.reshape(1, D))


def _in_proj_kernel(x_ref, w_ref, wab_ref, alog_ref, dtb_ref, o_ref, gt_ref, *, n_decay, n_gate_rows):
    @pl.when(pl.program_id(1) == 0)
    def _():
        ab = jnp.dot(x_ref[...], wab_ref[...], preferred_element_type=F32)
        lane = lax.broadcasted_iota(I32, ab.shape, 1)
        sp = ab + dtb_ref[...]
        softplus = jnp.maximum(sp, 0.0) + jnp.log(1.0 + jnp.exp(-jnp.abs(sp)))
        decay = -jnp.exp(alog_ref[...]) * softplus
        gt_ref[0] = jnp.where(lane < n_decay, decay, _sigmoid(ab)).T[:n_gate_rows, :]

    acc = jnp.dot(x_ref[...], w_ref[...], preferred_element_type=F32)
    for j in range(o_ref.shape[0]):
        o_ref[j] = acc[:, j * LANES:(j + 1) * LANES]


def _in_proj(hb, w, w_ab, a_log, dt_bias, *, B, L, tm=1024, tn=1024):
    T, D = hb.shape
    N = w.shape[1]
    n = a_log.size
    n_gate_rows = 2 * n
    pad = lambda v: jnp.pad(v.reshape(1, n).astype(F32), ((0, 0), (0, LANES - n)))
    wab = jnp.pad(w_ab, ((0, 0), (0, LANES - w_ab.shape[1]))).astype(BF16)
    vec = pl.BlockSpec((1, LANES), lambda i, j: (0, 0))
    per_batch = L // tm
    return pl.pallas_call(
        functools.partial(_in_proj_kernel, n_decay=n, n_gate_rows=n_gate_rows), grid=(T // tm, N // tn),
        in_specs=[pl.BlockSpec((tm, D), lambda i, j: (i, 0)), pl.BlockSpec((D, tn), lambda i, j: (0, j)),
                  pl.BlockSpec((D, LANES), lambda i, j: (0, 0)), vec, vec],
        out_specs=[pl.BlockSpec((tn // LANES, tm, LANES), lambda i, j: (j, i, 0)),
                   pl.BlockSpec((1, n_gate_rows, tm), lambda i, j: (i // per_batch, 0, i % per_batch))],
        out_shape=[jax.ShapeDtypeStruct((N // LANES, T, LANES), F32),
                   jax.ShapeDtypeStruct((B, n_gate_rows, L), F32)],
        compiler_params=_cparams("parallel", "arbitrary"), name="in_proj",
    )(hb, w, wab, pad(a_log), pad(dt_bias))


def _dot(a, b):
    return jnp.dot(a.astype(BF16), b.astype(BF16), preferred_element_type=F32)


def _dot_nt(a, b):
    return lax.dot_general(a.astype(BF16), b.astype(BF16), (((1,), (1,)), ((), ())),
                           preferred_element_type=F32)


def _dot_tn(a, b):
    return lax.dot_general(a.astype(BF16), b.astype(BF16), (((0,), (0,)), ((), ())),
                           preferred_element_type=F32)


def _deltanet_kernel(q_ref, k_ref, v_ref, cq_ref, ck_ref, cv_ref, gt_ref, o_ref,
                     qs, ks, vs, ob, wq_s, at_s, kd_s, u_s, gl_s, s_s, rhs_s, *, L, C, H, R, n_items):
    DH = DN_HEAD_DIM
    s = pl.program_id(0)
    h = jnp.minimum(s, n_items - 1) % H
    slot_p = s % 2
    slot_s = 1 - slot_p
    NC = L // C
    U = PREP_UNROLL

    def conv_rows(r0, first, last):
        rows = pl.ds(r0, R)
        rid = lax.broadcasted_iota(I32, (R, DH), 0)

        def conv_silu(ref, w_ref):
            x = ref[0, rows, :]
            if first:
                x_prev = jnp.where(rid == 0, 0.0, pltpu.roll(x, 1, 0))
            else:
                x_prev = ref[0, pl.ds(r0 - 1, R), :]
            if last:
                x_next = jnp.where(rid == R - 1, 0.0, pltpu.roll(x, R - 1, 0))
            else:
                x_next = ref[0, pl.ds(r0 + 1, R), :]
            return _silu(w_ref[0:1, :] * x_prev + w_ref[1:2, :] * x + w_ref[2:3, :] * x_next)

        q = conv_silu(q_ref, cq_ref)
        k = conv_silu(k_ref, ck_ref)
        qs[rows, :] = q * (lax.rsqrt(jnp.sum(q * q, -1, keepdims=True) + RMS_EPS) * (DH ** -0.5))
        ks[rows, :] = k * lax.rsqrt(jnp.sum(k * k, -1, keepdims=True) + RMS_EPS)
        vs[rows, :] = conv_silu(v_ref, cv_ref)

    @pl.when(s < n_items)
    def _():
        conv_rows(0, True, False)
        lax.fori_loop(1, L // R - 1, lambda i, c: (conv_rows(pl.multiple_of(i * R, R), False, False), c)[1], 0)
        conv_rows(L - R, False, True)

    ri = lax.broadcasted_iota(I32, (C, C), 0)
    ci = lax.broadcasted_iota(I32, (C, C), 1)
    eye = ri == ci
    eye_f = eye.astype(F32)
    incl = (ri >= ci, ri <= ci)
    strict = (ri > ci, ri < ci)
    n_sq = C.bit_length() - 2
    zero_b = jnp.zeros((C, C), BF16)

    def block_diag(a, b):
        return jnp.concatenate([jnp.concatenate([a, zero_b], axis=1),
                                jnp.concatenate([zero_b, b], axis=1)], axis=0)

    def prep_start(c, j):
        rows = pl.ds(pl.multiple_of(c * C, C), C)
        q = qs[rows, :]
        k = ks[rows, :]
        v = vs[rows, :]
        kq = _dot_nt(jnp.concatenate([k, q], axis=0), k)
        kk, qk = kq[:C], kq[C:]
        a2 = []
        for d in range(2):
            g_row = gt_ref[0, d * H + h, pl.ds(c, 1), :]
            b_row = gt_ref[0, (2 + d) * H + h, pl.ds(c, 1), :]
            m, ms, mt = incl[d], strict[d], incl[1 - d]
            g_col = jnp.sum(jnp.where(eye, g_row, 0.0), axis=1, keepdims=True)
            b_col = jnp.sum(jnp.where(eye, b_row, 0.0), axis=1, keepdims=True)
            gc_col = jnp.sum(jnp.where(m, g_row, 0.0), axis=1, keepdims=True)
            gc_row = jnp.sum(jnp.where(mt, g_col, 0.0), axis=0, keepdims=True)
            tot = jnp.sum(g_row, axis=1, keepdims=True)
            decay = jnp.where(m, jnp.exp(jnp.where(m, gc_col - gc_row, 0.0)), 0.0)
            a2.append(jnp.where(ms, -(kk * b_col * decay), 0.0))
            egc = jnp.exp(gc_col)
            rhs_s[d, j] = jnp.concatenate([v * b_col, k * (b_col * egc)], axis=1).astype(BF16)
            wq_s[slot_p, d, c, pl.ds(C, C), :] = (q * egc).astype(BF16)
            at_s[slot_p, d, c] = (qk * decay).astype(BF16)
            kd_s[slot_p, d, c] = (k * jnp.exp(tot - gc_col)).astype(BF16)
            gl_s[slot_p, d, pl.ds(c, 1), :] = jnp.broadcast_to(jnp.exp(tot), (1, DH))
        return jnp.concatenate(a2, axis=1)

    def prep_phases(i):
        cs = [i * U + j for j in range(U)]
        ps = []
        for j, c in enumerate(cs):
            ps.append(prep_start(c, j))
            if j == U // 2 - 1:
                yield
        eye2 = jnp.concatenate([eye_f, eye_f], axis=1)
        xs_ = [eye2 + p for p in ps]
        yield
        for _ in range(n_sq):
            pbs = [p.astype(BF16) for p in ps]
            ps = [jnp.dot(pb, block_diag(pb[:, :C], pb[:, C:]), preferred_element_type=F32) for pb in pbs]
            yield
            pbs = [p.astype(BF16) for p in ps]
            xs_ = [x + jnp.dot(x.astype(BF16), block_diag(pb[:, :C], pb[:, C:]), preferred_element_type=F32)
                   for x, pb in zip(xs_, pbs)]
            yield
        for j, (c, x) in enumerate(zip(cs, xs_)):
            for d in range(2):
                sol = jnp.dot(x[:, d * C:(d + 1) * C].astype(BF16), rhs_s[d, j], preferred_element_type=F32)
                u_s[slot_p, d, c] = sol[:, :DH]
                wq_s[slot_p, d, c, pl.ds(0, C), :] = sol[:, DH:].astype(BF16)
            if j == U // 2 - 1:
                yield
        yield

    def scan_stages(n):
        cf, cb = n, NC - 1 - n
        s_f, s_b = s_s[0], s_s[1]
        wq = jnp.concatenate([wq_s[slot_s, 0, cf], wq_s[slot_s, 1, cb]], axis=1)
        r1 = jnp.dot(wq, block_diag(s_f.astype(BF16), s_b.astype(BF16)), preferred_element_type=F32)
        yield
        u = jnp.concatenate([u_s[slot_s, 0, cf], u_s[slot_s, 1, cb]], axis=1)
        v_new = (u - r1[:C]).astype(BF16)
        vd = block_diag(v_new[:, :DH], v_new[:, DH:])
        at = jnp.concatenate([at_s[slot_s, 0, cf], at_s[slot_s, 1, cb]], axis=1)
        o = r1[C:] + jnp.dot(at, vd, preferred_element_type=F32)
        kd = jnp.concatenate([kd_s[slot_s, 0, cf], kd_s[slot_s, 1, cb]], axis=0)
        ds = lax.dot_general(kd, vd, (((0,), (0,)), ((), ())), preferred_element_type=F32)
        s_s[0] = s_f * gl_s[slot_s, 0, pl.ds(cf, 1), :] + ds[:, :DH]
        s_s[1] = s_b * gl_s[slot_s, 1, pl.ds(cb, 1), :] + ds[:, DH:]
        o_ref[0, pl.ds(pl.multiple_of(cf * C, C), C), :] = o[:, :DH]
        ob[pl.ds(pl.multiple_of(cb * C, C), C), :] = o[:, DH:]
        yield

    def run(prepare, scan):
        def trip(i, _):
            prep_it = prep_phases(i) if prepare else iter(())
            scan_it = itertools.chain.from_iterable(scan_stages(i * U + j) for j in range(U)) if scan else iter(())
            for _ in itertools.zip_longest(scan_it, prep_it):
                pass
            return 0
        if scan:
            s_s[...] = jnp.zeros_like(s_s)
        lax.fori_loop(0, NC // U, trip, 0)

    @pl.when(s == 0)
    def _():
        run(True, False)

    @pl.when((s > 0) & (s < n_items))
    def _():
        run(True, True)

    @pl.when(s == n_items)
    def _():
        run(False, True)

    def sum_rows(i, _):
        rows = pl.ds(pl.multiple_of(i * R, R), R)
        o_ref[0, rows, :] = o_ref[0, rows, :] + ob[rows, :]
        return 0

    @pl.when(s > 0)
    def _():
        lax.fori_loop(0, L // R, sum_rows, 0)


def _deltanet(proj, gt, conv_w, *, B, L, H):
    DH, C, R = DN_HEAD_DIM, DN_CHUNK, 256
    assert C == DH and L % (C * PREP_UNROLL) == 0
    NC = L // C
    n_items = B * H
    cur = lambda s: jnp.minimum(s, n_items - 1)
    prev = lambda s: jnp.maximum(s - 1, 0)
    col = lambda off: pl.BlockSpec((1, L, DH), lambda s: (off + cur(s) % H, cur(s) // H, 0))
    cw = lambda off: pl.BlockSpec((3, DH), lambda s: (0, off + cur(s) % H))
    kern = functools.partial(_deltanet_kernel, L=L, C=C, H=H, R=R, n_items=n_items)
    return pl.pallas_call(
        kern, grid=(n_items + 1,),
        in_specs=[col(0), col(H), col(2 * H), cw(0), cw(H), cw(2 * H),
                  pl.BlockSpec((1, 4 * H, NC, C), lambda s: (cur(s) // H, 0, 0, 0))],
        out_specs=pl.BlockSpec((1, L, DH), lambda s: (prev(s) // H, 0, prev(s) % H)),
        out_shape=jax.ShapeDtypeStruct((B, L, H * DH), F32),
        scratch_shapes=[pltpu.VMEM((L, DH), F32), pltpu.VMEM((L, DH), F32), pltpu.VMEM((L, DH), F32),
                        pltpu.VMEM((L, DH), F32),
                        pltpu.VMEM((2, 2, NC, 2 * C, DH), BF16), pltpu.VMEM((2, 2, NC, C, C), BF16),
                        pltpu.VMEM((2, 2, NC, C, DH), BF16), pltpu.VMEM((2, 2, NC, C, DH), F32),
                        pltpu.VMEM((2, 2, NC, DH), F32), pltpu.VMEM((2, DH, DH), F32),
                        pltpu.VMEM((2, PREP_UNROLL, C, 2 * DH), BF16)],
        compiler_params=_cparams("arbitrary"), name="deltanet",
    )(proj, proj, proj, conv_w, conv_w, conv_w, gt)


def _conformer_kernel(val_ref, gate_ref, w_ref, b_ref, g_ref, beta_ref, o_ref, ypad, *, L, K, R, HALO):
    pad = K // 2
    zeros = jnp.zeros((HALO, CV_GROUP), F32)
    ypad[pl.ds(0, HALO), :] = zeros
    ypad[pl.ds(HALO + L, HALO), :] = zeros

    def glu_rows(i, _):
        r0 = pl.multiple_of(i * R, R)
        ypad[pl.ds(HALO + r0, R), :] = val_ref[0, pl.ds(r0, R), :] * _sigmoid(gate_ref[0, pl.ds(r0, R), :])
        return 0

    lax.fori_loop(0, L // R, glu_rows, 0)

    def conv_rows(i, _):
        r0 = pl.multiple_of(i * R, R)
        acc = jnp.zeros((R, CV_GROUP), F32)
        for t in range(K):
            acc = acc + w_ref[t:t + 1, :] * ypad[pl.ds(r0 + (HALO - pad + t), R), :]
        y = _layer_norm(acc + b_ref[...], g_ref[...], beta_ref[...])
        o_ref[0, pl.ds(r0, R), :] = _silu(y)
        return 0

    lax.fori_loop(0, L // R, conv_rows, 0)


def _conformer(proj, w, b, g, beta, *, B, L, col0, width):
    K = w.shape[0]
    n = width // CV_GROUP
    HALO, R = 16, 256
    c0 = col0 // CV_GROUP
    vec = pl.BlockSpec((1, CV_GROUP), lambda bb, j: (0, j))
    kern = functools.partial(_conformer_kernel, L=L, K=K, R=R, HALO=HALO)
    return pl.pallas_call(
        kern, grid=(B, n),
        in_specs=[pl.BlockSpec((1, L, CV_GROUP), lambda bb, j: (c0 + j, bb, 0)),
                  pl.BlockSpec((1, L, CV_GROUP), lambda bb, j: (c0 + n + j, bb, 0)),
                  pl.BlockSpec((K, CV_GROUP), lambda bb, j: (0, j)), vec, vec, vec],
        out_specs=pl.BlockSpec((1, L, CV_GROUP), lambda bb, j: (bb, 0, j)),
        out_shape=jax.ShapeDtypeStruct((B, L, width), F32),
        scratch_shapes=[pltpu.VMEM((L + 2 * HALO, CV_GROUP), F32)],
        compiler_params=_cparams("parallel", "parallel"), name="conformer",
    )(proj, proj, w, b.reshape(1, width), g.reshape(1, width), beta.reshape(1, width))


def _out_router_kernel(dn_ref, z_ref, nw_ref, cv_ref, h_ref, w_ref, g_ref, b_ref, wr_ref, br_ref,
                       h1_ref, h1p_ref, route_ref, routet_ref, *, alpha, n_dn, sub):
    n_sub = h_ref.shape[0] // sub

    def stages(s):
        rows = pl.ds(s * sub, sub)
        heads = []
        for hh in range(n_dn // DN_HEAD_DIM):
            o = dn_ref[rows, hh * DN_HEAD_DIM:(hh + 1) * DN_HEAD_DIM]
            o = o * lax.rsqrt(jnp.mean(o * o, -1, keepdims=True) + RMS_EPS) * nw_ref[...]
            heads.append((o * _silu(z_ref[hh, rows, :])).astype(BF16))
        mix = jnp.dot(jnp.concatenate(heads, axis=1), w_ref[pl.ds(0, n_dn), :], preferred_element_type=F32)
        mix = mix + jnp.dot(cv_ref[rows, :].astype(BF16), w_ref[pl.ds(n_dn, w_ref.shape[0] - n_dn), :],
                            preferred_element_type=F32)
        yield
        h1 = _layer_norm(alpha * h_ref[rows, :] + mix, g_ref[...], b_ref[...])
        h1_ref[rows, :] = h1
        h1p_ref[rows, :] = _pack_halves(h1)
        h1b = h1.astype(BF16)
        yield
        logits = jnp.dot(h1b, wr_ref[...], preferred_element_type=F32) + br_ref[...]
        yield
        route = _route(logits)
        route_ref[rows, :] = route
        routet_ref[:, rows] = route.T[:8, :]
        yield

    n_stage = 4
    pipes = [stages(s) for s in range(n_sub)]
    for t in range(n_sub + n_stage - 1):
        for s in range(t, t - n_stage, -1):
            if 0 <= s < n_sub:
                next(pipes[s])


def _route(logits):
    lane = lax.broadcasted_iota(I32, logits.shape, 1)
    neg = jnp.float32(-jnp.inf)
    big = jnp.int32(LANES)
    gl = jnp.where(lane < N_GROUPS, logits, neg)
    gmax = jnp.max(gl, -1, keepdims=True)
    gsel = jnp.min(jnp.where(gl == gmax, lane, big), -1, keepdims=True)
    pg = 1.0 / jnp.sum(jnp.exp(gl - gmax), -1, keepdims=True)
    lo = N_GROUPS + gsel * EXPERTS_PER_GROUP
    el = jnp.where((lane >= lo) & (lane < lo + EXPERTS_PER_GROUP), logits, neg)
    e1 = jnp.max(el, -1, keepdims=True)
    i1 = jnp.min(jnp.where(el == e1, lane, big), -1, keepdims=True)
    el2 = jnp.where(lane == i1, neg, el)
    e2 = jnp.max(el2, -1, keepdims=True)
    i2 = jnp.min(jnp.where(el2 == e2, lane, big), -1, keepdims=True)
    r = jnp.exp(e2 - e1)
    p1 = 1.0 / (1.0 + r)
    p2 = r * p1
    route = jnp.where(lane == 0, (i1 - N_GROUPS).astype(F32),
                      jnp.where(lane == 1, (i2 - N_GROUPS).astype(F32),
                                jnp.where(lane == 2, pg * p1, jnp.where(lane == 3, pg * p2, 0.0))))
    return route


def _out_router(dn, proj, z_group, norm_w, cv, h, w_out, g, b, w_router, b_router, *, alpha, tm=512, sub=128):
    T, D = h.shape
    n_dn, n_cv = dn.shape[1], cv.shape[1]
    n_heads = n_dn // DN_HEAD_DIM
    assert z_group % n_heads == 0
    vec = pl.BlockSpec((1, D), lambda i: (0, 0))
    row = pl.BlockSpec((tm, D), lambda i: (i, 0))
    kern = functools.partial(_out_router_kernel, alpha=alpha, n_dn=n_dn, sub=sub)
    return pl.pallas_call(
        kern, grid=(T // tm,),
        in_specs=[pl.BlockSpec((tm, n_dn), lambda i: (i, 0)),
                  pl.BlockSpec((n_heads, tm, DN_HEAD_DIM), lambda i: (z_group // n_heads, i, 0)),
                  pl.BlockSpec((1, DN_HEAD_DIM), lambda i: (0, 0)),
                  pl.BlockSpec((tm, n_cv), lambda i: (i, 0)), row,
                  pl.BlockSpec((n_dn + n_cv, D), lambda i: (0, 0)), vec, vec,
                  pl.BlockSpec((D, LANES), lambda i: (0, 0)), pl.BlockSpec((1, LANES), lambda i: (0, 0))],
        out_specs=[row, pl.BlockSpec((tm, D // 2), lambda i: (i, 0)),
                   pl.BlockSpec((tm, LANES), lambda i: (i, 0)), pl.BlockSpec((8, tm), lambda i: (0, i))],
        out_shape=[jax.ShapeDtypeStruct((T, D), F32), jax.ShapeDtypeStruct((T, D // 2), U32),
                   jax.ShapeDtypeStruct((T, LANES), F32), jax.ShapeDtypeStruct((8, T), F32)],
        compiler_params=_cparams("parallel"), name="out_router",
    )(dn, proj, norm_w.reshape(1, DN_HEAD_DIM), cv, h, w_out, g.reshape(1, D), b.reshape(1, D), w_router, b_router)


def _onehot_t(routet_ref, tr):
    sub = lax.broadcasted_iota(I32, (2 * N_EXPERTS, tr), 0)
    e1 = routet_ref[0:1, :].astype(I32)
    e2 = routet_ref[1:2, :].astype(I32)
    return sub == jnp.where(sub < N_EXPERTS, e1, e2 + N_EXPERTS)


def _count_kernel(routet_ref, cnt_ref, *, tr):
    @pl.when(pl.program_id(0) == 0)
    def _():
        cnt_ref[...] = jnp.zeros_like(cnt_ref)
    oh = _onehot_t(routet_ref, tr).astype(F32)
    cnt_ref[...] += jnp.sum(oh, axis=1, keepdims=True)


def _place_kernel(routet_ref, base_ref, d1_ref, d2_ref, carry, *, tr):
    @pl.when(pl.program_id(0) == 0)
    def _():
        carry[...] = jnp.zeros_like(carry)
    oh = _onehot_t(routet_ref, tr)
    ohb = oh.astype(F32).astype(BF16)
    ri = lax.broadcasted_iota(I32, (tr, tr), 0)
    ci = lax.broadcasted_iota(I32, (tr, tr), 1)
    before = (ri < ci).astype(F32).astype(BF16)
    excl = jnp.dot(ohb, before, preferred_element_type=F32)
    pos = jnp.where(oh, excl + carry[...] + base_ref[...], 0.0)
    d1 = jnp.sum(pos[:N_EXPERTS], axis=0, keepdims=True)
    d2 = jnp.sum(pos[N_EXPERTS:], axis=0, keepdims=True)
    d1_ref[...] = d1.astype(I32)
    d2_ref[...] = d2.astype(I32)
    carry[...] += jnp.sum(oh.astype(F32), axis=1, keepdims=True)


def _placement(routet, tr=512):
    T = routet.shape[1]
    E = N_EXPERTS
    rt = pl.BlockSpec((8, tr), lambda i: (0, i))
    col = pl.BlockSpec((2 * E, 1), lambda i: (0, 0))
    cnt = pl.pallas_call(
        functools.partial(_count_kernel, tr=tr), grid=(T // tr,), in_specs=[rt], out_specs=col,
        out_shape=jax.ShapeDtypeStruct((2 * E, 1), F32),
        compiler_params=_cparams("arbitrary"), name="moe_count")(routet)
    c1 = cnt[:E, 0].astype(I32)
    c2 = cnt[E:, 0].astype(I32)
    padded = ((c1 + c2 + FFN_BLOCK - 1) // FFN_BLOCK) * FFN_BLOCK
    pend = jnp.cumsum(padded)
    pstart = pend - padded
    base = jnp.concatenate([pstart, pstart + c1]).astype(F32).reshape(2 * E, 1)
    lane_row = pl.BlockSpec((1, tr), lambda i: (0, i))
    d1, d2 = pl.pallas_call(
        functools.partial(_place_kernel, tr=tr), grid=(T // tr,), in_specs=[rt, col], out_specs=[lane_row, lane_row],
        out_shape=[jax.ShapeDtypeStruct((1, T), I32)] * 2,
        scratch_shapes=[pltpu.VMEM((2 * E, 1), F32)],
        compiler_params=_cparams("arbitrary"), name="moe_place")(routet, base)
    n_blocks = (T * TOP_K + E * (FFN_BLOCK - 1) + FFN_BLOCK - 1) // FFN_BLOCK
    starts = jnp.arange(n_blocks, dtype=I32) * FFN_BLOCK
    blk_e = jnp.minimum(jnp.sum((pend[None, :] <= starts[:, None]).astype(I32), axis=1), E - 1)
    blk_first = jnp.concatenate([jnp.ones((1,), I32), (blk_e[1:] != blk_e[:-1]).astype(I32)])
    n_used = (pend[-1] // FFN_BLOCK).astype(I32).reshape(1)
    ids = jnp.arange(E, dtype=I32)
    later = (padded > 0)[None, :] & (ids[None, :] > ids[:, None])
    next_tab = jnp.min(jnp.where(later, ids[None, :], E), axis=1)
    blk_next = jnp.where(next_tab == E, -1, next_tab)[blk_e]
    return d1.reshape(T), d2.reshape(T), pend.astype(I32), blk_e, blk_first, blk_next, n_used, n_blocks


def _dispatch_kernel(d1_ref, d2_ref, pend_ref, src_ref, xs_ref, zbuf, sem, zsem, *, tb):
    @pl.when(pl.program_id(0) == 0)
    def _():
        zbuf[...] = jnp.zeros_like(zbuf)
        tails = []
        for e in range(N_EXPERTS):
            end = pend_ref[e]
            nonempty = end > (pend_ref[e - 1] if e else 0)
            start = pl.multiple_of(jnp.maximum(end - FFN_BLOCK, 0), FFN_BLOCK)
            tails.append((nonempty, pltpu.make_async_copy(zbuf, xs_ref.at[pl.ds(start, FFN_BLOCK), :], zsem)))
        for nonempty, copy in tails:
            pl.when(nonempty)(copy.start)

        def spare(b):
            return pltpu.make_async_copy(
                zbuf, xs_ref.at[pl.ds(pl.multiple_of(b * FFN_BLOCK, FFN_BLOCK), FFN_BLOCK), :], zsem)
        first_spare = pend_ref[N_EXPERTS - 1] // FFN_BLOCK
        n_blocks = xs_ref.shape[0] // FFN_BLOCK
        lax.fori_loop(first_spare, n_blocks, lambda b, c: (spare(b).start(), c)[1], 0)
        for nonempty, copy in tails:
            pl.when(nonempty)(copy.wait)
        lax.fori_loop(first_spare, n_blocks, lambda b, c: (spare(b).wait(), c)[1], 0)

    def issue(t, _):
        row = src_ref.at[pl.ds(t, 1), :]
        pltpu.make_async_copy(row, xs_ref.at[pl.ds(d1_ref[t], 1), :], sem).start(priority=0)
        pltpu.make_async_copy(row, xs_ref.at[pl.ds(d2_ref[t], 1), :], sem).start(priority=1)
        return 0

    lax.fori_loop(0, tb, issue, 0, unroll=8)
    pltpu.make_async_copy(src_ref, xs_ref.at[pl.ds(0, tb), :], sem).wait()
    pltpu.make_async_copy(src_ref, xs_ref.at[pl.ds(0, tb), :], sem).wait()


def _dispatch(h1p, d1, d2, pend, n_rows, tb=1024):
    T, W = h1p.shape
    idx = pl.BlockSpec((tb,), lambda i: (i,), memory_space=pltpu.SMEM)
    return pl.pallas_call(
        functools.partial(_dispatch_kernel, tb=tb), grid=(T // tb,),
        in_specs=[idx, idx, pl.BlockSpec((N_EXPERTS,), lambda i: (0,), memory_space=pltpu.SMEM),
                  pl.BlockSpec((tb, W), lambda i: (i, 0))],
        out_specs=pl.BlockSpec(memory_space=pl.ANY),
        out_shape=jax.ShapeDtypeStruct((n_rows, W), U32),
        scratch_shapes=[pltpu.VMEM((FFN_BLOCK, W), U32), pltpu.SemaphoreType.DMA(()), pltpu.SemaphoreType.DMA(())],
        compiler_params=_cparams("arbitrary"), name="moe_dispatch",
    )(d1, d2, pend, h1p)


def _ffn_kernel(blk_e_ref, blk_first_ref, blk_next_ref, n_used_ref, xs_ref, wgu_hbm, wdn_hbm, y_ref,
                wgu_f, wdn_f, wgu_s, wdn_s, sem, *, ff, layer):
    i = pl.program_id(0)
    used = i < n_used_ref[0]

    def weight_copies(e):
        copies = []
        for k, (src, dst) in enumerate(((wgu_hbm, wgu_f), (wdn_hbm, wdn_f))):
            rows = dst.shape[0] // FFN_WEIGHT_CHUNKS
            for c in range(FFN_WEIGHT_CHUNKS):
                r = pl.ds(c * rows, rows)
                copies.append(pltpu.make_async_copy(src.at[layer, e, r, :], dst.at[r, :],
                                                    sem.at[k * FFN_WEIGHT_CHUNKS + c]))
        return copies

    @pl.when(i == 0)
    def _():
        for c in weight_copies(blk_e_ref[0]):
            c.start(priority=1)

    @pl.when(used & (blk_first_ref[i] == 1))
    def _():
        for c in weight_copies(blk_e_ref[i]):
            c.wait()
        wgu_s[...] = wgu_f[...].astype(BF16)
        wdn_s[...] = wdn_f[...].astype(BF16)

        @pl.when(blk_next_ref[i] >= 0)
        def _():
            for c in weight_copies(blk_next_ref[i]):
                c.start(priority=1)

    @pl.when(used)
    def _():
        lo, hi = _unpack_halves(xs_ref[...])
        x = jnp.concatenate([lo, hi], axis=1).astype(BF16)
        gu = jnp.dot(x, wgu_s[...], preferred_element_type=F32)
        act = (_silu(gu[:, :ff]) * gu[:, ff:]).astype(BF16)
        y_ref[...] = _pack_halves(jnp.dot(act, wdn_s[...], preferred_element_type=F32))

    @pl.when(i >= n_used_ref[0])
    def _():
        y_ref[...] = jnp.zeros_like(y_ref)


def _ffn(xs, w_gu, w_dn, layer, blk_e, blk_first, blk_next, n_used, n_blocks):
    _, E, D, FF2 = w_gu.shape
    ff = FF2 // 2
    W = xs.shape[1]
    any_spec = pl.BlockSpec(memory_space=pl.ANY)
    grid_spec = pltpu.PrefetchScalarGridSpec(
        num_scalar_prefetch=4, grid=(n_blocks,),
        in_specs=[pl.BlockSpec((FFN_BLOCK, W), lambda i, be, bf, bn, nu: (jnp.minimum(i, nu[0] - 1), 0)),
                  any_spec, any_spec],
        out_specs=pl.BlockSpec((FFN_BLOCK, W), lambda i, *_: (i, 0)),
        scratch_shapes=[pltpu.VMEM((D, FF2), F32), pltpu.VMEM((ff, D), F32),
                        pltpu.VMEM((D, FF2), BF16), pltpu.VMEM((ff, D), BF16),
                        pltpu.SemaphoreType.DMA((2 * FFN_WEIGHT_CHUNKS,))])
    return pl.pallas_call(
        functools.partial(_ffn_kernel, ff=ff, layer=layer), grid_spec=grid_spec,
        out_shape=jax.ShapeDtypeStruct(xs.shape, U32),
        compiler_params=_cparams("arbitrary"), name="moe_ffn",
    )(blk_e, blk_first, blk_next, n_used, xs, w_gu, w_dn)


def _combine_kernel(d1_ref, d2_ref, d1n_ref, d2n_ref, y_ref, h1_ref, route_ref, g_ref, b_ref, h2_ref, h2b_ref,
                    ybuf, sem, *, tc, alpha):
    i = pl.program_id(0)
    slot = i % 2

    def row_copies(r1, r2, s, t):
        return (pltpu.make_async_copy(y_ref.at[pl.ds(r1[t], 1), :], ybuf.at[s, 0, pl.ds(t, 1), :], sem.at[s]),
                pltpu.make_async_copy(y_ref.at[pl.ds(r2[t], 1), :], ybuf.at[s, 1, pl.ds(t, 1), :], sem.at[s]))

    def wait_slot(s):
        for j in range(TOP_K):
            pltpu.make_async_copy(y_ref.at[pl.ds(0, tc), :], ybuf.at[s, j], sem.at[s]).wait()

    @pl.when(i == 0)
    def _():
        def issue(t, _):
            for j, c in enumerate(row_copies(d1_ref, d2_ref, 0, t)):
                c.start(priority=j)
            return 0
        lax.fori_loop(0, tc, issue, 0, unroll=8)

    wait_slot(slot)
    for t in range(tc):
        for j, c in enumerate(row_copies(d1n_ref, d2n_ref, 1 - slot, t)):
            c.start(priority=j)

    route = route_ref[...]
    g1 = route[:, 2:3]
    g2 = route[:, 3:4]
    lo1, hi1 = _unpack_halves(ybuf[slot, 0])
    lo2, hi2 = _unpack_halves(ybuf[slot, 1])
    ffn = jnp.concatenate([g1 * lo1 + g2 * lo2, g1 * hi1 + g2 * hi2], axis=1)
    h2 = _layer_norm(alpha * h1_ref[...] + ffn, g_ref[...], b_ref[...])
    h2_ref[...] = h2
    h2b_ref[...] = h2.astype(BF16)

    @pl.when(i == pl.num_programs(0) - 1)
    def _():
        wait_slot(1 - slot)


def _combine(y, d1, d2, h1, route, g, b, *, alpha, tc=256):
    T, D = h1.shape
    W = y.shape[1]
    n = T // tc
    idx = pl.BlockSpec((tc,), lambda i: (i,), memory_space=pltpu.SMEM)
    idx_next = pl.BlockSpec((tc,), lambda i: (jnp.minimum(i + 1, n - 1),), memory_space=pltpu.SMEM)
    row = pl.BlockSpec((tc, D), lambda i: (i, 0))
    vec = pl.BlockSpec((1, D), lambda i: (0, 0))
    return pl.pallas_call(
        functools.partial(_combine_kernel, tc=tc, alpha=alpha), grid=(n,),
        in_specs=[idx, idx, idx_next, idx_next, pl.BlockSpec(memory_space=pl.ANY), row,
                  pl.BlockSpec((tc, LANES), lambda i: (i, 0)), vec, vec],
        out_specs=[row, row],
        out_shape=[jax.ShapeDtypeStruct((T, D), F32), jax.ShapeDtypeStruct((T, D), BF16)],
        scratch_shapes=[pltpu.VMEM((2, TOP_K, tc, W), U32), pltpu.SemaphoreType.DMA((2,))],
        compiler_params=_cparams("arbitrary"), name="moe_combine",
    )(d1, d2, d1, d2, y, h1, route, g.reshape(1, D), b.reshape(1, D))


def kernel(x, emb_ln_g, emb_ln_b, w_in, short_conv_w, a_log, dt_bias, dn_norm_w, dw_conv_w, dw_conv_b,
           conv_ln_g, conv_ln_b, w_out, ln1_g, ln1_b, w_group, b_group, w_expert, b_expert, w_gate_up,
           w_down, ln2_g, ln2_b):
    B, L, D = x.shape
    T = B * L
    depth = w_in.shape[0]
    H = a_log.shape[2]
    dn_w = H * DN_HEAD_DIM
    cv_w = dw_conv_w.shape[2]
    n_ab = 4 * H
    alpha = (2 * depth) ** 0.25
    C = DN_CHUNK

    h, hb = _emb_ln(x.reshape(T, D), emb_ln_g, emb_ln_b)
    for l in range(depth):
        w = w_in[l]
        w_main = jnp.concatenate([w[:, :4 * dn_w], w[:, 4 * dn_w + n_ab:]], axis=1).astype(BF16)
        proj, gt = _in_proj(hb, w_main, w[:, 4 * dn_w:4 * dn_w + n_ab], a_log[l], dt_bias[l], B=B, L=L)
        gt = gt.reshape(B, n_ab, L // C, C)
        dn = _deltanet(proj, gt, short_conv_w[l], B=B, L=L, H=H)
        cv = _conformer(proj, dw_conv_w[l], dw_conv_b[l], conv_ln_g[l], conv_ln_b[l],
                        B=B, L=L, col0=4 * dn_w, width=cv_w)
        n_r = N_GROUPS + N_EXPERTS
        w_router = jnp.pad(jnp.concatenate([w_group[l], w_expert[l]], axis=1),
                           ((0, 0), (0, LANES - n_r))).astype(BF16)
        b_router = jnp.pad(jnp.concatenate([b_group[l], b_expert[l]]), (0, LANES - n_r)).reshape(1, LANES)
        h1, h1p, route, routet = _out_router(
            dn.reshape(T, dn_w), proj, 3 * H, dn_norm_w[l], cv.reshape(T, cv_w), h, w_out[l].astype(BF16),
            ln1_g[l], ln1_b[l],
            w_router, b_router, alpha=alpha)
        d1, d2, pend, blk_e, blk_first, blk_next, n_used, n_blocks = _placement(routet)
        xs = _dispatch(h1p, d1, d2, pend, n_blocks * FFN_BLOCK)
        y = _ffn(xs, w_gate_up, w_down, l, blk_e, blk_first, blk_next, n_used, n_blocks)
        h, hb = _combine(y, d1, d2, h1, route, ln2_g[l], ln2_b[l], alpha=alpha)
    return h.reshape(B, L, D)
```

```python
import functools
import itertools

import jax
import jax.numpy as jnp
from jax import lax
from jax.experimental import pallas as pl
from jax.experimental.pallas import tpu as pltpu

F32 = jnp.float32
BF16 = jnp.bfloat16
U32 = jnp.uint32
I32 = jnp.int32

LANES = 128
DN_HEAD_DIM = 128
DN_CHUNK = 128
PREP_UNROLL = 8
CV_GROUP = 128
N_GROUPS = 8
EXPERTS_PER_GROUP = 8
N_EXPERTS = N_GROUPS * EXPERTS_PER_GROUP
TOP_K = 2
FFN_BLOCK = 256
FFN_WEIGHT_CHUNKS = 4
CAST_CHUNK_ELEMS = 32 * 1024
LN_EPS = 1e-5
RMS_EPS = 1e-6
VMEM_LIMIT = 56 * 1024 * 1024


def _cparams(*sem):
    return pltpu.CompilerParams(dimension_semantics=sem, vmem_limit_bytes=VMEM_LIMIT)


def _layer_norm(x, g, b):
    mu = jnp.mean(x, -1, keepdims=True)
    xc = x - mu
    var = jnp.mean(xc * xc, -1, keepdims=True)
    return xc * lax.rsqrt(var + LN_EPS) * g + b


def _sigmoid(x):
    return 1.0 / (1.0 + jnp.exp(-x))


def _silu(x):
    return x * _sigmoid(x)


def _pack_halves(y):
    n = y.shape[1] // 2
    bits = lax.bitcast_convert_type(y.astype(BF16).astype(F32), U32)
    return (bits[:, :n] >> 16) | bits[:, n:]


def _unpack_halves(u):
    lo = lax.bitcast_convert_type(u << 16, F32)
    hi = lax.bitcast_convert_type(u & jnp.uint32(0xFFFF0000), F32)
    return lo, hi


def _emb_ln_kernel(x_ref, g_ref, b_ref, h_ref, hb_ref):
    h = _layer_norm(x_ref[...], g_ref[...], b_ref[...])
    h_ref[...] = h
    hb_ref[...] = h.astype(BF16)


def _emb_ln(x, g, b, tm=512):
    T, D = x.shape
    row = pl.BlockSpec((tm, D), lambda i: (i, 0))
    vec = pl.BlockSpec((1, D), lambda i: (0, 0))
    return pl.pallas_call(
        _emb_ln_kernel, grid=(T // tm,), in_specs=[row, vec, vec], out_specs=[row, row],
        out_shape=[jax.ShapeDtypeStruct((T, D), F32), jax.ShapeDtypeStruct((T, D), BF16)],
        compiler_params=_cparams("parallel"), name="emb_ln")(x, g.reshape(1, D), b.reshape(1, D))


def _in_proj_kernel(x_ref, w_ref, wab_ref, alog_ref, dtb_ref, o_ref, gt_ref, *, n_decay, n_gate_rows):
    @pl.when(pl.program_id(1) == 0)
    def _():
        ab = jnp.dot(x_ref[...], wab_ref[0].astype(BF16), preferred_element_type=F32)
        lane = lax.broadcasted_iota(I32, ab.shape, 1)
        sp = ab + dtb_ref[...]
        softplus = jnp.maximum(sp, 0.0) + jnp.log(1.0 + jnp.exp(-jnp.abs(sp)))
        decay = -jnp.exp(alog_ref[...]) * softplus
        gt_ref[0] = jnp.where(lane < n_decay, decay, _sigmoid(ab)).T[:n_gate_rows, :]

    acc = jnp.dot(x_ref[...], w_ref[...], preferred_element_type=F32)
    for j in range(o_ref.shape[0]):
        o_ref[j] = acc[:, j * LANES:(j + 1) * LANES]


def _in_proj(hb, w, w_raw, layer, ab_col, a_log, dt_bias, *, B, L, tm=1024, tn=1024):
    T, D = hb.shape
    N = w.shape[1]
    n = a_log.size
    n_gate_rows = 2 * n
    assert ab_col % LANES == 0 and n_gate_rows <= LANES
    pad = lambda v: jnp.pad(v.reshape(1, n).astype(F32), ((0, 0), (0, LANES - n)))
    vec = pl.BlockSpec((1, LANES), lambda i, j: (0, 0))
    per_batch = L // tm
    return pl.pallas_call(
        functools.partial(_in_proj_kernel, n_decay=n, n_gate_rows=n_gate_rows), grid=(T // tm, N // tn),
        in_specs=[pl.BlockSpec((tm, D), lambda i, j: (i, 0)), pl.BlockSpec((D, tn), lambda i, j: (0, j)),
                  pl.BlockSpec((1, D, LANES), lambda i, j: (layer, 0, ab_col // LANES)), vec, vec],
        out_specs=[pl.BlockSpec((tn // LANES, tm, LANES), lambda i, j: (j, i, 0)),
                   pl.BlockSpec((1, n_gate_rows, tm), lambda i, j: (i // per_batch, 0, i % per_batch))],
        out_shape=[jax.ShapeDtypeStruct((N // LANES, T, LANES), F32),
                   jax.ShapeDtypeStruct((B, n_gate_rows, L), F32)],
        compiler_params=_cparams("parallel", "arbitrary"), name="in_proj",
    )(hb, w, w_raw, pad(a_log), pad(dt_bias))


def _dot(a, b):
    return jnp.dot(a.astype(BF16), b.astype(BF16), preferred_element_type=F32)


def _dot_nt(a, b):
    return lax.dot_general(a.astype(BF16), b.astype(BF16), (((1,), (1,)), ((), ())),
                           preferred_element_type=F32)


def _dot_tn(a, b):
    return lax.dot_general(a.astype(BF16), b.astype(BF16), (((0,), (0,)), ((), ())),
                           preferred_element_type=F32)


def _deltanet_kernel(q_ref, k_ref, v_ref, cq_ref, ck_ref, cv_ref, gt_ref, o_ref,
                     qs, ks, vs, ob, wq_s, at_s, kd_s, u_s, gl_s, s_s, rhs_s, *, L, C, H, R, n_items):
    DH = DN_HEAD_DIM
    s = pl.program_id(0)
    h = jnp.minimum(s, n_items - 1) % H
    slot_p = s % 2
    slot_s = 1 - slot_p
    NC = L // C
    U = PREP_UNROLL

    def conv_rows(r0, first, last):
        rows = pl.ds(r0, R)
        rid = lax.broadcasted_iota(I32, (R, DH), 0)

        def conv_silu(ref, w_ref):
            x = ref[0, rows, :]
            if first:
                x_prev = jnp.where(rid == 0, 0.0, pltpu.roll(x, 1, 0))
            else:
                x_prev = ref[0, pl.ds(r0 - 1, R), :]
            if last:
                x_next = jnp.where(rid == R - 1, 0.0, pltpu.roll(x, R - 1, 0))
            else:
                x_next = ref[0, pl.ds(r0 + 1, R), :]
            return _silu(w_ref[0:1, :] * x_prev + w_ref[1:2, :] * x + w_ref[2:3, :] * x_next)

        q = conv_silu(q_ref, cq_ref)
        k = conv_silu(k_ref, ck_ref)
        qs[rows, :] = q * (lax.rsqrt(jnp.sum(q * q, -1, keepdims=True) + RMS_EPS) * (DH ** -0.5))
        ks[rows, :] = k * lax.rsqrt(jnp.sum(k * k, -1, keepdims=True) + RMS_EPS)
        vs[rows, :] = conv_silu(v_ref, cv_ref)

    @pl.when(s < n_items)
    def _():
        conv_rows(0, True, False)
        lax.fori_loop(1, L // R - 1, lambda i, c: (conv_rows(pl.multiple_of(i * R, R), False, False), c)[1], 0)
        conv_rows(L - R, False, True)

    ri = lax.broadcasted_iota(I32, (C, C), 0)
    ci = lax.broadcasted_iota(I32, (C, C), 1)
    eye = ri == ci
    eye_f = eye.astype(F32)
    incl = (ri >= ci, ri <= ci)
    strict = (ri > ci, ri < ci)
    n_sq = C.bit_length() - 2
    zero_b = jnp.zeros((C, C), BF16)

    def block_diag(a, b):
        return jnp.concatenate([jnp.concatenate([a, zero_b], axis=1),
                                jnp.concatenate([zero_b, b], axis=1)], axis=0)

    def prep_start(c, j):
        rows = pl.ds(pl.multiple_of(c * C, C), C)
        q = qs[rows, :]
        k = ks[rows, :]
        v = vs[rows, :]
        kq = _dot_nt(jnp.concatenate([k, q], axis=0), k)
        kk, qk = kq[:C], kq[C:]
        a2 = []
        for d in range(2):
            g_row = gt_ref[0, d * H + h, pl.ds(c, 1), :]
            b_row = gt_ref[0, (2 + d) * H + h, pl.ds(c, 1), :]
            m, ms, mt = incl[d], strict[d], incl[1 - d]
            g_col = jnp.sum(jnp.where(eye, g_row, 0.0), axis=1, keepdims=True)
            b_col = jnp.sum(jnp.where(eye, b_row, 0.0), axis=1, keepdims=True)
            gc_col = jnp.sum(jnp.where(m, g_row, 0.0), axis=1, keepdims=True)
            gc_row = jnp.sum(jnp.where(mt, g_col, 0.0), axis=0, keepdims=True)
            tot = jnp.sum(g_row, axis=1, keepdims=True)
            decay = jnp.where(m, jnp.exp(jnp.where(m, gc_col - gc_row, 0.0)), 0.0)
            a2.append(jnp.where(ms, -(kk * b_col * decay), 0.0))
            egc = jnp.exp(gc_col)
            rhs_s[d, j] = jnp.concatenate([v * b_col, k * (b_col * egc)], axis=1).astype(BF16)
            wq_s[slot_p, d, c, pl.ds(C, C), :] = (q * egc).astype(BF16)
            at_s[slot_p, d, c] = (qk * decay).astype(BF16)
            kd_s[slot_p, d, c] = (k * jnp.exp(tot - gc_col)).astype(BF16)
            gl_s[slot_p, d, pl.ds(c, 1), :] = jnp.broadcast_to(jnp.exp(tot), (1, DH))
        return jnp.concatenate(a2, axis=1)

    def prep_phases(i):
        cs = [i * U + j for j in range(U)]
        ps = []
        for j, c in enumerate(cs):
            ps.append(prep_start(c, j))
            if j == U // 2 - 1:
                yield
        eye2 = jnp.concatenate([eye_f, eye_f], axis=1)
        xs_ = [eye2 + p for p in ps]
        yield
        for _ in range(n_sq):
            pbs = [p.astype(BF16) for p in ps]
            ps = [jnp.dot(pb, block_diag(pb[:, :C], pb[:, C:]), preferred_element_type=F32) for pb in pbs]
            yield
            pbs = [p.astype(BF16) for p in ps]
            xs_ = [x + jnp.dot(x.astype(BF16), block_diag(pb[:, :C], pb[:, C:]), preferred_element_type=F32)
                   for x, pb in zip(xs_, pbs)]
            yield
        for j, (c, x) in enumerate(zip(cs, xs_)):
            for d in range(2):
                sol = jnp.dot(x[:, d * C:(d + 1) * C].astype(BF16), rhs_s[d, j], preferred_element_type=F32)
                u_s[slot_p, d, c] = sol[:, :DH]
                wq_s[slot_p, d, c, pl.ds(0, C), :] = sol[:, DH:].astype(BF16)
            if j == U // 2 - 1:
                yield
        yield

    def scan_stages(n):
        cf, cb = n, NC - 1 - n
        s_f, s_b = s_s[0], s_s[1]
        wq = jnp.concatenate([wq_s[slot_s, 0, cf], wq_s[slot_s, 1, cb]], axis=1)
        r1 = jnp.dot(wq, block_diag(s_f.astype(BF16), s_b.astype(BF16)), preferred_element_type=F32)
        yield
        u = jnp.concatenate([u_s[slot_s, 0, cf], u_s[slot_s, 1, cb]], axis=1)
        v_new = (u - r1[:C]).astype(BF16)
        vd = block_diag(v_new[:, :DH], v_new[:, DH:])
        at = jnp.concatenate([at_s[slot_s, 0, cf], at_s[slot_s, 1, cb]], axis=1)
        o = r1[C:] + jnp.dot(at, vd, preferred_element_type=F32)
        kd = jnp.concatenate([kd_s[slot_s, 0, cf], kd_s[slot_s, 1, cb]], axis=0)
        ds = lax.dot_general(kd, vd, (((0,), (0,)), ((), ())), preferred_element_type=F32)
        s_s[0] = s_f * gl_s[slot_s, 0, pl.ds(cf, 1), :] + ds[:, :DH]
        s_s[1] = s_b * gl_s[slot_s, 1, pl.ds(cb, 1), :] + ds[:, DH:]
        o_ref[0, pl.ds(pl.multiple_of(cf * C, C), C), :] = o[:, :DH]
        ob[pl.ds(pl.multiple_of(cb * C, C), C), :] = o[:, DH:]
        yield

    def run(prepare, scan):
        def trip(i, _):
            prep_it = prep_phases(i) if prepare else iter(())
            scan_it = itertools.chain.from_iterable(scan_stages(i * U + j) for j in range(U)) if scan else iter(())
            for _ in itertools.zip_longest(scan_it, prep_it):
                pass
            return 0
        if scan:
            s_s[...] = jnp.zeros_like(s_s)
        lax.fori_loop(0, NC // U, trip, 0)

    @pl.when(s == 0)
    def _():
        run(True, False)

    @pl.when((s > 0) & (s < n_items))
    def _():
        run(True, True)

    @pl.when(s == n_items)
    def _():
        run(False, True)

    def sum_rows(i, _):
        rows = pl.ds(pl.multiple_of(i * R, R), R)
        o_ref[0, rows, :] = o_ref[0, rows, :] + ob[rows, :]
        return 0

    @pl.when(s > 0)
    def _():
        lax.fori_loop(0, L // R, sum_rows, 0)


def _deltanet(proj, gt, conv_w, *, B, L, H):
    DH, C, R = DN_HEAD_DIM, DN_CHUNK, 256
    assert C == DH and L % (C * PREP_UNROLL) == 0
    NC = L // C
    n_items = B * H
    cur = lambda s: jnp.minimum(s, n_items - 1)
    prev = lambda s: jnp.maximum(s - 1, 0)
    col = lambda off: pl.BlockSpec((1, L, DH), lambda s: (off + cur(s) % H, cur(s) // H, 0))
    cw = lambda off: pl.BlockSpec((3, DH), lambda s: (0, off + cur(s) % H))
    kern = functools.partial(_deltanet_kernel, L=L, C=C, H=H, R=R, n_items=n_items)
    return pl.pallas_call(
        kern, grid=(n_items + 1,),
        in_specs=[col(0), col(H), col(2 * H), cw(0), cw(H), cw(2 * H),
                  pl.BlockSpec((1, 4 * H, NC, C), lambda s: (cur(s) // H, 0, 0, 0))],
        out_specs=pl.BlockSpec((1, L, DH), lambda s: (prev(s) // H, 0, prev(s) % H)),
        out_shape=jax.ShapeDtypeStruct((B, L, H * DH), F32),
        scratch_shapes=[pltpu.VMEM((L, DH), F32), pltpu.VMEM((L, DH), F32), pltpu.VMEM((L, DH), F32),
                        pltpu.VMEM((L, DH), F32),
                        pltpu.VMEM((2, 2, NC, 2 * C, DH), BF16), pltpu.VMEM((2, 2, NC, C, C), BF16),
                        pltpu.VMEM((2, 2, NC, C, DH), BF16), pltpu.VMEM((2, 2, NC, C, DH), F32),
                        pltpu.VMEM((2, 2, NC, DH), F32), pltpu.VMEM((2, DH, DH), F32),
                        pltpu.VMEM((2, PREP_UNROLL, C, 2 * DH), BF16)],
        compiler_params=_cparams("arbitrary"), name="deltanet",
    )(proj, proj, proj, conv_w, conv_w, conv_w, gt)


def _conformer_kernel(val_ref, gate_ref, w_ref, b_ref, g_ref, beta_ref, o_ref, ypad, *, L, K, R, HALO):
    pad = K // 2
    zeros = jnp.zeros((HALO, CV_GROUP), F32)
    ypad[pl.ds(0, HALO), :] = zeros
    ypad[pl.ds(HALO + L, HALO), :] = zeros

    def glu_rows(i, _):
        r0 = pl.multiple_of(i * R, R)
        ypad[pl.ds(HALO + r0, R), :] = val_ref[0, pl.ds(r0, R), :] * _sigmoid(gate_ref[0, pl.ds(r0, R), :])
        return 0

    lax.fori_loop(0, L // R, glu_rows, 0)

    def conv_rows(i, _):
        r0 = pl.multiple_of(i * R, R)
        acc = jnp.zeros((R, CV_GROUP), F32)
        for t in range(K):
            acc = acc + w_ref[t:t + 1, :] * ypad[pl.ds(r0 + (HALO - pad + t), R), :]
        y = _layer_norm(acc + b_ref[...], g_ref[...], beta_ref[...])
        o_ref[0, pl.ds(r0, R), :] = _silu(y)
        return 0

    lax.fori_loop(0, L // R, conv_rows, 0)


def _conformer(proj, w, b, g, beta, *, B, L, col0, width):
    K = w.shape[0]
    n = width // CV_GROUP
    HALO, R = 16, 256
    c0 = col0 // CV_GROUP
    vec = pl.BlockSpec((1, CV_GROUP), lambda bb, j: (0, j))
    kern = functools.partial(_conformer_kernel, L=L, K=K, R=R, HALO=HALO)
    return pl.pallas_call(
        kern, grid=(B, n),
        in_specs=[pl.BlockSpec((1, L, CV_GROUP), lambda bb, j: (c0 + j, bb, 0)),
                  pl.BlockSpec((1, L, CV_GROUP), lambda bb, j: (c0 + n + j, bb, 0)),
                  pl.BlockSpec((K, CV_GROUP), lambda bb, j: (0, j)), vec, vec, vec],
        out_specs=pl.BlockSpec((1, L, CV_GROUP), lambda bb, j: (bb, 0, j)),
        out_shape=jax.ShapeDtypeStruct((B, L, width), F32),
        scratch_shapes=[pltpu.VMEM((L + 2 * HALO, CV_GROUP), F32)],
        compiler_params=_cparams("parallel", "parallel"), name="conformer",
    )(proj, proj, w, b.reshape(1, width), g.reshape(1, width), beta.reshape(1, width))


def _out_router_kernel(dn_ref, z_ref, nw_ref, cv_ref, h_ref, w_ref, g_ref, b_ref, wr_ref, br_ref,
                       h1_ref, h1p_ref, route_ref, routet_ref, *, alpha, n_dn, sub):
    n_sub = h_ref.shape[0] // sub

    def stages(s):
        rows = pl.ds(s * sub, sub)
        heads = []
        for hh in range(n_dn // DN_HEAD_DIM):
            o = dn_ref[rows, hh * DN_HEAD_DIM:(hh + 1) * DN_HEAD_DIM]
            o = o * lax.rsqrt(jnp.mean(o * o, -1, keepdims=True) + RMS_EPS) * nw_ref[...]
            heads.append((o * _silu(z_ref[hh, rows, :])).astype(BF16))
        mix = jnp.dot(jnp.concatenate(heads, axis=1), w_ref[pl.ds(0, n_dn), :], preferred_element_type=F32)
        mix = mix + jnp.dot(cv_ref[rows, :].astype(BF16), w_ref[pl.ds(n_dn, w_ref.shape[0] - n_dn), :],
                            preferred_element_type=F32)
        yield
        h1 = _layer_norm(alpha * h_ref[rows, :] + mix, g_ref[...], b_ref[...])
        h1_ref[rows, :] = h1
        h1p_ref[rows, :] = _pack_halves(h1)
        h1b = h1.astype(BF16)
        yield
        logits = jnp.dot(h1b, wr_ref[...], preferred_element_type=F32) + br_ref[...]
        yield
        route = _route(logits)
        route_ref[rows, :] = route
        routet_ref[:, rows] = route.T[:8, :]
        yield

    n_stage = 4
    pipes = [stages(s) for s in range(n_sub)]
    for t in range(n_sub + n_stage - 1):
        for s in range(t, t - n_stage, -1):
            if 0 <= s < n_sub:
                next(pipes[s])


def _route(logits):
    lane = lax.broadcasted_iota(I32, logits.shape, 1)
    neg = jnp.float32(-jnp.inf)
    big = jnp.int32(LANES)
    gl = jnp.where(lane < N_GROUPS, logits, neg)
    gmax = jnp.max(gl, -1, keepdims=True)
    gsel = jnp.min(jnp.where(gl == gmax, lane, big), -1, keepdims=True)
    pg = 1.0 / jnp.sum(jnp.exp(gl - gmax), -1, keepdims=True)
    lo = N_GROUPS + gsel * EXPERTS_PER_GROUP
    el = jnp.where((lane >= lo) & (lane < lo + EXPERTS_PER_GROUP), logits, neg)
    e1 = jnp.max(el, -1, keepdims=True)
    i1 = jnp.min(jnp.where(el == e1, lane, big), -1, keepdims=True)
    el2 = jnp.where(lane == i1, neg, el)
    e2 = jnp.max(el2, -1, keepdims=True)
    i2 = jnp.min(jnp.where(el2 == e2, lane, big), -1, keepdims=True)
    r = jnp.exp(e2 - e1)
    p1 = 1.0 / (1.0 + r)
    p2 = r * p1
    route = jnp.where(lane == 0, (i1 - N_GROUPS).astype(F32),
                      jnp.where(lane == 1, (i2 - N_GROUPS).astype(F32),
                                jnp.where(lane == 2, pg * p1, jnp.where(lane == 3, pg * p2, 0.0))))
    return route


def _out_router(dn, proj, z_group, norm_w, cv, h, w_out, g, b, w_router, b_router, *, alpha, tm=512, sub=128):
    T, D = h.shape
    n_dn, n_cv = dn.shape[1], cv.shape[1]
    n_heads = n_dn // DN_HEAD_DIM
    assert z_group % n_heads == 0
    vec = pl.BlockSpec((1, D), lambda i: (0, 0))
    row = pl.BlockSpec((tm, D), lambda i: (i, 0))
    kern = functools.partial(_out_router_kernel, alpha=alpha, n_dn=n_dn, sub=sub)
    return pl.pallas_call(
        kern, grid=(T // tm,),
        in_specs=[pl.BlockSpec((tm, n_dn), lambda i: (i, 0)),
                  pl.BlockSpec((n_heads, tm, DN_HEAD_DIM), lambda i: (z_group // n_heads, i, 0)),
                  pl.BlockSpec((1, DN_HEAD_DIM), lambda i: (0, 0)),
                  pl.BlockSpec((tm, n_cv), lambda i: (i, 0)), row,
                  pl.BlockSpec((n_dn + n_cv, D), lambda i: (0, 0)), vec, vec,
                  pl.BlockSpec((D, LANES), lambda i: (0, 0)), pl.BlockSpec((1, LANES), lambda i: (0, 0))],
        out_specs=[row, pl.BlockSpec((tm, D // 2), lambda i: (i, 0)),
                   pl.BlockSpec((tm, LANES), lambda i: (i, 0)), pl.BlockSpec((8, tm), lambda i: (0, i))],
        out_shape=[jax.ShapeDtypeStruct((T, D), F32), jax.ShapeDtypeStruct((T, D // 2), U32),
                   jax.ShapeDtypeStruct((T, LANES), F32), jax.ShapeDtypeStruct((8, T), F32)],
        compiler_params=_cparams("parallel"), name="out_router",
    )(dn, proj, norm_w.reshape(1, DN_HEAD_DIM), cv, h, w_out, g.reshape(1, D), b.reshape(1, D), w_router, b_router)


def _onehot_t(routet_ref, tr):
    sub = lax.broadcasted_iota(I32, (2 * N_EXPERTS, tr), 0)
    e1 = routet_ref[0:1, :].astype(I32)
    e2 = routet_ref[1:2, :].astype(I32)
    return sub == jnp.where(sub < N_EXPERTS, e1, e2 + N_EXPERTS)


def _count_kernel(routet_ref, cnt_ref, *, tr):
    @pl.when(pl.program_id(0) == 0)
    def _():
        cnt_ref[...] = jnp.zeros_like(cnt_ref)
    oh = _onehot_t(routet_ref, tr).astype(F32)
    cnt_ref[...] += jnp.sum(oh, axis=1, keepdims=True)


def _place_kernel(routet_ref, base_ref, d1_ref, d2_ref, carry, *, tr):
    @pl.when(pl.program_id(0) == 0)
    def _():
        carry[...] = jnp.zeros_like(carry)
    oh = _onehot_t(routet_ref, tr)
    ohb = oh.astype(F32).astype(BF16)
    ri = lax.broadcasted_iota(I32, (tr, tr), 0)
    ci = lax.broadcasted_iota(I32, (tr, tr), 1)
    before = (ri < ci).astype(F32).astype(BF16)
    excl = jnp.dot(ohb, before, preferred_element_type=F32)
    pos = jnp.where(oh, excl + carry[...] + base_ref[...], 0.0)
    d1 = jnp.sum(pos[:N_EXPERTS], axis=0, keepdims=True)
    d2 = jnp.sum(pos[N_EXPERTS:], axis=0, keepdims=True)
    d1_ref[...] = d1.astype(I32)
    d2_ref[...] = d2.astype(I32)
    carry[...] += jnp.sum(oh.astype(F32), axis=1, keepdims=True)


def _placement(routet, tr=512):
    T = routet.shape[1]
    E = N_EXPERTS
    rt = pl.BlockSpec((8, tr), lambda i: (0, i))
    col = pl.BlockSpec((2 * E, 1), lambda i: (0, 0))
    cnt = pl.pallas_call(
        functools.partial(_count_kernel, tr=tr), grid=(T // tr,), in_specs=[rt], out_specs=col,
        out_shape=jax.ShapeDtypeStruct((2 * E, 1), F32),
        compiler_params=_cparams("arbitrary"), name="moe_count")(routet)
    c1 = cnt[:E, 0].astype(I32)
    c2 = cnt[E:, 0].astype(I32)
    padded = ((c1 + c2 + FFN_BLOCK - 1) // FFN_BLOCK) * FFN_BLOCK
    pend = jnp.cumsum(padded)
    pstart = pend - padded
    base = jnp.concatenate([pstart, pstart + c1]).astype(F32).reshape(2 * E, 1)
    lane_row = pl.BlockSpec((1, tr), lambda i: (0, i))
    d1, d2 = pl.pallas_call(
        functools.partial(_place_kernel, tr=tr), grid=(T // tr,), in_specs=[rt, col], out_specs=[lane_row, lane_row],
        out_shape=[jax.ShapeDtypeStruct((1, T), I32)] * 2,
        scratch_shapes=[pltpu.VMEM((2 * E, 1), F32)],
        compiler_params=_cparams("arbitrary"), name="moe_place")(routet, base)
    n_blocks = (T * TOP_K + E * (FFN_BLOCK - 1) + FFN_BLOCK - 1) // FFN_BLOCK
    starts = jnp.arange(n_blocks, dtype=I32) * FFN_BLOCK
    blk_e = jnp.minimum(jnp.sum((pend[None, :] <= starts[:, None]).astype(I32), axis=1), E - 1)
    blk_first = jnp.concatenate([jnp.ones((1,), I32), (blk_e[1:] != blk_e[:-1]).astype(I32)])
    n_used = (pend[-1] // FFN_BLOCK).astype(I32).reshape(1)
    ids = jnp.arange(E, dtype=I32)
    later = (padded > 0)[None, :] & (ids[None, :] > ids[:, None])
    next_tab = jnp.min(jnp.where(later, ids[None, :], E), axis=1)
    blk_next = jnp.where(next_tab == E, -1, next_tab)[blk_e]
    return d1.reshape(T), d2.reshape(T), pend.astype(I32), blk_e, blk_first, blk_next, n_used, n_blocks


def _dispatch_kernel(d1_ref, d2_ref, pend_ref, src_ref, xs_ref, zbuf, sem, zsem, *, tb):
    @pl.when(pl.program_id(0) == 0)
    def _():
        zbuf[...] = jnp.zeros_like(zbuf)
        tails = []
        for e in range(N_EXPERTS):
            end = pend_ref[e]
            nonempty = end > (pend_ref[e - 1] if e else 0)
            start = pl.multiple_of(jnp.maximum(end - FFN_BLOCK, 0), FFN_BLOCK)
            tails.append((nonempty, pltpu.make_async_copy(zbuf, xs_ref.at[pl.ds(start, FFN_BLOCK), :], zsem)))
        for nonempty, copy in tails:
            pl.when(nonempty)(copy.start)

        def spare(b):
            return pltpu.make_async_copy(
                zbuf, xs_ref.at[pl.ds(pl.multiple_of(b * FFN_BLOCK, FFN_BLOCK), FFN_BLOCK), :], zsem)
        first_spare = pend_ref[N_EXPERTS - 1] // FFN_BLOCK
        n_blocks = xs_ref.shape[0] // FFN_BLOCK
        lax.fori_loop(first_spare, n_blocks, lambda b, c: (spare(b).start(), c)[1], 0)
        for nonempty, copy in tails:
            pl.when(nonempty)(copy.wait)
        lax.fori_loop(first_spare, n_blocks, lambda b, c: (spare(b).wait(), c)[1], 0)

    def issue(t, _):
        row = src_ref.at[pl.ds(t, 1), :]
        pltpu.make_async_copy(row, xs_ref.at[pl.ds(d1_ref[t], 1), :], sem).start(priority=0)
        pltpu.make_async_copy(row, xs_ref.at[pl.ds(d2_ref[t], 1), :], sem).start(priority=1)
        return 0

    lax.fori_loop(0, tb, issue, 0, unroll=8)
    pltpu.make_async_copy(src_ref, xs_ref.at[pl.ds(0, tb), :], sem).wait()
    pltpu.make_async_copy(src_ref, xs_ref.at[pl.ds(0, tb), :], sem).wait()


def _dispatch(h1p, d1, d2, pend, n_rows, tb=1024):
    T, W = h1p.shape
    idx = pl.BlockSpec((tb,), lambda i: (i,), memory_space=pltpu.SMEM)
    return pl.pallas_call(
        functools.partial(_dispatch_kernel, tb=tb), grid=(T // tb,),
        in_specs=[idx, idx, pl.BlockSpec((N_EXPERTS,), lambda i: (0,), memory_space=pltpu.SMEM),
                  pl.BlockSpec((tb, W), lambda i: (i, 0))],
        out_specs=pl.BlockSpec(memory_space=pl.ANY),
        out_shape=jax.ShapeDtypeStruct((n_rows, W), U32),
        scratch_shapes=[pltpu.VMEM((FFN_BLOCK, W), U32), pltpu.SemaphoreType.DMA(()), pltpu.SemaphoreType.DMA(())],
        compiler_params=_cparams("arbitrary"), name="moe_dispatch",
    )(d1, d2, pend, h1p)


def _cast_rows(src, dst):
    rows = CAST_CHUNK_ELEMS // src.shape[1]

    def body(i, c):
        r = pl.ds(pl.multiple_of(i * rows, rows), rows)
        dst[r, :] = src[r, :].astype(BF16)
        return c

    lax.fori_loop(0, src.shape[0] // rows, body, 0, unroll=4)


def _ffn_kernel(blk_e_ref, blk_first_ref, blk_next_ref, n_used_ref, xs_ref, wgu_hbm, wdn_hbm, y_ref,
                wgu_f, wdn_f, wgu_s, wdn_s, sem, *, ff, layer):
    i = pl.program_id(0)
    used = i < n_used_ref[0]

    def weight_copies(e):
        copies = []
        for k, (src, dst) in enumerate(((wgu_hbm, wgu_f), (wdn_hbm, wdn_f))):
            rows = dst.shape[0] // FFN_WEIGHT_CHUNKS
            for c in range(FFN_WEIGHT_CHUNKS):
                r = pl.ds(c * rows, rows)
                copies.append(pltpu.make_async_copy(src.at[layer, e, r, :], dst.at[r, :],
                                                    sem.at[k * FFN_WEIGHT_CHUNKS + c]))
        return copies

    @pl.when(i == 0)
    def _():
        for c in weight_copies(blk_e_ref[0]):
            c.start(priority=1)

    @pl.when(used & (blk_first_ref[i] == 1))
    def _():
        for c in weight_copies(blk_e_ref[i]):
            c.wait()
        _cast_rows(wgu_f, wgu_s)
        _cast_rows(wdn_f, wdn_s)

        @pl.when(blk_next_ref[i] >= 0)
        def _():
            for c in weight_copies(blk_next_ref[i]):
                c.start(priority=1)

    @pl.when(used)
    def _():
        lo, hi = _unpack_halves(xs_ref[...])
        x = jnp.concatenate([lo, hi], axis=1).astype(BF16)
        gu = jnp.dot(x, wgu_s[...], preferred_element_type=F32)
        act = (_silu(gu[:, :ff]) * gu[:, ff:]).astype(BF16)
        y_ref[...] = _pack_halves(jnp.dot(act, wdn_s[...], preferred_element_type=F32))

    @pl.when(i >= n_used_ref[0])
    def _():
        y_ref[...] = jnp.zeros_like(y_ref)


def _ffn(xs, w_gu, w_dn, layer, blk_e, blk_first, blk_next, n_used, n_blocks):
    _, E, D, FF2 = w_gu.shape
    ff = FF2 // 2
    W = xs.shape[1]
    any_spec = pl.BlockSpec(memory_space=pl.ANY)
    grid_spec = pltpu.PrefetchScalarGridSpec(
        num_scalar_prefetch=4, grid=(n_blocks,),
        in_specs=[pl.BlockSpec((FFN_BLOCK, W), lambda i, be, bf, bn, nu: (jnp.minimum(i, nu[0] - 1), 0)),
                  any_spec, any_spec],
        out_specs=pl.BlockSpec((FFN_BLOCK, W), lambda i, *_: (i, 0)),
        scratch_shapes=[pltpu.VMEM((D, FF2), F32), pltpu.VMEM((ff, D), F32),
                        pltpu.VMEM((D, FF2), BF16), pltpu.VMEM((ff, D), BF16),
                        pltpu.SemaphoreType.DMA((2 * FFN_WEIGHT_CHUNKS,))])
    return pl.pallas_call(
        functools.partial(_ffn_kernel, ff=ff, layer=layer), grid_spec=grid_spec,
        out_shape=jax.ShapeDtypeStruct(xs.shape, U32),
        compiler_params=_cparams("arbitrary"), name="moe_ffn",
    )(blk_e, blk_first, blk_next, n_used, xs, w_gu, w_dn)


def _combine_kernel(d1_ref, d2_ref, d1n_ref, d2n_ref, y_ref, h1_ref, route_ref, g_ref, b_ref, h2_ref, h2b_ref,
                    ybuf, sem, *, tc, alpha):
    i = pl.program_id(0)
    slot = i % 2

    def row_copies(r1, r2, s, t):
        return (pltpu.make_async_copy(y_ref.at[pl.ds(r1[t], 1), :], ybuf.at[s, 0, pl.ds(t, 1), :], sem.at[s]),
                pltpu.make_async_copy(y_ref.at[pl.ds(r2[t], 1), :], ybuf.at[s, 1, pl.ds(t, 1), :], sem.at[s]))

    def wait_slot(s):
        for j in range(TOP_K):
            pltpu.make_async_copy(y_ref.at[pl.ds(0, tc), :], ybuf.at[s, j], sem.at[s]).wait()

    @pl.when(i == 0)
    def _():
        def issue(t, _):
            for j, c in enumerate(row_copies(d1_ref, d2_ref, 0, t)):
                c.start(priority=j)
            return 0
        lax.fori_loop(0, tc, issue, 0, unroll=8)

    wait_slot(slot)
    for t in range(tc):
        for j, c in enumerate(row_copies(d1n_ref, d2n_ref, 1 - slot, t)):
            c.start(priority=j)

    route = route_ref[...]
    g1 = route[:, 2:3]
    g2 = route[:, 3:4]
    lo1, hi1 = _unpack_halves(ybuf[slot, 0])
    lo2, hi2 = _unpack_halves(ybuf[slot, 1])
    ffn = jnp.concatenate([g1 * lo1 + g2 * lo2, g1 * hi1 + g2 * hi2], axis=1)
    h2 = _layer_norm(alpha * h1_ref[...] + ffn, g_ref[...], b_ref[...])
    h2_ref[...] = h2
    h2b_ref[...] = h2.astype(BF16)

    @pl.when(i == pl.num_programs(0) - 1)
    def _():
        wait_slot(1 - slot)


def _combine(y, d1, d2, h1, route, g, b, *, alpha, tc=256):
    T, D = h1.shape
    W = y.shape[1]
    n = T // tc
    idx = pl.BlockSpec((tc,), lambda i: (i,), memory_space=pltpu.SMEM)
    idx_next = pl.BlockSpec((tc,), lambda i: (jnp.minimum(i + 1, n - 1),), memory_space=pltpu.SMEM)
    row = pl.BlockSpec((tc, D), lambda i: (i, 0))
    vec = pl.BlockSpec((1, D), lambda i: (0, 0))
    return pl.pallas_call(
        functools.partial(_combine_kernel, tc=tc, alpha=alpha), grid=(n,),
        in_specs=[idx, idx, idx_next, idx_next, pl.BlockSpec(memory_space=pl.ANY), row,
                  pl.BlockSpec((tc, LANES), lambda i: (i, 0)), vec, vec],
        out_specs=[row, row],
        out_shape=[jax.ShapeDtypeStruct((T, D), F32), jax.ShapeDtypeStruct((T, D), BF16)],
        scratch_shapes=[pltpu.VMEM((2, TOP_K, tc, W), U32), pltpu.SemaphoreType.DMA((2,))],
        compiler_params=_cparams("arbitrary"), name="moe_combine",
    )(d1, d2, d1, d2, y, h1, route, g.reshape(1, D), b.reshape(1, D))


def kernel(x, emb_ln_g, emb_ln_b, w_in, short_conv_w, a_log, dt_bias, dn_norm_w, dw_conv_w, dw_conv_b,
           conv_ln_g, conv_ln_b, w_out, ln1_g, ln1_b, w_group, b_group, w_expert, b_expert, w_gate_up,
           w_down, ln2_g, ln2_b):
    B, L, D = x.shape
    T = B * L
    depth = w_in.shape[0]
    H = a_log.shape[2]
    dn_w = H * DN_HEAD_DIM
    cv_w = dw_conv_w.shape[2]
    n_ab = 4 * H
    alpha = (2 * depth) ** 0.25
    C = DN_CHUNK

    h, hb = _emb_ln(x.reshape(T, D), emb_ln_g, emb_ln_b)
    for l in range(depth):
        w = w_in[l]
        w_main = jnp.concatenate([w[:, :4 * dn_w], w[:, 4 * dn_w + n_ab:]], axis=1).astype(BF16)
        proj, gt = _in_proj(hb, w_main, w_in, l, 4 * dn_w, a_log[l], dt_bias[l], B=B, L=L)
        gt = gt.reshape(B, n_ab, L // C, C)
        dn = _deltanet(proj, gt, short_conv_w[l], B=B, L=L, H=H)
        cv = _conformer(proj, dw_conv_w[l], dw_conv_b[l], conv_ln_g[l], conv_ln_b[l],
                        B=B, L=L, col0=4 * dn_w, width=cv_w)
        n_r = N_GROUPS + N_EXPERTS
        w_router = jnp.pad(jnp.concatenate([w_group[l], w_expert[l]], axis=1),
                           ((0, 0), (0, LANES - n_r))).astype(BF16)
        b_router = jnp.pad(jnp.concatenate([b_group[l], b_expert[l]]), (0, LANES - n_r)).reshape(1, LANES)
        h1, h1p, route, routet = _out_router(
            dn.reshape(T, dn_w), proj, 3 * H, dn_norm_w[l], cv.reshape(T, cv_w), h, w_out[l].astype(BF16),
            ln1_g[l], ln1_b[l],
            w_router, b_router, alpha=alpha)
        d1, d2, pend, blk_e, blk_first, blk_next, n_used, n_blocks = _placement(routet)
        xs = _dispatch(h1p, d1, d2, pend, n_blocks * FFN_BLOCK)
        y = _ffn(xs, w_gate_up, w_down, l, blk_e, blk_first, blk_next, n_used, n_blocks)
        h, hb = _combine(y, d1, d2, h1, route, ln2_g[l], ln2_b[l], alpha=alpha)
    return h.reshape(B, L, D)
```

```python
import functools
import itertools

import jax
import jax.numpy as jnp
from jax import lax
from jax.experimental import pallas as pl
from jax.experimental.pallas import tpu as pltpu

F32 = jnp.float32
BF16 = jnp.bfloat16
U32 = jnp.uint32
I32 = jnp.int32

LANES = 128
DN_HEAD_DIM = 128
DN_CHUNK = 128
PREP_UNROLL = 8
CV_GROUP = 128
N_GROUPS = 8
EXPERTS_PER_GROUP = 8
N_EXPERTS = N_GROUPS * EXPERTS_PER_GROUP
TOP_K = 2
FFN_BLOCK = 256
FFN_WEIGHT_CHUNKS = 4
CAST_CHUNK_ELEMS = 32 * 1024
LN_EPS = 1e-5
RMS_EPS = 1e-6
VMEM_LIMIT = 56 * 1024 * 1024


def _cparams(*sem):
    return pltpu.CompilerParams(dimension_semantics=sem, vmem_limit_bytes=VMEM_LIMIT)


def _layer_norm(x, g, b):
    mu = jnp.mean(x, -1, keepdims=True)
    xc = x - mu
    var = jnp.mean(xc * xc, -1, keepdims=True)
    return xc * lax.rsqrt(var + LN_EPS) * g + b


def _sigmoid(x):
    return 1.0 / (1.0 + jnp.exp(-x))


def _silu(x):
    return x * _sigmoid(x)


def _pack_halves(y):
    n = y.shape[1] // 2
    bits = lax.bitcast_convert_type(y.astype(BF16).astype(F32), U32)
    return (bits[:, :n] >> 16) | bits[:, n:]


def _unpack_halves(u):
    lo = lax.bitcast_convert_type(u << 16, F32)
    hi = lax.bitcast_convert_type(u & jnp.uint32(0xFFFF0000), F32)
    return lo, hi


def _emb_ln_kernel(x_ref, g_ref, b_ref, h_ref, hb_ref):
    h = _layer_norm(x_ref[...], g_ref[...], b_ref[...])
    h_ref[...] = h
    hb_ref[...] = h.astype(BF16)


def _emb_ln(x, g, b, tm=512):
    T, D = x.shape
    row = pl.BlockSpec((tm, D), lambda i: (i, 0))
    vec = pl.BlockSpec((1, D), lambda i: (0, 0))
    return pl.pallas_call(
        _emb_ln_kernel, grid=(T // tm,), in_specs=[row, vec, vec], out_specs=[row, row],
        out_shape=[jax.ShapeDtypeStruct((T, D), F32), jax.ShapeDtypeStruct((T, D), BF16)],
        compiler_params=_cparams("parallel"), name="emb_ln")(x, g.reshape(1, D), b.reshape(1, D))


def _in_proj_kernel(x_ref, w_ref, wab_ref, alog_ref, dtb_ref, o_ref, gt_ref, *, n_decay, n_gate_rows):
    @pl.when(pl.program_id(1) == 0)
    def _():
        ab = jnp.dot(x_ref[...], wab_ref[...], preferred_element_type=F32)
        lane = lax.broadcasted_iota(I32, ab.shape, 1)
        sp = ab + dtb_ref[...]
        softplus = jnp.maximum(sp, 0.0) + jnp.log(1.0 + jnp.exp(-jnp.abs(sp)))
        decay = -jnp.exp(alog_ref[...]) * softplus
        gt_ref[0] = jnp.where(lane < n_decay, decay, _sigmoid(ab)).T[:n_gate_rows, :]

    acc = jnp.dot(x_ref[...], w_ref[...], preferred_element_type=F32)
    for j in range(o_ref.shape[0]):
        o_ref[j] = acc[:, j * LANES:(j + 1) * LANES]


def _in_proj(hb, w, w_ab, a_log, dt_bias, *, B, L, tm=1024, tn=2048):
    T, D = hb.shape
    N = w.shape[1]
    n = a_log.size
    n_gate_rows = 2 * n
    pad = lambda v: jnp.pad(v.reshape(1, n).astype(F32), ((0, 0), (0, LANES - n)))
    wab = jnp.pad(w_ab, ((0, 0), (0, LANES - w_ab.shape[1]))).astype(BF16)
    vec = pl.BlockSpec((1, LANES), lambda i, j: (0, 0))
    per_batch = L // tm
    return pl.pallas_call(
        functools.partial(_in_proj_kernel, n_decay=n, n_gate_rows=n_gate_rows), grid=(T // tm, N // tn),
        in_specs=[pl.BlockSpec((tm, D), lambda i, j: (i, 0)), pl.BlockSpec((D, tn), lambda i, j: (0, j)),
                  pl.BlockSpec((D, LANES), lambda i, j: (0, 0)), vec, vec],
        out_specs=[pl.BlockSpec((tn // LANES, tm, LANES), lambda i, j: (j, i, 0)),
                   pl.BlockSpec((1, n_gate_rows, tm), lambda i, j: (i // per_batch, 0, i % per_batch))],
        out_shape=[jax.ShapeDtypeStruct((N // LANES, T, LANES), F32),
                   jax.ShapeDtypeStruct((B, n_gate_rows, L), F32)],
        compiler_params=_cparams("parallel", "arbitrary"), name="in_proj",
    )(hb, w, wab, pad(a_log), pad(dt_bias))


def _dot(a, b):
    return jnp.dot(a.astype(BF16), b.astype(BF16), preferred_element_type=F32)


def _dot_nt(a, b):
    return lax.dot_general(a.astype(BF16), b.astype(BF16), (((1,), (1,)), ((), ())),
                           preferred_element_type=F32)


def _dot_tn(a, b):
    return lax.dot_general(a.astype(BF16), b.astype(BF16), (((0,), (0,)), ((), ())),
                           preferred_element_type=F32)


def _deltanet_kernel(q_ref, k_ref, v_ref, cq_ref, ck_ref, cv_ref, gt_ref, o_ref,
                     qs, ks, vs, ob, wq_s, at_s, kd_s, u_s, gl_s, s_s, rhs_s, *, L, C, H, R, n_items):
    DH = DN_HEAD_DIM
    s = pl.program_id(0)
    h = jnp.minimum(s, n_items - 1) % H
    slot_p = s % 2
    slot_s = 1 - slot_p
    NC = L // C
    U = PREP_UNROLL

    def conv_rows(r0, first, last):
        rows = pl.ds(r0, R)
        rid = lax.broadcasted_iota(I32, (R, DH), 0)

        def conv_silu(ref, w_ref):
            x = ref[0, rows, :]
            if first:
                x_prev = jnp.where(rid == 0, 0.0, pltpu.roll(x, 1, 0))
            else:
                x_prev = ref[0, pl.ds(r0 - 1, R), :]
            if last:
                x_next = jnp.where(rid == R - 1, 0.0, pltpu.roll(x, R - 1, 0))
            else:
                x_next = ref[0, pl.ds(r0 + 1, R), :]
            return _silu(w_ref[0:1, :] * x_prev + w_ref[1:2, :] * x + w_ref[2:3, :] * x_next)

        q = conv_silu(q_ref, cq_ref)
        k = conv_silu(k_ref, ck_ref)
        qs[rows, :] = q * (lax.rsqrt(jnp.sum(q * q, -1, keepdims=True) + RMS_EPS) * (DH ** -0.5))
        ks[rows, :] = k * lax.rsqrt(jnp.sum(k * k, -1, keepdims=True) + RMS_EPS)
        vs[rows, :] = conv_silu(v_ref, cv_ref)

    @pl.when(s < n_items)
    def _():
        conv_rows(0, True, False)
        lax.fori_loop(1, L // R - 1, lambda i, c: (conv_rows(pl.multiple_of(i * R, R), False, False), c)[1], 0)
        conv_rows(L - R, False, True)

    ri = lax.broadcasted_iota(I32, (C, C), 0)
    ci = lax.broadcasted_iota(I32, (C, C), 1)
    eye = ri == ci
    eye_f = eye.astype(F32)
    incl = (ri >= ci, ri <= ci)
    strict = (ri > ci, ri < ci)
    n_sq = C.bit_length() - 2
    zero_b = jnp.zeros((C, C), BF16)

    def block_diag(a, b):
        return jnp.concatenate([jnp.concatenate([a, zero_b], axis=1),
                                jnp.concatenate([zero_b, b], axis=1)], axis=0)

    def prep_start(c, j):
        rows = pl.ds(pl.multiple_of(c * C, C), C)
        q = qs[rows, :]
        k = ks[rows, :]
        v = vs[rows, :]
        kq = _dot_nt(jnp.concatenate([k, q], axis=0), k)
        kk, qk = kq[:C], kq[C:]
        a2 = []
        for d in range(2):
            g_row = gt_ref[0, d * H + h, pl.ds(c, 1), :]
            b_row = gt_ref[0, (2 + d) * H + h, pl.ds(c, 1), :]
            m, ms, mt = incl[d], strict[d], incl[1 - d]
            g_col = jnp.sum(jnp.where(eye, g_row, 0.0), axis=1, keepdims=True)
            b_col = jnp.sum(jnp.where(eye, b_row, 0.0), axis=1, keepdims=True)
            gc_col = jnp.sum(jnp.where(m, g_row, 0.0), axis=1, keepdims=True)
            gc_row = jnp.sum(jnp.where(mt, g_col, 0.0), axis=0, keepdims=True)
            tot = jnp.sum(g_row, axis=1, keepdims=True)
            decay = jnp.where(m, jnp.exp(jnp.where(m, gc_col - gc_row, 0.0)), 0.0)
            a2.append(jnp.where(ms, -(kk * b_col * decay), 0.0))
            egc = jnp.exp(gc_col)
            rhs_s[d, j] = jnp.concatenate([v * b_col, k * (b_col * egc)], axis=1).astype(BF16)
            wq_s[slot_p, d, c, pl.ds(C, C), :] = (q * egc).astype(BF16)
            at_s[slot_p, d, c] = (qk * decay).astype(BF16)
            kd_s[slot_p, d, c] = (k * jnp.exp(tot - gc_col)).astype(BF16)
            gl_s[slot_p, d, pl.ds(c, 1), :] = jnp.broadcast_to(jnp.exp(tot), (1, DH))
        return jnp.concatenate(a2, axis=1)

    def prep_phases(i):
        cs = [i * U + j for j in range(U)]
        ps = []
        for j, c in enumerate(cs):
            ps.append(prep_start(c, j))
            if j == U // 2 - 1:
                yield
        eye2 = jnp.concatenate([eye_f, eye_f], axis=1)
        xs_ = [eye2 + p for p in ps]
        yield
        for _ in range(n_sq):
            pbs = [p.astype(BF16) for p in ps]
            ps = [jnp.dot(pb, block_diag(pb[:, :C], pb[:, C:]), preferred_element_type=F32) for pb in pbs]
            yield
            pbs = [p.astype(BF16) for p in ps]
            xs_ = [x + jnp.dot(x.astype(BF16), block_diag(pb[:, :C], pb[:, C:]), preferred_element_type=F32)
                   for x, pb in zip(xs_, pbs)]
            yield
        for j, (c, x) in enumerate(zip(cs, xs_)):
            for d in range(2):
                sol = jnp.dot(x[:, d * C:(d + 1) * C].astype(BF16), rhs_s[d, j], preferred_element_type=F32)
                u_s[slot_p, d, c] = sol[:, :DH]
                wq_s[slot_p, d, c, pl.ds(0, C), :] = sol[:, DH:].astype(BF16)
            if j == U // 2 - 1:
                yield
        yield

    def scan_stages(n):
        cf, cb = n, NC - 1 - n
        s_f, s_b = s_s[0], s_s[1]
        wq = jnp.concatenate([wq_s[slot_s, 0, cf], wq_s[slot_s, 1, cb]], axis=1)
        r1 = jnp.dot(wq, block_diag(s_f.astype(BF16), s_b.astype(BF16)), preferred_element_type=F32)
        yield
        u = jnp.concatenate([u_s[slot_s, 0, cf], u_s[slot_s, 1, cb]], axis=1)
        v_new = (u - r1[:C]).astype(BF16)
        vd = block_diag(v_new[:, :DH], v_new[:, DH:])
        at = jnp.concatenate([at_s[slot_s, 0, cf], at_s[slot_s, 1, cb]], axis=1)
        o = r1[C:] + jnp.dot(at, vd, preferred_element_type=F32)
        kd = jnp.concatenate([kd_s[slot_s, 0, cf], kd_s[slot_s, 1, cb]], axis=0)
        ds = lax.dot_general(kd, vd, (((0,), (0,)), ((), ())), preferred_element_type=F32)
        s_s[0] = s_f * gl_s[slot_s, 0, pl.ds(cf, 1), :] + ds[:, :DH]
        s_s[1] = s_b * gl_s[slot_s, 1, pl.ds(cb, 1), :] + ds[:, DH:]
        o_ref[0, pl.ds(pl.multiple_of(cf * C, C), C), :] = o[:, :DH]
        ob[pl.ds(pl.multiple_of(cb * C, C), C), :] = o[:, DH:]
        yield

    def run(prepare, scan):
        def trip(i, _):
            prep_it = prep_phases(i) if prepare else iter(())
            scan_it = itertools.chain.from_iterable(scan_stages(i * U + j) for j in range(U)) if scan else iter(())
            for _ in itertools.zip_longest(scan_it, prep_it):
                pass
            return 0
        if scan:
            s_s[...] = jnp.zeros_like(s_s)
        lax.fori_loop(0, NC // U, trip, 0)

    @pl.when(s == 0)
    def _():
        run(True, False)

    @pl.when((s > 0) & (s < n_items))
    def _():
        run(True, True)

    @pl.when(s == n_items)
    def _():
        run(False, True)

    def sum_rows(i, _):
        rows = pl.ds(pl.multiple_of(i * R, R), R)
        o_ref[0, rows, :] = o_ref[0, rows, :] + ob[rows, :]
        return 0

    @pl.when(s > 0)
    def _():
        lax.fori_loop(0, L // R, sum_rows, 0)


def _deltanet(proj, gt, conv_w, *, B, L, H):
    DH, C, R = DN_HEAD_DIM, DN_CHUNK, 256
    assert C == DH and L % (C * PREP_UNROLL) == 0
    NC = L // C
    n_items = B * H
    cur = lambda s: jnp.minimum(s, n_items - 1)
    prev = lambda s: jnp.maximum(s - 1, 0)
    col = lambda off: pl.BlockSpec((1, L, DH), lambda s: (off + cur(s) % H, cur(s) // H, 0))
    cw = lambda off: pl.BlockSpec((3, DH), lambda s: (0, off + cur(s) % H))
    kern = functools.partial(_deltanet_kernel, L=L, C=C, H=H, R=R, n_items=n_items)
    return pl.pallas_call(
        kern, grid=(n_items + 1,),
        in_specs=[col(0), col(H), col(2 * H), cw(0), cw(H), cw(2 * H),
                  pl.BlockSpec((1, 4 * H, NC, C), lambda s: (cur(s) // H, 0, 0, 0))],
        out_specs=pl.BlockSpec((1, L, DH), lambda s: (prev(s) // H, 0, prev(s) % H)),
        out_shape=jax.ShapeDtypeStruct((B, L, H * DH), F32),
        scratch_shapes=[pltpu.VMEM((L, DH), F32), pltpu.VMEM((L, DH), F32), pltpu.VMEM((L, DH), F32),
                        pltpu.VMEM((L, DH), F32),
                        pltpu.VMEM((2, 2, NC, 2 * C, DH), BF16), pltpu.VMEM((2, 2, NC, C, C), BF16),
                        pltpu.VMEM((2, 2, NC, C, DH), BF16), pltpu.VMEM((2, 2, NC, C, DH), F32),
                        pltpu.VMEM((2, 2, NC, DH), F32), pltpu.VMEM((2, DH, DH), F32),
                        pltpu.VMEM((2, PREP_UNROLL, C, 2 * DH), BF16)],
        compiler_params=_cparams("arbitrary"), name="deltanet",
    )(proj, proj, proj, conv_w, conv_w, conv_w, gt)


def _conformer_kernel(val_ref, gate_ref, w_ref, b_ref, g_ref, beta_ref, o_ref, ypad, *, L, K, R, HALO):
    pad = K // 2
    zeros = jnp.zeros((HALO, CV_GROUP), F32)
    ypad[pl.ds(0, HALO), :] = zeros
    ypad[pl.ds(HALO + L, HALO), :] = zeros

    def glu_rows(i, _):
        r0 = pl.multiple_of(i * R, R)
        ypad[pl.ds(HALO + r0, R), :] = val_ref[0, pl.ds(r0, R), :] * _sigmoid(gate_ref[0, pl.ds(r0, R), :])
        return 0

    lax.fori_loop(0, L // R, glu_rows, 0)

    def conv_rows(i, _):
        r0 = pl.multiple_of(i * R, R)
        acc = jnp.zeros((R, CV_GROUP), F32)
        for t in range(K):
            acc = acc + w_ref[t:t + 1, :] * ypad[pl.ds(r0 + (HALO - pad + t), R), :]
        y = _layer_norm(acc + b_ref[...], g_ref[...], beta_ref[...])
        o_ref[0, pl.ds(r0, R), :] = _silu(y)
        return 0

    lax.fori_loop(0, L // R, conv_rows, 0)


def _conformer(proj, w, b, g, beta, *, B, L, col0, width):
    K = w.shape[0]
    n = width // CV_GROUP
    HALO, R = 16, 256
    c0 = col0 // CV_GROUP
    vec = pl.BlockSpec((1, CV_GROUP), lambda bb, j: (0, j))
    kern = functools.partial(_conformer_kernel, L=L, K=K, R=R, HALO=HALO)
    return pl.pallas_call(
        kern, grid=(B, n),
        in_specs=[pl.BlockSpec((1, L, CV_GROUP), lambda bb, j: (c0 + j, bb, 0)),
                  pl.BlockSpec((1, L, CV_GROUP), lambda bb, j: (c0 + n + j, bb, 0)),
                  pl.BlockSpec((K, CV_GROUP), lambda bb, j: (0, j)), vec, vec, vec],
        out_specs=pl.BlockSpec((1, L, CV_GROUP), lambda bb, j: (bb, 0, j)),
        out_shape=jax.ShapeDtypeStruct((B, L, width), F32),
        scratch_shapes=[pltpu.VMEM((L + 2 * HALO, CV_GROUP), F32)],
        compiler_params=_cparams("parallel", "parallel"), name="conformer",
    )(proj, proj, w, b.reshape(1, width), g.reshape(1, width), beta.reshape(1, width))


def _out_router_kernel(dn_ref, z_ref, nw_ref, cv_ref, h_ref, w_ref, g_ref, b_ref, wr_ref, br_ref,
                       h1_ref, h1p_ref, route_ref, routet_ref, *, alpha, n_dn, sub):
    n_sub = h_ref.shape[0] // sub

    def stages(s):
        rows = pl.ds(s * sub, sub)
        heads = []
        for hh in range(n_dn // DN_HEAD_DIM):
            o = dn_ref[rows, hh * DN_HEAD_DIM:(hh + 1) * DN_HEAD_DIM]
            o = o * lax.rsqrt(jnp.mean(o * o, -1, keepdims=True) + RMS_EPS) * nw_ref[...]
            heads.append((o * _silu(z_ref[hh, rows, :])).astype(BF16))
        mix = jnp.dot(jnp.concatenate(heads, axis=1), w_ref[pl.ds(0, n_dn), :], preferred_element_type=F32)
        mix = mix + jnp.dot(cv_ref[rows, :].astype(BF16), w_ref[pl.ds(n_dn, w_ref.shape[0] - n_dn), :],
                            preferred_element_type=F32)
        yield
        h1 = _layer_norm(alpha * h_ref[rows, :] + mix, g_ref[...], b_ref[...])
        h1_ref[rows, :] = h1
        h1p_ref[rows, :] = _pack_halves(h1)
        h1b = h1.astype(BF16)
        yield
        logits = jnp.dot(h1b, wr_ref[...], preferred_element_type=F32) + br_ref[...]
        yield
        route = _route(logits)
        route_ref[rows, :] = route
        routet_ref[:, rows] = route.T[:8, :]
        yield

    n_stage = 4
    pipes = [stages(s) for s in range(n_sub)]
    for t in range(n_sub + n_stage - 1):
        for s in range(t, t - n_stage, -1):
            if 0 <= s < n_sub:
                next(pipes[s])


def _route(logits):
    lane = lax.broadcasted_iota(I32, logits.shape, 1)
    neg = jnp.float32(-jnp.inf)
    big = jnp.int32(LANES)
    gl = jnp.where(lane < N_GROUPS, logits, neg)
    gmax = jnp.max(gl, -1, keepdims=True)
    gsel = jnp.min(jnp.where(gl == gmax, lane, big), -1, keepdims=True)
    pg = 1.0 / jnp.sum(jnp.exp(gl - gmax), -1, keepdims=True)
    lo = N_GROUPS + gsel * EXPERTS_PER_GROUP
    el = jnp.where((lane >= lo) & (lane < lo + EXPERTS_PER_GROUP), logits, neg)
    e1 = jnp.max(el, -1, keepdims=True)
    i1 = jnp.min(jnp.where(el == e1, lane, big), -1, keepdims=True)
    el2 = jnp.where(lane == i1, neg, el)
    e2 = jnp.max(el2, -1, keepdims=True)
    i2 = jnp.min(jnp.where(el2 == e2, lane, big), -1, keepdims=True)
    r = jnp.exp(e2 - e1)
    p1 = 1.0 / (1.0 + r)
    p2 = r * p1
    route = jnp.where(lane == 0, (i1 - N_GROUPS).astype(F32),
                      jnp.where(lane == 1, (i2 - N_GROUPS).astype(F32),
                                jnp.where(lane == 2, pg * p1, jnp.where(lane == 3, pg * p2, 0.0))))
    return route


def _out_router(dn, proj, z_group, norm_w, cv, h, w_out, g, b, w_router, b_router, *, alpha, tm=512, sub=128):
    T, D = h.shape
    n_dn, n_cv = dn.shape[1], cv.shape[1]
    n_heads = n_dn // DN_HEAD_DIM
    assert z_group % n_heads == 0
    vec = pl.BlockSpec((1, D), lambda i: (0, 0))
    row = pl.BlockSpec((tm, D), lambda i: (i, 0))
    kern = functools.partial(_out_router_kernel, alpha=alpha, n_dn=n_dn, sub=sub)
    return pl.pallas_call(
        kern, grid=(T // tm,),
        in_specs=[pl.BlockSpec((tm, n_dn), lambda i: (i, 0)),
                  pl.BlockSpec((n_heads, tm, DN_HEAD_DIM), lambda i: (z_group // n_heads, i, 0)),
                  pl.BlockSpec((1, DN_HEAD_DIM), lambda i: (0, 0)),
                  pl.BlockSpec((tm, n_cv), lambda i: (i, 0)), row,
                  pl.BlockSpec((n_dn + n_cv, D), lambda i: (0, 0)), vec, vec,
                  pl.BlockSpec((D, LANES), lambda i: (0, 0)), pl.BlockSpec((1, LANES), lambda i: (0, 0))],
        out_specs=[row, pl.BlockSpec((tm, D // 2), lambda i: (i, 0)),
                   pl.BlockSpec((tm, LANES), lambda i: (i, 0)), pl.BlockSpec((8, tm), lambda i: (0, i))],
        out_shape=[jax.ShapeDtypeStruct((T, D), F32), jax.ShapeDtypeStruct((T, D // 2), U32),
                   jax.ShapeDtypeStruct((T, LANES), F32), jax.ShapeDtypeStruct((8, T), F32)],
        compiler_params=_cparams("parallel"), name="out_router",
    )(dn, proj, norm_w.reshape(1, DN_HEAD_DIM), cv, h, w_out, g.reshape(1, D), b.reshape(1, D), w_router, b_router)


def _onehot_t(routet_ref, tr):
    sub = lax.broadcasted_iota(I32, (2 * N_EXPERTS, tr), 0)
    e1 = routet_ref[0:1, :].astype(I32)
    e2 = routet_ref[1:2, :].astype(I32)
    return sub == jnp.where(sub < N_EXPERTS, e1, e2 + N_EXPERTS)


def _count_kernel(routet_ref, cnt_ref, *, tr):
    @pl.when(pl.program_id(0) == 0)
    def _():
        cnt_ref[...] = jnp.zeros_like(cnt_ref)
    oh = _onehot_t(routet_ref, tr).astype(F32)
    cnt_ref[...] += jnp.sum(oh, axis=1, keepdims=True)


def _place_kernel(routet_ref, base_ref, d1_ref, d2_ref, carry, *, tr):
    @pl.when(pl.program_id(0) == 0)
    def _():
        carry[...] = jnp.zeros_like(carry)
    oh = _onehot_t(routet_ref, tr)
    ohb = oh.astype(F32).astype(BF16)
    ri = lax.broadcasted_iota(I32, (tr, tr), 0)
    ci = lax.broadcasted_iota(I32, (tr, tr), 1)
    before = (ri < ci).astype(F32).astype(BF16)
    excl = jnp.dot(ohb, before, preferred_element_type=F32)
    pos = jnp.where(oh, excl + carry[...] + base_ref[...], 0.0)
    d1 = jnp.sum(pos[:N_EXPERTS], axis=0, keepdims=True)
    d2 = jnp.sum(pos[N_EXPERTS:], axis=0, keepdims=True)
    d1_ref[...] = d1.astype(I32)
    d2_ref[...] = d2.astype(I32)
    carry[...] += jnp.sum(oh.astype(F32), axis=1, keepdims=True)


def _placement(routet, tr=512):
    T = routet.shape[1]
    E = N_EXPERTS
    rt = pl.BlockSpec((8, tr), lambda i: (0, i))
    col = pl.BlockSpec((2 * E, 1), lambda i: (0, 0))
    cnt = pl.pallas_call(
        functools.partial(_count_kernel, tr=tr), grid=(T // tr,), in_specs=[rt], out_specs=col,
        out_shape=jax.ShapeDtypeStruct((2 * E, 1), F32),
        compiler_params=_cparams("arbitrary"), name="moe_count")(routet)
    c1 = cnt[:E, 0].astype(I32)
    c2 = cnt[E:, 0].astype(I32)
    padded = ((c1 + c2 + FFN_BLOCK - 1) // FFN_BLOCK) * FFN_BLOCK
    pend = jnp.cumsum(padded)
    pstart = pend - padded
    base = jnp.concatenate([pstart, pstart + c1]).astype(F32).reshape(2 * E, 1)
    lane_row = pl.BlockSpec((1, tr), lambda i: (0, i))
    d1, d2 = pl.pallas_call(
        functools.partial(_place_kernel, tr=tr), grid=(T // tr,), in_specs=[rt, col], out_specs=[lane_row, lane_row],
        out_shape=[jax.ShapeDtypeStruct((1, T), I32)] * 2,
        scratch_shapes=[pltpu.VMEM((2 * E, 1), F32)],
        compiler_params=_cparams("arbitrary"), name="moe_place")(routet, base)
    n_blocks = (T * TOP_K + E * (FFN_BLOCK - 1) + FFN_BLOCK - 1) // FFN_BLOCK
    starts = jnp.arange(n_blocks, dtype=I32) * FFN_BLOCK
    blk_e = jnp.minimum(jnp.sum((pend[None, :] <= starts[:, None]).astype(I32), axis=1), E - 1)
    blk_first = jnp.concatenate([jnp.ones((1,), I32), (blk_e[1:] != blk_e[:-1]).astype(I32)])
    n_used = (pend[-1] // FFN_BLOCK).astype(I32).reshape(1)
    ids = jnp.arange(E, dtype=I32)
    owns = padded > 0
    later = owns[None, :] & (ids[None, :] > ids[:, None])
    next_tab = jnp.min(jnp.where(later, ids[None, :], E), axis=1)
    next_tab = jnp.concatenate([next_tab, jnp.full((1,), E, I32)])
    next2_tab = next_tab[next_tab[:E]]
    hide = lambda t: jnp.where(t == E, -1, t)
    slot_tab = (jnp.cumsum(owns.astype(I32)) - 1) % 2
    steer = (blk_e, blk_first, slot_tab[blk_e], hide(next_tab[:E])[blk_e], hide(next2_tab)[blk_e])
    return d1.reshape(T), d2.reshape(T), pend.astype(I32), steer, n_used, n_blocks


def _dispatch_kernel(d1_ref, d2_ref, pend_ref, src_ref, xs_ref, zbuf, sem, zsem, *, tb):
    @pl.when(pl.program_id(0) == 0)
    def _():
        zbuf[...] = jnp.zeros_like(zbuf)
        tails = []
        for e in range(N_EXPERTS):
            end = pend_ref[e]
            nonempty = end > (pend_ref[e - 1] if e else 0)
            start = pl.multiple_of(jnp.maximum(end - FFN_BLOCK, 0), FFN_BLOCK)
            tails.append((nonempty, pltpu.make_async_copy(zbuf, xs_ref.at[pl.ds(start, FFN_BLOCK), :], zsem)))
        for nonempty, copy in tails:
            pl.when(nonempty)(copy.start)

        def spare(b):
            return pltpu.make_async_copy(
                zbuf, xs_ref.at[pl.ds(pl.multiple_of(b * FFN_BLOCK, FFN_BLOCK), FFN_BLOCK), :], zsem)
        first_spare = pend_ref[N_EXPERTS - 1] // FFN_BLOCK
        n_blocks = xs_ref.shape[0] // FFN_BLOCK
        lax.fori_loop(first_spare, n_blocks, lambda b, c: (spare(b).start(), c)[1], 0)
        for nonempty, copy in tails:
            pl.when(nonempty)(copy.wait)
        lax.fori_loop(first_spare, n_blocks, lambda b, c: (spare(b).wait(), c)[1], 0)

    def issue(t, _):
        row = src_ref.at[pl.ds(t, 1), :]
        pltpu.make_async_copy(row, xs_ref.at[pl.ds(d1_ref[t], 1), :], sem).start(priority=0)
        pltpu.make_async_copy(row, xs_ref.at[pl.ds(d2_ref[t], 1), :], sem).start(priority=1)
        return 0

    lax.fori_loop(0, tb, issue, 0, unroll=8)
    pltpu.make_async_copy(src_ref, xs_ref.at[pl.ds(0, tb), :], sem).wait()
    pltpu.make_async_copy(src_ref, xs_ref.at[pl.ds(0, tb), :], sem).wait()


def _dispatch(h1p, d1, d2, pend, n_rows, tb=1024):
    T, W = h1p.shape
    idx = pl.BlockSpec((tb,), lambda i: (i,), memory_space=pltpu.SMEM)
    return pl.pallas_call(
        functools.partial(_dispatch_kernel, tb=tb), grid=(T // tb,),
        in_specs=[idx, idx, pl.BlockSpec((N_EXPERTS,), lambda i: (0,), memory_space=pltpu.SMEM),
                  pl.BlockSpec((tb, W), lambda i: (i, 0))],
        out_specs=pl.BlockSpec(memory_space=pl.ANY),
        out_shape=jax.ShapeDtypeStruct((n_rows, W), U32),
        scratch_shapes=[pltpu.VMEM((FFN_BLOCK, W), U32), pltpu.SemaphoreType.DMA(()), pltpu.SemaphoreType.DMA(())],
        compiler_params=_cparams("arbitrary"), name="moe_dispatch",
    )(d1, d2, pend, h1p)


def _cast_rows(src, dst):
    rows = CAST_CHUNK_ELEMS // src.shape[1]

    def body(i, c):
        r = pl.ds(pl.multiple_of(i * rows, rows), rows)
        dst[r, :] = src[r, :].astype(BF16)
        return c

    lax.fori_loop(0, src.shape[0] // rows, body, 0, unroll=4)


def _ffn_kernel(blk_e_ref, blk_first_ref, blk_slot_ref, blk_next_ref, blk_next2_ref, n_used_ref,
                xs_ref, wgu_hbm, wdn_hbm, y_ref, wgu_f, wdn_f, wgu_s, wdn_s, sem, *, ff, layer):
    i = pl.program_id(0)
    used = i < n_used_ref[0]

    def weight_copies(e, slot):
        copies = []
        for k, (src, dst) in enumerate(((wgu_hbm, wgu_f), (wdn_hbm, wdn_f))):
            rows = dst.shape[1] // FFN_WEIGHT_CHUNKS
            for c in range(FFN_WEIGHT_CHUNKS):
                r = pl.ds(c * rows, rows)
                copies.append(pltpu.make_async_copy(src.at[layer, e, r, :], dst.at[slot, r, :],
                                                    sem.at[slot, k * FFN_WEIGHT_CHUNKS + c]))
        return copies

    def start(e, slot):
        for c in weight_copies(e, slot):
            c.start(priority=1)

    @pl.when(i == 0)
    def _():
        start(blk_e_ref[0], 0)

        @pl.when(blk_next_ref[0] >= 0)
        def _():
            start(blk_next_ref[0], 1)

    @pl.when(used & (blk_first_ref[i] == 1))
    def _():
        slot = blk_slot_ref[i]
        for c in weight_copies(blk_e_ref[i], slot):
            c.wait()
        _cast_rows(wgu_f.at[slot], wgu_s)
        _cast_rows(wdn_f.at[slot], wdn_s)

        @pl.when(blk_next2_ref[i] >= 0)
        def _():
            start(blk_next2_ref[i], slot)

    @pl.when(used)
    def _():
        lo, hi = _unpack_halves(xs_ref[...])
        x = jnp.concatenate([lo, hi], axis=1).astype(BF16)
        gu = jnp.dot(x, wgu_s[...], preferred_element_type=F32)
        act = (_silu(gu[:, :ff]) * gu[:, ff:]).astype(BF16)
        y_ref[...] = _pack_halves(jnp.dot(act, wdn_s[...], preferred_element_type=F32))

    @pl.when(i >= n_used_ref[0])
    def _():
        y_ref[...] = jnp.zeros_like(y_ref)


def _ffn(xs, w_gu, w_dn, layer, blk_e, blk_first, blk_slot, blk_next, blk_next2, n_used, n_blocks):
    _, E, D, FF2 = w_gu.shape
    ff = FF2 // 2
    W = xs.shape[1]
    any_spec = pl.BlockSpec(memory_space=pl.ANY)
    grid_spec = pltpu.PrefetchScalarGridSpec(
        num_scalar_prefetch=6, grid=(n_blocks,),
        in_specs=[pl.BlockSpec((FFN_BLOCK, W), lambda i, be, bf, bs, bn, bn2, nu: (jnp.minimum(i, nu[0] - 1), 0)),
                  any_spec, any_spec],
        out_specs=pl.BlockSpec((FFN_BLOCK, W), lambda i, *_: (i, 0)),
        scratch_shapes=[pltpu.VMEM((2, D, FF2), F32), pltpu.VMEM((2, ff, D), F32),
                        pltpu.VMEM((D, FF2), BF16), pltpu.VMEM((ff, D), BF16),
                        pltpu.SemaphoreType.DMA((2, 2 * FFN_WEIGHT_CHUNKS))])
    return pl.pallas_call(
        functools.partial(_ffn_kernel, ff=ff, layer=layer), grid_spec=grid_spec,
        out_shape=jax.ShapeDtypeStruct(xs.shape, U32),
        compiler_params=_cparams("arbitrary"), name="moe_ffn",
    )(blk_e, blk_first, blk_slot, blk_next, blk_next2, n_used, xs, w_gu, w_dn)


def _combine_kernel(d1_ref, d2_ref, d1n_ref, d2n_ref, y_ref, h1_ref, route_ref, g_ref, b_ref, h2_ref, h2b_ref,
                    ybuf, sem, *, tc, alpha):
    i = pl.program_id(0)
    slot = i % 2

    def row_copies(r1, r2, s, t):
        return (pltpu.make_async_copy(y_ref.at[pl.ds(r1[t], 1), :], ybuf.at[s, 0, pl.ds(t, 1), :], sem.at[s]),
                pltpu.make_async_copy(y_ref.at[pl.ds(r2[t], 1), :], ybuf.at[s, 1, pl.ds(t, 1), :], sem.at[s]))

    def wait_slot(s):
        for j in range(TOP_K):
            pltpu.make_async_copy(y_ref.at[pl.ds(0, tc), :], ybuf.at[s, j], sem.at[s]).wait()

    @pl.when(i == 0)
    def _():
        def issue(t, _):
            for j, c in enumerate(row_copies(d1_ref, d2_ref, 0, t)):
                c.start(priority=j)
            return 0
        lax.fori_loop(0, tc, issue, 0, unroll=8)

    wait_slot(slot)
    for t in range(tc):
        for j, c in enumerate(row_copies(d1n_ref, d2n_ref, 1 - slot, t)):
            c.start(priority=j)

    route = route_ref[...]
    g1 = route[:, 2:3]
    g2 = route[:, 3:4]
    lo1, hi1 = _unpack_halves(ybuf[slot, 0])
    lo2, hi2 = _unpack_halves(ybuf[slot, 1])
    ffn = jnp.concatenate([g1 * lo1 + g2 * lo2, g1 * hi1 + g2 * hi2], axis=1)
    h2 = _layer_norm(alpha * h1_ref[...] + ffn, g_ref[...], b_ref[...])
    h2_ref[...] = h2
    h2b_ref[...] = h2.astype(BF16)

    @pl.when(i == pl.num_programs(0) - 1)
    def _():
        wait_slot(1 - slot)


def _combine(y, d1, d2, h1, route, g, b, *, alpha, tc=256):
    T, D = h1.shape
    W = y.shape[1]
    n = T // tc
    idx = pl.BlockSpec((tc,), lambda i: (i,), memory_space=pltpu.SMEM)
    idx_next = pl.BlockSpec((tc,), lambda i: (jnp.minimum(i + 1, n - 1),), memory_space=pltpu.SMEM)
    row = pl.BlockSpec((tc, D), lambda i: (i, 0))
    vec = pl.BlockSpec((1, D), lambda i: (0, 0))
    return pl.pallas_call(
        functools.partial(_combine_kernel, tc=tc, alpha=alpha), grid=(n,),
        in_specs=[idx, idx, idx_next, idx_next, pl.BlockSpec(memory_space=pl.ANY), row,
                  pl.BlockSpec((tc, LANES), lambda i: (i, 0)), vec, vec],
        out_specs=[row, row],
        out_shape=[jax.ShapeDtypeStruct((T, D), F32), jax.ShapeDtypeStruct((T, D), BF16)],
        scratch_shapes=[pltpu.VMEM((2, TOP_K, tc, W), U32), pltpu.SemaphoreType.DMA((2,))],
        compiler_params=_cparams("arbitrary"), name="moe_combine",
    )(d1, d2, d1, d2, y, h1, route, g.reshape(1, D), b.reshape(1, D))


def kernel(x, emb_ln_g, emb_ln_b, w_in, short_conv_w, a_log, dt_bias, dn_norm_w, dw_conv_w, dw_conv_b,
           conv_ln_g, conv_ln_b, w_out, ln1_g, ln1_b, w_group, b_group, w_expert, b_expert, w_gate_up,
           w_down, ln2_g, ln2_b):
    B, L, D = x.shape
    T = B * L
    depth = w_in.shape[0]
    H = a_log.shape[2]
    dn_w = H * DN_HEAD_DIM
    cv_w = dw_conv_w.shape[2]
    n_ab = 4 * H
    alpha = (2 * depth) ** 0.25
    C = DN_CHUNK

    h, hb = _emb_ln(x.reshape(T, D), emb_ln_g, emb_ln_b)
    for l in range(depth):
        w = w_in[l]
        w_main = jnp.concatenate([w[:, :4 * dn_w], w[:, 4 * dn_w + n_ab:]], axis=1).astype(BF16)
        proj, gt = _in_proj(hb, w_main, w[:, 4 * dn_w:4 * dn_w + n_ab], a_log[l], dt_bias[l], B=B, L=L)
        gt = gt.reshape(B, n_ab, L // C, C)
        dn = _deltanet(proj, gt, short_conv_w[l], B=B, L=L, H=H)
        cv = _conformer(proj, dw_conv_w[l], dw_conv_b[l], conv_ln_g[l], conv_ln_b[l],
                        B=B, L=L, col0=4 * dn_w, width=cv_w)
        n_r = N_GROUPS + N_EXPERTS
        w_router = jnp.pad(jnp.concatenate([w_group[l], w_expert[l]], axis=1),
                           ((0, 0), (0, LANES - n_r))).astype(BF16)
        b_router = jnp.pad(jnp.concatenate([b_group[l], b_expert[l]]), (0, LANES - n_r)).reshape(1, LANES)
        h1, h1p, route, routet = _out_router(
            dn.reshape(T, dn_w), proj, 3 * H, dn_norm_w[l], cv.reshape(T, cv_w), h, w_out[l].astype(BF16),
            ln1_g[l], ln1_b[l],
            w_router, b_router, alpha=alpha)
        d1, d2, pend, steer, n_used, n_blocks = _placement(routet)
        xs = _dispatch(h1p, d1, d2, pend, n_blocks * FFN_BLOCK)
        y = _ffn(xs, w_gate_up, w_down, l, *steer, n_used, n_blocks)
        h, hb = _combine(y, d1, d2, h1, route, ln2_g[l], ln2_b[l], alpha=alpha)
    return h.reshape(B, L, D)
```

```python
import functools
import itertools

import jax
import jax.numpy as jnp
from jax import lax
from jax.experimental import pallas as pl
from jax.experimental.pallas import tpu as pltpu

F32 = jnp.float32
BF16 = jnp.bfloat16
U32 = jnp.uint32
I32 = jnp.int32

LANES = 128
DN_HEAD_DIM = 128
DN_CHUNK = 128
PREP_UNROLL = 8
CONV_SPREAD = 4
CV_GROUP = 128
N_GROUPS = 8
EXPERTS_PER_GROUP = 8
N_EXPERTS = N_GROUPS * EXPERTS_PER_GROUP
TOP_K = 2
FFN_BLOCK = 256
FFN_WEIGHT_CHUNKS = 4
CAST_CHUNK_ELEMS = 32 * 1024
LN_EPS = 1e-5
RMS_EPS = 1e-6
VMEM_LIMIT = 56 * 1024 * 1024


def _cparams(*sem):
    return pltpu.CompilerParams(dimension_semantics=sem, vmem_limit_bytes=VMEM_LIMIT)


def _layer_norm(x, g, b):
    mu = jnp.mean(x, -1, keepdims=True)
    xc = x - mu
    var = jnp.mean(xc * xc, -1, keepdims=True)
    return xc * lax.rsqrt(var + LN_EPS) * g + b


def _sigmoid(x):
    return 1.0 / (1.0 + jnp.exp(-x))


def _silu(x):
    return x * _sigmoid(x)


def _pack_halves(y):
    n = y.shape[1] // 2
    bits = lax.bitcast_convert_type(y.astype(BF16).astype(F32), U32)
    return (bits[:, :n] >> 16) | bits[:, n:]


def _unpack_halves(u):
    lo = lax.bitcast_convert_type(u << 16, F32)
    hi = lax.bitcast_convert_type(u & jnp.uint32(0xFFFF0000), F32)
    return lo, hi


def _emb_ln_kernel(x_ref, g_ref, b_ref, h_ref, hb_ref):
    h = _layer_norm(x_ref[...], g_ref[...], b_ref[...])
    h_ref[...] = h
    hb_ref[...] = h.astype(BF16)


def _emb_ln(x, g, b, tm=512):
    T, D = x.shape
    row = pl.BlockSpec((tm, D), lambda i: (i, 0))
    vec = pl.BlockSpec((1, D), lambda i: (0, 0))
    return pl.pallas_call(
        _emb_ln_kernel, grid=(T // tm,), in_specs=[row, vec, vec], out_specs=[row, row],
        out_shape=[jax.ShapeDtypeStruct((T, D), F32), jax.ShapeDtypeStruct((T, D), BF16)],
        compiler_params=_cparams("parallel"), name="emb_ln")(x, g.reshape(1, D), b.reshape(1, D))


def _in_proj_kernel(x_ref, w_ref, wab_ref, alog_ref, dtb_ref, o_ref, gt_ref, *, n_decay, n_gate_rows):
    @pl.when(pl.program_id(1) == 0)
    def _():
        ab = jnp.dot(x_ref[...], wab_ref[...], preferred_element_type=F32)
        lane = lax.broadcasted_iota(I32, ab.shape, 1)
        sp = ab + dtb_ref[...]
        softplus = jnp.maximum(sp, 0.0) + jnp.log(1.0 + jnp.exp(-jnp.abs(sp)))
        decay = -jnp.exp(alog_ref[...]) * softplus
        gt_ref[0] = jnp.where(lane < n_decay, decay, _sigmoid(ab)).T[:n_gate_rows, :]

    acc = jnp.dot(x_ref[...], w_ref[...], preferred_element_type=F32)
    for j in range(o_ref.shape[0]):
        o_ref[j] = acc[:, j * LANES:(j + 1) * LANES]


def _in_proj(hb, w, w_ab, a_log, dt_bias, *, B, L, tm=1024, tn=2048):
    T, D = hb.shape
    N = w.shape[1]
    n = a_log.size
    n_gate_rows = 2 * n
    pad = lambda v: jnp.pad(v.reshape(1, n).astype(F32), ((0, 0), (0, LANES - n)))
    wab = jnp.pad(w_ab, ((0, 0), (0, LANES - w_ab.shape[1]))).astype(BF16)
    vec = pl.BlockSpec((1, LANES), lambda i, j: (0, 0))
    per_batch = L // tm
    return pl.pallas_call(
        functools.partial(_in_proj_kernel, n_decay=n, n_gate_rows=n_gate_rows), grid=(T // tm, N // tn),
        in_specs=[pl.BlockSpec((tm, D), lambda i, j: (i, 0)), pl.BlockSpec((D, tn), lambda i, j: (0, j)),
                  pl.BlockSpec((D, LANES), lambda i, j: (0, 0)), vec, vec],
        out_specs=[pl.BlockSpec((tn // LANES, tm, LANES), lambda i, j: (j, i, 0)),
                   pl.BlockSpec((1, n_gate_rows, tm), lambda i, j: (i // per_batch, 0, i % per_batch))],
        out_shape=[jax.ShapeDtypeStruct((N // LANES, T, LANES), F32),
                   jax.ShapeDtypeStruct((B, n_gate_rows, L), F32)],
        compiler_params=_cparams("parallel", "arbitrary"), name="in_proj",
    )(hb, w, wab, pad(a_log), pad(dt_bias))


def _dot(a, b):
    return jnp.dot(a.astype(BF16), b.astype(BF16), preferred_element_type=F32)


def _dot_nt(a, b):
    return lax.dot_general(a.astype(BF16), b.astype(BF16), (((1,), (1,)), ((), ())),
                           preferred_element_type=F32)


def _dot_tn(a, b):
    return lax.dot_general(a.astype(BF16), b.astype(BF16), (((0,), (0,)), ((), ())),
                           preferred_element_type=F32)


def _deltanet_kernel(q_ref, k_ref, v_ref, cq_ref, ck_ref, cv_ref, gt_ref, o_ref,
                     qs, ks, vs, ob, wq_s, at_s, kd_s, u_s, gl_s, s_s, rhs_s, *, L, C, H, R, n_items):
    DH = DN_HEAD_DIM
    s = pl.program_id(0)
    h = jnp.minimum(s, n_items - 1) % H
    slot_p = s % 2
    slot_s = 1 - slot_p
    NC = L // C
    U = PREP_UNROLL

    def conv_rows(r0, first=False, last=False):
        rows = pl.ds(r0, R)
        rid = lax.broadcasted_iota(I32, (R, DH), 0)

        def conv_silu(ref, w_ref):
            x = ref[0, rows, :]
            if first:
                x_prev = jnp.where(rid == 0, 0.0, pltpu.roll(x, 1, 0))
            else:
                x_prev = ref[0, pl.ds(r0 - 1, R), :]
            if last is False:
                x_next = ref[0, pl.ds(r0 + 1, R), :]
            else:
                at_end = jnp.where(rid == R - 1, 0.0, pltpu.roll(x, R - 1, 0))
                x_next = at_end if last is True else jnp.where(
                    last, at_end, ref[0, pl.ds(jnp.minimum(r0 + 1, L - R), R), :])
            return _silu(w_ref[0:1, :] * x_prev + w_ref[1:2, :] * x + w_ref[2:3, :] * x_next)

        q = conv_silu(q_ref, cq_ref)
        k = conv_silu(k_ref, ck_ref)
        qs[rows, :] = q * (lax.rsqrt(jnp.sum(q * q, -1, keepdims=True) + RMS_EPS) * (DH ** -0.5))
        ks[rows, :] = k * lax.rsqrt(jnp.sum(k * k, -1, keepdims=True) + RMS_EPS)
        vs[rows, :] = conv_silu(v_ref, cv_ref)

    n_trips = NC // U
    trip_blocks = (U * C) // R

    @pl.when(s < n_items)
    def _():
        for b in range(trip_blocks):
            conv_rows(b * R, first=b == 0, last=(b + 1) * R == L)

    def conv_pieces(i):
        nxt = jnp.minimum(i + 1, n_trips - 1)
        for b in range(trip_blocks):
            r0 = pl.multiple_of((nxt * trip_blocks + b) * R, R)
            conv_rows(r0, last=(r0 + R == L) if b == trip_blocks - 1 else False)
            for _ in range(CONV_SPREAD):
                yield

    ri = lax.broadcasted_iota(I32, (C, C), 0)
    ci = lax.broadcasted_iota(I32, (C, C), 1)
    eye = ri == ci
    eye_f = eye.astype(F32)
    incl = (ri >= ci, ri <= ci)
    strict = (ri > ci, ri < ci)
    n_sq = C.bit_length() - 2
    zero_b = jnp.zeros((C, C), BF16)

    def block_diag(a, b):
        return jnp.concatenate([jnp.concatenate([a, zero_b], axis=1),
                                jnp.concatenate([zero_b, b], axis=1)], axis=0)

    def prep_start(c, j):
        rows = pl.ds(pl.multiple_of(c * C, C), C)
        q = qs[rows, :]
        k = ks[rows, :]
        v = vs[rows, :]
        kq = _dot_nt(jnp.concatenate([k, q], axis=0), k)
        kk, qk = kq[:C], kq[C:]
        a2 = []
        for d in range(2):
            g_row = gt_ref[0, d * H + h, pl.ds(c, 1), :]
            b_row = gt_ref[0, (2 + d) * H + h, pl.ds(c, 1), :]
            m, ms, mt = incl[d], strict[d], incl[1 - d]
            g_col = jnp.sum(jnp.where(eye, g_row, 0.0), axis=1, keepdims=True)
            b_col = jnp.sum(jnp.where(eye, b_row, 0.0), axis=1, keepdims=True)
            gc_col = jnp.sum(jnp.where(m, g_row, 0.0), axis=1, keepdims=True)
            gc_row = jnp.sum(jnp.where(mt, g_col, 0.0), axis=0, keepdims=True)
            tot = jnp.sum(g_row, axis=1, keepdims=True)
            decay = jnp.where(m, jnp.exp(jnp.where(m, gc_col - gc_row, 0.0)), 0.0)
            a2.append(jnp.where(ms, -(kk * b_col * decay), 0.0))
            egc = jnp.exp(gc_col)
            rhs_s[d, j] = jnp.concatenate([v * b_col, k * (b_col * egc)], axis=1).astype(BF16)
            wq_s[slot_p, d, c, pl.ds(C, C), :] = (q * egc).astype(BF16)
            at_s[slot_p, d, c] = (qk * decay).astype(BF16)
            kd_s[slot_p, d, c] = (k * jnp.exp(tot - gc_col)).astype(BF16)
            gl_s[slot_p, d, pl.ds(c, 1), :] = jnp.broadcast_to(jnp.exp(tot), (1, DH))
        return jnp.concatenate(a2, axis=1)

    def prep_phases(i):
        cs = [i * U + j for j in range(U)]
        ps = []
        for j, c in enumerate(cs):
            ps.append(prep_start(c, j))
            if j == U // 2 - 1:
                yield
        eye2 = jnp.concatenate([eye_f, eye_f], axis=1)
        xs_ = [eye2 + p for p in ps]
        yield
        for _ in range(n_sq):
            pbs = [p.astype(BF16) for p in ps]
            ps = [jnp.dot(pb, block_diag(pb[:, :C], pb[:, C:]), preferred_element_type=F32) for pb in pbs]
            yield
            pbs = [p.astype(BF16) for p in ps]
            xs_ = [x + jnp.dot(x.astype(BF16), block_diag(pb[:, :C], pb[:, C:]), preferred_element_type=F32)
                   for x, pb in zip(xs_, pbs)]
            yield
        for j, (c, x) in enumerate(zip(cs, xs_)):
            for d in range(2):
                sol = jnp.dot(x[:, d * C:(d + 1) * C].astype(BF16), rhs_s[d, j], preferred_element_type=F32)
                u_s[slot_p, d, c] = sol[:, :DH]
                wq_s[slot_p, d, c, pl.ds(0, C), :] = sol[:, DH:].astype(BF16)
            if j == U // 2 - 1:
                yield
        yield

    def scan_stages(n):
        cf, cb = n, NC - 1 - n
        s_f, s_b = s_s[0], s_s[1]
        wq = jnp.concatenate([wq_s[slot_s, 0, cf], wq_s[slot_s, 1, cb]], axis=1)
        r1 = jnp.dot(wq, block_diag(s_f.astype(BF16), s_b.astype(BF16)), preferred_element_type=F32)
        yield
        u = jnp.concatenate([u_s[slot_s, 0, cf], u_s[slot_s, 1, cb]], axis=1)
        v_new = (u - r1[:C]).astype(BF16)
        vd = block_diag(v_new[:, :DH], v_new[:, DH:])
        at = jnp.concatenate([at_s[slot_s, 0, cf], at_s[slot_s, 1, cb]], axis=1)
        o = r1[C:] + jnp.dot(at, vd, preferred_element_type=F32)
        kd = jnp.concatenate([kd_s[slot_s, 0, cf], kd_s[slot_s, 1, cb]], axis=0)
        ds = lax.dot_general(kd, vd, (((0,), (0,)), ((), ())), preferred_element_type=F32)
        s_s[0] = s_f * gl_s[slot_s, 0, pl.ds(cf, 1), :] + ds[:, :DH]
        s_s[1] = s_b * gl_s[slot_s, 1, pl.ds(cb, 1), :] + ds[:, DH:]
        o_ref[0, pl.ds(pl.multiple_of(cf * C, C), C), :] = o[:, :DH]
        ob[pl.ds(pl.multiple_of(cb * C, C), C), :] = o[:, DH:]
        yield

    def run(prepare, scan):
        def trip(i, _):
            prep_it = prep_phases(i) if prepare else iter(())
            conv_it = conv_pieces(i) if prepare and n_trips > 1 else iter(())
            scan_it = itertools.chain.from_iterable(scan_stages(i * U + j) for j in range(U)) if scan else iter(())
            for _ in itertools.zip_longest(scan_it, prep_it, conv_it):
                pass
            return 0
        if scan:
            s_s[...] = jnp.zeros_like(s_s)
        lax.fori_loop(0, n_trips, trip, 0)

    @pl.when(s == 0)
    def _():
        run(True, False)

    @pl.when((s > 0) & (s < n_items))
    def _():
        run(True, True)

    @pl.when(s == n_items)
    def _():
        run(False, True)

    def sum_rows(i, _):
        rows = pl.ds(pl.multiple_of(i * R, R), R)
        o_ref[0, rows, :] = o_ref[0, rows, :] + ob[rows, :]
        return 0

    @pl.when(s > 0)
    def _():
        lax.fori_loop(0, L // R, sum_rows, 0)


def _deltanet(proj, gt, conv_w, *, B, L, H):
    DH, C, R = DN_HEAD_DIM, DN_CHUNK, 256
    assert C == DH and L % (C * PREP_UNROLL) == 0
    NC = L // C
    n_items = B * H
    cur = lambda s: jnp.minimum(s, n_items - 1)
    prev = lambda s: jnp.maximum(s - 1, 0)
    col = lambda off: pl.BlockSpec((1, L, DH), lambda s: (off + cur(s) % H, cur(s) // H, 0))
    cw = lambda off: pl.BlockSpec((3, DH), lambda s: (0, off + cur(s) % H))
    kern = functools.partial(_deltanet_kernel, L=L, C=C, H=H, R=R, n_items=n_items)
    return pl.pallas_call(
        kern, grid=(n_items + 1,),
        in_specs=[col(0), col(H), col(2 * H), cw(0), cw(H), cw(2 * H),
                  pl.BlockSpec((1, 4 * H, NC, C), lambda s: (cur(s) // H, 0, 0, 0))],
        out_specs=pl.BlockSpec((1, L, DH), lambda s: (prev(s) // H, 0, prev(s) % H)),
        out_shape=jax.ShapeDtypeStruct((B, L, H * DH), F32),
        scratch_shapes=[pltpu.VMEM((L, DH), F32), pltpu.VMEM((L, DH), F32), pltpu.VMEM((L, DH), F32),
                        pltpu.VMEM((L, DH), F32),
                        pltpu.VMEM((2, 2, NC, 2 * C, DH), BF16), pltpu.VMEM((2, 2, NC, C, C), BF16),
                        pltpu.VMEM((2, 2, NC, C, DH), BF16), pltpu.VMEM((2, 2, NC, C, DH), F32),
                        pltpu.VMEM((2, 2, NC, DH), F32), pltpu.VMEM((2, DH, DH), F32),
                        pltpu.VMEM((2, PREP_UNROLL, C, 2 * DH), BF16)],
        compiler_params=_cparams("arbitrary"), name="deltanet",
    )(proj, proj, proj, conv_w, conv_w, conv_w, gt)


def _conformer_kernel(val_ref, gate_ref, w_ref, b_ref, g_ref, beta_ref, o_ref, ypad, *, L, K, R, HALO):
    pad = K // 2
    zeros = jnp.zeros((HALO, CV_GROUP), F32)
    ypad[pl.ds(0, HALO), :] = zeros
    ypad[pl.ds(HALO + L, HALO), :] = zeros

    def glu_rows(i, _):
        r0 = pl.multiple_of(i * R, R)
        ypad[pl.ds(HALO + r0, R), :] = val_ref[0, pl.ds(r0, R), :] * _sigmoid(gate_ref[0, pl.ds(r0, R), :])
        return 0

    lax.fori_loop(0, L // R, glu_rows, 0)

    def conv_rows(i, _):
        r0 = pl.multiple_of(i * R, R)
        acc = jnp.zeros((R, CV_GROUP), F32)
        for t in range(K):
            acc = acc + w_ref[t:t + 1, :] * ypad[pl.ds(r0 + (HALO - pad + t), R), :]
        y = _layer_norm(acc + b_ref[...], g_ref[...], beta_ref[...])
        o_ref[0, pl.ds(r0, R), :] = _silu(y)
        return 0

    lax.fori_loop(0, L // R, conv_rows, 0)


def _conformer(proj, w, b, g, beta, *, B, L, col0, width):
    K = w.shape[0]
    n = width // CV_GROUP
    HALO, R = 16, 256
    c0 = col0 // CV_GROUP
    vec = pl.BlockSpec((1, CV_GROUP), lambda bb, j: (0, j))
    kern = functools.partial(_conformer_kernel, L=L, K=K, R=R, HALO=HALO)
    return pl.pallas_call(
        kern, grid=(B, n),
        in_specs=[pl.BlockSpec((1, L, CV_GROUP), lambda bb, j: (c0 + j, bb, 0)),
                  pl.BlockSpec((1, L, CV_GROUP), lambda bb, j: (c0 + n + j, bb, 0)),
                  pl.BlockSpec((K, CV_GROUP), lambda bb, j: (0, j)), vec, vec, vec],
        out_specs=pl.BlockSpec((1, L, CV_GROUP), lambda bb, j: (bb, 0, j)),
        out_shape=jax.ShapeDtypeStruct((B, L, width), F32),
        scratch_shapes=[pltpu.VMEM((L + 2 * HALO, CV_GROUP), F32)],
        compiler_params=_cparams("parallel", "parallel"), name="conformer",
    )(proj, proj, w, b.reshape(1, width), g.reshape(1, width), beta.reshape(1, width))


def _out_router_kernel(dn_ref, z_ref, nw_ref, cv_ref, h_ref, w_ref, g_ref, b_ref, wr_ref, br_ref,
                       h1_ref, h1p_ref, route_ref, routet_ref, *, alpha, n_dn, sub):
    n_sub = h_ref.shape[0] // sub

    def stages(s):
        rows = pl.ds(s * sub, sub)
        heads = []
        for hh in range(n_dn // DN_HEAD_DIM):
            o = dn_ref[rows, hh * DN_HEAD_DIM:(hh + 1) * DN_HEAD_DIM]
            o = o * lax.rsqrt(jnp.mean(o * o, -1, keepdims=True) + RMS_EPS) * nw_ref[...]
            heads.append((o * _silu(z_ref[hh, rows, :])).astype(BF16))
        mix = jnp.dot(jnp.concatenate(heads, axis=1), w_ref[pl.ds(0, n_dn), :], preferred_element_type=F32)
        mix = mix + jnp.dot(cv_ref[rows, :].astype(BF16), w_ref[pl.ds(n_dn, w_ref.shape[0] - n_dn), :],
                            preferred_element_type=F32)
        yield
        h1 = _layer_norm(alpha * h_ref[rows, :] + mix, g_ref[...], b_ref[...])
        h1_ref[rows, :] = h1
        h1p_ref[rows, :] = _pack_halves(h1)
        h1b = h1.astype(BF16)
        yield
        logits = jnp.dot(h1b, wr_ref[...], preferred_element_type=F32) + br_ref[...]
        yield
        route = _route(logits)
        route_ref[rows, :] = route
        routet_ref[:, rows] = route.T[:8, :]
        yield

    n_stage = 4
    pipes = [stages(s) for s in range(n_sub)]
    for t in range(n_sub + n_stage - 1):
        for s in range(t, t - n_stage, -1):
            if 0 <= s < n_sub:
                next(pipes[s])


def _route(logits):
    lane = lax.broadcasted_iota(I32, logits.shape, 1)
    neg = jnp.float32(-jnp.inf)
    big = jnp.int32(LANES)
    gl = jnp.where(lane < N_GROUPS, logits, neg)
    gmax = jnp.max(gl, -1, keepdims=True)
    gsel = jnp.min(jnp.where(gl == gmax, lane, big), -1, keepdims=True)
    pg = 1.0 / jnp.sum(jnp.exp(gl - gmax), -1, keepdims=True)
    lo = N_GROUPS + gsel * EXPERTS_PER_GROUP
    el = jnp.where((lane >= lo) & (lane < lo + EXPERTS_PER_GROUP), logits, neg)
    e1 = jnp.max(el, -1, keepdims=True)
    i1 = jnp.min(jnp.where(el == e1, lane, big), -1, keepdims=True)
    el2 = jnp.where(lane == i1, neg, el)
    e2 = jnp.max(el2, -1, keepdims=True)
    i2 = jnp.min(jnp.where(el2 == e2, lane, big), -1, keepdims=True)
    r = jnp.exp(e2 - e1)
    p1 = 1.0 / (1.0 + r)
    p2 = r * p1
    route = jnp.where(lane == 0, (i1 - N_GROUPS).astype(F32),
                      jnp.where(lane == 1, (i2 - N_GROUPS).astype(F32),
                                jnp.where(lane == 2, pg * p1, jnp.where(lane == 3, pg * p2, 0.0))))
    return route


def _out_router(dn, proj, z_group, norm_w, cv, h, w_out, g, b, w_router, b_router, *, alpha, tm=512, sub=128):
    T, D = h.shape
    n_dn, n_cv = dn.shape[1], cv.shape[1]
    n_heads = n_dn // DN_HEAD_DIM
    assert z_group % n_heads == 0
    vec = pl.BlockSpec((1, D), lambda i: (0, 0))
    row = pl.BlockSpec((tm, D), lambda i: (i, 0))
    kern = functools.partial(_out_router_kernel, alpha=alpha, n_dn=n_dn, sub=sub)
    return pl.pallas_call(
        kern, grid=(T // tm,),
        in_specs=[pl.BlockSpec((tm, n_dn), lambda i: (i, 0)),
                  pl.BlockSpec((n_heads, tm, DN_HEAD_DIM), lambda i: (z_group // n_heads, i, 0)),
                  pl.BlockSpec((1, DN_HEAD_DIM), lambda i: (0, 0)),
                  pl.BlockSpec((tm, n_cv), lambda i: (i, 0)), row,
                  pl.BlockSpec((n_dn + n_cv, D), lambda i: (0, 0)), vec, vec,
                  pl.BlockSpec((D, LANES), lambda i: (0, 0)), pl.BlockSpec((1, LANES), lambda i: (0, 0))],
        out_specs=[row, pl.BlockSpec((tm, D // 2), lambda i: (i, 0)),
                   pl.BlockSpec((tm, LANES), lambda i: (i, 0)), pl.BlockSpec((8, tm), lambda i: (0, i))],
        out_shape=[jax.ShapeDtypeStruct((T, D), F32), jax.ShapeDtypeStruct((T, D // 2), U32),
                   jax.ShapeDtypeStruct((T, LANES), F32), jax.ShapeDtypeStruct((8, T), F32)],
        compiler_params=_cparams("parallel"), name="out_router",
    )(dn, proj, norm_w.reshape(1, DN_HEAD_DIM), cv, h, w_out, g.reshape(1, D), b.reshape(1, D), w_router, b_router)


def _onehot_t(routet_ref, tr):
    sub = lax.broadcasted_iota(I32, (2 * N_EXPERTS, tr), 0)
    e1 = routet_ref[0:1, :].astype(I32)
    e2 = routet_ref[1:2, :].astype(I32)
    return sub == jnp.where(sub < N_EXPERTS, e1, e2 + N_EXPERTS)


def _count_kernel(routet_ref, cnt_ref, *, tr):
    @pl.when(pl.program_id(0) == 0)
    def _():
        cnt_ref[...] = jnp.zeros_like(cnt_ref)
    oh = _onehot_t(routet_ref, tr).astype(F32)
    cnt_ref[...] += jnp.sum(oh, axis=1, keepdims=True)


def _place_kernel(routet_ref, base_ref, d1_ref, d2_ref, carry, *, tr):
    @pl.when(pl.program_id(0) == 0)
    def _():
        carry[...] = jnp.zeros_like(carry)
    oh = _onehot_t(routet_ref, tr)
    ohb = oh.astype(F32).astype(BF16)
    ri = lax.broadcasted_iota(I32, (tr, tr), 0)
    ci = lax.broadcasted_iota(I32, (tr, tr), 1)
    before = (ri < ci).astype(F32).astype(BF16)
    excl = jnp.dot(ohb, before, preferred_element_type=F32)
    pos = jnp.where(oh, excl + carry[...] + base_ref[...], 0.0)
    d1 = jnp.sum(pos[:N_EXPERTS], axis=0, keepdims=True)
    d2 = jnp.sum(pos[N_EXPERTS:], axis=0, keepdims=True)
    d1_ref[...] = d1.astype(I32)
    d2_ref[...] = d2.astype(I32)
    carry[...] += jnp.sum(oh.astype(F32), axis=1, keepdims=True)


def _placement(routet, tr=512):
    T = routet.shape[1]
    E = N_EXPERTS
    rt = pl.BlockSpec((8, tr), lambda i: (0, i))
    col = pl.BlockSpec((2 * E, 1), lambda i: (0, 0))
    cnt = pl.pallas_call(
        functools.partial(_count_kernel, tr=tr), grid=(T // tr,), in_specs=[rt], out_specs=col,
        out_shape=jax.ShapeDtypeStruct((2 * E, 1), F32),
        compiler_params=_cparams("arbitrary"), name="moe_count")(routet)
    c1 = cnt[:E, 0].astype(I32)
    c2 = cnt[E:, 0].astype(I32)
    padded = ((c1 + c2 + FFN_BLOCK - 1) // FFN_BLOCK) * FFN_BLOCK
    pend = jnp.cumsum(padded)
    pstart = pend - padded
    base = jnp.concatenate([pstart, pstart + c1]).astype(F32).reshape(2 * E, 1)
    lane_row = pl.BlockSpec((1, tr), lambda i: (0, i))
    d1, d2 = pl.pallas_call(
        functools.partial(_place_kernel, tr=tr), grid=(T // tr,), in_specs=[rt, col], out_specs=[lane_row, lane_row],
        out_shape=[jax.ShapeDtypeStruct((1, T), I32)] * 2,
        scratch_shapes=[pltpu.VMEM((2 * E, 1), F32)],
        compiler_params=_cparams("arbitrary"), name="moe_place")(routet, base)
    n_blocks = (T * TOP_K + E * (FFN_BLOCK - 1) + FFN_BLOCK - 1) // FFN_BLOCK
    starts = jnp.arange(n_blocks, dtype=I32) * FFN_BLOCK
    blk_e = jnp.minimum(jnp.sum((pend[None, :] <= starts[:, None]).astype(I32), axis=1), E - 1)
    blk_first = jnp.concatenate([jnp.ones((1,), I32), (blk_e[1:] != blk_e[:-1]).astype(I32)])
    n_used = (pend[-1] // FFN_BLOCK).astype(I32).reshape(1)
    ids = jnp.arange(E, dtype=I32)
    owns = padded > 0
    later = owns[None, :] & (ids[None, :] > ids[:, None])
    next_tab = jnp.min(jnp.where(later, ids[None, :], E), axis=1)
    next_tab = jnp.concatenate([next_tab, jnp.full((1,), E, I32)])
    next2_tab = next_tab[next_tab[:E]]
    hide = lambda t: jnp.where(t == E, -1, t)
    slot_tab = (jnp.cumsum(owns.astype(I32)) - 1) % 2
    steer = (blk_e, blk_first, slot_tab[blk_e], hide(next_tab[:E])[blk_e], hide(next2_tab)[blk_e])
    return d1.reshape(T), d2.reshape(T), pend.astype(I32), steer, n_used, n_blocks


def _dispatch_kernel(d1_ref, d2_ref, pend_ref, src_ref, xs_ref, zbuf, sem, zsem, *, tb):
    @pl.when(pl.program_id(0) == 0)
    def _():
        zbuf[...] = jnp.zeros_like(zbuf)
        tails = []
        for e in range(N_EXPERTS):
            end = pend_ref[e]
            nonempty = end > (pend_ref[e - 1] if e else 0)
            start = pl.multiple_of(jnp.maximum(end - FFN_BLOCK, 0), FFN_BLOCK)
            tails.append((nonempty, pltpu.make_async_copy(zbuf, xs_ref.at[pl.ds(start, FFN_BLOCK), :], zsem)))
        for nonempty, copy in tails:
            pl.when(nonempty)(copy.start)

        def spare(b):
            return pltpu.make_async_copy(
                zbuf, xs_ref.at[pl.ds(pl.multiple_of(b * FFN_BLOCK, FFN_BLOCK), FFN_BLOCK), :], zsem)
        first_spare = pend_ref[N_EXPERTS - 1] // FFN_BLOCK
        n_blocks = xs_ref.shape[0] // FFN_BLOCK
        lax.fori_loop(first_spare, n_blocks, lambda b, c: (spare(b).start(), c)[1], 0)
        for nonempty, copy in tails:
            pl.when(nonempty)(copy.wait)
        lax.fori_loop(first_spare, n_blocks, lambda b, c: (spare(b).wait(), c)[1], 0)

    def issue(t, _):
        row = src_ref.at[pl.ds(t, 1), :]
        pltpu.make_async_copy(row, xs_ref.at[pl.ds(d1_ref[t], 1), :], sem).start(priority=0)
        pltpu.make_async_copy(row, xs_ref.at[pl.ds(d2_ref[t], 1), :], sem).start(priority=1)
        return 0

    lax.fori_loop(0, tb, issue, 0, unroll=8)
    pltpu.make_async_copy(src_ref, xs_ref.at[pl.ds(0, tb), :], sem).wait()
    pltpu.make_async_copy(src_ref, xs_ref.at[pl.ds(0, tb), :], sem).wait()


def _dispatch(h1p, d1, d2, pend, n_rows, tb=1024):
    T, W = h1p.shape
    idx = pl.BlockSpec((tb,), lambda i: (i,), memory_space=pltpu.SMEM)
    return pl.pallas_call(
        functools.partial(_dispatch_kernel, tb=tb), grid=(T // tb,),
        in_specs=[idx, idx, pl.BlockSpec((N_EXPERTS,), lambda i: (0,), memory_space=pltpu.SMEM),
                  pl.BlockSpec((tb, W), lambda i: (i, 0))],
        out_specs=pl.BlockSpec(memory_space=pl.ANY),
        out_shape=jax.ShapeDtypeStruct((n_rows, W), U32),
        scratch_shapes=[pltpu.VMEM((FFN_BLOCK, W), U32), pltpu.SemaphoreType.DMA(()), pltpu.SemaphoreType.DMA(())],
        compiler_params=_cparams("arbitrary"), name="moe_dispatch",
    )(d1, d2, pend, h1p)


def _cast_rows(src, dst):
    rows = CAST_CHUNK_ELEMS // src.shape[1]

    def body(i, c):
        r = pl.ds(pl.multiple_of(i * rows, rows), rows)
        dst[r, :] = src[r, :].astype(BF16)
        return c

    lax.fori_loop(0, src.shape[0] // rows, body, 0, unroll=4)


def _ffn_kernel(blk_e_ref, blk_first_ref, blk_slot_ref, blk_next_ref, blk_next2_ref, n_used_ref,
                xs_ref, wgu_hbm, wdn_hbm, y_ref, wgu_f, wdn_f, wgu_s, wdn_s, sem, *, ff, layer):
    i = pl.program_id(0)
    used = i < n_used_ref[0]

    def weight_copies(e, slot):
        copies = []
        for k, (src, dst) in enumerate(((wgu_hbm, wgu_f), (wdn_hbm, wdn_f))):
            rows = dst.shape[1] // FFN_WEIGHT_CHUNKS
            for c in range(FFN_WEIGHT_CHUNKS):
                r = pl.ds(c * rows, rows)
                copies.append(pltpu.make_async_copy(src.at[layer, e, r, :], dst.at[slot, r, :],
                                                    sem.at[slot, k * FFN_WEIGHT_CHUNKS + c]))
        return copies

    def start(e, slot):
        for c in weight_copies(e, slot):
            c.start(priority=1)

    @pl.when(i == 0)
    def _():
        start(blk_e_ref[0], 0)

        @pl.when(blk_next_ref[0] >= 0)
        def _():
            start(blk_next_ref[0], 1)

    @pl.when(used & (blk_first_ref[i] == 1))
    def _():
        slot = blk_slot_ref[i]
        for c in weight_copies(blk_e_ref[i], slot):
            c.wait()
        _cast_rows(wgu_f.at[slot], wgu_s)
        _cast_rows(wdn_f.at[slot], wdn_s)

        @pl.when(blk_next2_ref[i] >= 0)
        def _():
            start(blk_next2_ref[i], slot)

    @pl.when(used)
    def _():
        lo, hi = _unpack_halves(xs_ref[...])
        x = jnp.concatenate([lo, hi], axis=1).astype(BF16)
        gu = jnp.dot(x, wgu_s[...], preferred_element_type=F32)
        act = (_silu(gu[:, :ff]) * gu[:, ff:]).astype(BF16)
        y_ref[...] = _pack_halves(jnp.dot(act, wdn_s[...], preferred_element_type=F32))

    @pl.when(i >= n_used_ref[0])
    def _():
        y_ref[...] = jnp.zeros_like(y_ref)


def _ffn(xs, w_gu, w_dn, layer, blk_e, blk_first, blk_slot, blk_next, blk_next2, n_used, n_blocks):
    _, E, D, FF2 = w_gu.shape
    ff = FF2 // 2
    W = xs.shape[1]
    any_spec = pl.BlockSpec(memory_space=pl.ANY)
    grid_spec = pltpu.PrefetchScalarGridSpec(
        num_scalar_prefetch=6, grid=(n_blocks,),
        in_specs=[pl.BlockSpec((FFN_BLOCK, W), lambda i, be, bf, bs, bn, bn2, nu: (jnp.minimum(i, nu[0] - 1), 0)),
                  any_spec, any_spec],
        out_specs=pl.BlockSpec((FFN_BLOCK, W), lambda i, *_: (i, 0)),
        scratch_shapes=[pltpu.VMEM((2, D, FF2), F32), pltpu.VMEM((2, ff, D), F32),
                        pltpu.VMEM((D, FF2), BF16), pltpu.VMEM((ff, D), BF16),
                        pltpu.SemaphoreType.DMA((2, 2 * FFN_WEIGHT_CHUNKS))])
    return pl.pallas_call(
        functools.partial(_ffn_kernel, ff=ff, layer=layer), grid_spec=grid_spec,
        out_shape=jax.ShapeDtypeStruct(xs.shape, U32),
        compiler_params=_cparams("arbitrary"), name="moe_ffn",
    )(blk_e, blk_first, blk_slot, blk_next, blk_next2, n_used, xs, w_gu, w_dn)


def _combine_kernel(d1_ref, d2_ref, d1n_ref, d2n_ref, y_ref, h1_ref, route_ref, g_ref, b_ref, h2_ref, h2b_ref,
                    ybuf, sem, *, tc, alpha):
    i = pl.program_id(0)
    slot = i % 2

    def row_copies(r1, r2, s, t):
        return (pltpu.make_async_copy(y_ref.at[pl.ds(r1[t], 1), :], ybuf.at[s, 0, pl.ds(t, 1), :], sem.at[s]),
                pltpu.make_async_copy(y_ref.at[pl.ds(r2[t], 1), :], ybuf.at[s, 1, pl.ds(t, 1), :], sem.at[s]))

    def wait_slot(s):
        for j in range(TOP_K):
            pltpu.make_async_copy(y_ref.at[pl.ds(0, tc), :], ybuf.at[s, j], sem.at[s]).wait()

    @pl.when(i == 0)
    def _():
        def issue(t, _):
            for j, c in enumerate(row_copies(d1_ref, d2_ref, 0, t)):
                c.start(priority=j)
            return 0
        lax.fori_loop(0, tc, issue, 0, unroll=8)

    wait_slot(slot)
    for t in range(tc):
        for j, c in enumerate(row_copies(d1n_ref, d2n_ref, 1 - slot, t)):
            c.start(priority=j)

    route = route_ref[...]
    g1 = route[:, 2:3]
    g2 = route[:, 3:4]
    lo1, hi1 = _unpack_halves(ybuf[slot, 0])
    lo2, hi2 = _unpack_halves(ybuf[slot, 1])
    ffn = jnp.concatenate([g1 * lo1 + g2 * lo2, g1 * hi1 + g2 * hi2], axis=1)
    h2 = _layer_norm(alpha * h1_ref[...] + ffn, g_ref[...], b_ref[...])
    h2_ref[...] = h2
    h2b_ref[...] = h2.astype(BF16)

    @pl.when(i == pl.num_programs(0) - 1)
    def _():
        wait_slot(1 - slot)


def _combine(y, d1, d2, h1, route, g, b, *, alpha, tc=256):
    T, D = h1.shape
    W = y.shape[1]
    n = T // tc
    idx = pl.BlockSpec((tc,), lambda i: (i,), memory_space=pltpu.SMEM)
    idx_next = pl.BlockSpec((tc,), lambda i: (jnp.minimum(i + 1, n - 1),), memory_space=pltpu.SMEM)
    row = pl.BlockSpec((tc, D), lambda i: (i, 0))
    vec = pl.BlockSpec((1, D), lambda i: (0, 0))
    return pl.pallas_call(
        functools.partial(_combine_kernel, tc=tc, alpha=alpha), grid=(n,),
        in_specs=[idx, idx, idx_next, idx_next, pl.BlockSpec(memory_space=pl.ANY), row,
                  pl.BlockSpec((tc, LANES), lambda i: (i, 0)), vec, vec],
        out_specs=[row, row],
        out_shape=[jax.ShapeDtypeStruct((T, D), F32), jax.ShapeDtypeStruct((T, D), BF16)],
        scratch_shapes=[pltpu.VMEM((2, TOP_K, tc, W), U32), pltpu.SemaphoreType.DMA((2,))],
        compiler_params=_cparams("arbitrary"), name="moe_combine",
    )(d1, d2, d1, d2, y, h1, route, g.reshape(1, D), b.reshape(1, D))


def kernel(x, emb_ln_g, emb_ln_b, w_in, short_conv_w, a_log, dt_bias, dn_norm_w, dw_conv_w, dw_conv_b,
           conv_ln_g, conv_ln_b, w_out, ln1_g, ln1_b, w_group, b_group, w_expert, b_expert, w_gate_up,
           w_down, ln2_g, ln2_b):
    B, L, D = x.shape
    T = B * L
    depth = w_in.shape[0]
    H = a_log.shape[2]
    dn_w = H * DN_HEAD_DIM
    cv_w = dw_conv_w.shape[2]
    n_ab = 4 * H
    alpha = (2 * depth) ** 0.25
    C = DN_CHUNK

    h, hb = _emb_ln(x.reshape(T, D), emb_ln_g, emb_ln_b)
    for l in range(depth):
        w = w_in[l]
        w_main = jnp.concatenate([w[:, :4 * dn_w], w[:, 4 * dn_w + n_ab:]], axis=1).astype(BF16)
        proj, gt = _in_proj(hb, w_main, w[:, 4 * dn_w:4 * dn_w + n_ab], a_log[l], dt_bias[l], B=B, L=L)
        gt = gt.reshape(B, n_ab, L // C, C)
        dn = _deltanet(proj, gt, short_conv_w[l], B=B, L=L, H=H)
        cv = _conformer(proj, dw_conv_w[l], dw_conv_b[l], conv_ln_g[l], conv_ln_b[l],
                        B=B, L=L, col0=4 * dn_w, width=cv_w)
        n_r = N_GROUPS + N_EXPERTS
        w_router = jnp.pad(jnp.concatenate([w_group[l], w_expert[l]], axis=1),
                           ((0, 0), (0, LANES - n_r))).astype(BF16)
        b_router = jnp.pad(jnp.concatenate([b_group[l], b_expert[l]]), (0, LANES - n_r)).reshape(1, LANES)
        h1, h1p, route, routet = _out_router(
            dn.reshape(T, dn_w), proj, 3 * H, dn_norm_w[l], cv.reshape(T, cv_w), h, w_out[l].astype(BF16),
            ln1_g[l], ln1_b[l],
            w_router, b_router, alpha=alpha)
        d1, d2, pend, steer, n_used, n_blocks = _placement(routet)
        xs = _dispatch(h1p, d1, d2, pend, n_blocks * FFN_BLOCK)
        y = _ffn(xs, w_gate_up, w_down, l, *steer, n_used, n_blocks)
        h, hb = _combine(y, d1, d2, h1, route, ln2_g[l], ln2_b[l], alpha=alpha)
    return h.reshape(B, L, D)
```

```python
import functools
import itertools

import jax
import jax.numpy as jnp
from jax import lax
from jax.experimental import pallas as pl
from jax.experimental.pallas import tpu as pltpu

F32 = jnp.float32
BF16 = jnp.bfloat16
U32 = jnp.uint32
I32 = jnp.int32

LANES = 128
DN_HEAD_DIM = 128
DN_CHUNK = 128
PREP_UNROLL = 8
CONV_SPREAD = 4
CV_GROUP = 128
N_GROUPS = 8
EXPERTS_PER_GROUP = 8
N_EXPERTS = N_GROUPS * EXPERTS_PER_GROUP
TOP_K = 2
FFN_BLOCK = 256
FFN_BLOCKS_PER_STEP = 2
FFN_WEIGHT_CHUNKS = 4
CAST_CHUNK_ELEMS = 32 * 1024
LN_EPS = 1e-5
RMS_EPS = 1e-6
VMEM_LIMIT = 56 * 1024 * 1024


def _cparams(*sem):
    return pltpu.CompilerParams(dimension_semantics=sem, vmem_limit_bytes=VMEM_LIMIT)


def _layer_norm(x, g, b):
    mu = jnp.mean(x, -1, keepdims=True)
    xc = x - mu
    var = jnp.mean(xc * xc, -1, keepdims=True)
    return xc * lax.rsqrt(var + LN_EPS) * g + b


def _sigmoid(x):
    return 1.0 / (1.0 + jnp.exp(-x))


def _silu(x):
    return x * _sigmoid(x)


def _pack_halves(y):
    n = y.shape[1] // 2
    bits = lax.bitcast_convert_type(y.astype(BF16).astype(F32), U32)
    return (bits[:, :n] >> 16) | bits[:, n:]


def _unpack_halves(u):
    lo = lax.bitcast_convert_type(u << 16, F32)
    hi = lax.bitcast_convert_type(u & jnp.uint32(0xFFFF0000), F32)
    return lo, hi


def _emb_ln_kernel(x_ref, g_ref, b_ref, h_ref, hb_ref):
    h = _layer_norm(x_ref[...], g_ref[...], b_ref[...])
    h_ref[...] = h
    hb_ref[...] = h.astype(BF16)


def _emb_ln(x, g, b, tm=512):
    T, D = x.shape
    row = pl.BlockSpec((tm, D), lambda i: (i, 0))
    vec = pl.BlockSpec((1, D), lambda i: (0, 0))
    return pl.pallas_call(
        _emb_ln_kernel, grid=(T // tm,), in_specs=[row, vec, vec], out_specs=[row, row],
        out_shape=[jax.ShapeDtypeStruct((T, D), F32), jax.ShapeDtypeStruct((T, D), BF16)],
        compiler_params=_cparams("parallel"), name="emb_ln")(x, g.reshape(1, D), b.reshape(1, D))


def _in_proj_kernel(x_ref, w_ref, wab_ref, alog_ref, dtb_ref, o_ref, gt_ref, *, n_decay, n_gate_rows):
    @pl.when(pl.program_id(1) == 0)
    def _():
        ab = jnp.dot(x_ref[...], wab_ref[...], preferred_element_type=F32)
        lane = lax.broadcasted_iota(I32, ab.shape, 1)
        sp = ab + dtb_ref[...]
        softplus = jnp.maximum(sp, 0.0) + jnp.log(1.0 + jnp.exp(-jnp.abs(sp)))
        decay = -jnp.exp(alog_ref[...]) * softplus
        gt_ref[0] = jnp.where(lane < n_decay, decay, _sigmoid(ab)).T[:n_gate_rows, :]

    acc = jnp.dot(x_ref[...], w_ref[...], preferred_element_type=F32)
    for j in range(o_ref.shape[0]):
        o_ref[j] = acc[:, j * LANES:(j + 1) * LANES]


def _in_proj(hb, w, w_ab, a_log, dt_bias, *, B, L, tm=1024, tn=2048):
    T, D = hb.shape
    N = w.shape[1]
    n = a_log.size
    n_gate_rows = 2 * n
    pad = lambda v: jnp.pad(v.reshape(1, n).astype(F32), ((0, 0), (0, LANES - n)))
    wab = jnp.pad(w_ab, ((0, 0), (0, LANES - w_ab.shape[1]))).astype(BF16)
    vec = pl.BlockSpec((1, LANES), lambda i, j: (0, 0))
    per_batch = L // tm
    return pl.pallas_call(
        functools.partial(_in_proj_kernel, n_decay=n, n_gate_rows=n_gate_rows), grid=(T // tm, N // tn),
        in_specs=[pl.BlockSpec((tm, D), lambda i, j: (i, 0)), pl.BlockSpec((D, tn), lambda i, j: (0, j)),
                  pl.BlockSpec((D, LANES), lambda i, j: (0, 0)), vec, vec],
        out_specs=[pl.BlockSpec((tn // LANES, tm, LANES), lambda i, j: (j, i, 0)),
                   pl.BlockSpec((1, n_gate_rows, tm), lambda i, j: (i // per_batch, 0, i % per_batch))],
        out_shape=[jax.ShapeDtypeStruct((N // LANES, T, LANES), F32),
                   jax.ShapeDtypeStruct((B, n_gate_rows, L), F32)],
        compiler_params=_cparams("parallel", "arbitrary"), name="in_proj",
    )(hb, w, wab, pad(a_log), pad(dt_bias))


def _dot(a, b):
    return jnp.dot(a.astype(BF16), b.astype(BF16), preferred_element_type=F32)


def _dot_nt(a, b):
    return lax.dot_general(a.astype(BF16), b.astype(BF16), (((1,), (1,)), ((), ())),
                           preferred_element_type=F32)


def _dot_tn(a, b):
    return lax.dot_general(a.astype(BF16), b.astype(BF16), (((0,), (0,)), ((), ())),
                           preferred_element_type=F32)


def _deltanet_kernel(q_ref, k_ref, v_ref, cq_ref, ck_ref, cv_ref, gt_ref, o_ref,
                     qs, ks, vs, ob, wq_s, at_s, kd_s, u_s, gl_s, s_s, rhs_s, *, L, C, H, R, n_items):
    DH = DN_HEAD_DIM
    s = pl.program_id(0)
    h = jnp.minimum(s, n_items - 1) % H
    slot_p = s % 2
    slot_s = 1 - slot_p
    NC = L // C
    U = PREP_UNROLL

    def conv_rows(r0, first=False, last=False):
        rows = pl.ds(r0, R)
        rid = lax.broadcasted_iota(I32, (R, DH), 0)

        def conv_silu(ref, w_ref):
            x = ref[0, rows, :]
            if first:
                x_prev = jnp.where(rid == 0, 0.0, pltpu.roll(x, 1, 0))
            else:
                x_prev = ref[0, pl.ds(r0 - 1, R), :]
            if last is False:
                x_next = ref[0, pl.ds(r0 + 1, R), :]
            else:
                at_end = jnp.where(rid == R - 1, 0.0, pltpu.roll(x, R - 1, 0))
                x_next = at_end if last is True else jnp.where(
                    last, at_end, ref[0, pl.ds(jnp.minimum(r0 + 1, L - R), R), :])
            return _silu(w_ref[0:1, :] * x_prev + w_ref[1:2, :] * x + w_ref[2:3, :] * x_next)

        q = conv_silu(q_ref, cq_ref)
        k = conv_silu(k_ref, ck_ref)
        qs[rows, :] = q * (lax.rsqrt(jnp.sum(q * q, -1, keepdims=True) + RMS_EPS) * (DH ** -0.5))
        ks[rows, :] = k * lax.rsqrt(jnp.sum(k * k, -1, keepdims=True) + RMS_EPS)
        vs[rows, :] = conv_silu(v_ref, cv_ref)

    n_trips = NC // U
    trip_blocks = (U * C) // R

    @pl.when(s < n_items)
    def _():
        for b in range(trip_blocks):
            conv_rows(b * R, first=b == 0, last=(b + 1) * R == L)

    def conv_pieces(i):
        nxt = jnp.minimum(i + 1, n_trips - 1)
        for b in range(trip_blocks):
            r0 = pl.multiple_of((nxt * trip_blocks + b) * R, R)
            conv_rows(r0, last=(r0 + R == L) if b == trip_blocks - 1 else False)
            for _ in range(CONV_SPREAD):
                yield

    ri = lax.broadcasted_iota(I32, (C, C), 0)
    ci = lax.broadcasted_iota(I32, (C, C), 1)
    eye = ri == ci
    eye_f = eye.astype(F32)
    incl = (ri >= ci, ri <= ci)
    strict = (ri > ci, ri < ci)
    n_sq = C.bit_length() - 2
    zero_b = jnp.zeros((C, C), BF16)

    def block_diag(a, b):
        return jnp.concatenate([jnp.concatenate([a, zero_b], axis=1),
                                jnp.concatenate([zero_b, b], axis=1)], axis=0)

    def prep_start(c, j):
        rows = pl.ds(pl.multiple_of(c * C, C), C)
        q = qs[rows, :]
        k = ks[rows, :]
        v = vs[rows, :]
        kq = _dot_nt(jnp.concatenate([k, q], axis=0), k)
        kk, qk = kq[:C], kq[C:]
        a2 = []
        for d in range(2):
            g_row = gt_ref[0, d * H + h, pl.ds(c, 1), :]
            b_row = gt_ref[0, (2 + d) * H + h, pl.ds(c, 1), :]
            m, ms, mt = incl[d], strict[d], incl[1 - d]
            g_col = jnp.sum(jnp.where(eye, g_row, 0.0), axis=1, keepdims=True)
            b_col = jnp.sum(jnp.where(eye, b_row, 0.0), axis=1, keepdims=True)
            gc_col = jnp.sum(jnp.where(m, g_row, 0.0), axis=1, keepdims=True)
            gc_row = jnp.sum(jnp.where(mt, g_col, 0.0), axis=0, keepdims=True)
            tot = jnp.sum(g_row, axis=1, keepdims=True)
            decay = jnp.where(m, jnp.exp(jnp.where(m, gc_col - gc_row, 0.0)), 0.0)
            a2.append(jnp.where(ms, -(kk * b_col * decay), 0.0))
            egc = jnp.exp(gc_col)
            rhs_s[d, j] = jnp.concatenate([v * b_col, k * (b_col * egc)], axis=1).astype(BF16)
            wq_s[slot_p, d, c, pl.ds(C, C), :] = (q * egc).astype(BF16)
            at_s[slot_p, d, c] = (qk * decay).astype(BF16)
            kd_s[slot_p, d, c] = (k * jnp.exp(tot - gc_col)).astype(BF16)
            gl_s[slot_p, d, pl.ds(c, 1), :] = jnp.broadcast_to(jnp.exp(tot), (1, DH))
        return jnp.concatenate(a2, axis=1)

    def prep_phases(i):
        cs = [i * U + j for j in range(U)]
        ps = []
        for j, c in enumerate(cs):
            ps.append(prep_start(c, j))
            if j == U // 2 - 1:
                yield
        eye2 = jnp.concatenate([eye_f, eye_f], axis=1)
        xs_ = [eye2 + p for p in ps]
        yield
        for _ in range(n_sq):
            pbs = [p.astype(BF16) for p in ps]
            ps = [jnp.dot(pb, block_diag(pb[:, :C], pb[:, C:]), preferred_element_type=F32) for pb in pbs]
            yield
            pbs = [p.astype(BF16) for p in ps]
            xs_ = [x + jnp.dot(x.astype(BF16), block_diag(pb[:, :C], pb[:, C:]), preferred_element_type=F32)
                   for x, pb in zip(xs_, pbs)]
            yield
        for j, (c, x) in enumerate(zip(cs, xs_)):
            for d in range(2):
                sol = jnp.dot(x[:, d * C:(d + 1) * C].astype(BF16), rhs_s[d, j], preferred_element_type=F32)
                u_s[slot_p, d, c] = sol[:, :DH]
                wq_s[slot_p, d, c, pl.ds(0, C), :] = sol[:, DH:].astype(BF16)
            if j == U // 2 - 1:
                yield
        yield

    def scan_stages(n):
        cf, cb = n, NC - 1 - n
        s_f, s_b = s_s[0], s_s[1]
        wq = jnp.concatenate([wq_s[slot_s, 0, cf], wq_s[slot_s, 1, cb]], axis=1)
        r1 = jnp.dot(wq, block_diag(s_f.astype(BF16), s_b.astype(BF16)), preferred_element_type=F32)
        yield
        u = jnp.concatenate([u_s[slot_s, 0, cf], u_s[slot_s, 1, cb]], axis=1)
        v_new = (u - r1[:C]).astype(BF16)
        vd = block_diag(v_new[:, :DH], v_new[:, DH:])
        at = jnp.concatenate([at_s[slot_s, 0, cf], at_s[slot_s, 1, cb]], axis=1)
        o = r1[C:] + jnp.dot(at, vd, preferred_element_type=F32)
        kd = jnp.concatenate([kd_s[slot_s, 0, cf], kd_s[slot_s, 1, cb]], axis=0)
        ds = lax.dot_general(kd, vd, (((0,), (0,)), ((), ())), preferred_element_type=F32)
        s_s[0] = s_f * gl_s[slot_s, 0, pl.ds(cf, 1), :] + ds[:, :DH]
        s_s[1] = s_b * gl_s[slot_s, 1, pl.ds(cb, 1), :] + ds[:, DH:]
        o_ref[0, pl.ds(pl.multiple_of(cf * C, C), C), :] = o[:, :DH]
        ob[pl.ds(pl.multiple_of(cb * C, C), C), :] = o[:, DH:]
        yield

    def run(prepare, scan):
        def trip(i, _):
            prep_it = prep_phases(i) if prepare else iter(())
            conv_it = conv_pieces(i) if prepare and n_trips > 1 else iter(())
            scan_it = itertools.chain.from_iterable(scan_stages(i * U + j) for j in range(U)) if scan else iter(())
            for _ in itertools.zip_longest(scan_it, prep_it, conv_it):
                pass
            return 0
        if scan:
            s_s[...] = jnp.zeros_like(s_s)
        lax.fori_loop(0, n_trips, trip, 0)

    @pl.when(s == 0)
    def _():
        run(True, False)

    @pl.when((s > 0) & (s < n_items))
    def _():
        run(True, True)

    @pl.when(s == n_items)
    def _():
        run(False, True)

    def sum_rows(i, _):
        rows = pl.ds(pl.multiple_of(i * R, R), R)
        o_ref[0, rows, :] = o_ref[0, rows, :] + ob[rows, :]
        return 0

    @pl.when(s > 0)
    def _():
        lax.fori_loop(0, L // R, sum_rows, 0)


def _deltanet(proj, gt, conv_w, *, B, L, H):
    DH, C, R = DN_HEAD_DIM, DN_CHUNK, 256
    assert C == DH and L % (C * PREP_UNROLL) == 0
    NC = L // C
    n_items = B * H
    cur = lambda s: jnp.minimum(s, n_items - 1)
    prev = lambda s: jnp.maximum(s - 1, 0)
    col = lambda off: pl.BlockSpec((1, L, DH), lambda s: (off + cur(s) % H, cur(s) // H, 0))
    cw = lambda off: pl.BlockSpec((3, DH), lambda s: (0, off + cur(s) % H))
    kern = functools.partial(_deltanet_kernel, L=L, C=C, H=H, R=R, n_items=n_items)
    return pl.pallas_call(
        kern, grid=(n_items + 1,),
        in_specs=[col(0), col(H), col(2 * H), cw(0), cw(H), cw(2 * H),
                  pl.BlockSpec((1, 4 * H, NC, C), lambda s: (cur(s) // H, 0, 0, 0))],
        out_specs=pl.BlockSpec((1, L, DH), lambda s: (prev(s) // H, 0, prev(s) % H)),
        out_shape=jax.ShapeDtypeStruct((B, L, H * DH), F32),
        scratch_shapes=[pltpu.VMEM((L, DH), F32), pltpu.VMEM((L, DH), F32), pltpu.VMEM((L, DH), F32),
                        pltpu.VMEM((L, DH), F32),
                        pltpu.VMEM((2, 2, NC, 2 * C, DH), BF16), pltpu.VMEM((2, 2, NC, C, C), BF16),
                        pltpu.VMEM((2, 2, NC, C, DH), BF16), pltpu.VMEM((2, 2, NC, C, DH), F32),
                        pltpu.VMEM((2, 2, NC, DH), F32), pltpu.VMEM((2, DH, DH), F32),
                        pltpu.VMEM((2, PREP_UNROLL, C, 2 * DH), BF16)],
        compiler_params=_cparams("arbitrary"), name="deltanet",
    )(proj, proj, proj, conv_w, conv_w, conv_w, gt)


def _conformer_kernel(val_ref, gate_ref, w_ref, b_ref, g_ref, beta_ref, o_ref, ypad, *, L, K, R, HALO):
    pad = K // 2
    zeros = jnp.zeros((HALO, CV_GROUP), F32)
    ypad[pl.ds(0, HALO), :] = zeros
    ypad[pl.ds(HALO + L, HALO), :] = zeros

    def glu_rows(i, _):
        r0 = pl.multiple_of(i * R, R)
        ypad[pl.ds(HALO + r0, R), :] = val_ref[0, pl.ds(r0, R), :] * _sigmoid(gate_ref[0, pl.ds(r0, R), :])
        return 0

    lax.fori_loop(0, L // R, glu_rows, 0)

    def conv_rows(i, _):
        r0 = pl.multiple_of(i * R, R)
        acc = jnp.zeros((R, CV_GROUP), F32)
        for t in range(K):
            acc = acc + w_ref[t:t + 1, :] * ypad[pl.ds(r0 + (HALO - pad + t), R), :]
        y = _layer_norm(acc + b_ref[...], g_ref[...], beta_ref[...])
        o_ref[0, pl.ds(r0, R), :] = _silu(y)
        return 0

    lax.fori_loop(0, L // R, conv_rows, 0)


def _conformer(proj, w, b, g, beta, *, B, L, col0, width):
    K = w.shape[0]
    n = width // CV_GROUP
    HALO, R = 16, 256
    c0 = col0 // CV_GROUP
    vec = pl.BlockSpec((1, CV_GROUP), lambda bb, j: (0, j))
    kern = functools.partial(_conformer_kernel, L=L, K=K, R=R, HALO=HALO)
    return pl.pallas_call(
        kern, grid=(B, n),
        in_specs=[pl.BlockSpec((1, L, CV_GROUP), lambda bb, j: (c0 + j, bb, 0)),
                  pl.BlockSpec((1, L, CV_GROUP), lambda bb, j: (c0 + n + j, bb, 0)),
                  pl.BlockSpec((K, CV_GROUP), lambda bb, j: (0, j)), vec, vec, vec],
        out_specs=pl.BlockSpec((1, L, CV_GROUP), lambda bb, j: (bb, 0, j)),
        out_shape=jax.ShapeDtypeStruct((B, L, width), F32),
        scratch_shapes=[pltpu.VMEM((L + 2 * HALO, CV_GROUP), F32)],
        compiler_params=_cparams("parallel", "parallel"), name="conformer",
    )(proj, proj, w, b.reshape(1, width), g.reshape(1, width), beta.reshape(1, width))


def _out_router_kernel(dn_ref, z_ref, nw_ref, cv_ref, h_ref, w_ref, g_ref, b_ref, wr_ref, br_ref,
                       h1_ref, h1p_ref, route_ref, routet_ref, *, alpha, n_dn, sub):
    n_sub = h_ref.shape[0] // sub

    def stages(s):
        rows = pl.ds(s * sub, sub)
        heads = []
        for hh in range(n_dn // DN_HEAD_DIM):
            o = dn_ref[rows, hh * DN_HEAD_DIM:(hh + 1) * DN_HEAD_DIM]
            o = o * lax.rsqrt(jnp.mean(o * o, -1, keepdims=True) + RMS_EPS) * nw_ref[...]
            heads.append((o * _silu(z_ref[hh, rows, :])).astype(BF16))
        mix = jnp.dot(jnp.concatenate(heads, axis=1), w_ref[pl.ds(0, n_dn), :], preferred_element_type=F32)
        mix = mix + jnp.dot(cv_ref[rows, :].astype(BF16), w_ref[pl.ds(n_dn, w_ref.shape[0] - n_dn), :],
                            preferred_element_type=F32)
        yield
        h1 = _layer_norm(alpha * h_ref[rows, :] + mix, g_ref[...], b_ref[...])
        h1_ref[rows, :] = h1
        h1p_ref[rows, :] = _pack_halves(h1)
        h1b = h1.astype(BF16)
        yield
        logits = jnp.dot(h1b, wr_ref[...], preferred_element_type=F32) + br_ref[...]
        yield
        route = _route(logits)
        route_ref[rows, :] = route
        routet_ref[:, rows] = route.T[:8, :]
        yield

    n_stage = 4
    pipes = [stages(s) for s in range(n_sub)]
    for t in range(n_sub + n_stage - 1):
        for s in range(t, t - n_stage, -1):
            if 0 <= s < n_sub:
                next(pipes[s])


def _route(logits):
    lane = lax.broadcasted_iota(I32, logits.shape, 1)
    neg = jnp.float32(-jnp.inf)
    big = jnp.int32(LANES)
    gl = jnp.where(lane < N_GROUPS, logits, neg)
    gmax = jnp.max(gl, -1, keepdims=True)
    gsel = jnp.min(jnp.where(gl == gmax, lane, big), -1, keepdims=True)
    pg = 1.0 / jnp.sum(jnp.exp(gl - gmax), -1, keepdims=True)
    lo = N_GROUPS + gsel * EXPERTS_PER_GROUP
    el = jnp.where((lane >= lo) & (lane < lo + EXPERTS_PER_GROUP), logits, neg)
    e1 = jnp.max(el, -1, keepdims=True)
    i1 = jnp.min(jnp.where(el == e1, lane, big), -1, keepdims=True)
    el2 = jnp.where(lane == i1, neg, el)
    e2 = jnp.max(el2, -1, keepdims=True)
    i2 = jnp.min(jnp.where(el2 == e2, lane, big), -1, keepdims=True)
    r = jnp.exp(e2 - e1)
    p1 = 1.0 / (1.0 + r)
    p2 = r * p1
    route = jnp.where(lane == 0, (i1 - N_GROUPS).astype(F32),
                      jnp.where(lane == 1, (i2 - N_GROUPS).astype(F32),
                                jnp.where(lane == 2, pg * p1, jnp.where(lane == 3, pg * p2, 0.0))))
    return route


def _out_router(dn, proj, z_group, norm_w, cv, h, w_out, g, b, w_router, b_router, *, alpha, tm=512, sub=128):
    T, D = h.shape
    n_dn, n_cv = dn.shape[1], cv.shape[1]
    n_heads = n_dn // DN_HEAD_DIM
    assert z_group % n_heads == 0
    vec = pl.BlockSpec((1, D), lambda i: (0, 0))
    row = pl.BlockSpec((tm, D), lambda i: (i, 0))
    kern = functools.partial(_out_router_kernel, alpha=alpha, n_dn=n_dn, sub=sub)
    return pl.pallas_call(
        kern, grid=(T // tm,),
        in_specs=[pl.BlockSpec((tm, n_dn), lambda i: (i, 0)),
                  pl.BlockSpec((n_heads, tm, DN_HEAD_DIM), lambda i: (z_group // n_heads, i, 0)),
                  pl.BlockSpec((1, DN_HEAD_DIM), lambda i: (0, 0)),
                  pl.BlockSpec((tm, n_cv), lambda i: (i, 0)), row,
                  pl.BlockSpec((n_dn + n_cv, D), lambda i: (0, 0)), vec, vec,
                  pl.BlockSpec((D, LANES), lambda i: (0, 0)), pl.BlockSpec((1, LANES), lambda i: (0, 0))],
        out_specs=[row, pl.BlockSpec((tm, D // 2), lambda i: (i, 0)),
                   pl.BlockSpec((tm, LANES), lambda i: (i, 0)), pl.BlockSpec((8, tm), lambda i: (0, i))],
        out_shape=[jax.ShapeDtypeStruct((T, D), F32), jax.ShapeDtypeStruct((T, D // 2), U32),
                   jax.ShapeDtypeStruct((T, LANES), F32), jax.ShapeDtypeStruct((8, T), F32)],
        compiler_params=_cparams("parallel"), name="out_router",
    )(dn, proj, norm_w.reshape(1, DN_HEAD_DIM), cv, h, w_out, g.reshape(1, D), b.reshape(1, D), w_router, b_router)


def _onehot_t(routet_ref, tr):
    sub = lax.broadcasted_iota(I32, (2 * N_EXPERTS, tr), 0)
    e1 = routet_ref[0:1, :].astype(I32)
    e2 = routet_ref[1:2, :].astype(I32)
    return sub == jnp.where(sub < N_EXPERTS, e1, e2 + N_EXPERTS)


def _count_kernel(routet_ref, cnt_ref, *, tr):
    @pl.when(pl.program_id(0) == 0)
    def _():
        cnt_ref[...] = jnp.zeros_like(cnt_ref)
    oh = _onehot_t(routet_ref, tr).astype(F32)
    cnt_ref[...] += jnp.sum(oh, axis=1, keepdims=True)


def _place_kernel(routet_ref, base_ref, d1_ref, d2_ref, carry, *, tr):
    @pl.when(pl.program_id(0) == 0)
    def _():
        carry[...] = jnp.zeros_like(carry)
    oh = _onehot_t(routet_ref, tr)
    ohb = oh.astype(F32).astype(BF16)
    ri = lax.broadcasted_iota(I32, (tr, tr), 0)
    ci = lax.broadcasted_iota(I32, (tr, tr), 1)
    before = (ri < ci).astype(F32).astype(BF16)
    excl = jnp.dot(ohb, before, preferred_element_type=F32)
    pos = jnp.where(oh, excl + carry[...] + base_ref[...], 0.0)
    d1 = jnp.sum(pos[:N_EXPERTS], axis=0, keepdims=True)
    d2 = jnp.sum(pos[N_EXPERTS:], axis=0, keepdims=True)
    d1_ref[...] = d1.astype(I32)
    d2_ref[...] = d2.astype(I32)
    carry[...] += jnp.sum(oh.astype(F32), axis=1, keepdims=True)


def _placement(routet, tr=512):
    T = routet.shape[1]
    E = N_EXPERTS
    rt = pl.BlockSpec((8, tr), lambda i: (0, i))
    col = pl.BlockSpec((2 * E, 1), lambda i: (0, 0))
    cnt = pl.pallas_call(
        functools.partial(_count_kernel, tr=tr), grid=(T // tr,), in_specs=[rt], out_specs=col,
        out_shape=jax.ShapeDtypeStruct((2 * E, 1), F32),
        compiler_params=_cparams("arbitrary"), name="moe_count")(routet)
    c1 = cnt[:E, 0].astype(I32)
    c2 = cnt[E:, 0].astype(I32)
    padded = ((c1 + c2 + FFN_BLOCK - 1) // FFN_BLOCK) * FFN_BLOCK
    pend = jnp.cumsum(padded)
    pstart = pend - padded
    base = jnp.concatenate([pstart, pstart + c1]).astype(F32).reshape(2 * E, 1)
    lane_row = pl.BlockSpec((1, tr), lambda i: (0, i))
    d1, d2 = pl.pallas_call(
        functools.partial(_place_kernel, tr=tr), grid=(T // tr,), in_specs=[rt, col], out_specs=[lane_row, lane_row],
        out_shape=[jax.ShapeDtypeStruct((1, T), I32)] * 2,
        scratch_shapes=[pltpu.VMEM((2 * E, 1), F32)],
        compiler_params=_cparams("arbitrary"), name="moe_place")(routet, base)
    n_blocks = (T * TOP_K + E * (FFN_BLOCK - 1) + FFN_BLOCK - 1) // FFN_BLOCK
    starts = jnp.arange(n_blocks, dtype=I32) * FFN_BLOCK
    blk_e = jnp.minimum(jnp.sum((pend[None, :] <= starts[:, None]).astype(I32), axis=1), E - 1)
    blk_first = jnp.concatenate([jnp.ones((1,), I32), (blk_e[1:] != blk_e[:-1]).astype(I32)])
    n_used = (pend[-1] // FFN_BLOCK).astype(I32).reshape(1)
    ids = jnp.arange(E, dtype=I32)
    owns = padded > 0
    later = owns[None, :] & (ids[None, :] > ids[:, None])
    next_tab = jnp.min(jnp.where(later, ids[None, :], E), axis=1)
    next_tab = jnp.concatenate([next_tab, jnp.full((1,), E, I32)])
    next2_tab = next_tab[next_tab[:E]]
    hide = lambda t: jnp.where(t == E, -1, t)
    slot_tab = (jnp.cumsum(owns.astype(I32)) - 1) % 2
    steer = (blk_e, blk_first, slot_tab[blk_e], hide(next_tab[:E])[blk_e], hide(next2_tab)[blk_e])
    return d1.reshape(T), d2.reshape(T), pend.astype(I32), steer, n_used, n_blocks


def _dispatch_kernel(d1_ref, d2_ref, pend_ref, src_ref, xs_ref, zbuf, sem, zsem, *, tb):
    @pl.when(pl.program_id(0) == 0)
    def _():
        zbuf[...] = jnp.zeros_like(zbuf)
        tails = []
        for e in range(N_EXPERTS):
            end = pend_ref[e]
            nonempty = end > (pend_ref[e - 1] if e else 0)
            start = pl.multiple_of(jnp.maximum(end - FFN_BLOCK, 0), FFN_BLOCK)
            tails.append((nonempty, pltpu.make_async_copy(zbuf, xs_ref.at[pl.ds(start, FFN_BLOCK), :], zsem)))
        for nonempty, copy in tails:
            pl.when(nonempty)(copy.start)

        def spare(b):
            return pltpu.make_async_copy(
                zbuf, xs_ref.at[pl.ds(pl.multiple_of(b * FFN_BLOCK, FFN_BLOCK), FFN_BLOCK), :], zsem)
        first_spare = pend_ref[N_EXPERTS - 1] // FFN_BLOCK
        n_blocks = xs_ref.shape[0] // FFN_BLOCK
        lax.fori_loop(first_spare, n_blocks, lambda b, c: (spare(b).start(), c)[1], 0)
        for nonempty, copy in tails:
            pl.when(nonempty)(copy.wait)
        lax.fori_loop(first_spare, n_blocks, lambda b, c: (spare(b).wait(), c)[1], 0)

    def issue(t, _):
        row = src_ref.at[pl.ds(t, 1), :]
        pltpu.make_async_copy(row, xs_ref.at[pl.ds(d1_ref[t], 1), :], sem).start(priority=0)
        pltpu.make_async_copy(row, xs_ref.at[pl.ds(d2_ref[t], 1), :], sem).start(priority=1)
        return 0

    lax.fori_loop(0, tb, issue, 0, unroll=8)
    pltpu.make_async_copy(src_ref, xs_ref.at[pl.ds(0, tb), :], sem).wait()
    pltpu.make_async_copy(src_ref, xs_ref.at[pl.ds(0, tb), :], sem).wait()


def _dispatch(h1p, d1, d2, pend, n_rows, tb=1024):
    T, W = h1p.shape
    idx = pl.BlockSpec((tb,), lambda i: (i,), memory_space=pltpu.SMEM)
    return pl.pallas_call(
        functools.partial(_dispatch_kernel, tb=tb), grid=(T // tb,),
        in_specs=[idx, idx, pl.BlockSpec((N_EXPERTS,), lambda i: (0,), memory_space=pltpu.SMEM),
                  pl.BlockSpec((tb, W), lambda i: (i, 0))],
        out_specs=pl.BlockSpec(memory_space=pl.ANY),
        out_shape=jax.ShapeDtypeStruct((n_rows, W), U32),
        scratch_shapes=[pltpu.VMEM((FFN_BLOCK, W), U32), pltpu.SemaphoreType.DMA(()), pltpu.SemaphoreType.DMA(())],
        compiler_params=_cparams("arbitrary"), name="moe_dispatch",
    )(d1, d2, pend, h1p)


def _cast_rows(src, dst):
    rows = CAST_CHUNK_ELEMS // src.shape[1]

    def body(i, c):
        r = pl.ds(pl.multiple_of(i * rows, rows), rows)
        dst[r, :] = src[r, :].astype(BF16)
        return c

    lax.fori_loop(0, src.shape[0] // rows, body, 0, unroll=4)


def _ffn_kernel(blk_e_ref, blk_first_ref, blk_slot_ref, blk_next_ref, blk_next2_ref, n_used_ref,
                xs_ref, wgu_hbm, wdn_hbm, y_ref, wgu_f, wdn_f, wgu_s, wdn_s, sem, *, ff, layer):

    def weight_copies(e, slot):
        copies = []
        for k, (src, dst) in enumerate(((wgu_hbm, wgu_f), (wdn_hbm, wdn_f))):
            rows = dst.shape[1] // FFN_WEIGHT_CHUNKS
            for c in range(FFN_WEIGHT_CHUNKS):
                r = pl.ds(c * rows, rows)
                copies.append(pltpu.make_async_copy(src.at[layer, e, r, :], dst.at[slot, r, :],
                                                    sem.at[slot, k * FFN_WEIGHT_CHUNKS + c]))
        return copies

    def start(e, slot):
        for c in weight_copies(e, slot):
            c.start(priority=1)

    @pl.when(pl.program_id(0) == 0)
    def _():
        start(blk_e_ref[0], 0)

        @pl.when(blk_next_ref[0] >= 0)
        def _():
            start(blk_next_ref[0], 1)

    for j in range(FFN_BLOCKS_PER_STEP):
        i = pl.program_id(0) * FFN_BLOCKS_PER_STEP + j
        rows = pl.ds(j * FFN_BLOCK, FFN_BLOCK)
        used = i < n_used_ref[0]

        @pl.when(used & (blk_first_ref[i] == 1))
        def _():
            slot = blk_slot_ref[i]
            for c in weight_copies(blk_e_ref[i], slot):
                c.wait()
            _cast_rows(wgu_f.at[slot], wgu_s)
            _cast_rows(wdn_f.at[slot], wdn_s)

            @pl.when(blk_next2_ref[i] >= 0)
            def _():
                start(blk_next2_ref[i], slot)

        @pl.when(used)
        def _():
            lo, hi = _unpack_halves(xs_ref[rows, :])
            x = jnp.concatenate([lo, hi], axis=1).astype(BF16)
            gu = jnp.dot(x, wgu_s[...], preferred_element_type=F32)
            act = (_silu(gu[:, :ff]) * gu[:, ff:]).astype(BF16)
            y_ref[rows, :] = _pack_halves(jnp.dot(act, wdn_s[...], preferred_element_type=F32))

        @pl.when(jnp.logical_not(used))
        def _():
            y_ref[rows, :] = jnp.zeros((FFN_BLOCK, y_ref.shape[1]), y_ref.dtype)


def _ffn(xs, w_gu, w_dn, layer, blk_e, blk_first, blk_slot, blk_next, blk_next2, n_used, n_blocks):
    _, E, D, FF2 = w_gu.shape
    ff = FF2 // 2
    W = xs.shape[1]
    assert n_blocks % FFN_BLOCKS_PER_STEP == 0
    step_rows = FFN_BLOCK * FFN_BLOCKS_PER_STEP
    any_spec = pl.BlockSpec(memory_space=pl.ANY)
    last_used_step = lambda nu: (nu[0] - 1) // FFN_BLOCKS_PER_STEP
    grid_spec = pltpu.PrefetchScalarGridSpec(
        num_scalar_prefetch=6, grid=(n_blocks // FFN_BLOCKS_PER_STEP,),
        in_specs=[pl.BlockSpec((step_rows, W), lambda i, be, bf, bs, bn, bn2, nu: (jnp.minimum(i, last_used_step(nu)), 0)),
                  any_spec, any_spec],
        out_specs=pl.BlockSpec((step_rows, W), lambda i, *_: (i, 0)),
        scratch_shapes=[pltpu.VMEM((2, D, FF2), F32), pltpu.VMEM((2, ff, D), F32),
                        pltpu.VMEM((D, FF2), BF16), pltpu.VMEM((ff, D), BF16),
                        pltpu.SemaphoreType.DMA((2, 2 * FFN_WEIGHT_CHUNKS))])
    return pl.pallas_call(
        functools.partial(_ffn_kernel, ff=ff, layer=layer), grid_spec=grid_spec,
        out_shape=jax.ShapeDtypeStruct(xs.shape, U32),
        compiler_params=_cparams("arbitrary"), name="moe_ffn",
    )(blk_e, blk_first, blk_slot, blk_next, blk_next2, n_used, xs, w_gu, w_dn)


def _combine_kernel(d1_ref, d2_ref, d1n_ref, d2n_ref, y_ref, h1_ref, route_ref, g_ref, b_ref, h2_ref, h2b_ref,
                    ybuf, sem, *, tc, alpha):
    i = pl.program_id(0)
    slot = i % 2

    def row_copies(r1, r2, s, t):
        return (pltpu.make_async_copy(y_ref.at[pl.ds(r1[t], 1), :], ybuf.at[s, 0, pl.ds(t, 1), :], sem.at[s]),
                pltpu.make_async_copy(y_ref.at[pl.ds(r2[t], 1), :], ybuf.at[s, 1, pl.ds(t, 1), :], sem.at[s]))

    def wait_slot(s):
        for j in range(TOP_K):
            pltpu.make_async_copy(y_ref.at[pl.ds(0, tc), :], ybuf.at[s, j], sem.at[s]).wait()

    @pl.when(i == 0)
    def _():
        def issue(t, _):
            for j, c in enumerate(row_copies(d1_ref, d2_ref, 0, t)):
                c.start(priority=j)
            return 0
        lax.fori_loop(0, tc, issue, 0, unroll=8)

    wait_slot(slot)
    for t in range(tc):
        for j, c in enumerate(row_copies(d1n_ref, d2n_ref, 1 - slot, t)):
            c.start(priority=j)

    route = route_ref[...]
    g1 = route[:, 2:3]
    g2 = route[:, 3:4]
    lo1, hi1 = _unpack_halves(ybuf[slot, 0])
    lo2, hi2 = _unpack_halves(ybuf[slot, 1])
    ffn = jnp.concatenate([g1 * lo1 + g2 * lo2, g1 * hi1 + g2 * hi2], axis=1)
    h2 = _layer_norm(alpha * h1_ref[...] + ffn, g_ref[...], b_ref[...])
    h2_ref[...] = h2
    h2b_ref[...] = h2.astype(BF16)

    @pl.when(i == pl.num_programs(0) - 1)
    def _():
        wait_slot(1 - slot)


def _combine(y, d1, d2, h1, route, g, b, *, alpha, tc=512):
    T, D = h1.shape
    W = y.shape[1]
    n = T // tc
    idx = pl.BlockSpec((tc,), lambda i: (i,), memory_space=pltpu.SMEM)
    idx_next = pl.BlockSpec((tc,), lambda i: (jnp.minimum(i + 1, n - 1),), memory_space=pltpu.SMEM)
    row = pl.BlockSpec((tc, D), lambda i: (i, 0))
    vec = pl.BlockSpec((1, D), lambda i: (0, 0))
    return pl.pallas_call(
        functools.partial(_combine_kernel, tc=tc, alpha=alpha), grid=(n,),
        in_specs=[idx, idx, idx_next, idx_next, pl.BlockSpec(memory_space=pl.ANY), row,
                  pl.BlockSpec((tc, LANES), lambda i: (i, 0)), vec, vec],
        out_specs=[row, row],
        out_shape=[jax.ShapeDtypeStruct((T, D), F32), jax.ShapeDtypeStruct((T, D), BF16)],
        scratch_shapes=[pltpu.VMEM((2, TOP_K, tc, W), U32), pltpu.SemaphoreType.DMA((2,))],
        compiler_params=_cparams("arbitrary"), name="moe_combine",
    )(d1, d2, d1, d2, y, h1, route, g.reshape(1, D), b.reshape(1, D))


def kernel(x, emb_ln_g, emb_ln_b, w_in, short_conv_w, a_log, dt_bias, dn_norm_w, dw_conv_w, dw_conv_b,
           conv_ln_g, conv_ln_b, w_out, ln1_g, ln1_b, w_group, b_group, w_expert, b_expert, w_gate_up,
           w_down, ln2_g, ln2_b):
    B, L, D = x.shape
    T = B * L
    depth = w_in.shape[0]
    H = a_log.shape[2]
    dn_w = H * DN_HEAD_DIM
    cv_w = dw_conv_w.shape[2]
    n_ab = 4 * H
    alpha = (2 * depth) ** 0.25
    C = DN_CHUNK

    h, hb = _emb_ln(x.reshape(T, D), emb_ln_g, emb_ln_b)
    for l in range(depth):
        w = w_in[l]
        w_main = jnp.concatenate([w[:, :4 * dn_w], w[:, 4 * dn_w + n_ab:]], axis=1).astype(BF16)
        proj, gt = _in_proj(hb, w_main, w[:, 4 * dn_w:4 * dn_w + n_ab], a_log[l], dt_bias[l], B=B, L=L)
        gt = gt.reshape(B, n_ab, L // C, C)
        dn = _deltanet(proj, gt, short_conv_w[l], B=B, L=L, H=H)
        cv = _conformer(proj, dw_conv_w[l], dw_conv_b[l], conv_ln_g[l], conv_ln_b[l],
                        B=B, L=L, col0=4 * dn_w, width=cv_w)
        n_r = N_GROUPS + N_EXPERTS
        w_router = jnp.pad(jnp.concatenate([w_group[l], w_expert[l]], axis=1),
                           ((0, 0), (0, LANES - n_r))).astype(BF16)
        b_router = jnp.pad(jnp.concatenate([b_group[l], b_expert[l]]), (0, LANES - n_r)).reshape(1, LANES)
        h1, h1p, route, routet = _out_router(
            dn.reshape(T, dn_w), proj, 3 * H, dn_norm_w[l], cv.reshape(T, cv_w), h, w_out[l].astype(BF16),
            ln1_g[l], ln1_b[l],
            w_router, b_router, alpha=alpha)
        d1, d2, pend, steer, n_used, n_blocks = _placement(routet)
        xs = _dispatch(h1p, d1, d2, pend, n_blocks * FFN_BLOCK)
        y = _ffn(xs, w_gate_up, w_down, l, *steer, n_used, n_blocks)
        h, hb = _combine(y, d1, d2, h1, route, ln2_g[l], ln2_b[l], alpha=alpha)
    return h.reshape(B, L, D)
```

```python
import functools
import itertools

import jax
import jax.numpy as jnp
from jax import lax
from jax.experimental import pallas as pl
from jax.experimental.pallas import tpu as pltpu

F32 = jnp.float32
BF16 = jnp.bfloat16
U32 = jnp.uint32
I32 = jnp.int32

LANES = 128
DN_HEAD_DIM = 128
DN_CHUNK = 128
PREP_UNROLL = 8
CONV_SPREAD = 4
CV_GROUP = 128
N_GROUPS = 8
EXPERTS_PER_GROUP = 8
N_EXPERTS = N_GROUPS * EXPERTS_PER_GROUP
TOP_K = 2
FFN_BLOCK = 256
FFN_BLOCKS_PER_STEP = 4
FFN_WEIGHT_CHUNKS = 4
CAST_CHUNK_ELEMS = 32 * 1024
LN_EPS = 1e-5
RMS_EPS = 1e-6
VMEM_LIMIT = 56 * 1024 * 1024


def _cparams(*sem):
    return pltpu.CompilerParams(dimension_semantics=sem, vmem_limit_bytes=VMEM_LIMIT)


def _layer_norm(x, g, b):
    mu = jnp.mean(x, -1, keepdims=True)
    xc = x - mu
    var = jnp.mean(xc * xc, -1, keepdims=True)
    return xc * lax.rsqrt(var + LN_EPS) * g + b


def _sigmoid(x):
    return 1.0 / (1.0 + jnp.exp(-x))


def _silu(x):
    return x * _sigmoid(x)


def _pack_halves(y):
    n = y.shape[1] // 2
    bits = lax.bitcast_convert_type(y.astype(BF16).astype(F32), U32)
    return (bits[:, :n] >> 16) | bits[:, n:]


def _unpack_halves(u):
    lo = lax.bitcast_convert_type(u << 16, F32)
    hi = lax.bitcast_convert_type(u & jnp.uint32(0xFFFF0000), F32)
    return lo, hi


def _emb_ln_kernel(x_ref, g_ref, b_ref, h_ref, hb_ref):
    h = _layer_norm(x_ref[...], g_ref[...], b_ref[...])
    h_ref[...] = h
    hb_ref[...] = h.astype(BF16)


def _emb_ln(x, g, b, tm=512):
    T, D = x.shape
    row = pl.BlockSpec((tm, D), lambda i: (i, 0))
    vec = pl.BlockSpec((1, D), lambda i: (0, 0))
    return pl.pallas_call(
        _emb_ln_kernel, grid=(T // tm,), in_specs=[row, vec, vec], out_specs=[row, row],
        out_shape=[jax.ShapeDtypeStruct((T, D), F32), jax.ShapeDtypeStruct((T, D), BF16)],
        compiler_params=_cparams("parallel"), name="emb_ln")(x, g.reshape(1, D), b.reshape(1, D))


def _in_proj_kernel(x_ref, w_ref, wab_ref, alog_ref, dtb_ref, o_ref, gt_ref, *, n_decay, n_gate_rows):
    @pl.when(pl.program_id(1) == 0)
    def _():
        ab = jnp.dot(x_ref[...], wab_ref[...], preferred_element_type=F32)
        lane = lax.broadcasted_iota(I32, ab.shape, 1)
        sp = ab + dtb_ref[...]
        softplus = jnp.maximum(sp, 0.0) + jnp.log(1.0 + jnp.exp(-jnp.abs(sp)))
        decay = -jnp.exp(alog_ref[...]) * softplus
        gt_ref[0] = jnp.where(lane < n_decay, decay, _sigmoid(ab)).T[:n_gate_rows, :]

    acc = jnp.dot(x_ref[...], w_ref[...], preferred_element_type=F32)
    for j in range(o_ref.shape[0]):
        o_ref[j] = acc[:, j * LANES:(j + 1) * LANES]


def _in_proj(hb, w, w_ab, a_log, dt_bias, *, B, L, tm=1024, tn=2048):
    T, D = hb.shape
    N = w.shape[1]
    n = a_log.size
    n_gate_rows = 2 * n
    pad = lambda v: jnp.pad(v.reshape(1, n).astype(F32), ((0, 0), (0, LANES - n)))
    wab = jnp.pad(w_ab, ((0, 0), (0, LANES - w_ab.shape[1]))).astype(BF16)
    vec = pl.BlockSpec((1, LANES), lambda i, j: (0, 0))
    per_batch = L // tm
    return pl.pallas_call(
        functools.partial(_in_proj_kernel, n_decay=n, n_gate_rows=n_gate_rows), grid=(T // tm, N // tn),
        in_specs=[pl.BlockSpec((tm, D), lambda i, j: (i, 0)), pl.BlockSpec((D, tn), lambda i, j: (0, j)),
                  pl.BlockSpec((D, LANES), lambda i, j: (0, 0)), vec, vec],
        out_specs=[pl.BlockSpec((tn // LANES, tm, LANES), lambda i, j: (j, i, 0)),
                   pl.BlockSpec((1, n_gate_rows, tm), lambda i, j: (i // per_batch, 0, i % per_batch))],
        out_shape=[jax.ShapeDtypeStruct((N // LANES, T, LANES), F32),
                   jax.ShapeDtypeStruct((B, n_gate_rows, L), F32)],
        compiler_params=_cparams("parallel", "arbitrary"), name="in_proj",
    )(hb, w, wab, pad(a_log), pad(dt_bias))


def _dot(a, b):
    return jnp.dot(a.astype(BF16), b.astype(BF16), preferred_element_type=F32)


def _dot_nt(a, b):
    return lax.dot_general(a.astype(BF16), b.astype(BF16), (((1,), (1,)), ((), ())),
                           preferred_element_type=F32)


def _dot_tn(a, b):
    return lax.dot_general(a.astype(BF16), b.astype(BF16), (((0,), (0,)), ((), ())),
                           preferred_element_type=F32)


def _deltanet_kernel(q_ref, k_ref, v_ref, cq_ref, ck_ref, cv_ref, gt_ref, o_ref,
                     qs, ks, vs, ob, wq_s, at_s, kd_s, u_s, gl_s, s_s, rhs_s, *, L, C, H, R, n_items):
    DH = DN_HEAD_DIM
    s = pl.program_id(0)
    h = jnp.minimum(s, n_items - 1) % H
    slot_p = s % 2
    slot_s = 1 - slot_p
    NC = L // C
    U = PREP_UNROLL

    def conv_rows(r0, first=False, last=False):
        rows = pl.ds(r0, R)
        rid = lax.broadcasted_iota(I32, (R, DH), 0)

        def conv_silu(ref, w_ref):
            x = ref[0, rows, :]
            if first:
                x_prev = jnp.where(rid == 0, 0.0, pltpu.roll(x, 1, 0))
            else:
                x_prev = ref[0, pl.ds(r0 - 1, R), :]
            if last is False:
                x_next = ref[0, pl.ds(r0 + 1, R), :]
            else:
                at_end = jnp.where(rid == R - 1, 0.0, pltpu.roll(x, R - 1, 0))
                x_next = at_end if last is True else jnp.where(
                    last, at_end, ref[0, pl.ds(jnp.minimum(r0 + 1, L - R), R), :])
            return _silu(w_ref[0:1, :] * x_prev + w_ref[1:2, :] * x + w_ref[2:3, :] * x_next)

        q = conv_silu(q_ref, cq_ref)
        k = conv_silu(k_ref, ck_ref)
        qs[rows, :] = q * (lax.rsqrt(jnp.sum(q * q, -1, keepdims=True) + RMS_EPS) * (DH ** -0.5))
        ks[rows, :] = k * lax.rsqrt(jnp.sum(k * k, -1, keepdims=True) + RMS_EPS)
        vs[rows, :] = conv_silu(v_ref, cv_ref)

    n_trips = NC // U
    trip_blocks = (U * C) // R

    @pl.when(s < n_items)
    def _():
        for b in range(trip_blocks):
            conv_rows(b * R, first=b == 0, last=(b + 1) * R == L)

    def conv_pieces(i):
        nxt = jnp.minimum(i + 1, n_trips - 1)
        for b in range(trip_blocks):
            r0 = pl.multiple_of((nxt * trip_blocks + b) * R, R)
            conv_rows(r0, last=(r0 + R == L) if b == trip_blocks - 1 else False)
            for _ in range(CONV_SPREAD):
                yield

    ri = lax.broadcasted_iota(I32, (C, C), 0)
    ci = lax.broadcasted_iota(I32, (C, C), 1)
    eye = ri == ci
    eye_f = eye.astype(F32)
    incl = (ri >= ci, ri <= ci)
    strict = (ri > ci, ri < ci)
    n_sq = C.bit_length() - 2
    zero_b = jnp.zeros((C, C), BF16)

    def block_diag(a, b):
        return jnp.concatenate([jnp.concatenate([a, zero_b], axis=1),
                                jnp.concatenate([zero_b, b], axis=1)], axis=0)

    def prep_start(c, j):
        rows = pl.ds(pl.multiple_of(c * C, C), C)
        q = qs[rows, :]
        k = ks[rows, :]
        v = vs[rows, :]
        kq = _dot_nt(jnp.concatenate([k, q], axis=0), k)
        kk, qk = kq[:C], kq[C:]
        a2 = []
        for d in range(2):
            g_row = gt_ref[0, d * H + h, pl.ds(c, 1), :]
            b_row = gt_ref[0, (2 + d) * H + h, pl.ds(c, 1), :]
            m, ms, mt = incl[d], strict[d], incl[1 - d]
            g_col = jnp.sum(jnp.where(eye, g_row, 0.0), axis=1, keepdims=True)
            b_col = jnp.sum(jnp.where(eye, b_row, 0.0), axis=1, keepdims=True)
            gc_col = jnp.sum(jnp.where(m, g_row, 0.0), axis=1, keepdims=True)
            gc_row = jnp.sum(jnp.where(mt, g_col, 0.0), axis=0, keepdims=True)
            tot = jnp.sum(g_row, axis=1, keepdims=True)
            decay = jnp.where(m, jnp.exp(jnp.where(m, gc_col - gc_row, 0.0)), 0.0)
            a2.append(jnp.where(ms, -(kk * b_col * decay), 0.0))
            egc = jnp.exp(gc_col)
            rhs_s[d, j] = jnp.concatenate([v * b_col, k * (b_col * egc)], axis=1).astype(BF16)
            wq_s[slot_p, d, c, pl.ds(C, C), :] = (q * egc).astype(BF16)
            at_s[slot_p, d, c] = (qk * decay).astype(BF16)
            kd_s[slot_p, d, c] = (k * jnp.exp(tot - gc_col)).astype(BF16)
            gl_s[slot_p, d, pl.ds(c, 1), :] = jnp.broadcast_to(jnp.exp(tot), (1, DH))
        return jnp.concatenate(a2, axis=1)

    def prep_phases(i):
        cs = [i * U + j for j in range(U)]
        ps = []
        for j, c in enumerate(cs):
            ps.append(prep_start(c, j))
            if j == U // 2 - 1:
                yield
        eye2 = jnp.concatenate([eye_f, eye_f], axis=1)
        xs_ = [eye2 + p for p in ps]
        yield
        for _ in range(n_sq):
            pbs = [p.astype(BF16) for p in ps]
            ps = [jnp.dot(pb, block_diag(pb[:, :C], pb[:, C:]), preferred_element_type=F32) for pb in pbs]
            yield
            pbs = [p.astype(BF16) for p in ps]
            xs_ = [x + jnp.dot(x.astype(BF16), block_diag(pb[:, :C], pb[:, C:]), preferred_element_type=F32)
                   for x, pb in zip(xs_, pbs)]
            yield
        for j, (c, x) in enumerate(zip(cs, xs_)):
            for d in range(2):
                sol = jnp.dot(x[:, d * C:(d + 1) * C].astype(BF16), rhs_s[d, j], preferred_element_type=F32)
                u_s[slot_p, d, c] = sol[:, :DH]
                wq_s[slot_p, d, c, pl.ds(0, C), :] = sol[:, DH:].astype(BF16)
            if j == U // 2 - 1:
                yield
        yield

    def scan_stages(n):
        cf, cb = n, NC - 1 - n
        s_f, s_b = s_s[0], s_s[1]
        wq = jnp.concatenate([wq_s[slot_s, 0, cf], wq_s[slot_s, 1, cb]], axis=1)
        r1 = jnp.dot(wq, block_diag(s_f.astype(BF16), s_b.astype(BF16)), preferred_element_type=F32)
        yield
        u = jnp.concatenate([u_s[slot_s, 0, cf], u_s[slot_s, 1, cb]], axis=1)
        v_new = (u - r1[:C]).astype(BF16)
        vd = block_diag(v_new[:, :DH], v_new[:, DH:])
        at = jnp.concatenate([at_s[slot_s, 0, cf], at_s[slot_s, 1, cb]], axis=1)
        o = r1[C:] + jnp.dot(at, vd, preferred_element_type=F32)
        kd = jnp.concatenate([kd_s[slot_s, 0, cf], kd_s[slot_s, 1, cb]], axis=0)
        ds = lax.dot_general(kd, vd, (((0,), (0,)), ((), ())), preferred_element_type=F32)
        s_s[0] = s_f * gl_s[slot_s, 0, pl.ds(cf, 1), :] + ds[:, :DH]
        s_s[1] = s_b * gl_s[slot_s, 1, pl.ds(cb, 1), :] + ds[:, DH:]
        o_ref[0, pl.ds(pl.multiple_of(cf * C, C), C), :] = o[:, :DH]
        ob[pl.ds(pl.multiple_of(cb * C, C), C), :] = o[:, DH:]
        yield

    def run(prepare, scan):
        def trip(i, _):
            prep_it = prep_phases(i) if prepare else iter(())
            conv_it = conv_pieces(i) if prepare and n_trips > 1 else iter(())
            scan_it = itertools.chain.from_iterable(scan_stages(i * U + j) for j in range(U)) if scan else iter(())
            for _ in itertools.zip_longest(scan_it, prep_it, conv_it):
                pass
            return 0
        if scan:
            s_s[...] = jnp.zeros_like(s_s)
        lax.fori_loop(0, n_trips, trip, 0)

    @pl.when(s == 0)
    def _():
        run(True, False)

    @pl.when((s > 0) & (s < n_items))
    def _():
        run(True, True)

    @pl.when(s == n_items)
    def _():
        run(False, True)

    def sum_rows(i, _):
        rows = pl.ds(pl.multiple_of(i * R, R), R)
        o_ref[0, rows, :] = o_ref[0, rows, :] + ob[rows, :]
        return 0

    @pl.when(s > 0)
    def _():
        lax.fori_loop(0, L // R, sum_rows, 0)


def _deltanet(proj, gt, conv_w, *, B, L, H):
    DH, C, R = DN_HEAD_DIM, DN_CHUNK, 256
    assert C == DH and L % (C * PREP_UNROLL) == 0
    NC = L // C
    n_items = B * H
    cur = lambda s: jnp.minimum(s, n_items - 1)
    prev = lambda s: jnp.maximum(s - 1, 0)
    col = lambda off: pl.BlockSpec((1, L, DH), lambda s: (off + cur(s) % H, cur(s) // H, 0))
    cw = lambda off: pl.BlockSpec((3, DH), lambda s: (0, off + cur(s) % H))
    kern = functools.partial(_deltanet_kernel, L=L, C=C, H=H, R=R, n_items=n_items)
    return pl.pallas_call(
        kern, grid=(n_items + 1,),
        in_specs=[col(0), col(H), col(2 * H), cw(0), cw(H), cw(2 * H),
                  pl.BlockSpec((1, 4 * H, NC, C), lambda s: (cur(s) // H, 0, 0, 0))],
        out_specs=pl.BlockSpec((1, L, DH), lambda s: (prev(s) // H, 0, prev(s) % H)),
        out_shape=jax.ShapeDtypeStruct((B, L, H * DH), F32),
        scratch_shapes=[pltpu.VMEM((L, DH), F32), pltpu.VMEM((L, DH), F32), pltpu.VMEM((L, DH), F32),
                        pltpu.VMEM((L, DH), F32),
                        pltpu.VMEM((2, 2, NC, 2 * C, DH), BF16), pltpu.VMEM((2, 2, NC, C, C), BF16),
                        pltpu.VMEM((2, 2, NC, C, DH), BF16), pltpu.VMEM((2, 2, NC, C, DH), F32),
                        pltpu.VMEM((2, 2, NC, DH), F32), pltpu.VMEM((2, DH, DH), F32),
                        pltpu.VMEM((2, PREP_UNROLL, C, 2 * DH), BF16)],
        compiler_params=_cparams("arbitrary"), name="deltanet",
    )(proj, proj, proj, conv_w, conv_w, conv_w, gt)


def _conformer_kernel(val_ref, gate_ref, w_ref, b_ref, g_ref, beta_ref, o_ref, ypad, *, L, K, R, HALO):
    pad = K // 2
    zeros = jnp.zeros((HALO, CV_GROUP), F32)
    ypad[pl.ds(0, HALO), :] = zeros
    ypad[pl.ds(HALO + L, HALO), :] = zeros

    def glu_rows(i, _):
        r0 = pl.multiple_of(i * R, R)
        ypad[pl.ds(HALO + r0, R), :] = val_ref[0, pl.ds(r0, R), :] * _sigmoid(gate_ref[0, pl.ds(r0, R), :])
        return 0

    lax.fori_loop(0, L // R, glu_rows, 0)

    def conv_rows(i, _):
        r0 = pl.multiple_of(i * R, R)
        acc = jnp.zeros((R, CV_GROUP), F32)
        for t in range(K):
            acc = acc + w_ref[t:t + 1, :] * ypad[pl.ds(r0 + (HALO - pad + t), R), :]
        y = _layer_norm(acc + b_ref[...], g_ref[...], beta_ref[...])
        o_ref[0, pl.ds(r0, R), :] = _silu(y)
        return 0

    lax.fori_loop(0, L // R, conv_rows, 0)


def _conformer(proj, w, b, g, beta, *, B, L, col0, width):
    K = w.shape[0]
    n = width // CV_GROUP
    HALO, R = 16, 256
    c0 = col0 // CV_GROUP
    vec = pl.BlockSpec((1, CV_GROUP), lambda bb, j: (0, j))
    kern = functools.partial(_conformer_kernel, L=L, K=K, R=R, HALO=HALO)
    return pl.pallas_call(
        kern, grid=(B, n),
        in_specs=[pl.BlockSpec((1, L, CV_GROUP), lambda bb, j: (c0 + j, bb, 0)),
                  pl.BlockSpec((1, L, CV_GROUP), lambda bb, j: (c0 + n + j, bb, 0)),
                  pl.BlockSpec((K, CV_GROUP), lambda bb, j: (0, j)), vec, vec, vec],
        out_specs=pl.BlockSpec((1, L, CV_GROUP), lambda bb, j: (bb, 0, j)),
        out_shape=jax.ShapeDtypeStruct((B, L, width), F32),
        scratch_shapes=[pltpu.VMEM((L + 2 * HALO, CV_GROUP), F32)],
        compiler_params=_cparams("parallel", "parallel"), name="conformer",
    )(proj, proj, w, b.reshape(1, width), g.reshape(1, width), beta.reshape(1, width))


def _out_router_kernel(dn_ref, z_ref, nw_ref, cv_ref, h_ref, w_ref, g_ref, b_ref, wr_ref, br_ref,
                       h1_ref, h1p_ref, route_ref, routet_ref, *, alpha, n_dn, sub):
    n_sub = h_ref.shape[0] // sub

    def stages(s):
        rows = pl.ds(s * sub, sub)
        heads = []
        for hh in range(n_dn // DN_HEAD_DIM):
            o = dn_ref[rows, hh * DN_HEAD_DIM:(hh + 1) * DN_HEAD_DIM]
            o = o * lax.rsqrt(jnp.mean(o * o, -1, keepdims=True) + RMS_EPS) * nw_ref[...]
            heads.append((o * _silu(z_ref[hh, rows, :])).astype(BF16))
        mix = jnp.dot(jnp.concatenate(heads, axis=1), w_ref[pl.ds(0, n_dn), :], preferred_element_type=F32)
        mix = mix + jnp.dot(cv_ref[rows, :].astype(BF16), w_ref[pl.ds(n_dn, w_ref.shape[0] - n_dn), :],
                            preferred_element_type=F32)
        yield
        h1 = _layer_norm(alpha * h_ref[rows, :] + mix, g_ref[...], b_ref[...])
        h1_ref[rows, :] = h1
        h1p_ref[rows, :] = _pack_halves(h1)
        h1b = h1.astype(BF16)
        yield
        logits = jnp.dot(h1b, wr_ref[...], preferred_element_type=F32) + br_ref[...]
        yield
        route = _route(logits)
        route_ref[rows, :] = route
        routet_ref[:, rows] = route.T[:8, :]
        yield

    n_stage = 4
    pipes = [stages(s) for s in range(n_sub)]
    for t in range(n_sub + n_stage - 1):
        for s in range(t, t - n_stage, -1):
            if 0 <= s < n_sub:
                next(pipes[s])


def _route(logits):
    lane = lax.broadcasted_iota(I32, logits.shape, 1)
    neg = jnp.float32(-jnp.inf)
    big = jnp.int32(LANES)
    gl = jnp.where(lane < N_GROUPS, logits, neg)
    gmax = jnp.max(gl, -1, keepdims=True)
    gsel = jnp.min(jnp.where(gl == gmax, lane, big), -1, keepdims=True)
    pg = 1.0 / jnp.sum(jnp.exp(gl - gmax), -1, keepdims=True)
    lo = N_GROUPS + gsel * EXPERTS_PER_GROUP
    el = jnp.where((lane >= lo) & (lane < lo + EXPERTS_PER_GROUP), logits, neg)
    e1 = jnp.max(el, -1, keepdims=True)
    i1 = jnp.min(jnp.where(el == e1, lane, big), -1, keepdims=True)
    el2 = jnp.where(lane == i1, neg, el)
    e2 = jnp.max(el2, -1, keepdims=True)
    i2 = jnp.min(jnp.where(el2 == e2, lane, big), -1, keepdims=True)
    r = jnp.exp(e2 - e1)
    p1 = 1.0 / (1.0 + r)
    p2 = r * p1
    route = jnp.where(lane == 0, (i1 - N_GROUPS).astype(F32),
                      jnp.where(lane == 1, (i2 - N_GROUPS).astype(F32),
                                jnp.where(lane == 2, pg * p1, jnp.where(lane == 3, pg * p2, 0.0))))
    return route


def _out_router(dn, proj, z_group, norm_w, cv, h, w_out, g, b, w_router, b_router, *, alpha, tm=512, sub=128):
    T, D = h.shape
    n_dn, n_cv = dn.shape[1], cv.shape[1]
    n_heads = n_dn // DN_HEAD_DIM
    assert z_group % n_heads == 0
    vec = pl.BlockSpec((1, D), lambda i: (0, 0))
    row = pl.BlockSpec((tm, D), lambda i: (i, 0))
    kern = functools.partial(_out_router_kernel, alpha=alpha, n_dn=n_dn, sub=sub)
    return pl.pallas_call(
        kern, grid=(T // tm,),
        in_specs=[pl.BlockSpec((tm, n_dn), lambda i: (i, 0)),
                  pl.BlockSpec((n_heads, tm, DN_HEAD_DIM), lambda i: (z_group // n_heads, i, 0)),
                  pl.BlockSpec((1, DN_HEAD_DIM), lambda i: (0, 0)),
                  pl.BlockSpec((tm, n_cv), lambda i: (i, 0)), row,
                  pl.BlockSpec((n_dn + n_cv, D), lambda i: (0, 0)), vec, vec,
                  pl.BlockSpec((D, LANES), lambda i: (0, 0)), pl.BlockSpec((1, LANES), lambda i: (0, 0))],
        out_specs=[row, pl.BlockSpec((tm, D // 2), lambda i: (i, 0)),
                   pl.BlockSpec((tm, LANES), lambda i: (i, 0)), pl.BlockSpec((8, tm), lambda i: (0, i))],
        out_shape=[jax.ShapeDtypeStruct((T, D), F32), jax.ShapeDtypeStruct((T, D // 2), U32),
                   jax.ShapeDtypeStruct((T, LANES), F32), jax.ShapeDtypeStruct((8, T), F32)],
        compiler_params=_cparams("parallel"), name="out_router",
    )(dn, proj, norm_w.reshape(1, DN_HEAD_DIM), cv, h, w_out, g.reshape(1, D), b.reshape(1, D), w_router, b_router)


def _onehot_t(routet_ref, tr):
    sub = lax.broadcasted_iota(I32, (2 * N_EXPERTS, tr), 0)
    e1 = routet_ref[0:1, :].astype(I32)
    e2 = routet_ref[1:2, :].astype(I32)
    return sub == jnp.where(sub < N_EXPERTS, e1, e2 + N_EXPERTS)


def _count_kernel(routet_ref, cnt_ref, *, tr):
    @pl.when(pl.program_id(0) == 0)
    def _():
        cnt_ref[...] = jnp.zeros_like(cnt_ref)
    oh = _onehot_t(routet_ref, tr).astype(F32)
    cnt_ref[...] += jnp.sum(oh, axis=1, keepdims=True)


def _place_kernel(routet_ref, base_ref, d1_ref, d2_ref, carry, *, tr):
    @pl.when(pl.program_id(0) == 0)
    def _():
        carry[...] = jnp.zeros_like(carry)
    oh = _onehot_t(routet_ref, tr)
    ohb = oh.astype(F32).astype(BF16)
    ri = lax.broadcasted_iota(I32, (tr, tr), 0)
    ci = lax.broadcasted_iota(I32, (tr, tr), 1)
    before = (ri < ci).astype(F32).astype(BF16)
    excl = jnp.dot(ohb, before, preferred_element_type=F32)
    pos = jnp.where(oh, excl + carry[...] + base_ref[...], 0.0)
    d1 = jnp.sum(pos[:N_EXPERTS], axis=0, keepdims=True)
    d2 = jnp.sum(pos[N_EXPERTS:], axis=0, keepdims=True)
    d1_ref[...] = d1.astype(I32)
    d2_ref[...] = d2.astype(I32)
    carry[...] += jnp.sum(oh.astype(F32), axis=1, keepdims=True)


def _placement(routet, tr=512):
    T = routet.shape[1]
    E = N_EXPERTS
    rt = pl.BlockSpec((8, tr), lambda i: (0, i))
    col = pl.BlockSpec((2 * E, 1), lambda i: (0, 0))
    cnt = pl.pallas_call(
        functools.partial(_count_kernel, tr=tr), grid=(T // tr,), in_specs=[rt], out_specs=col,
        out_shape=jax.ShapeDtypeStruct((2 * E, 1), F32),
        compiler_params=_cparams("arbitrary"), name="moe_count")(routet)
    c1 = cnt[:E, 0].astype(I32)
    c2 = cnt[E:, 0].astype(I32)
    padded = ((c1 + c2 + FFN_BLOCK - 1) // FFN_BLOCK) * FFN_BLOCK
    pend = jnp.cumsum(padded)
    pstart = pend - padded
    base = jnp.concatenate([pstart, pstart + c1]).astype(F32).reshape(2 * E, 1)
    lane_row = pl.BlockSpec((1, tr), lambda i: (0, i))
    d1, d2 = pl.pallas_call(
        functools.partial(_place_kernel, tr=tr), grid=(T // tr,), in_specs=[rt, col], out_specs=[lane_row, lane_row],
        out_shape=[jax.ShapeDtypeStruct((1, T), I32)] * 2,
        scratch_shapes=[pltpu.VMEM((2 * E, 1), F32)],
        compiler_params=_cparams("arbitrary"), name="moe_place")(routet, base)
    n_blocks = (T * TOP_K + E * (FFN_BLOCK - 1) + FFN_BLOCK - 1) // FFN_BLOCK
    starts = jnp.arange(n_blocks, dtype=I32) * FFN_BLOCK
    blk_e = jnp.minimum(jnp.sum((pend[None, :] <= starts[:, None]).astype(I32), axis=1), E - 1)
    blk_first = jnp.concatenate([jnp.ones((1,), I32), (blk_e[1:] != blk_e[:-1]).astype(I32)])
    n_used = (pend[-1] // FFN_BLOCK).astype(I32).reshape(1)
    ids = jnp.arange(E, dtype=I32)
    owns = padded > 0
    later = owns[None, :] & (ids[None, :] > ids[:, None])
    next_tab = jnp.min(jnp.where(later, ids[None, :], E), axis=1)
    next_tab = jnp.concatenate([next_tab, jnp.full((1,), E, I32)])
    next2_tab = next_tab[next_tab[:E]]
    hide = lambda t: jnp.where(t == E, -1, t)
    slot_tab = (jnp.cumsum(owns.astype(I32)) - 1) % 2
    steer = (blk_e, blk_first, slot_tab[blk_e], hide(next_tab[:E])[blk_e], hide(next2_tab)[blk_e])
    return d1.reshape(T), d2.reshape(T), pend.astype(I32), steer, n_used, n_blocks


def _dispatch_kernel(d1_ref, d2_ref, pend_ref, src_ref, xs_ref, zbuf, sem, zsem, *, tb):
    @pl.when(pl.program_id(0) == 0)
    def _():
        zbuf[...] = jnp.zeros_like(zbuf)
        tails = []
        for e in range(N_EXPERTS):
            end = pend_ref[e]
            nonempty = end > (pend_ref[e - 1] if e else 0)
            start = pl.multiple_of(jnp.maximum(end - FFN_BLOCK, 0), FFN_BLOCK)
            tails.append((nonempty, pltpu.make_async_copy(zbuf, xs_ref.at[pl.ds(start, FFN_BLOCK), :], zsem)))
        for nonempty, copy in tails:
            pl.when(nonempty)(copy.start)

        def spare(b):
            return pltpu.make_async_copy(
                zbuf, xs_ref.at[pl.ds(pl.multiple_of(b * FFN_BLOCK, FFN_BLOCK), FFN_BLOCK), :], zsem)
        first_spare = pend_ref[N_EXPERTS - 1] // FFN_BLOCK
        n_blocks = xs_ref.shape[0] // FFN_BLOCK
        lax.fori_loop(first_spare, n_blocks, lambda b, c: (spare(b).start(), c)[1], 0)
        for nonempty, copy in tails:
            pl.when(nonempty)(copy.wait)
        lax.fori_loop(first_spare, n_blocks, lambda b, c: (spare(b).wait(), c)[1], 0)

    def issue(t, _):
        row = src_ref.at[pl.ds(t, 1), :]
        pltpu.make_async_copy(row, xs_ref.at[pl.ds(d1_ref[t], 1), :], sem).start(priority=0)
        pltpu.make_async_copy(row, xs_ref.at[pl.ds(d2_ref[t], 1), :], sem).start(priority=1)
        return 0

    lax.fori_loop(0, tb, issue, 0, unroll=8)
    pltpu.make_async_copy(src_ref, xs_ref.at[pl.ds(0, tb), :], sem).wait()
    pltpu.make_async_copy(src_ref, xs_ref.at[pl.ds(0, tb), :], sem).wait()


def _dispatch(h1p, d1, d2, pend, n_rows, tb=1024):
    T, W = h1p.shape
    idx = pl.BlockSpec((tb,), lambda i: (i,), memory_space=pltpu.SMEM)
    return pl.pallas_call(
        functools.partial(_dispatch_kernel, tb=tb), grid=(T // tb,),
        in_specs=[idx, idx, pl.BlockSpec((N_EXPERTS,), lambda i: (0,), memory_space=pltpu.SMEM),
                  pl.BlockSpec((tb, W), lambda i: (i, 0))],
        out_specs=pl.BlockSpec(memory_space=pl.ANY),
        out_shape=jax.ShapeDtypeStruct((n_rows, W), U32),
        scratch_shapes=[pltpu.VMEM((FFN_BLOCK, W), U32), pltpu.SemaphoreType.DMA(()), pltpu.SemaphoreType.DMA(())],
        compiler_params=_cparams("arbitrary"), name="moe_dispatch",
    )(d1, d2, pend, h1p)


def _cast_rows(src, dst):
    rows = CAST_CHUNK_ELEMS // src.shape[1]

    def body(i, c):
        r = pl.ds(pl.multiple_of(i * rows, rows), rows)
        dst[r, :] = src[r, :].astype(BF16)
        return c

    lax.fori_loop(0, src.shape[0] // rows, body, 0, unroll=4)


def _ffn_kernel(blk_e_ref, blk_first_ref, blk_slot_ref, blk_next_ref, blk_next2_ref, n_used_ref,
                xs_ref, wgu_hbm, wdn_hbm, y_ref, wgu_f, wdn_f, wgu_s, wdn_s, sem, *, ff, layer):

    def weight_copies(e, slot):
        copies = []
        for k, (src, dst) in enumerate(((wgu_hbm, wgu_f), (wdn_hbm, wdn_f))):
            rows = dst.shape[1] // FFN_WEIGHT_CHUNKS
            for c in range(FFN_WEIGHT_CHUNKS):
                r = pl.ds(c * rows, rows)
                copies.append(pltpu.make_async_copy(src.at[layer, e, r, :], dst.at[slot, r, :],
                                                    sem.at[slot, k * FFN_WEIGHT_CHUNKS + c]))
        return copies

    def start(e, slot):
        for c in weight_copies(e, slot):
            c.start(priority=1)

    @pl.when(pl.program_id(0) == 0)
    def _():
        start(blk_e_ref[0], 0)

        @pl.when(blk_next_ref[0] >= 0)
        def _():
            start(blk_next_ref[0], 1)

    for j in range(FFN_BLOCKS_PER_STEP):
        i = pl.program_id(0) * FFN_BLOCKS_PER_STEP + j
        rows = pl.ds(j * FFN_BLOCK, FFN_BLOCK)
        used = i < n_used_ref[0]

        @pl.when(used & (blk_first_ref[i] == 1))
        def _():
            slot = blk_slot_ref[i]
            for c in weight_copies(blk_e_ref[i], slot):
                c.wait()
            _cast_rows(wgu_f.at[slot], wgu_s)
            _cast_rows(wdn_f.at[slot], wdn_s)

            @pl.when(blk_next2_ref[i] >= 0)
            def _():
                start(blk_next2_ref[i], slot)

        @pl.when(used)
        def _():
            lo, hi = _unpack_halves(xs_ref[rows, :])
            x = jnp.concatenate([lo, hi], axis=1).astype(BF16)
            gu = jnp.dot(x, wgu_s[...], preferred_element_type=F32)
            act = (_silu(gu[:, :ff]) * gu[:, ff:]).astype(BF16)
            y_ref[rows, :] = _pack_halves(jnp.dot(act, wdn_s[...], preferred_element_type=F32))

        @pl.when(jnp.logical_not(used))
        def _():
            y_ref[rows, :] = jnp.zeros((FFN_BLOCK, y_ref.shape[1]), y_ref.dtype)


def _ffn(xs, w_gu, w_dn, layer, blk_e, blk_first, blk_slot, blk_next, blk_next2, n_used, n_blocks):
    _, E, D, FF2 = w_gu.shape
    ff = FF2 // 2
    W = xs.shape[1]
    assert n_blocks % FFN_BLOCKS_PER_STEP == 0
    step_rows = FFN_BLOCK * FFN_BLOCKS_PER_STEP
    any_spec = pl.BlockSpec(memory_space=pl.ANY)
    last_used_step = lambda nu: (nu[0] - 1) // FFN_BLOCKS_PER_STEP
    grid_spec = pltpu.PrefetchScalarGridSpec(
        num_scalar_prefetch=6, grid=(n_blocks // FFN_BLOCKS_PER_STEP,),
        in_specs=[pl.BlockSpec((step_rows, W), lambda i, be, bf, bs, bn, bn2, nu: (jnp.minimum(i, last_used_step(nu)), 0)),
                  any_spec, any_spec],
        out_specs=pl.BlockSpec((step_rows, W), lambda i, *_: (i, 0)),
        scratch_shapes=[pltpu.VMEM((2, D, FF2), F32), pltpu.VMEM((2, ff, D), F32),
                        pltpu.VMEM((D, FF2), BF16), pltpu.VMEM((ff, D), BF16),
                        pltpu.SemaphoreType.DMA((2, 2 * FFN_WEIGHT_CHUNKS))])
    return pl.pallas_call(
        functools.partial(_ffn_kernel, ff=ff, layer=layer), grid_spec=grid_spec,
        out_shape=jax.ShapeDtypeStruct(xs.shape, U32),
        compiler_params=_cparams("arbitrary"), name="moe_ffn",
    )(blk_e, blk_first, blk_slot, blk_next, blk_next2, n_used, xs, w_gu, w_dn)


def _combine_kernel(d1_ref, d2_ref, d1n_ref, d2n_ref, y_ref, h1_ref, route_ref, g_ref, b_ref, h2_ref, h2b_ref,
                    ybuf, sem, *, tc, alpha):
    i = pl.program_id(0)
    slot = i % 2

    def row_copies(r1, r2, s, t):
        return (pltpu.make_async_copy(y_ref.at[pl.ds(r1[t], 1), :], ybuf.at[s, 0, pl.ds(t, 1), :], sem.at[s]),
                pltpu.make_async_copy(y_ref.at[pl.ds(r2[t], 1), :], ybuf.at[s, 1, pl.ds(t, 1), :], sem.at[s]))

    def wait_slot(s):
        for j in range(TOP_K):
            pltpu.make_async_copy(y_ref.at[pl.ds(0, tc), :], ybuf.at[s, j], sem.at[s]).wait()

    @pl.when(i == 0)
    def _():
        def issue(t, _):
            for j, c in enumerate(row_copies(d1_ref, d2_ref, 0, t)):
                c.start(priority=j)
            return 0
        lax.fori_loop(0, tc, issue, 0, unroll=8)

    wait_slot(slot)
    for t in range(tc):
        for j, c in enumerate(row_copies(d1n_ref, d2n_ref, 1 - slot, t)):
            c.start(priority=j)

    route = route_ref[...]
    g1 = route[:, 2:3]
    g2 = route[:, 3:4]
    lo1, hi1 = _unpack_halves(ybuf[slot, 0])
    lo2, hi2 = _unpack_halves(ybuf[slot, 1])
    ffn = jnp.concatenate([g1 * lo1 + g2 * lo2, g1 * hi1 + g2 * hi2], axis=1)
    h2 = _layer_norm(alpha * h1_ref[...] + ffn, g_ref[...], b_ref[...])
    h2_ref[...] = h2
    h2b_ref[...] = h2.astype(BF16)

    @pl.when(i == pl.num_programs(0) - 1)
    def _():
        wait_slot(1 - slot)


def _combine(y, d1, d2, h1, route, g, b, *, alpha, tc=256):
    T, D = h1.shape
    W = y.shape[1]
    n = T // tc
    idx = pl.BlockSpec((tc,), lambda i: (i,), memory_space=pltpu.SMEM)
    idx_next = pl.BlockSpec((tc,), lambda i: (jnp.minimum(i + 1, n - 1),), memory_space=pltpu.SMEM)
    row = pl.BlockSpec((tc, D), lambda i: (i, 0))
    vec = pl.BlockSpec((1, D), lambda i: (0, 0))
    return pl.pallas_call(
        functools.partial(_combine_kernel, tc=tc, alpha=alpha), grid=(n,),
        in_specs=[idx, idx, idx_next, idx_next, pl.BlockSpec(memory_space=pl.ANY), row,
                  pl.BlockSpec((tc, LANES), lambda i: (i, 0)), vec, vec],
        out_specs=[row, row],
        out_shape=[jax.ShapeDtypeStruct((T, D), F32), jax.ShapeDtypeStruct((T, D), BF16)],
        scratch_shapes=[pltpu.VMEM((2, TOP_K, tc, W), U32), pltpu.SemaphoreType.DMA((2,))],
        compiler_params=_cparams("arbitrary"), name="moe_combine",
    )(d1, d2, d1, d2, y, h1, route, g.reshape(1, D), b.reshape(1, D))


def kernel(x, emb_ln_g, emb_ln_b, w_in, short_conv_w, a_log, dt_bias, dn_norm_w, dw_conv_w, dw_conv_b,
           conv_ln_g, conv_ln_b, w_out, ln1_g, ln1_b, w_group, b_group, w_expert, b_expert, w_gate_up,
           w_down, ln2_g, ln2_b):
    B, L, D = x.shape
    T = B * L
    depth = w_in.shape[0]
    H = a_log.shape[2]
    dn_w = H * DN_HEAD_DIM
    cv_w = dw_conv_w.shape[2]
    n_ab = 4 * H
    alpha = (2 * depth) ** 0.25
    C = DN_CHUNK

    h, hb = _emb_ln(x.reshape(T, D), emb_ln_g, emb_ln_b)
    for l in range(depth):
        w = w_in[l]
        w_main = jnp.concatenate([w[:, :4 * dn_w], w[:, 4 * dn_w + n_ab:]], axis=1).astype(BF16)
        proj, gt = _in_proj(hb, w_main, w[:, 4 * dn_w:4 * dn_w + n_ab], a_log[l], dt_bias[l], B=B, L=L)
        gt = gt.reshape(B, n_ab, L // C, C)
        dn = _deltanet(proj, gt, short_conv_w[l], B=B, L=L, H=H)
        cv = _conformer(proj, dw_conv_w[l], dw_conv_b[l], conv_ln_g[l], conv_ln_b[l],
                        B=B, L=L, col0=4 * dn_w, width=cv_w)
        n_r = N_GROUPS + N_EXPERTS
        w_router = jnp.pad(jnp.concatenate([w_group[l], w_expert[l]], axis=1),
                           ((0, 0), (0, LANES - n_r))).astype(BF16)
        b_router = jnp.pad(jnp.concatenate([b_group[l], b_expert[l]]), (0, LANES - n_r)).reshape(1, LANES)
        h1, h1p, route, routet = _out_router(
            dn.reshape(T, dn_w), proj, 3 * H, dn_norm_w[l], cv.reshape(T, cv_w), h, w_out[l].astype(BF16),
            ln1_g[l], ln1_b[l],
            w_router, b_router, alpha=alpha)
        d1, d2, pend, steer, n_used, n_blocks = _placement(routet)
        xs = _dispatch(h1p, d1, d2, pend, n_blocks * FFN_BLOCK)
        y = _ffn(xs, w_gate_up, w_down, l, *steer, n_used, n_blocks)
        h, hb = _combine(y, d1, d2, h1, route, ln2_g[l], ln2_b[l], alpha=alpha)
    return h.reshape(B, L, D)
```

```python
import functools
import itertools

import jax
import jax.numpy as jnp
from jax import lax
from jax.experimental import pallas as pl
from jax.experimental.pallas import tpu as pltpu

F32 = jnp.float32
BF16 = jnp.bfloat16
U32 = jnp.uint32
I32 = jnp.int32

LANES = 128
DN_HEAD_DIM = 128
DN_CHUNK = 128
PREP_UNROLL = 8
CONV_SPREAD = 4
CV_GROUP = 128
N_GROUPS = 8
EXPERTS_PER_GROUP = 8
N_EXPERTS = N_GROUPS * EXPERTS_PER_GROUP
TOP_K = 2
FFN_BLOCK = 256
FFN_BLOCKS_PER_STEP = 4
FFN_WEIGHT_CHUNKS = 4
CAST_CHUNK_ELEMS = 32 * 1024
LN_EPS = 1e-5
RMS_EPS = 1e-6
VMEM_LIMIT = 56 * 1024 * 1024


def _cparams(*sem):
    return pltpu.CompilerParams(dimension_semantics=sem, vmem_limit_bytes=VMEM_LIMIT)


def _layer_norm(x, g, b):
    mu = jnp.mean(x, -1, keepdims=True)
    xc = x - mu
    var = jnp.mean(xc * xc, -1, keepdims=True)
    return xc * lax.rsqrt(var + LN_EPS) * g + b


def _sigmoid(x):
    return 1.0 / (1.0 + jnp.exp(-x))


def _silu(x):
    return x * _sigmoid(x)


def _pack_halves(y):
    n = y.shape[1] // 2
    bits = lax.bitcast_convert_type(y.astype(BF16).astype(F32), U32)
    return (bits[:, :n] >> 16) | bits[:, n:]


def _unpack_halves(u):
    lo = lax.bitcast_convert_type(u << 16, F32)
    hi = lax.bitcast_convert_type(u & jnp.uint32(0xFFFF0000), F32)
    return lo, hi


def _emb_ln_kernel(x_ref, g_ref, b_ref, h_ref, hb_ref):
    h = _layer_norm(x_ref[...], g_ref[...], b_ref[...])
    h_ref[...] = h
    hb_ref[...] = h.astype(BF16)


def _emb_ln(x, g, b, tm=1024):
    T, D = x.shape
    row = pl.BlockSpec((tm, D), lambda i: (i, 0))
    vec = pl.BlockSpec((1, D), lambda i: (0, 0))
    return pl.pallas_call(
        _emb_ln_kernel, grid=(T // tm,), in_specs=[row, vec, vec], out_specs=[row, row],
        out_shape=[jax.ShapeDtypeStruct((T, D), F32), jax.ShapeDtypeStruct((T, D), BF16)],
        compiler_params=_cparams("parallel"), name="emb_ln")(x, g.reshape(1, D), b.reshape(1, D))


def _in_proj_kernel(x_ref, w_ref, wab_ref, alog_ref, dtb_ref, o_ref, gt_ref, *, n_decay, n_gate_rows):
    @pl.when(pl.program_id(1) == 0)
    def _():
        ab = jnp.dot(x_ref[...], wab_ref[...], preferred_element_type=F32)
        lane = lax.broadcasted_iota(I32, ab.shape, 1)
        sp = ab + dtb_ref[...]
        softplus = jnp.maximum(sp, 0.0) + jnp.log(1.0 + jnp.exp(-jnp.abs(sp)))
        decay = -jnp.exp(alog_ref[...]) * softplus
        gt_ref[0] = jnp.where(lane < n_decay, decay, _sigmoid(ab)).T[:n_gate_rows, :]

    acc = jnp.dot(x_ref[...], w_ref[...], preferred_element_type=F32)
    for j in range(o_ref.shape[0]):
        o_ref[j] = acc[:, j * LANES:(j + 1) * LANES]


def _in_proj(hb, w, w_ab, a_log, dt_bias, *, B, L, tm=1024, tn=2048):
    T, D = hb.shape
    N = w.shape[1]
    n = a_log.size
    n_gate_rows = 2 * n
    pad = lambda v: jnp.pad(v.reshape(1, n).astype(F32), ((0, 0), (0, LANES - n)))
    wab = jnp.pad(w_ab, ((0, 0), (0, LANES - w_ab.shape[1]))).astype(BF16)
    vec = pl.BlockSpec((1, LANES), lambda i, j: (0, 0))
    per_batch = L // tm
    return pl.pallas_call(
        functools.partial(_in_proj_kernel, n_decay=n, n_gate_rows=n_gate_rows), grid=(T // tm, N // tn),
        in_specs=[pl.BlockSpec((tm, D), lambda i, j: (i, 0)), pl.BlockSpec((D, tn), lambda i, j: (0, j)),
                  pl.BlockSpec((D, LANES), lambda i, j: (0, 0)), vec, vec],
        out_specs=[pl.BlockSpec((tn // LANES, tm, LANES), lambda i, j: (j, i, 0)),
                   pl.BlockSpec((1, n_gate_rows, tm), lambda i, j: (i // per_batch, 0, i % per_batch))],
        out_shape=[jax.ShapeDtypeStruct((N // LANES, T, LANES), F32),
                   jax.ShapeDtypeStruct((B, n_gate_rows, L), F32)],
        compiler_params=_cparams("parallel", "arbitrary"), name="in_proj",
    )(hb, w, wab, pad(a_log), pad(dt_bias))


def _dot(a, b):
    return jnp.dot(a.astype(BF16), b.astype(BF16), preferred_element_type=F32)


def _dot_nt(a, b):
    return lax.dot_general(a.astype(BF16), b.astype(BF16), (((1,), (1,)), ((), ())),
                           preferred_element_type=F32)


def _dot_tn(a, b):
    return lax.dot_general(a.astype(BF16), b.astype(BF16), (((0,), (0,)), ((), ())),
                           preferred_element_type=F32)


def _deltanet_kernel(q_ref, k_ref, v_ref, cq_ref, ck_ref, cv_ref, gt_ref, o_ref,
                     qs, ks, vs, ob, wq_s, at_s, kd_s, u_s, gl_s, s_s, rhs_s, *, L, C, H, R, n_items):
    DH = DN_HEAD_DIM
    s = pl.program_id(0)
    h = jnp.minimum(s, n_items - 1) % H
    slot_p = s % 2
    slot_s = 1 - slot_p
    NC = L // C
    U = PREP_UNROLL

    def conv_rows(r0, first=False, last=False):
        rows = pl.ds(r0, R)
        rid = lax.broadcasted_iota(I32, (R, DH), 0)

        def conv_silu(ref, w_ref):
            x = ref[0, rows, :]
            if first:
                x_prev = jnp.where(rid == 0, 0.0, pltpu.roll(x, 1, 0))
            else:
                x_prev = ref[0, pl.ds(r0 - 1, R), :]
            if last is False:
                x_next = ref[0, pl.ds(r0 + 1, R), :]
            else:
                at_end = jnp.where(rid == R - 1, 0.0, pltpu.roll(x, R - 1, 0))
                x_next = at_end if last is True else jnp.where(
                    last, at_end, ref[0, pl.ds(jnp.minimum(r0 + 1, L - R), R), :])
            return _silu(w_ref[0:1, :] * x_prev + w_ref[1:2, :] * x + w_ref[2:3, :] * x_next)

        q = conv_silu(q_ref, cq_ref)
        k = conv_silu(k_ref, ck_ref)
        qs[rows, :] = q * (lax.rsqrt(jnp.sum(q * q, -1, keepdims=True) + RMS_EPS) * (DH ** -0.5))
        ks[rows, :] = k * lax.rsqrt(jnp.sum(k * k, -1, keepdims=True) + RMS_EPS)
        vs[rows, :] = conv_silu(v_ref, cv_ref)

    n_trips = NC // U
    trip_blocks = (U * C) // R

    @pl.when(s < n_items)
    def _():
        for b in range(trip_blocks):
            conv_rows(b * R, first=b == 0, last=(b + 1) * R == L)

    def conv_pieces(i):
        nxt = jnp.minimum(i + 1, n_trips - 1)
        for b in range(trip_blocks):
            r0 = pl.multiple_of((nxt * trip_blocks + b) * R, R)
            conv_rows(r0, last=(r0 + R == L) if b == trip_blocks - 1 else False)
            for _ in range(CONV_SPREAD):
                yield

    ri = lax.broadcasted_iota(I32, (C, C), 0)
    ci = lax.broadcasted_iota(I32, (C, C), 1)
    eye = ri == ci
    eye_f = eye.astype(F32)
    incl = (ri >= ci, ri <= ci)
    strict = (ri > ci, ri < ci)
    n_sq = C.bit_length() - 2
    zero_b = jnp.zeros((C, C), BF16)

    def block_diag(a, b):
        return jnp.concatenate([jnp.concatenate([a, zero_b], axis=1),
                                jnp.concatenate([zero_b, b], axis=1)], axis=0)

    def prep_start(c, j):
        rows = pl.ds(pl.multiple_of(c * C, C), C)
        q = qs[rows, :]
        k = ks[rows, :]
        v = vs[rows, :]
        kq = _dot_nt(jnp.concatenate([k, q], axis=0), k)
        kk, qk = kq[:C], kq[C:]
        a2 = []
        for d in range(2):
            g_row = gt_ref[0, d * H + h, pl.ds(c, 1), :]
            b_row = gt_ref[0, (2 + d) * H + h, pl.ds(c, 1), :]
            m, ms, mt = incl[d], strict[d], incl[1 - d]
            g_col = jnp.sum(jnp.where(eye, g_row, 0.0), axis=1, keepdims=True)
            b_col = jnp.sum(jnp.where(eye, b_row, 0.0), axis=1, keepdims=True)
            gc_col = jnp.sum(jnp.where(m, g_row, 0.0), axis=1, keepdims=True)
            gc_row = jnp.sum(jnp.where(mt, g_col, 0.0), axis=0, keepdims=True)
            tot = jnp.sum(g_row, axis=1, keepdims=True)
            decay = jnp.where(m, jnp.exp(jnp.where(m, gc_col - gc_row, 0.0)), 0.0)
            a2.append(jnp.where(ms, -(kk * b_col * decay), 0.0))
            egc = jnp.exp(gc_col)
            rhs_s[d, j] = jnp.concatenate([v * b_col, k * (b_col * egc)], axis=1).astype(BF16)
            wq_s[slot_p, d, c, pl.ds(C, C), :] = (q * egc).astype(BF16)
            at_s[slot_p, d, c] = (qk * decay).astype(BF16)
            kd_s[slot_p, d, c] = (k * jnp.exp(tot - gc_col)).astype(BF16)
            gl_s[slot_p, d, pl.ds(c, 1), :] = jnp.broadcast_to(jnp.exp(tot), (1, DH))
        return jnp.concatenate(a2, axis=1)

    def prep_phases(i):
        cs = [i * U + j for j in range(U)]
        ps = []
        for j, c in enumerate(cs):
            ps.append(prep_start(c, j))
            if j == U // 2 - 1:
                yield
        eye2 = jnp.concatenate([eye_f, eye_f], axis=1)
        xs_ = [eye2 + p for p in ps]
        yield
        for _ in range(n_sq):
            pbs = [p.astype(BF16) for p in ps]
            ps = [jnp.dot(pb, block_diag(pb[:, :C], pb[:, C:]), preferred_element_type=F32) for pb in pbs]
            yield
            pbs = [p.astype(BF16) for p in ps]
            xs_ = [x + jnp.dot(x.astype(BF16), block_diag(pb[:, :C], pb[:, C:]), preferred_element_type=F32)
                   for x, pb in zip(xs_, pbs)]
            yield
        for j, (c, x) in enumerate(zip(cs, xs_)):
            for d in range(2):
                sol = jnp.dot(x[:, d * C:(d + 1) * C].astype(BF16), rhs_s[d, j], preferred_element_type=F32)
                u_s[slot_p, d, c] = sol[:, :DH]
                wq_s[slot_p, d, c, pl.ds(0, C), :] = sol[:, DH:].astype(BF16)
            if j == U // 2 - 1:
                yield
        yield

    def scan_stages(n):
        cf, cb = n, NC - 1 - n
        s_f, s_b = s_s[0], s_s[1]
        wq = jnp.concatenate([wq_s[slot_s, 0, cf], wq_s[slot_s, 1, cb]], axis=1)
        r1 = jnp.dot(wq, block_diag(s_f.astype(BF16), s_b.astype(BF16)), preferred_element_type=F32)
        yield
        u = jnp.concatenate([u_s[slot_s, 0, cf], u_s[slot_s, 1, cb]], axis=1)
        v_new = (u - r1[:C]).astype(BF16)
        vd = block_diag(v_new[:, :DH], v_new[:, DH:])
        at = jnp.concatenate([at_s[slot_s, 0, cf], at_s[slot_s, 1, cb]], axis=1)
        o = r1[C:] + jnp.dot(at, vd, preferred_element_type=F32)
        kd = jnp.concatenate([kd_s[slot_s, 0, cf], kd_s[slot_s, 1, cb]], axis=0)
        ds = lax.dot_general(kd, vd, (((0,), (0,)), ((), ())), preferred_element_type=F32)
        s_s[0] = s_f * gl_s[slot_s, 0, pl.ds(cf, 1), :] + ds[:, :DH]
        s_s[1] = s_b * gl_s[slot_s, 1, pl.ds(cb, 1), :] + ds[:, DH:]
        o_ref[0, pl.ds(pl.multiple_of(cf * C, C), C), :] = o[:, :DH]
        ob[pl.ds(pl.multiple_of(cb * C, C), C), :] = o[:, DH:]
        yield

    def run(prepare, scan):
        def trip(i, _):
            prep_it = prep_phases(i) if prepare else iter(())
            conv_it = conv_pieces(i) if prepare and n_trips > 1 else iter(())
            scan_it = itertools.chain.from_iterable(scan_stages(i * U + j) for j in range(U)) if scan else iter(())
            for _ in itertools.zip_longest(scan_it, prep_it, conv_it):
                pass
            return 0
        if scan:
            s_s[...] = jnp.zeros_like(s_s)
        lax.fori_loop(0, n_trips, trip, 0)

    @pl.when(s == 0)
    def _():
        run(True, False)

    @pl.when((s > 0) & (s < n_items))
    def _():
        run(True, True)

    @pl.when(s == n_items)
    def _():
        run(False, True)

    def sum_rows(i, _):
        rows = pl.ds(pl.multiple_of(i * R, R), R)
        o_ref[0, rows, :] = o_ref[0, rows, :] + ob[rows, :]
        return 0

    @pl.when(s > 0)
    def _():
        lax.fori_loop(0, L // R, sum_rows, 0)


def _deltanet(proj, gt, conv_w, *, B, L, H):
    DH, C, R = DN_HEAD_DIM, DN_CHUNK, 256
    assert C == DH and L % (C * PREP_UNROLL) == 0
    NC = L // C
    n_items = B * H
    cur = lambda s: jnp.minimum(s, n_items - 1)
    prev = lambda s: jnp.maximum(s - 1, 0)
    col = lambda off: pl.BlockSpec((1, L, DH), lambda s: (off + cur(s) % H, cur(s) // H, 0))
    cw = lambda off: pl.BlockSpec((3, DH), lambda s: (0, off + cur(s) % H))
    kern = functools.partial(_deltanet_kernel, L=L, C=C, H=H, R=R, n_items=n_items)
    return pl.pallas_call(
        kern, grid=(n_items + 1,),
        in_specs=[col(0), col(H), col(2 * H), cw(0), cw(H), cw(2 * H),
                  pl.BlockSpec((1, 4 * H, NC, C), lambda s: (cur(s) // H, 0, 0, 0))],
        out_specs=pl.BlockSpec((1, L, DH), lambda s: (prev(s) // H, 0, prev(s) % H)),
        out_shape=jax.ShapeDtypeStruct((B, L, H * DH), F32),
        scratch_shapes=[pltpu.VMEM((L, DH), F32), pltpu.VMEM((L, DH), F32), pltpu.VMEM((L, DH), F32),
                        pltpu.VMEM((L, DH), F32),
                        pltpu.VMEM((2, 2, NC, 2 * C, DH), BF16), pltpu.VMEM((2, 2, NC, C, C), BF16),
                        pltpu.VMEM((2, 2, NC, C, DH), BF16), pltpu.VMEM((2, 2, NC, C, DH), F32),
                        pltpu.VMEM((2, 2, NC, DH), F32), pltpu.VMEM((2, DH, DH), F32),
                        pltpu.VMEM((2, PREP_UNROLL, C, 2 * DH), BF16)],
        compiler_params=_cparams("arbitrary"), name="deltanet",
    )(proj, proj, proj, conv_w, conv_w, conv_w, gt)


def _conformer_kernel(val_ref, gate_ref, w_ref, b_ref, g_ref, beta_ref, o_ref, ypad, *, L, K, R, HALO):
    pad = K // 2
    zeros = jnp.zeros((HALO, CV_GROUP), F32)
    ypad[pl.ds(0, HALO), :] = zeros
    ypad[pl.ds(HALO + L, HALO), :] = zeros

    def glu_rows(i, _):
        r0 = pl.multiple_of(i * R, R)
        ypad[pl.ds(HALO + r0, R), :] = val_ref[0, pl.ds(r0, R), :] * _sigmoid(gate_ref[0, pl.ds(r0, R), :])
        return 0

    lax.fori_loop(0, L // R, glu_rows, 0)

    def conv_rows(i, _):
        r0 = pl.multiple_of(i * R, R)
        acc = jnp.zeros((R, CV_GROUP), F32)
        for t in range(K):
            acc = acc + w_ref[t:t + 1, :] * ypad[pl.ds(r0 + (HALO - pad + t), R), :]
        y = _layer_norm(acc + b_ref[...], g_ref[...], beta_ref[...])
        o_ref[0, pl.ds(r0, R), :] = _silu(y)
        return 0

    lax.fori_loop(0, L // R, conv_rows, 0)


def _conformer(proj, w, b, g, beta, *, B, L, col0, width):
    K = w.shape[0]
    n = width // CV_GROUP
    HALO, R = 16, 256
    c0 = col0 // CV_GROUP
    vec = pl.BlockSpec((1, CV_GROUP), lambda bb, j: (0, j))
    kern = functools.partial(_conformer_kernel, L=L, K=K, R=R, HALO=HALO)
    return pl.pallas_call(
        kern, grid=(B, n),
        in_specs=[pl.BlockSpec((1, L, CV_GROUP), lambda bb, j: (c0 + j, bb, 0)),
                  pl.BlockSpec((1, L, CV_GROUP), lambda bb, j: (c0 + n + j, bb, 0)),
                  pl.BlockSpec((K, CV_GROUP), lambda bb, j: (0, j)), vec, vec, vec],
        out_specs=pl.BlockSpec((1, L, CV_GROUP), lambda bb, j: (bb, 0, j)),
        out_shape=jax.ShapeDtypeStruct((B, L, width), F32),
        scratch_shapes=[pltpu.VMEM((L + 2 * HALO, CV_GROUP), F32)],
        compiler_params=_cparams("parallel", "parallel"), name="conformer",
    )(proj, proj, w, b.reshape(1, width), g.reshape(1, width), beta.reshape(1, width))


def _out_router_kernel(dn_ref, z_ref, nw_ref, cv_ref, h_ref, w_ref, g_ref, b_ref, wr_ref, br_ref,
                       h1_ref, h1p_ref, route_ref, routet_ref, cnt_ref, *, alpha, n_dn, sub):
    n_sub = h_ref.shape[0] // sub

    @pl.when(pl.program_id(0) == 0)
    def _():
        cnt_ref[...] = jnp.zeros_like(cnt_ref)

    def stages(s):
        rows = pl.ds(s * sub, sub)
        heads = []
        for hh in range(n_dn // DN_HEAD_DIM):
            o = dn_ref[rows, hh * DN_HEAD_DIM:(hh + 1) * DN_HEAD_DIM]
            o = o * lax.rsqrt(jnp.mean(o * o, -1, keepdims=True) + RMS_EPS) * nw_ref[...]
            heads.append((o * _silu(z_ref[hh, rows, :])).astype(BF16))
        mix = jnp.dot(jnp.concatenate(heads, axis=1), w_ref[pl.ds(0, n_dn), :], preferred_element_type=F32)
        mix = mix + jnp.dot(cv_ref[rows, :].astype(BF16), w_ref[pl.ds(n_dn, w_ref.shape[0] - n_dn), :],
                            preferred_element_type=F32)
        yield
        h1 = _layer_norm(alpha * h_ref[rows, :] + mix, g_ref[...], b_ref[...])
        h1_ref[rows, :] = h1
        h1p_ref[rows, :] = _pack_halves(h1)
        h1b = h1.astype(BF16)
        yield
        logits = jnp.dot(h1b, wr_ref[...], preferred_element_type=F32) + br_ref[...]
        yield
        route = _route(logits)
        route_ref[rows, :] = route
        routet_ref[:, rows] = route.T[:8, :]
        lane = lax.broadcasted_iota(I32, route.shape, 1)
        e1 = route[:, 0:1].astype(I32)
        e2 = route[:, 1:2].astype(I32) + N_EXPERTS
        hits = jnp.where(lane == e1, 1.0, jnp.where(lane == e2, 1.0, 0.0))
        cnt_ref[0:1, :] += jnp.sum(hits, axis=0, keepdims=True)
        yield

    n_stage = 4
    pipes = [stages(s) for s in range(n_sub)]
    for t in range(n_sub + n_stage - 1):
        for s in range(t, t - n_stage, -1):
            if 0 <= s < n_sub:
                next(pipes[s])


def _route(logits):
    lane = lax.broadcasted_iota(I32, logits.shape, 1)
    neg = jnp.float32(-jnp.inf)
    big = jnp.int32(LANES)
    gl = jnp.where(lane < N_GROUPS, logits, neg)
    gmax = jnp.max(gl, -1, keepdims=True)
    gsel = jnp.min(jnp.where(gl == gmax, lane, big), -1, keepdims=True)
    pg = 1.0 / jnp.sum(jnp.exp(gl - gmax), -1, keepdims=True)
    lo = N_GROUPS + gsel * EXPERTS_PER_GROUP
    el = jnp.where((lane >= lo) & (lane < lo + EXPERTS_PER_GROUP), logits, neg)
    e1 = jnp.max(el, -1, keepdims=True)
    i1 = jnp.min(jnp.where(el == e1, lane, big), -1, keepdims=True)
    el2 = jnp.where(lane == i1, neg, el)
    e2 = jnp.max(el2, -1, keepdims=True)
    i2 = jnp.min(jnp.where(el2 == e2, lane, big), -1, keepdims=True)
    r = jnp.exp(e2 - e1)
    p1 = 1.0 / (1.0 + r)
    p2 = r * p1
    route = jnp.where(lane == 0, (i1 - N_GROUPS).astype(F32),
                      jnp.where(lane == 1, (i2 - N_GROUPS).astype(F32),
                                jnp.where(lane == 2, pg * p1, jnp.where(lane == 3, pg * p2, 0.0))))
    return route


def _out_router(dn, proj, z_group, norm_w, cv, h, w_out, g, b, w_router, b_router, *, alpha, tm=512, sub=128):
    T, D = h.shape
    n_dn, n_cv = dn.shape[1], cv.shape[1]
    n_heads = n_dn // DN_HEAD_DIM
    assert z_group % n_heads == 0
    vec = pl.BlockSpec((1, D), lambda i: (0, 0))
    row = pl.BlockSpec((tm, D), lambda i: (i, 0))
    kern = functools.partial(_out_router_kernel, alpha=alpha, n_dn=n_dn, sub=sub)
    return pl.pallas_call(
        kern, grid=(T // tm,),
        in_specs=[pl.BlockSpec((tm, n_dn), lambda i: (i, 0)),
                  pl.BlockSpec((n_heads, tm, DN_HEAD_DIM), lambda i: (z_group // n_heads, i, 0)),
                  pl.BlockSpec((1, DN_HEAD_DIM), lambda i: (0, 0)),
                  pl.BlockSpec((tm, n_cv), lambda i: (i, 0)), row,
                  pl.BlockSpec((n_dn + n_cv, D), lambda i: (0, 0)), vec, vec,
                  pl.BlockSpec((D, LANES), lambda i: (0, 0)), pl.BlockSpec((1, LANES), lambda i: (0, 0))],
        out_specs=[row, pl.BlockSpec((tm, D // 2), lambda i: (i, 0)),
                   pl.BlockSpec((tm, LANES), lambda i: (i, 0)), pl.BlockSpec((8, tm), lambda i: (0, i)),
                   pl.BlockSpec((8, LANES), lambda i: (0, 0))],
        out_shape=[jax.ShapeDtypeStruct((T, D), F32), jax.ShapeDtypeStruct((T, D // 2), U32),
                   jax.ShapeDtypeStruct((T, LANES), F32), jax.ShapeDtypeStruct((8, T), F32),
                   jax.ShapeDtypeStruct((8, LANES), F32)],
        compiler_params=_cparams("arbitrary"), name="out_router",
    )(dn, proj, norm_w.reshape(1, DN_HEAD_DIM), cv, h, w_out, g.reshape(1, D), b.reshape(1, D), w_router, b_router)


def _onehot_t(routet_ref, tr):
    sub = lax.broadcasted_iota(I32, (2 * N_EXPERTS, tr), 0)
    e1 = routet_ref[0:1, :].astype(I32)
    e2 = routet_ref[1:2, :].astype(I32)
    return sub == jnp.where(sub < N_EXPERTS, e1, e2 + N_EXPERTS)


def _place_kernel(routet_ref, base_ref, d1_ref, d2_ref, carry, *, tr):
    @pl.when(pl.program_id(0) == 0)
    def _():
        carry[...] = jnp.zeros_like(carry)
    oh = _onehot_t(routet_ref, tr)
    ohb = oh.astype(F32).astype(BF16)
    ri = lax.broadcasted_iota(I32, (tr, tr), 0)
    ci = lax.broadcasted_iota(I32, (tr, tr), 1)
    before = (ri < ci).astype(F32).astype(BF16)
    excl = jnp.dot(ohb, before, preferred_element_type=F32)
    pos = jnp.where(oh, excl + carry[...] + base_ref[...], 0.0)
    d1 = jnp.sum(pos[:N_EXPERTS], axis=0, keepdims=True)
    d2 = jnp.sum(pos[N_EXPERTS:], axis=0, keepdims=True)
    d1_ref[...] = d1.astype(I32)
    d2_ref[...] = d2.astype(I32)
    carry[...] += jnp.sum(oh.astype(F32), axis=1, keepdims=True)


def _placement(routet, cnt, tr=512):
    T = routet.shape[1]
    E = N_EXPERTS
    rt = pl.BlockSpec((8, tr), lambda i: (0, i))
    col = pl.BlockSpec((2 * E, 1), lambda i: (0, 0))
    c1 = cnt[0, :E].astype(I32)
    c2 = cnt[0, E:].astype(I32)
    padded = ((c1 + c2 + FFN_BLOCK - 1) // FFN_BLOCK) * FFN_BLOCK
    pend = jnp.cumsum(padded)
    pstart = pend - padded
    base = jnp.concatenate([pstart, pstart + c1]).astype(F32).reshape(2 * E, 1)
    lane_row = pl.BlockSpec((1, tr), lambda i: (0, i))
    d1, d2 = pl.pallas_call(
        functools.partial(_place_kernel, tr=tr), grid=(T // tr,), in_specs=[rt, col], out_specs=[lane_row, lane_row],
        out_shape=[jax.ShapeDtypeStruct((1, T), I32)] * 2,
        scratch_shapes=[pltpu.VMEM((2 * E, 1), F32)],
        compiler_params=_cparams("arbitrary"), name="moe_place")(routet, base)
    n_blocks = (T * TOP_K + E * (FFN_BLOCK - 1) + FFN_BLOCK - 1) // FFN_BLOCK
    starts = jnp.arange(n_blocks, dtype=I32) * FFN_BLOCK
    blk_e = jnp.minimum(jnp.sum((pend[None, :] <= starts[:, None]).astype(I32), axis=1), E - 1)
    blk_first = jnp.concatenate([jnp.ones((1,), I32), (blk_e[1:] != blk_e[:-1]).astype(I32)])
    n_used = (pend[-1] // FFN_BLOCK).astype(I32).reshape(1)
    ids = jnp.arange(E, dtype=I32)
    owns = padded > 0
    later = owns[None, :] & (ids[None, :] > ids[:, None])
    next_tab = jnp.min(jnp.where(later, ids[None, :], E), axis=1)
    next_tab = jnp.concatenate([next_tab, jnp.full((1,), E, I32)])
    next2_tab = next_tab[next_tab[:E]]
    hide = lambda t: jnp.where(t == E, -1, t)
    slot_tab = (jnp.cumsum(owns.astype(I32)) - 1) % 2
    steer = (blk_e, blk_first, slot_tab[blk_e], hide(next_tab[:E])[blk_e], hide(next2_tab)[blk_e])
    return d1.reshape(T), d2.reshape(T), pend.astype(I32), steer, n_used, n_blocks


def _dispatch_kernel(d1_ref, d2_ref, pend_ref, src_ref, xs_ref, zbuf, sem, zsem, *, tb):
    @pl.when(pl.program_id(0) == 0)
    def _():
        zbuf[...] = jnp.zeros_like(zbuf)
        tails = []
        for e in range(N_EXPERTS):
            end = pend_ref[e]
            nonempty = end > (pend_ref[e - 1] if e else 0)
            start = pl.multiple_of(jnp.maximum(end - FFN_BLOCK, 0), FFN_BLOCK)
            tails.append((nonempty, pltpu.make_async_copy(zbuf, xs_ref.at[pl.ds(start, FFN_BLOCK), :], zsem)))
        for nonempty, copy in tails:
            pl.when(nonempty)(copy.start)

        def spare(b):
            return pltpu.make_async_copy(
                zbuf, xs_ref.at[pl.ds(pl.multiple_of(b * FFN_BLOCK, FFN_BLOCK), FFN_BLOCK), :], zsem)
        first_spare = pend_ref[N_EXPERTS - 1] // FFN_BLOCK
        n_blocks = xs_ref.shape[0] // FFN_BLOCK
        lax.fori_loop(first_spare, n_blocks, lambda b, c: (spare(b).start(), c)[1], 0)
        for nonempty, copy in tails:
            pl.when(nonempty)(copy.wait)
        lax.fori_loop(first_spare, n_blocks, lambda b, c: (spare(b).wait(), c)[1], 0)

    def issue(t, _):
        row = src_ref.at[pl.ds(t, 1), :]
        pltpu.make_async_copy(row, xs_ref.at[pl.ds(d1_ref[t], 1), :], sem).start(priority=0)
        pltpu.make_async_copy(row, xs_ref.at[pl.ds(d2_ref[t], 1), :], sem).start(priority=1)
        return 0

    lax.fori_loop(0, tb, issue, 0, unroll=8)
    pltpu.make_async_copy(src_ref, xs_ref.at[pl.ds(0, tb), :], sem).wait()
    pltpu.make_async_copy(src_ref, xs_ref.at[pl.ds(0, tb), :], sem).wait()


def _dispatch(h1p, d1, d2, pend, n_rows, tb=2048):
    T, W = h1p.shape
    idx = pl.BlockSpec((tb,), lambda i: (i,), memory_space=pltpu.SMEM)
    return pl.pallas_call(
        functools.partial(_dispatch_kernel, tb=tb), grid=(T // tb,),
        in_specs=[idx, idx, pl.BlockSpec((N_EXPERTS,), lambda i: (0,), memory_space=pltpu.SMEM),
                  pl.BlockSpec((tb, W), lambda i: (i, 0))],
        out_specs=pl.BlockSpec(memory_space=pl.ANY),
        out_shape=jax.ShapeDtypeStruct((n_rows, W), U32),
        scratch_shapes=[pltpu.VMEM((FFN_BLOCK, W), U32), pltpu.SemaphoreType.DMA(()), pltpu.SemaphoreType.DMA(())],
        compiler_params=_cparams("arbitrary"), name="moe_dispatch",
    )(d1, d2, pend, h1p)


def _cast_rows(src, dst):
    rows = CAST_CHUNK_ELEMS // src.shape[1]

    def body(i, c):
        r = pl.ds(pl.multiple_of(i * rows, rows), rows)
        dst[r, :] = src[r, :].astype(BF16)
        return c

    lax.fori_loop(0, src.shape[0] // rows, body, 0, unroll=4)


def _ffn_kernel(blk_e_ref, blk_first_ref, blk_slot_ref, blk_next_ref, blk_next2_ref, n_used_ref,
                xs_ref, wgu_hbm, wdn_hbm, y_ref, wgu_f, wdn_f, wgu_s, wdn_s, sem, *, ff, layer):

    def weight_copies(e, slot):
        copies = []
        for k, (src, dst) in enumerate(((wgu_hbm, wgu_f), (wdn_hbm, wdn_f))):
            rows = dst.shape[1] // FFN_WEIGHT_CHUNKS
            for c in range(FFN_WEIGHT_CHUNKS):
                r = pl.ds(c * rows, rows)
                copies.append(pltpu.make_async_copy(src.at[layer, e, r, :], dst.at[slot, r, :],
                                                    sem.at[slot, k * FFN_WEIGHT_CHUNKS + c]))
        return copies

    def start(e, slot):
        for c in weight_copies(e, slot):
            c.start(priority=1)

    @pl.when(pl.program_id(0) == 0)
    def _():
        start(blk_e_ref[0], 0)

        @pl.when(blk_next_ref[0] >= 0)
        def _():
            start(blk_next_ref[0], 1)

    for j in range(FFN_BLOCKS_PER_STEP):
        i = pl.program_id(0) * FFN_BLOCKS_PER_STEP + j
        rows = pl.ds(j * FFN_BLOCK, FFN_BLOCK)
        used = i < n_used_ref[0]

        @pl.when(used & (blk_first_ref[i] == 1))
        def _():
            slot = blk_slot_ref[i]
            for c in weight_copies(blk_e_ref[i], slot):
                c.wait()
            _cast_rows(wgu_f.at[slot], wgu_s)
            _cast_rows(wdn_f.at[slot], wdn_s)

            @pl.when(blk_next2_ref[i] >= 0)
            def _():
                start(blk_next2_ref[i], slot)

        @pl.when(used)
        def _():
            lo, hi = _unpack_halves(xs_ref[rows, :])
            x = jnp.concatenate([lo, hi], axis=1).astype(BF16)
            gu = jnp.dot(x, wgu_s[...], preferred_element_type=F32)
            act = (_silu(gu[:, :ff]) * gu[:, ff:]).astype(BF16)
            y_ref[rows, :] = _pack_halves(jnp.dot(act, wdn_s[...], preferred_element_type=F32))

        @pl.when(jnp.logical_not(used))
        def _():
            y_ref[rows, :] = jnp.zeros((FFN_BLOCK, y_ref.shape[1]), y_ref.dtype)


def _ffn(xs, w_gu, w_dn, layer, blk_e, blk_first, blk_slot, blk_next, blk_next2, n_used, n_blocks):
    _, E, D, FF2 = w_gu.shape
    ff = FF2 // 2
    W = xs.shape[1]
    assert n_blocks % FFN_BLOCKS_PER_STEP == 0
    step_rows = FFN_BLOCK * FFN_BLOCKS_PER_STEP
    any_spec = pl.BlockSpec(memory_space=pl.ANY)
    last_used_step = lambda nu: (nu[0] - 1) // FFN_BLOCKS_PER_STEP
    grid_spec = pltpu.PrefetchScalarGridSpec(
        num_scalar_prefetch=6, grid=(n_blocks // FFN_BLOCKS_PER_STEP,),
        in_specs=[pl.BlockSpec((step_rows, W), lambda i, be, bf, bs, bn, bn2, nu: (jnp.minimum(i, last_used_step(nu)), 0)),
                  any_spec, any_spec],
        out_specs=pl.BlockSpec((step_rows, W), lambda i, *_: (i, 0)),
        scratch_shapes=[pltpu.VMEM((2, D, FF2), F32), pltpu.VMEM((2, ff, D), F32),
                        pltpu.VMEM((D, FF2), BF16), pltpu.VMEM((ff, D), BF16),
                        pltpu.SemaphoreType.DMA((2, 2 * FFN_WEIGHT_CHUNKS))])
    return pl.pallas_call(
        functools.partial(_ffn_kernel, ff=ff, layer=layer), grid_spec=grid_spec,
        out_shape=jax.ShapeDtypeStruct(xs.shape, U32),
        compiler_params=_cparams("arbitrary"), name="moe_ffn",
    )(blk_e, blk_first, blk_slot, blk_next, blk_next2, n_used, xs, w_gu, w_dn)


def _combine_kernel(d1_ref, d2_ref, d1n_ref, d2n_ref, y_ref, h1_ref, route_ref, g_ref, b_ref, h2_ref, h2b_ref,
                    ybuf, sem, *, tc, alpha):
    i = pl.program_id(0)
    slot = i % 2

    def row_copies(r1, r2, s, t):
        return (pltpu.make_async_copy(y_ref.at[pl.ds(r1[t], 1), :], ybuf.at[s, 0, pl.ds(t, 1), :], sem.at[s]),
                pltpu.make_async_copy(y_ref.at[pl.ds(r2[t], 1), :], ybuf.at[s, 1, pl.ds(t, 1), :], sem.at[s]))

    def wait_slot(s):
        for j in range(TOP_K):
            pltpu.make_async_copy(y_ref.at[pl.ds(0, tc), :], ybuf.at[s, j], sem.at[s]).wait()

    @pl.when(i == 0)
    def _():
        def issue(t, _):
            for j, c in enumerate(row_copies(d1_ref, d2_ref, 0, t)):
                c.start(priority=j)
            return 0
        lax.fori_loop(0, tc, issue, 0, unroll=8)

    wait_slot(slot)
    for t in range(tc):
        for j, c in enumerate(row_copies(d1n_ref, d2n_ref, 1 - slot, t)):
            c.start(priority=j)

    route = route_ref[...]
    g1 = route[:, 2:3]
    g2 = route[:, 3:4]
    lo1, hi1 = _unpack_halves(ybuf[slot, 0])
    lo2, hi2 = _unpack_halves(ybuf[slot, 1])
    ffn = jnp.concatenate([g1 * lo1 + g2 * lo2, g1 * hi1 + g2 * hi2], axis=1)
    h2 = _layer_norm(alpha * h1_ref[...] + ffn, g_ref[...], b_ref[...])
    h2_ref[...] = h2
    h2b_ref[...] = h2.astype(BF16)

    @pl.when(i == pl.num_programs(0) - 1)
    def _():
        wait_slot(1 - slot)


def _combine(y, d1, d2, h1, route, g, b, *, alpha, tc=256):
    T, D = h1.shape
    W = y.shape[1]
    n = T // tc
    idx = pl.BlockSpec((tc,), lambda i: (i,), memory_space=pltpu.SMEM)
    idx_next = pl.BlockSpec((tc,), lambda i: (jnp.minimum(i + 1, n - 1),), memory_space=pltpu.SMEM)
    row = pl.BlockSpec((tc, D), lambda i: (i, 0))
    vec = pl.BlockSpec((1, D), lambda i: (0, 0))
    return pl.pallas_call(
        functools.partial(_combine_kernel, tc=tc, alpha=alpha), grid=(n,),
        in_specs=[idx, idx, idx_next, idx_next, pl.BlockSpec(memory_space=pl.ANY), row,
                  pl.BlockSpec((tc, LANES), lambda i: (i, 0)), vec, vec],
        out_specs=[row, row],
        out_shape=[jax.ShapeDtypeStruct((T, D), F32), jax.ShapeDtypeStruct((T, D), BF16)],
        scratch_shapes=[pltpu.VMEM((2, TOP_K, tc, W), U32), pltpu.SemaphoreType.DMA((2,))],
        compiler_params=_cparams("arbitrary"), name="moe_combine",
    )(d1, d2, d1, d2, y, h1, route, g.reshape(1, D), b.reshape(1, D))


def kernel(x, emb_ln_g, emb_ln_b, w_in, short_conv_w, a_log, dt_bias, dn_norm_w, dw_conv_w, dw_conv_b,
           conv_ln_g, conv_ln_b, w_out, ln1_g, ln1_b, w_group, b_group, w_expert, b_expert, w_gate_up,
           w_down, ln2_g, ln2_b):
    B, L, D = x.shape
    T = B * L
    depth = w_in.shape[0]
    H = a_log.shape[2]
    dn_w = H * DN_HEAD_DIM
    cv_w = dw_conv_w.shape[2]
    n_ab = 4 * H
    alpha = (2 * depth) ** 0.25
    C = DN_CHUNK

    h, hb = _emb_ln(x.reshape(T, D), emb_ln_g, emb_ln_b)
    for l in range(depth):
        w = w_in[l]
        w_main = jnp.concatenate([w[:, :4 * dn_w], w[:, 4 * dn_w + n_ab:]], axis=1).astype(BF16)
        proj, gt = _in_proj(hb, w_main, w[:, 4 * dn_w:4 * dn_w + n_ab], a_log[l], dt_bias[l], B=B, L=L)
        gt = gt.reshape(B, n_ab, L // C, C)
        dn = _deltanet(proj, gt, short_conv_w[l], B=B, L=L, H=H)
        cv = _conformer(proj, dw_conv_w[l], dw_conv_b[l], conv_ln_g[l], conv_ln_b[l],
                        B=B, L=L, col0=4 * dn_w, width=cv_w)
        n_r = N_GROUPS + N_EXPERTS
        w_router = jnp.pad(jnp.concatenate([w_group[l], w_expert[l]], axis=1),
                           ((0, 0), (0, LANES - n_r))).astype(BF16)
        b_router = jnp.pad(jnp.concatenate([b_group[l], b_expert[l]]), (0, LANES - n_r)).reshape(1, LANES)
        h1, h1p, route, routet, cnt = _out_router(
            dn.reshape(T, dn_w), proj, 3 * H, dn_norm_w[l], cv.reshape(T, cv_w), h, w_out[l].astype(BF16),
            ln1_g[l], ln1_b[l],
            w_router, b_router, alpha=alpha)
        d1, d2, pend, steer, n_used, n_blocks = _placement(routet, cnt)
        xs = _dispatch(h1p, d1, d2, pend, n_blocks * FFN_BLOCK)
        y = _ffn(xs, w_gate_up, w_down, l, *steer, n_used, n_blocks)
        h, hb = _combine(y, d1, d2, h1, route, ln2_g[l], ln2_b[l], alpha=alpha)
    return h.reshape(B, L, D)
```

```python
import functools
import itertools

import jax
import jax.numpy as jnp
from jax import lax
from jax.experimental import pallas as pl
from jax.experimental.pallas import tpu as pltpu

F32 = jnp.float32
BF16 = jnp.bfloat16
U32 = jnp.uint32
I32 = jnp.int32

LANES = 128
DN_HEAD_DIM = 128
DN_CHUNK = 128
PREP_UNROLL = 8
CONV_SPREAD = 4
CV_GROUP = 128
N_GROUPS = 8
EXPERTS_PER_GROUP = 8
N_EXPERTS = N_GROUPS * EXPERTS_PER_GROUP
TOP_K = 2
FFN_BLOCK = 256
FFN_BLOCKS_PER_STEP = 4
FFN_WEIGHT_CHUNKS = 4
CAST_CHUNK_ELEMS = 32 * 1024
LN_EPS = 1e-5
RMS_EPS = 1e-6
VMEM_LIMIT = 56 * 1024 * 1024


def _cparams(*sem):
    return pltpu.CompilerParams(dimension_semantics=sem, vmem_limit_bytes=VMEM_LIMIT)


def _layer_norm(x, g, b):
    mu = jnp.mean(x, -1, keepdims=True)
    xc = x - mu
    var = jnp.mean(xc * xc, -1, keepdims=True)
    return xc * lax.rsqrt(var + LN_EPS) * g + b


def _sigmoid(x):
    return 1.0 / (1.0 + jnp.exp(-x))


def _silu(x):
    return x * _sigmoid(x)


def _pack_halves(y):
    n = y.shape[1] // 2
    bits = lax.bitcast_convert_type(y.astype(BF16).astype(F32), U32)
    return (bits[:, :n] >> 16) | bits[:, n:]


def _unpack_halves(u):
    lo = lax.bitcast_convert_type(u << 16, F32)
    hi = lax.bitcast_convert_type(u & jnp.uint32(0xFFFF0000), F32)
    return lo, hi


def _emb_ln_kernel(x_ref, g_ref, b_ref, h_ref, hb_ref):
    h = _layer_norm(x_ref[...], g_ref[...], b_ref[...])
    h_ref[...] = h
    hb_ref[...] = h.astype(BF16)


def _emb_ln(x, g, b, tm=1024):
    T, D = x.shape
    row = pl.BlockSpec((tm, D), lambda i: (i, 0))
    vec = pl.BlockSpec((1, D), lambda i: (0, 0))
    return pl.pallas_call(
        _emb_ln_kernel, grid=(T // tm,), in_specs=[row, vec, vec], out_specs=[row, row],
        out_shape=[jax.ShapeDtypeStruct((T, D), F32), jax.ShapeDtypeStruct((T, D), BF16)],
        compiler_params=_cparams("parallel"), name="emb_ln")(x, g.reshape(1, D), b.reshape(1, D))


def _in_proj_kernel(x_ref, w_ref, wab_ref, alog_ref, dtb_ref, o_ref, gt_ref, *, n_decay, n_gate_rows):
    @pl.when(pl.program_id(1) == 0)
    def _():
        ab = jnp.dot(x_ref[...], wab_ref[...], preferred_element_type=F32)
        lane = lax.broadcasted_iota(I32, ab.shape, 1)
        sp = ab + dtb_ref[...]
        softplus = jnp.maximum(sp, 0.0) + jnp.log(1.0 + jnp.exp(-jnp.abs(sp)))
        decay = -jnp.exp(alog_ref[...]) * softplus
        gt_ref[0] = jnp.where(lane < n_decay, decay, _sigmoid(ab)).T[:n_gate_rows, :]

    acc = jnp.dot(x_ref[...], w_ref[...], preferred_element_type=F32)
    for j in range(o_ref.shape[0]):
        o_ref[j] = acc[:, j * LANES:(j + 1) * LANES]


def _in_proj(hb, w, w_ab, a_log, dt_bias, *, B, L, tm=1024, tn=2048):
    T, D = hb.shape
    N = w.shape[1]
    n = a_log.size
    n_gate_rows = 2 * n
    pad = lambda v: jnp.pad(v.reshape(1, n).astype(F32), ((0, 0), (0, LANES - n)))
    wab = jnp.pad(w_ab, ((0, 0), (0, LANES - w_ab.shape[1]))).astype(BF16)
    vec = pl.BlockSpec((1, LANES), lambda i, j: (0, 0))
    per_batch = L // tm
    return pl.pallas_call(
        functools.partial(_in_proj_kernel, n_decay=n, n_gate_rows=n_gate_rows), grid=(T // tm, N // tn),
        in_specs=[pl.BlockSpec((tm, D), lambda i, j: (i, 0)), pl.BlockSpec((D, tn), lambda i, j: (0, j)),
                  pl.BlockSpec((D, LANES), lambda i, j: (0, 0)), vec, vec],
        out_specs=[pl.BlockSpec((tn // LANES, tm, LANES), lambda i, j: (j, i, 0)),
                   pl.BlockSpec((1, n_gate_rows, tm), lambda i, j: (i // per_batch, 0, i % per_batch))],
        out_shape=[jax.ShapeDtypeStruct((N // LANES, T, LANES), F32),
                   jax.ShapeDtypeStruct((B, n_gate_rows, L), F32)],
        compiler_params=_cparams("parallel", "arbitrary"), name="in_proj",
    )(hb, w, wab, pad(a_log), pad(dt_bias))


def _dot(a, b):
    return jnp.dot(a.astype(BF16), b.astype(BF16), preferred_element_type=F32)


def _dot_nt(a, b):
    return lax.dot_general(a.astype(BF16), b.astype(BF16), (((1,), (1,)), ((), ())),
                           preferred_element_type=F32)


def _dot_tn(a, b):
    return lax.dot_general(a.astype(BF16), b.astype(BF16), (((0,), (0,)), ((), ())),
                           preferred_element_type=F32)


def _deltanet_kernel(q_ref, k_ref, v_ref, cq_ref, ck_ref, cv_ref, gt_ref, o_ref,
                     qs, ks, vs, ob, wq_s, at_s, kd_s, u_s, gl_s, s_s, rhs_s, *, L, C, H, R, n_items):
    DH = DN_HEAD_DIM
    s = pl.program_id(0)
    h = jnp.minimum(s, n_items - 1) % H
    slot_p = s % 2
    slot_s = 1 - slot_p
    NC = L // C
    U = PREP_UNROLL

    def conv_rows(r0, first=False, last=False):
        rows = pl.ds(r0, R)
        rid = lax.broadcasted_iota(I32, (R, DH), 0)

        def conv_silu(ref, w_ref):
            x = ref[0, rows, :]
            if first:
                x_prev = jnp.where(rid == 0, 0.0, pltpu.roll(x, 1, 0))
            else:
                x_prev = ref[0, pl.ds(r0 - 1, R), :]
            if last is False:
                x_next = ref[0, pl.ds(r0 + 1, R), :]
            else:
                at_end = jnp.where(rid == R - 1, 0.0, pltpu.roll(x, R - 1, 0))
                x_next = at_end if last is True else jnp.where(
                    last, at_end, ref[0, pl.ds(jnp.minimum(r0 + 1, L - R), R), :])
            return _silu(w_ref[0:1, :] * x_prev + w_ref[1:2, :] * x + w_ref[2:3, :] * x_next)

        q = conv_silu(q_ref, cq_ref)
        k = conv_silu(k_ref, ck_ref)
        qs[rows, :] = q * (lax.rsqrt(jnp.sum(q * q, -1, keepdims=True) + RMS_EPS) * (DH ** -0.5))
        ks[rows, :] = k * lax.rsqrt(jnp.sum(k * k, -1, keepdims=True) + RMS_EPS)
        vs[rows, :] = conv_silu(v_ref, cv_ref)

    n_trips = NC // U
    trip_blocks = (U * C) // R

    @pl.when(s < n_items)
    def _():
        for b in range(trip_blocks):
            conv_rows(b * R, first=b == 0, last=(b + 1) * R == L)

    def conv_pieces(i):
        nxt = jnp.minimum(i + 1, n_trips - 1)
        for b in range(trip_blocks):
            r0 = pl.multiple_of((nxt * trip_blocks + b) * R, R)
            conv_rows(r0, last=(r0 + R == L) if b == trip_blocks - 1 else False)
            for _ in range(CONV_SPREAD):
                yield

    ri = lax.broadcasted_iota(I32, (C, C), 0)
    ci = lax.broadcasted_iota(I32, (C, C), 1)
    eye = ri == ci
    eye_f = eye.astype(F32)
    incl = (ri >= ci, ri <= ci)
    strict = (ri > ci, ri < ci)
    n_sq = C.bit_length() - 2
    zero_b = jnp.zeros((C, C), BF16)

    def block_diag(a, b):
        return jnp.concatenate([jnp.concatenate([a, zero_b], axis=1),
                                jnp.concatenate([zero_b, b], axis=1)], axis=0)

    def prep_start(c, j):
        rows = pl.ds(pl.multiple_of(c * C, C), C)
        q = qs[rows, :]
        k = ks[rows, :]
        v = vs[rows, :]
        kq = _dot_nt(jnp.concatenate([k, q], axis=0), k)
        kk, qk = kq[:C], kq[C:]
        a2 = []
        for d in range(2):
            g_row = gt_ref[0, d * H + h, pl.ds(c, 1), :]
            b_row = gt_ref[0, (2 + d) * H + h, pl.ds(c, 1), :]
            m, ms, mt = incl[d], strict[d], incl[1 - d]
            g_col = jnp.sum(jnp.where(eye, g_row, 0.0), axis=1, keepdims=True)
            b_col = jnp.sum(jnp.where(eye, b_row, 0.0), axis=1, keepdims=True)
            gc_col = jnp.sum(jnp.where(m, g_row, 0.0), axis=1, keepdims=True)
            gc_row = jnp.sum(jnp.where(mt, g_col, 0.0), axis=0, keepdims=True)
            tot = jnp.sum(g_row, axis=1, keepdims=True)
            decay = jnp.where(m, jnp.exp(jnp.where(m, gc_col - gc_row, 0.0)), 0.0)
            a2.append(jnp.where(ms, -(kk * b_col * decay), 0.0))
            egc = jnp.exp(gc_col)
            rhs_s[d, j] = jnp.concatenate([v * b_col, k * (b_col * egc)], axis=1).astype(BF16)
            wq_s[slot_p, d, c, pl.ds(C, C), :] = (q * egc).astype(BF16)
            at_s[slot_p, d, c] = (qk * decay).astype(BF16)
            kd_s[slot_p, d, c] = (k * jnp.exp(tot - gc_col)).astype(BF16)
            gl_s[slot_p, d, pl.ds(c, 1), :] = jnp.broadcast_to(jnp.exp(tot), (1, DH))
        return jnp.concatenate(a2, axis=1)

    def prep_phases(i):
        cs = [i * U + j for j in range(U)]
        ps = []
        for j, c in enumerate(cs):
            ps.append(prep_start(c, j))
            if j == U // 2 - 1:
                yield
        eye2 = jnp.concatenate([eye_f, eye_f], axis=1)
        xs_ = [eye2 + p for p in ps]
        yield
        for _ in range(n_sq):
            pbs = [p.astype(BF16) for p in ps]
            ps = [jnp.dot(pb, block_diag(pb[:, :C], pb[:, C:]), preferred_element_type=F32) for pb in pbs]
            yield
            pbs = [p.astype(BF16) for p in ps]
            xs_ = [x + jnp.dot(x.astype(BF16), block_diag(pb[:, :C], pb[:, C:]), preferred_element_type=F32)
                   for x, pb in zip(xs_, pbs)]
            yield
        for j, (c, x) in enumerate(zip(cs, xs_)):
            for d in range(2):
                sol = jnp.dot(x[:, d * C:(d + 1) * C].astype(BF16), rhs_s[d, j], preferred_element_type=F32)
                u_s[slot_p, d, c] = sol[:, :DH]
                wq_s[slot_p, d, c, pl.ds(0, C), :] = sol[:, DH:].astype(BF16)
            if j == U // 2 - 1:
                yield
        yield

    def scan_stages(n):
        cf, cb = n, NC - 1 - n
        s_f, s_b = s_s[0], s_s[1]
        wq = jnp.concatenate([wq_s[slot_s, 0, cf], wq_s[slot_s, 1, cb]], axis=1)
        r1 = jnp.dot(wq, block_diag(s_f.astype(BF16), s_b.astype(BF16)), preferred_element_type=F32)
        yield
        u = jnp.concatenate([u_s[slot_s, 0, cf], u_s[slot_s, 1, cb]], axis=1)
        v_new = (u - r1[:C]).astype(BF16)
        vd = block_diag(v_new[:, :DH], v_new[:, DH:])
        kd = jnp.concatenate([kd_s[slot_s, 0, cf], kd_s[slot_s, 1, cb]], axis=0)
        ds = lax.dot_general(kd, vd, (((0,), (0,)), ((), ())), preferred_element_type=F32)
        at = jnp.concatenate([at_s[slot_s, 0, cf], at_s[slot_s, 1, cb]], axis=1)
        o = r1[C:] + jnp.dot(at, vd, preferred_element_type=F32)
        s_s[0] = s_f * gl_s[slot_s, 0, pl.ds(cf, 1), :] + ds[:, :DH]
        s_s[1] = s_b * gl_s[slot_s, 1, pl.ds(cb, 1), :] + ds[:, DH:]
        o_ref[0, pl.ds(pl.multiple_of(cf * C, C), C), :] = o[:, :DH]
        ob[pl.ds(pl.multiple_of(cb * C, C), C), :] = o[:, DH:]
        yield

    def run(prepare, scan):
        def trip(i, _):
            prep_it = prep_phases(i) if prepare else iter(())
            conv_it = conv_pieces(i) if prepare and n_trips > 1 else iter(())
            scan_it = itertools.chain.from_iterable(scan_stages(i * U + j) for j in range(U)) if scan else iter(())
            for _ in itertools.zip_longest(scan_it, prep_it, conv_it):
                pass
            return 0
        if scan:
            s_s[...] = jnp.zeros_like(s_s)
        lax.fori_loop(0, n_trips, trip, 0)

    @pl.when(s == 0)
    def _():
        run(True, False)

    @pl.when((s > 0) & (s < n_items))
    def _():
        run(True, True)

    @pl.when(s == n_items)
    def _():
        run(False, True)

    def sum_rows(i, _):
        rows = pl.ds(pl.multiple_of(i * R, R), R)
        o_ref[0, rows, :] = o_ref[0, rows, :] + ob[rows, :]
        return 0

    @pl.when(s > 0)
    def _():
        lax.fori_loop(0, L // R, sum_rows, 0)


def _deltanet(proj, gt, conv_w, *, B, L, H):
    DH, C, R = DN_HEAD_DIM, DN_CHUNK, 256
    assert C == DH and L % (C * PREP_UNROLL) == 0
    NC = L // C
    n_items = B * H
    cur = lambda s: jnp.minimum(s, n_items - 1)
    prev = lambda s: jnp.maximum(s - 1, 0)
    col = lambda off: pl.BlockSpec((1, L, DH), lambda s: (off + cur(s) % H, cur(s) // H, 0))
    cw = lambda off: pl.BlockSpec((3, DH), lambda s: (0, off + cur(s) % H))
    kern = functools.partial(_deltanet_kernel, L=L, C=C, H=H, R=R, n_items=n_items)
    return pl.pallas_call(
        kern, grid=(n_items + 1,),
        in_specs=[col(0), col(H), col(2 * H), cw(0), cw(H), cw(2 * H),
                  pl.BlockSpec((1, 4 * H, NC, C), lambda s: (cur(s) // H, 0, 0, 0))],
        out_specs=pl.BlockSpec((1, L, DH), lambda s: (prev(s) // H, 0, prev(s) % H)),
        out_shape=jax.ShapeDtypeStruct((B, L, H * DH), F32),
        scratch_shapes=[pltpu.VMEM((L, DH), F32), pltpu.VMEM((L, DH), F32), pltpu.VMEM((L, DH), F32),
                        pltpu.VMEM((L, DH), F32),
                        pltpu.VMEM((2, 2, NC, 2 * C, DH), BF16), pltpu.VMEM((2, 2, NC, C, C), BF16),
                        pltpu.VMEM((2, 2, NC, C, DH), BF16), pltpu.VMEM((2, 2, NC, C, DH), F32),
                        pltpu.VMEM((2, 2, NC, DH), F32), pltpu.VMEM((2, DH, DH), F32),
                        pltpu.VMEM((2, PREP_UNROLL, C, 2 * DH), BF16)],
        compiler_params=_cparams("arbitrary"), name="deltanet",
    )(proj, proj, proj, conv_w, conv_w, conv_w, gt)


def _conformer_kernel(val_ref, gate_ref, w_ref, b_ref, g_ref, beta_ref, o_ref, ypad, *, L, K, R, HALO):
    pad = K // 2
    zeros = jnp.zeros((HALO, CV_GROUP), F32)
    ypad[pl.ds(0, HALO), :] = zeros
    ypad[pl.ds(HALO + L, HALO), :] = zeros

    def glu_rows(i, _):
        r0 = pl.multiple_of(i * R, R)
        ypad[pl.ds(HALO + r0, R), :] = val_ref[0, pl.ds(r0, R), :] * _sigmoid(gate_ref[0, pl.ds(r0, R), :])
        return 0

    lax.fori_loop(0, L // R, glu_rows, 0)

    def conv_rows(i, _):
        r0 = pl.multiple_of(i * R, R)
        acc = w_ref[0:1, :] * ypad[pl.ds(r0 + (HALO - pad), R), :]
        for t in range(1, K):
            acc = acc + w_ref[t:t + 1, :] * ypad[pl.ds(r0 + (HALO - pad + t), R), :]
        y = _layer_norm(acc + b_ref[...], g_ref[...], beta_ref[...])
        o_ref[0, pl.ds(r0, R), :] = _silu(y)
        return 0

    lax.fori_loop(0, L // R, conv_rows, 0)


def _conformer(proj, w, b, g, beta, *, B, L, col0, width):
    K = w.shape[0]
    n = width // CV_GROUP
    HALO, R = 16, 256
    c0 = col0 // CV_GROUP
    vec = pl.BlockSpec((1, CV_GROUP), lambda bb, j: (0, j))
    kern = functools.partial(_conformer_kernel, L=L, K=K, R=R, HALO=HALO)
    return pl.pallas_call(
        kern, grid=(B, n),
        in_specs=[pl.BlockSpec((1, L, CV_GROUP), lambda bb, j: (c0 + j, bb, 0)),
                  pl.BlockSpec((1, L, CV_GROUP), lambda bb, j: (c0 + n + j, bb, 0)),
                  pl.BlockSpec((K, CV_GROUP), lambda bb, j: (0, j)), vec, vec, vec],
        out_specs=pl.BlockSpec((1, L, CV_GROUP), lambda bb, j: (bb, 0, j)),
        out_shape=jax.ShapeDtypeStruct((B, L, width), F32),
        scratch_shapes=[pltpu.VMEM((L + 2 * HALO, CV_GROUP), F32)],
        compiler_params=_cparams("parallel", "parallel"), name="conformer",
    )(proj, proj, w, b.reshape(1, width), g.reshape(1, width), beta.reshape(1, width))


def _out_router_kernel(dn_ref, z_ref, nw_ref, cv_ref, h_ref, w_ref, g_ref, b_ref, wr_ref, br_ref,
                       h1_ref, h1p_ref, route_ref, routet_ref, cnt_ref, *, alpha, n_dn, sub):
    n_sub = h_ref.shape[0] // sub

    @pl.when(pl.program_id(0) == 0)
    def _():
        cnt_ref[...] = jnp.zeros_like(cnt_ref)

    def stages(s):
        rows = pl.ds(s * sub, sub)
        heads = []
        for hh in range(n_dn // DN_HEAD_DIM):
            o = dn_ref[rows, hh * DN_HEAD_DIM:(hh + 1) * DN_HEAD_DIM]
            o = o * lax.rsqrt(jnp.mean(o * o, -1, keepdims=True) + RMS_EPS) * nw_ref[...]
            heads.append((o * _silu(z_ref[hh, rows, :])).astype(BF16))
        mix = jnp.dot(jnp.concatenate(heads, axis=1), w_ref[pl.ds(0, n_dn), :], preferred_element_type=F32)
        mix = mix + jnp.dot(cv_ref[rows, :].astype(BF16), w_ref[pl.ds(n_dn, w_ref.shape[0] - n_dn), :],
                            preferred_element_type=F32)
        yield
        h1 = _layer_norm(alpha * h_ref[rows, :] + mix, g_ref[...], b_ref[...])
        h1_ref[rows, :] = h1
        h1p_ref[rows, :] = _pack_halves(h1)
        h1b = h1.astype(BF16)
        yield
        logits = jnp.dot(h1b, wr_ref[...], preferred_element_type=F32) + br_ref[...]
        yield
        route = _route(logits)
        route_ref[rows, :] = route
        routet_ref[:, rows] = route.T[:8, :]
        lane = lax.broadcasted_iota(I32, route.shape, 1)
        e1 = route[:, 0:1].astype(I32)
        e2 = route[:, 1:2].astype(I32) + N_EXPERTS
        hits = jnp.where(lane == e1, 1.0, jnp.where(lane == e2, 1.0, 0.0))
        cnt_ref[0:1, :] += jnp.sum(hits, axis=0, keepdims=True)
        yield

    n_stage = 4
    pipes = [stages(s) for s in range(n_sub)]
    for t in range(n_sub + n_stage - 1):
        for s in range(t, t - n_stage, -1):
            if 0 <= s < n_sub:
                next(pipes[s])


def _route(logits):
    lane = lax.broadcasted_iota(I32, logits.shape, 1)
    neg = jnp.float32(-jnp.inf)
    big = jnp.int32(LANES)
    gl = jnp.where(lane < N_GROUPS, logits, neg)
    gmax = jnp.max(gl, -1, keepdims=True)
    gsel = jnp.min(jnp.where(gl == gmax, lane, big), -1, keepdims=True)
    pg = 1.0 / jnp.sum(jnp.exp(gl - gmax), -1, keepdims=True)
    lo = N_GROUPS + gsel * EXPERTS_PER_GROUP
    el = jnp.where((lane >= lo) & (lane < lo + EXPERTS_PER_GROUP), logits, neg)
    e1 = jnp.max(el, -1, keepdims=True)
    i1 = jnp.min(jnp.where(el == e1, lane, big), -1, keepdims=True)
    el2 = jnp.where(lane == i1, neg, el)
    e2 = jnp.max(el2, -1, keepdims=True)
    i2 = jnp.min(jnp.where(el2 == e2, lane, big), -1, keepdims=True)
    r = jnp.exp(e2 - e1)
    p1 = 1.0 / (1.0 + r)
    p2 = r * p1
    route = jnp.where(lane == 0, (i1 - N_GROUPS).astype(F32),
                      jnp.where(lane == 1, (i2 - N_GROUPS).astype(F32),
                                jnp.where(lane == 2, pg * p1, jnp.where(lane == 3, pg * p2, 0.0))))
    return route


def _out_router(dn, proj, z_group, norm_w, cv, h, w_out, g, b, w_router, b_router, *, alpha, tm=512, sub=128):
    T, D = h.shape
    n_dn, n_cv = dn.shape[1], cv.shape[1]
    n_heads = n_dn // DN_HEAD_DIM
    assert z_group % n_heads == 0
    vec = pl.BlockSpec((1, D), lambda i: (0, 0))
    row = pl.BlockSpec((tm, D), lambda i: (i, 0))
    kern = functools.partial(_out_router_kernel, alpha=alpha, n_dn=n_dn, sub=sub)
    return pl.pallas_call(
        kern, grid=(T // tm,),
        in_specs=[pl.BlockSpec((tm, n_dn), lambda i: (i, 0)),
                  pl.BlockSpec((n_heads, tm, DN_HEAD_DIM), lambda i: (z_group // n_heads, i, 0)),
                  pl.BlockSpec((1, DN_HEAD_DIM), lambda i: (0, 0)),
                  pl.BlockSpec((tm, n_cv), lambda i: (i, 0)), row,
                  pl.BlockSpec((n_dn + n_cv, D), lambda i: (0, 0)), vec, vec,
                  pl.BlockSpec((D, LANES), lambda i: (0, 0)), pl.BlockSpec((1, LANES), lambda i: (0, 0))],
        out_specs=[row, pl.BlockSpec((tm, D // 2), lambda i: (i, 0)),
                   pl.BlockSpec((tm, LANES), lambda i: (i, 0)), pl.BlockSpec((8, tm), lambda i: (0, i)),
                   pl.BlockSpec((8, LANES), lambda i: (0, 0))],
        out_shape=[jax.ShapeDtypeStruct((T, D), F32), jax.ShapeDtypeStruct((T, D // 2), U32),
                   jax.ShapeDtypeStruct((T, LANES), F32), jax.ShapeDtypeStruct((8, T), F32),
                   jax.ShapeDtypeStruct((8, LANES), F32)],
        compiler_params=_cparams("arbitrary"), name="out_router",
    )(dn, proj, norm_w.reshape(1, DN_HEAD_DIM), cv, h, w_out, g.reshape(1, D), b.reshape(1, D), w_router, b_router)


def _onehot_t(routet_ref, tr):
    sub = lax.broadcasted_iota(I32, (2 * N_EXPERTS, tr), 0)
    e1 = routet_ref[0:1, :].astype(I32)
    e2 = routet_ref[1:2, :].astype(I32)
    return sub == jnp.where(sub < N_EXPERTS, e1, e2 + N_EXPERTS)


def _place_kernel(routet_ref, base_ref, d1_ref, d2_ref, carry, *, tr):
    @pl.when(pl.program_id(0) == 0)
    def _():
        carry[...] = jnp.zeros_like(carry)
    oh = _onehot_t(routet_ref, tr)
    ohb = oh.astype(F32).astype(BF16)
    ri = lax.broadcasted_iota(I32, (tr, tr), 0)
    ci = lax.broadcasted_iota(I32, (tr, tr), 1)
    before = (ri < ci).astype(F32).astype(BF16)
    excl = jnp.dot(ohb, before, preferred_element_type=F32)
    pos = jnp.where(oh, excl + carry[...] + base_ref[...], 0.0)
    d1 = jnp.sum(pos[:N_EXPERTS], axis=0, keepdims=True)
    d2 = jnp.sum(pos[N_EXPERTS:], axis=0, keepdims=True)
    d1_ref[...] = d1.astype(I32)
    d2_ref[...] = d2.astype(I32)
    carry[...] += jnp.sum(oh.astype(F32), axis=1, keepdims=True)


def _placement(routet, cnt, tr=512):
    T = routet.shape[1]
    E = N_EXPERTS
    rt = pl.BlockSpec((8, tr), lambda i: (0, i))
    col = pl.BlockSpec((2 * E, 1), lambda i: (0, 0))
    c1 = cnt[0, :E].astype(I32)
    c2 = cnt[0, E:].astype(I32)
    padded = ((c1 + c2 + FFN_BLOCK - 1) // FFN_BLOCK) * FFN_BLOCK
    pend = jnp.cumsum(padded)
    pstart = pend - padded
    base = jnp.concatenate([pstart, pstart + c1]).astype(F32).reshape(2 * E, 1)
    lane_row = pl.BlockSpec((1, tr), lambda i: (0, i))
    d1, d2 = pl.pallas_call(
        functools.partial(_place_kernel, tr=tr), grid=(T // tr,), in_specs=[rt, col], out_specs=[lane_row, lane_row],
        out_shape=[jax.ShapeDtypeStruct((1, T), I32)] * 2,
        scratch_shapes=[pltpu.VMEM((2 * E, 1), F32)],
        compiler_params=_cparams("arbitrary"), name="moe_place")(routet, base)
    n_blocks = (T * TOP_K + E * (FFN_BLOCK - 1) + FFN_BLOCK - 1) // FFN_BLOCK
    starts = jnp.arange(n_blocks, dtype=I32) * FFN_BLOCK
    blk_e = jnp.minimum(jnp.sum((pend[None, :] <= starts[:, None]).astype(I32), axis=1), E - 1)
    blk_first = jnp.concatenate([jnp.ones((1,), I32), (blk_e[1:] != blk_e[:-1]).astype(I32)])
    n_used = (pend[-1] // FFN_BLOCK).astype(I32).reshape(1)
    ids = jnp.arange(E, dtype=I32)
    owns = padded > 0
    later = owns[None, :] & (ids[None, :] > ids[:, None])
    next_tab = jnp.min(jnp.where(later, ids[None, :], E), axis=1)
    next_tab = jnp.concatenate([next_tab, jnp.full((1,), E, I32)])
    next2_tab = next_tab[next_tab[:E]]
    hide = lambda t: jnp.where(t == E, -1, t)
    slot_tab = (jnp.cumsum(owns.astype(I32)) - 1) % 2
    steer = (blk_e, blk_first, slot_tab[blk_e], hide(next_tab[:E])[blk_e], hide(next2_tab)[blk_e])
    return d1.reshape(T), d2.reshape(T), pend.astype(I32), steer, n_used, n_blocks


def _dispatch_kernel(d1_ref, d2_ref, pend_ref, src_ref, xs_ref, zbuf, sem, zsem, *, tb):
    @pl.when(pl.program_id(0) == 0)
    def _():
        zbuf[...] = jnp.zeros_like(zbuf)
        tails = []
        for e in range(N_EXPERTS):
            end = pend_ref[e]
            nonempty = end > (pend_ref[e - 1] if e else 0)
            start = pl.multiple_of(jnp.maximum(end - FFN_BLOCK, 0), FFN_BLOCK)
            tails.append((nonempty, pltpu.make_async_copy(zbuf, xs_ref.at[pl.ds(start, FFN_BLOCK), :], zsem)))
        for nonempty, copy in tails:
            pl.when(nonempty)(copy.start)

        def spare(b):
            return pltpu.make_async_copy(
                zbuf, xs_ref.at[pl.ds(pl.multiple_of(b * FFN_BLOCK, FFN_BLOCK), FFN_BLOCK), :], zsem)
        first_spare = pend_ref[N_EXPERTS - 1] // FFN_BLOCK
        n_blocks = xs_ref.shape[0] // FFN_BLOCK
        lax.fori_loop(first_spare, n_blocks, lambda b, c: (spare(b).start(), c)[1], 0)
        for nonempty, copy in tails:
            pl.when(nonempty)(copy.wait)
        lax.fori_loop(first_spare, n_blocks, lambda b, c: (spare(b).wait(), c)[1], 0)

    def issue(t, _):
        row = src_ref.at[pl.ds(t, 1), :]
        pltpu.make_async_copy(row, xs_ref.at[pl.ds(d1_ref[t], 1), :], sem).start(priority=0)
        pltpu.make_async_copy(row, xs_ref.at[pl.ds(d2_ref[t], 1), :], sem).start(priority=1)
        return 0

    lax.fori_loop(0, tb, issue, 0, unroll=8)
    pltpu.make_async_copy(src_ref, xs_ref.at[pl.ds(0, tb), :], sem).wait()
    pltpu.make_async_copy(src_ref, xs_ref.at[pl.ds(0, tb), :], sem).wait()


def _dispatch(h1p, d1, d2, pend, n_rows, tb=2048):
    T, W = h1p.shape
    idx = pl.BlockSpec((tb,), lambda i: (i,), memory_space=pltpu.SMEM)
    return pl.pallas_call(
        functools.partial(_dispatch_kernel, tb=tb), grid=(T // tb,),
        in_specs=[idx, idx, pl.BlockSpec((N_EXPERTS,), lambda i: (0,), memory_space=pltpu.SMEM),
                  pl.BlockSpec((tb, W), lambda i: (i, 0))],
        out_specs=pl.BlockSpec(memory_space=pl.ANY),
        out_shape=jax.ShapeDtypeStruct((n_rows, W), U32),
        scratch_shapes=[pltpu.VMEM((FFN_BLOCK, W), U32), pltpu.SemaphoreType.DMA(()), pltpu.SemaphoreType.DMA(())],
        compiler_params=_cparams("arbitrary"), name="moe_dispatch",
    )(d1, d2, pend, h1p)


def _cast_rows(src, dst):
    rows = CAST_CHUNK_ELEMS // src.shape[1]

    def body(i, c):
        r = pl.ds(pl.multiple_of(i * rows, rows), rows)
        dst[r, :] = src[r, :].astype(BF16)
        return c

    lax.fori_loop(0, src.shape[0] // rows, body, 0, unroll=4)


def _ffn_kernel(blk_e_ref, blk_first_ref, blk_slot_ref, blk_next_ref, blk_next2_ref, n_used_ref,
                xs_ref, wgu_hbm, wdn_hbm, y_ref, wgu_f, wdn_f, wgu_s, wdn_s, sem, *, ff, layer):

    def weight_copies(e, slot):
        copies = []
        for k, (src, dst) in enumerate(((wgu_hbm, wgu_f), (wdn_hbm, wdn_f))):
            rows = dst.shape[1] // FFN_WEIGHT_CHUNKS
            for c in range(FFN_WEIGHT_CHUNKS):
                r = pl.ds(c * rows, rows)
                copies.append(pltpu.make_async_copy(src.at[layer, e, r, :], dst.at[slot, r, :],
                                                    sem.at[slot, k * FFN_WEIGHT_CHUNKS + c]))
        return copies

    def start(e, slot):
        for c in weight_copies(e, slot):
            c.start(priority=1)

    @pl.when(pl.program_id(0) == 0)
    def _():
        start(blk_e_ref[0], 0)

        @pl.when(blk_next_ref[0] >= 0)
        def _():
            start(blk_next_ref[0], 1)

    for j in range(FFN_BLOCKS_PER_STEP):
        i = pl.program_id(0) * FFN_BLOCKS_PER_STEP + j
        rows = pl.ds(j * FFN_BLOCK, FFN_BLOCK)
        used = i < n_used_ref[0]

        @pl.when(used & (blk_first_ref[i] == 1))
        def _():
            slot = blk_slot_ref[i]
            for c in weight_copies(blk_e_ref[i], slot):
                c.wait()
            _cast_rows(wgu_f.at[slot], wgu_s)
            _cast_rows(wdn_f.at[slot], wdn_s)

            @pl.when(blk_next2_ref[i] >= 0)
            def _():
                start(blk_next2_ref[i], slot)

        @pl.when(used)
        def _():
            lo, hi = _unpack_halves(xs_ref[rows, :])
            x = jnp.concatenate([lo, hi], axis=1).astype(BF16)
            gu = jnp.dot(x, wgu_s[...], preferred_element_type=F32)
            act = (_silu(gu[:, :ff]) * gu[:, ff:]).astype(BF16)
            y_ref[rows, :] = _pack_halves(jnp.dot(act, wdn_s[...], preferred_element_type=F32))

        @pl.when(jnp.logical_not(used))
        def _():
            y_ref[rows, :] = jnp.zeros((FFN_BLOCK, y_ref.shape[1]), y_ref.dtype)


def _ffn(xs, w_gu, w_dn, layer, blk_e, blk_first, blk_slot, blk_next, blk_next2, n_used, n_blocks):
    _, E, D, FF2 = w_gu.shape
    ff = FF2 // 2
    W = xs.shape[1]
    assert n_blocks % FFN_BLOCKS_PER_STEP == 0
    step_rows = FFN_BLOCK * FFN_BLOCKS_PER_STEP
    any_spec = pl.BlockSpec(memory_space=pl.ANY)
    last_used_step = lambda nu: (nu[0] - 1) // FFN_BLOCKS_PER_STEP
    grid_spec = pltpu.PrefetchScalarGridSpec(
        num_scalar_prefetch=6, grid=(n_blocks // FFN_BLOCKS_PER_STEP,),
        in_specs=[pl.BlockSpec((step_rows, W), lambda i, be, bf, bs, bn, bn2, nu: (jnp.minimum(i, last_used_step(nu)), 0)),
                  any_spec, any_spec],
        out_specs=pl.BlockSpec((step_rows, W), lambda i, *_: (i, 0)),
        scratch_shapes=[pltpu.VMEM((2, D, FF2), F32), pltpu.VMEM((2, ff, D), F32),
                        pltpu.VMEM((D, FF2), BF16), pltpu.VMEM((ff, D), BF16),
                        pltpu.SemaphoreType.DMA((2, 2 * FFN_WEIGHT_CHUNKS))])
    return pl.pallas_call(
        functools.partial(_ffn_kernel, ff=ff, layer=layer), grid_spec=grid_spec,
        out_shape=jax.ShapeDtypeStruct(xs.shape, U32),
        compiler_params=_cparams("arbitrary"), name="moe_ffn",
    )(blk_e, blk_first, blk_slot, blk_next, blk_next2, n_used, xs, w_gu, w_dn)


def _combine_kernel(d1_ref, d2_ref, d1n_ref, d2n_ref, y_ref, h1_ref, route_ref, g_ref, b_ref, h2_ref, h2b_ref,
                    ybuf, sem, *, tc, alpha):
    i = pl.program_id(0)
    slot = i % 2

    def row_copies(r1, r2, s, t):
        return (pltpu.make_async_copy(y_ref.at[pl.ds(r1[t], 1), :], ybuf.at[s, 0, pl.ds(t, 1), :], sem.at[s]),
                pltpu.make_async_copy(y_ref.at[pl.ds(r2[t], 1), :], ybuf.at[s, 1, pl.ds(t, 1), :], sem.at[s]))

    def wait_slot(s):
        for j in range(TOP_K):
            pltpu.make_async_copy(y_ref.at[pl.ds(0, tc), :], ybuf.at[s, j], sem.at[s]).wait()

    @pl.when(i == 0)
    def _():
        def issue(t, _):
            for j, c in enumerate(row_copies(d1_ref, d2_ref, 0, t)):
                c.start(priority=j)
            return 0
        lax.fori_loop(0, tc, issue, 0, unroll=8)

    wait_slot(slot)
    for t in range(tc):
        for j, c in enumerate(row_copies(d1n_ref, d2n_ref, 1 - slot, t)):
            c.start(priority=j)

    route = route_ref[...]
    g1 = route[:, 2:3]
    g2 = route[:, 3:4]
    lo1, hi1 = _unpack_halves(ybuf[slot, 0])
    lo2, hi2 = _unpack_halves(ybuf[slot, 1])
    ffn = jnp.concatenate([g1 * lo1 + g2 * lo2, g1 * hi1 + g2 * hi2], axis=1)
    h2 = _layer_norm(alpha * h1_ref[...] + ffn, g_ref[...], b_ref[...])
    h2_ref[...] = h2
    h2b_ref[...] = h2.astype(BF16)

    @pl.when(i == pl.num_programs(0) - 1)
    def _():
        wait_slot(1 - slot)


def _combine(y, d1, d2, h1, route, g, b, *, alpha, tc=256):
    T, D = h1.shape
    W = y.shape[1]
    n = T // tc
    idx = pl.BlockSpec((tc,), lambda i: (i,), memory_space=pltpu.SMEM)
    idx_next = pl.BlockSpec((tc,), lambda i: (jnp.minimum(i + 1, n - 1),), memory_space=pltpu.SMEM)
    row = pl.BlockSpec((tc, D), lambda i: (i, 0))
    vec = pl.BlockSpec((1, D), lambda i: (0, 0))
    return pl.pallas_call(
        functools.partial(_combine_kernel, tc=tc, alpha=alpha), grid=(n,),
        in_specs=[idx, idx, idx_next, idx_next, pl.BlockSpec(memory_space=pl.ANY), row,
                  pl.BlockSpec((tc, LANES), lambda i: (i, 0)), vec, vec],
        out_specs=[row, row],
        out_shape=[jax.ShapeDtypeStruct((T, D), F32), jax.ShapeDtypeStruct((T, D), BF16)],
        scratch_shapes=[pltpu.VMEM((2, TOP_K, tc, W), U32), pltpu.SemaphoreType.DMA((2,))],
        compiler_params=_cparams("arbitrary"), name="moe_combine",
    )(d1, d2, d1, d2, y, h1, route, g.reshape(1, D), b.reshape(1, D))


def kernel(x, emb_ln_g, emb_ln_b, w_in, short_conv_w, a_log, dt_bias, dn_norm_w, dw_conv_w, dw_conv_b,
           conv_ln_g, conv_ln_b, w_out, ln1_g, ln1_b, w_group, b_group, w_expert, b_expert, w_gate_up,
           w_down, ln2_g, ln2_b):
    B, L, D = x.shape
    T = B * L
    depth = w_in.shape[0]
    H = a_log.shape[2]
    dn_w = H * DN_HEAD_DIM
    cv_w = dw_conv_w.shape[2]
    n_ab = 4 * H
    alpha = (2 * depth) ** 0.25
    C = DN_CHUNK

    h, hb = _emb_ln(x.reshape(T, D), emb_ln_g, emb_ln_b)
    for l in range(depth):
        w = w_in[l]
        w_main = jnp.concatenate([w[:, :4 * dn_w], w[:, 4 * dn_w + n_ab:]], axis=1).astype(BF16)
        proj, gt = _in_proj(hb, w_main, w[:, 4 * dn_w:4 * dn_w + n_ab], a_log[l], dt_bias[l], B=B, L=L)
        gt = gt.reshape(B, n_ab, L // C, C)
        dn = _deltanet(proj, gt, short_conv_w[l], B=B, L=L, H=H)
        cv = _conformer(proj, dw_conv_w[l], dw_conv_b[l], conv_ln_g[l], conv_ln_b[l],
                        B=B, L=L, col0=4 * dn_w, width=cv_w)
        n_r = N_GROUPS + N_EXPERTS
        w_router = jnp.pad(jnp.concatenate([w_group[l], w_expert[l]], axis=1),
                           ((0, 0), (0, LANES - n_r))).astype(BF16)
        b_router = jnp.pad(jnp.concatenate([b_group[l], b_expert[l]]), (0, LANES - n_r)).reshape(1, LANES)
        h1, h1p, route, routet, cnt = _out_router(
            dn.reshape(T, dn_w), proj, 3 * H, dn_norm_w[l], cv.reshape(T, cv_w), h, w_out[l].astype(BF16),
            ln1_g[l], ln1_b[l],
            w_router, b_router, alpha=alpha)
        d1, d2, pend, steer, n_used, n_blocks = _placement(routet, cnt)
        xs = _dispatch(h1p, d1, d2, pend, n_blocks * FFN_BLOCK)
        y = _ffn(xs, w_gate_up, w_down, l, *steer, n_used, n_blocks)
        h, hb = _combine(y, d1, d2, h1, route, ln2_g[l], ln2_b[l], alpha=alpha)
    return h.reshape(B, L, D)
```

```python
import functools
import itertools

import jax
import jax.numpy as jnp
from jax import lax
from jax.experimental import pallas as pl
from jax.experimental.pallas import tpu as pltpu

F32 = jnp.float32
BF16 = jnp.bfloat16
U32 = jnp.uint32
I32 = jnp.int32

LANES = 128
DN_HEAD_DIM = 128
DN_CHUNK = 128
PREP_UNROLL = 8
CONV_SPREAD = 4
CV_GROUP = 128
N_GROUPS = 8
EXPERTS_PER_GROUP = 8
N_EXPERTS = N_GROUPS * EXPERTS_PER_GROUP
TOP_K = 2
FFN_BLOCK = 256
FFN_BLOCKS_PER_STEP = 4
FFN_WEIGHT_CHUNKS = 4
CAST_CHUNK_ELEMS = 32 * 1024
LN_EPS = 1e-5
RMS_EPS = 1e-6
VMEM_LIMIT = 56 * 1024 * 1024


def _cparams(*sem):
    return pltpu.CompilerParams(dimension_semantics=sem, vmem_limit_bytes=VMEM_LIMIT)


def _layer_norm(x, g, b):
    mu = jnp.mean(x, -1, keepdims=True)
    xc = x - mu
    var = jnp.mean(xc * xc, -1, keepdims=True)
    return xc * lax.rsqrt(var + LN_EPS) * g + b


def _sigmoid(x):
    return 1.0 / (1.0 + jnp.exp(-x))


def _silu(x):
    return x * _sigmoid(x)


def _pack_halves(y):
    n = y.shape[1] // 2
    bits = lax.bitcast_convert_type(y.astype(BF16).astype(F32), U32)
    return (bits[:, :n] >> 16) | bits[:, n:]


def _unpack_halves(u):
    lo = lax.bitcast_convert_type(u << 16, F32)
    hi = lax.bitcast_convert_type(u & jnp.uint32(0xFFFF0000), F32)
    return lo, hi


def _emb_ln_kernel(x_ref, g_ref, b_ref, h_ref, hb_ref):
    h = _layer_norm(x_ref[...], g_ref[...], b_ref[...])
    h_ref[...] = h
    hb_ref[...] = h.astype(BF16)


def _emb_ln(x, g, b, tm=1024):
    T, D = x.shape
    row = pl.BlockSpec((tm, D), lambda i: (i, 0))
    vec = pl.BlockSpec((1, D), lambda i: (0, 0))
    return pl.pallas_call(
        _emb_ln_kernel, grid=(T // tm,), in_specs=[row, vec, vec], out_specs=[row, row],
        out_shape=[jax.ShapeDtypeStruct((T, D), F32), jax.ShapeDtypeStruct((T, D), BF16)],
        compiler_params=_cparams("parallel"), name="emb_ln")(x, g.reshape(1, D), b.reshape(1, D))


def _in_proj_kernel(x_ref, w_ref, wab_ref, alog_ref, dtb_ref, o_ref, gt_ref, *, n_decay, n_gate_rows):
    @pl.when(pl.program_id(1) == 0)
    def _():
        ab = jnp.dot(x_ref[...], wab_ref[...], preferred_element_type=F32)
        lane = lax.broadcasted_iota(I32, ab.shape, 1)
        sp = ab + dtb_ref[...]
        softplus = jnp.maximum(sp, 0.0) + jnp.log(1.0 + jnp.exp(-jnp.abs(sp)))
        decay = -jnp.exp(alog_ref[...]) * softplus
        gt_ref[0] = jnp.where(lane < n_decay, decay, _sigmoid(ab)).T[:n_gate_rows, :]

    acc = jnp.dot(x_ref[...], w_ref[...], preferred_element_type=F32)
    for j in range(o_ref.shape[0]):
        o_ref[j] = acc[:, j * LANES:(j + 1) * LANES]


def _in_proj(hb, w, w_ab, a_log, dt_bias, *, B, L, tm=1024, tn=2048):
    T, D = hb.shape
    N = w.shape[1]
    n = a_log.size
    n_gate_rows = 2 * n
    pad = lambda v: jnp.pad(v.reshape(1, n).astype(F32), ((0, 0), (0, LANES - n)))
    wab = jnp.pad(w_ab, ((0, 0), (0, LANES - w_ab.shape[1]))).astype(BF16)
    vec = pl.BlockSpec((1, LANES), lambda i, j: (0, 0))
    per_batch = L // tm
    return pl.pallas_call(
        functools.partial(_in_proj_kernel, n_decay=n, n_gate_rows=n_gate_rows), grid=(T // tm, N // tn),
        in_specs=[pl.BlockSpec((tm, D), lambda i, j: (i, 0)), pl.BlockSpec((D, tn), lambda i, j: (0, j)),
                  pl.BlockSpec((D, LANES), lambda i, j: (0, 0)), vec, vec],
        out_specs=[pl.BlockSpec((tn // LANES, tm, LANES), lambda i, j: (j, i, 0)),
                   pl.BlockSpec((1, n_gate_rows, tm), lambda i, j: (i // per_batch, 0, i % per_batch))],
        out_shape=[jax.ShapeDtypeStruct((N // LANES, T, LANES), F32),
                   jax.ShapeDtypeStruct((B, n_gate_rows, L), F32)],
        compiler_params=_cparams("parallel", "arbitrary"), name="in_proj",
    )(hb, w, wab, pad(a_log), pad(dt_bias))


def _dot(a, b):
    return jnp.dot(a.astype(BF16), b.astype(BF16), preferred_element_type=F32)


def _dot_nt(a, b):
    return lax.dot_general(a.astype(BF16), b.astype(BF16), (((1,), (1,)), ((), ())),
                           preferred_element_type=F32)


def _dot_tn(a, b):
    return lax.dot_general(a.astype(BF16), b.astype(BF16), (((0,), (0,)), ((), ())),
                           preferred_element_type=F32)


def _deltanet_kernel(q_ref, k_ref, v_ref, cq_ref, ck_ref, cv_ref, gt_ref, o_ref,
                     qs, ks, vs, ob, wq_s, at_s, kd_s, u_s, gl_s, s_s, rhs_s, *, L, C, H, R, n_items):
    DH = DN_HEAD_DIM
    s = pl.program_id(0)
    h = jnp.minimum(s, n_items - 1) % H
    slot_p = s % 2
    slot_s = 1 - slot_p
    NC = L // C
    U = PREP_UNROLL

    def conv_rows(r0, first=False, last=False):
        rows = pl.ds(r0, R)
        rid = lax.broadcasted_iota(I32, (R, DH), 0)

        def conv_silu(ref, w_ref):
            x = ref[0, rows, :]
            if first:
                x_prev = jnp.where(rid == 0, 0.0, pltpu.roll(x, 1, 0))
            else:
                x_prev = ref[0, pl.ds(r0 - 1, R), :]
            if last is False:
                x_next = ref[0, pl.ds(r0 + 1, R), :]
            else:
                at_end = jnp.where(rid == R - 1, 0.0, pltpu.roll(x, R - 1, 0))
                x_next = at_end if last is True else jnp.where(
                    last, at_end, ref[0, pl.ds(jnp.minimum(r0 + 1, L - R), R), :])
            return _silu(w_ref[0:1, :] * x_prev + w_ref[1:2, :] * x + w_ref[2:3, :] * x_next)

        q = conv_silu(q_ref, cq_ref)
        k = conv_silu(k_ref, ck_ref)
        qs[rows, :] = q * (lax.rsqrt(jnp.sum(q * q, -1, keepdims=True) + RMS_EPS) * (DH ** -0.5))
        ks[rows, :] = k * lax.rsqrt(jnp.sum(k * k, -1, keepdims=True) + RMS_EPS)
        vs[rows, :] = conv_silu(v_ref, cv_ref)

    n_trips = NC // U
    trip_blocks = (U * C) // R

    @pl.when(s < n_items)
    def _():
        for b in range(trip_blocks):
            conv_rows(b * R, first=b == 0, last=(b + 1) * R == L)

    def conv_pieces(i):
        nxt = jnp.minimum(i + 1, n_trips - 1)
        for b in range(trip_blocks):
            r0 = pl.multiple_of((nxt * trip_blocks + b) * R, R)
            conv_rows(r0, last=(r0 + R == L) if b == trip_blocks - 1 else False)
            for _ in range(CONV_SPREAD):
                yield

    ri = lax.broadcasted_iota(I32, (C, C), 0)
    ci = lax.broadcasted_iota(I32, (C, C), 1)
    eye = ri == ci
    eye_f = eye.astype(F32)
    incl = (ri >= ci, ri <= ci)
    strict = (ri > ci, ri < ci)
    n_sq = C.bit_length() - 2
    zero_b = jnp.zeros((C, C), BF16)

    def block_diag(a, b):
        return jnp.concatenate([jnp.concatenate([a, zero_b], axis=1),
                                jnp.concatenate([zero_b, b], axis=1)], axis=0)

    def prep_start(c, j):
        rows = pl.ds(pl.multiple_of(c * C, C), C)
        q = qs[rows, :]
        k = ks[rows, :]
        v = vs[rows, :]
        kq = _dot_nt(jnp.concatenate([k, q], axis=0), k)
        kk, qk = kq[:C], kq[C:]
        a2 = []
        for d in range(2):
            g_row = gt_ref[0, d * H + h, pl.ds(c, 1), :]
            b_row = gt_ref[0, (2 + d) * H + h, pl.ds(c, 1), :]
            m, ms, mt = incl[d], strict[d], incl[1 - d]
            g_col = jnp.sum(jnp.where(eye, g_row, 0.0), axis=1, keepdims=True)
            b_col = jnp.sum(jnp.where(eye, b_row, 0.0), axis=1, keepdims=True)
            gc_col = jnp.sum(jnp.where(m, g_row, 0.0), axis=1, keepdims=True)
            gc_row = jnp.sum(jnp.where(mt, g_col, 0.0), axis=0, keepdims=True)
            tot = jnp.sum(g_row, axis=1, keepdims=True)
            decay = jnp.where(m, jnp.exp(jnp.where(m, gc_col - gc_row, 0.0)), 0.0)
            a2.append(jnp.where(ms, -(kk * b_col * decay), 0.0))
            egc = jnp.exp(gc_col)
            rhs_s[d, j] = jnp.concatenate([v * b_col, k * (b_col * egc)], axis=1).astype(BF16)
            wq_s[slot_p, d, c, pl.ds(C, C), :] = (q * egc).astype(BF16)
            at_s[slot_p, d, c] = (qk * decay).astype(BF16)
            kd_s[slot_p, d, c] = (k * jnp.exp(tot - gc_col)).astype(BF16)
            gl_s[slot_p, d, pl.ds(c, 1), :] = jnp.broadcast_to(jnp.exp(tot), (1, DH))
        return jnp.concatenate(a2, axis=1)

    def prep_phases(i):
        cs = [i * U + j for j in range(U)]
        ps = []
        for j, c in enumerate(cs):
            ps.append(prep_start(c, j))
            if j == U // 2 - 1:
                yield
        eye2 = jnp.concatenate([eye_f, eye_f], axis=1)
        xs_ = [eye2 + p for p in ps]
        yield
        for _ in range(n_sq):
            pbs = [p.astype(BF16) for p in ps]
            ps = [jnp.dot(pb, block_diag(pb[:, :C], pb[:, C:]), preferred_element_type=F32) for pb in pbs]
            yield
            pbs = [p.astype(BF16) for p in ps]
            xs_ = [x + jnp.dot(x.astype(BF16), block_diag(pb[:, :C], pb[:, C:]), preferred_element_type=F32)
                   for x, pb in zip(xs_, pbs)]
            yield
        for j, (c, x) in enumerate(zip(cs, xs_)):
            for d in range(2):
                sol = jnp.dot(x[:, d * C:(d + 1) * C].astype(BF16), rhs_s[d, j], preferred_element_type=F32)
                u_s[slot_p, d, c] = sol[:, :DH]
                wq_s[slot_p, d, c, pl.ds(0, C), :] = sol[:, DH:].astype(BF16)
            if j == U // 2 - 1:
                yield
        yield

    def scan_stages(n):
        cf, cb = n, NC - 1 - n
        s_f, s_b = s_s[0], s_s[1]
        sd = block_diag(s_f.astype(BF16), s_b.astype(BF16))
        wq = jnp.concatenate([wq_s[slot_s, 0, cf], wq_s[slot_s, 1, cb]], axis=1)
        ws = jnp.dot(wq[:C], sd, preferred_element_type=F32)
        yield
        u = jnp.concatenate([u_s[slot_s, 0, cf], u_s[slot_s, 1, cb]], axis=1)
        v_new = (u - ws).astype(BF16)
        vd = block_diag(v_new[:, :DH], v_new[:, DH:])
        kd = jnp.concatenate([kd_s[slot_s, 0, cf], kd_s[slot_s, 1, cb]], axis=0)
        ds = lax.dot_general(kd, vd, (((0,), (0,)), ((), ())), preferred_element_type=F32)
        at = jnp.concatenate([at_s[slot_s, 0, cf], at_s[slot_s, 1, cb]], axis=1)
        o = jnp.dot(wq[C:], sd, preferred_element_type=F32) + jnp.dot(at, vd, preferred_element_type=F32)
        s_s[0] = s_f * gl_s[slot_s, 0, pl.ds(cf, 1), :] + ds[:, :DH]
        s_s[1] = s_b * gl_s[slot_s, 1, pl.ds(cb, 1), :] + ds[:, DH:]
        o_ref[0, pl.ds(pl.multiple_of(cf * C, C), C), :] = o[:, :DH]
        ob[pl.ds(pl.multiple_of(cb * C, C), C), :] = o[:, DH:]
        yield

    def run(prepare, scan):
        def trip(i, _):
            prep_it = prep_phases(i) if prepare else iter(())
            conv_it = conv_pieces(i) if prepare and n_trips > 1 else iter(())
            scan_it = itertools.chain.from_iterable(scan_stages(i * U + j) for j in range(U)) if scan else iter(())
            for _ in itertools.zip_longest(scan_it, prep_it, conv_it):
                pass
            return 0
        if scan:
            s_s[...] = jnp.zeros_like(s_s)
        lax.fori_loop(0, n_trips, trip, 0)

    @pl.when(s == 0)
    def _():
        run(True, False)

    @pl.when((s > 0) & (s < n_items))
    def _():
        run(True, True)

    @pl.when(s == n_items)
    def _():
        run(False, True)

    def sum_rows(i, _):
        rows = pl.ds(pl.multiple_of(i * R, R), R)
        o_ref[0, rows, :] = o_ref[0, rows, :] + ob[rows, :]
        return 0

    @pl.when(s > 0)
    def _():
        lax.fori_loop(0, L // R, sum_rows, 0)


def _deltanet(proj, gt, conv_w, *, B, L, H):
    DH, C, R = DN_HEAD_DIM, DN_CHUNK, 256
    assert C == DH and L % (C * PREP_UNROLL) == 0
    NC = L // C
    n_items = B * H
    cur = lambda s: jnp.minimum(s, n_items - 1)
    prev = lambda s: jnp.maximum(s - 1, 0)
    col = lambda off: pl.BlockSpec((1, L, DH), lambda s: (off + cur(s) % H, cur(s) // H, 0))
    cw = lambda off: pl.BlockSpec((3, DH), lambda s: (0, off + cur(s) % H))
    kern = functools.partial(_deltanet_kernel, L=L, C=C, H=H, R=R, n_items=n_items)
    return pl.pallas_call(
        kern, grid=(n_items + 1,),
        in_specs=[col(0), col(H), col(2 * H), cw(0), cw(H), cw(2 * H),
                  pl.BlockSpec((1, 4 * H, NC, C), lambda s: (cur(s) // H, 0, 0, 0))],
        out_specs=pl.BlockSpec((1, L, DH), lambda s: (prev(s) // H, 0, prev(s) % H)),
        out_shape=jax.ShapeDtypeStruct((B, L, H * DH), F32),
        scratch_shapes=[pltpu.VMEM((L, DH), F32), pltpu.VMEM((L, DH), F32), pltpu.VMEM((L, DH), F32),
                        pltpu.VMEM((L, DH), F32),
                        pltpu.VMEM((2, 2, NC, 2 * C, DH), BF16), pltpu.VMEM((2, 2, NC, C, C), BF16),
                        pltpu.VMEM((2, 2, NC, C, DH), BF16), pltpu.VMEM((2, 2, NC, C, DH), F32),
                        pltpu.VMEM((2, 2, NC, DH), F32), pltpu.VMEM((2, DH, DH), F32),
                        pltpu.VMEM((2, PREP_UNROLL, C, 2 * DH), BF16)],
        compiler_params=_cparams("arbitrary"), name="deltanet",
    )(proj, proj, proj, conv_w, conv_w, conv_w, gt)


def _conformer_kernel(val_ref, gate_ref, w_ref, b_ref, g_ref, beta_ref, o_ref, ypad, *, L, K, R, HALO):
    pad = K // 2
    zeros = jnp.zeros((HALO, CV_GROUP), F32)
    ypad[pl.ds(0, HALO), :] = zeros
    ypad[pl.ds(HALO + L, HALO), :] = zeros

    def glu_rows(i, _):
        r0 = pl.multiple_of(i * R, R)
        ypad[pl.ds(HALO + r0, R), :] = val_ref[0, pl.ds(r0, R), :] * _sigmoid(gate_ref[0, pl.ds(r0, R), :])
        return 0

    lax.fori_loop(0, L // R, glu_rows, 0)

    def conv_rows(i, _):
        r0 = pl.multiple_of(i * R, R)
        acc = w_ref[0:1, :] * ypad[pl.ds(r0 + (HALO - pad), R), :]
        for t in range(1, K):
            acc = acc + w_ref[t:t + 1, :] * ypad[pl.ds(r0 + (HALO - pad + t), R), :]
        y = _layer_norm(acc + b_ref[...], g_ref[...], beta_ref[...])
        o_ref[0, pl.ds(r0, R), :] = _silu(y)
        return 0

    lax.fori_loop(0, L // R, conv_rows, 0)


def _conformer(proj, w, b, g, beta, *, B, L, col0, width):
    K = w.shape[0]
    n = width // CV_GROUP
    HALO, R = 16, 256
    c0 = col0 // CV_GROUP
    vec = pl.BlockSpec((1, CV_GROUP), lambda bb, j: (0, j))
    kern = functools.partial(_conformer_kernel, L=L, K=K, R=R, HALO=HALO)
    return pl.pallas_call(
        kern, grid=(B, n),
        in_specs=[pl.BlockSpec((1, L, CV_GROUP), lambda bb, j: (c0 + j, bb, 0)),
                  pl.BlockSpec((1, L, CV_GROUP), lambda bb, j: (c0 + n + j, bb, 0)),
                  pl.BlockSpec((K, CV_GROUP), lambda bb, j: (0, j)), vec, vec, vec],
        out_specs=pl.BlockSpec((1, L, CV_GROUP), lambda bb, j: (bb, 0, j)),
        out_shape=jax.ShapeDtypeStruct((B, L, width), F32),
        scratch_shapes=[pltpu.VMEM((L + 2 * HALO, CV_GROUP), F32)],
        compiler_params=_cparams("parallel", "parallel"), name="conformer",
    )(proj, proj, w, b.reshape(1, width), g.reshape(1, width), beta.reshape(1, width))


def _out_router_kernel(dn_ref, z_ref, nw_ref, cv_ref, h_ref, w_ref, g_ref, b_ref, wr_ref, br_ref,
                       h1_ref, h1p_ref, route_ref, routet_ref, cnt_ref, *, alpha, n_dn, sub):
    n_sub = h_ref.shape[0] // sub

    @pl.when(pl.program_id(0) == 0)
    def _():
        cnt_ref[...] = jnp.zeros_like(cnt_ref)

    def stages(s):
        rows = pl.ds(s * sub, sub)
        heads = []
        for hh in range(n_dn // DN_HEAD_DIM):
            o = dn_ref[rows, hh * DN_HEAD_DIM:(hh + 1) * DN_HEAD_DIM]
            o = o * lax.rsqrt(jnp.mean(o * o, -1, keepdims=True) + RMS_EPS) * nw_ref[...]
            heads.append((o * _silu(z_ref[hh, rows, :])).astype(BF16))
        mix = jnp.dot(jnp.concatenate(heads, axis=1), w_ref[pl.ds(0, n_dn), :], preferred_element_type=F32)
        mix = mix + jnp.dot(cv_ref[rows, :].astype(BF16), w_ref[pl.ds(n_dn, w_ref.shape[0] - n_dn), :],
                            preferred_element_type=F32)
        yield
        h1 = _layer_norm(alpha * h_ref[rows, :] + mix, g_ref[...], b_ref[...])
        h1_ref[rows, :] = h1
        h1p_ref[rows, :] = _pack_halves(h1)
        h1b = h1.astype(BF16)
        yield
        logits = jnp.dot(h1b, wr_ref[...], preferred_element_type=F32) + br_ref[...]
        yield
        route = _route(logits)
        route_ref[rows, :] = route
        routet_ref[:, rows] = route.T[:8, :]
        lane = lax.broadcasted_iota(I32, route.shape, 1)
        e1 = route[:, 0:1].astype(I32)
        e2 = route[:, 1:2].astype(I32) + N_EXPERTS
        hits = jnp.where(lane == e1, 1.0, jnp.where(lane == e2, 1.0, 0.0))
        cnt_ref[0:1, :] += jnp.sum(hits, axis=0, keepdims=True)
        yield

    n_stage = 4
    pipes = [stages(s) for s in range(n_sub)]
    for t in range(n_sub + n_stage - 1):
        for s in range(t, t - n_stage, -1):
            if 0 <= s < n_sub:
                next(pipes[s])


def _route(logits):
    lane = lax.broadcasted_iota(I32, logits.shape, 1)
    neg = jnp.float32(-jnp.inf)
    big = jnp.int32(LANES)
    gl = jnp.where(lane < N_GROUPS, logits, neg)
    gmax = jnp.max(gl, -1, keepdims=True)
    gsel = jnp.min(jnp.where(gl == gmax, lane, big), -1, keepdims=True)
    pg = 1.0 / jnp.sum(jnp.exp(gl - gmax), -1, keepdims=True)
    lo = N_GROUPS + gsel * EXPERTS_PER_GROUP
    el = jnp.where((lane >= lo) & (lane < lo + EXPERTS_PER_GROUP), logits, neg)
    e1 = jnp.max(el, -1, keepdims=True)
    i1 = jnp.min(jnp.where(el == e1, lane, big), -1, keepdims=True)
    el2 = jnp.where(lane == i1, neg, el)
    e2 = jnp.max(el2, -1, keepdims=True)
    i2 = jnp.min(jnp.where(el2 == e2, lane, big), -1, keepdims=True)
    r = jnp.exp(e2 - e1)
    p1 = 1.0 / (1.0 + r)
    p2 = r * p1
    route = jnp.where(lane == 0, (i1 - N_GROUPS).astype(F32),
                      jnp.where(lane == 1, (i2 - N_GROUPS).astype(F32),
                                jnp.where(lane == 2, pg * p1, jnp.where(lane == 3, pg * p2, 0.0))))
    return route


def _out_router(dn, proj, z_group, norm_w, cv, h, w_out, g, b, w_router, b_router, *, alpha, tm=512, sub=128):
    T, D = h.shape
    n_dn, n_cv = dn.shape[1], cv.shape[1]
    n_heads = n_dn // DN_HEAD_DIM
    assert z_group % n_heads == 0
    vec = pl.BlockSpec((1, D), lambda i: (0, 0))
    row = pl.BlockSpec((tm, D), lambda i: (i, 0))
    kern = functools.partial(_out_router_kernel, alpha=alpha, n_dn=n_dn, sub=sub)
    return pl.pallas_call(
        kern, grid=(T // tm,),
        in_specs=[pl.BlockSpec((tm, n_dn), lambda i: (i, 0)),
                  pl.BlockSpec((n_heads, tm, DN_HEAD_DIM), lambda i: (z_group // n_heads, i, 0)),
                  pl.BlockSpec((1, DN_HEAD_DIM), lambda i: (0, 0)),
                  pl.BlockSpec((tm, n_cv), lambda i: (i, 0)), row,
                  pl.BlockSpec((n_dn + n_cv, D), lambda i: (0, 0)), vec, vec,
                  pl.BlockSpec((D, LANES), lambda i: (0, 0)), pl.BlockSpec((1, LANES), lambda i: (0, 0))],
        out_specs=[row, pl.BlockSpec((tm, D // 2), lambda i: (i, 0)),
                   pl.BlockSpec((tm, LANES), lambda i: (i, 0)), pl.BlockSpec((8, tm), lambda i: (0, i)),
                   pl.BlockSpec((8, LANES), lambda i: (0, 0))],
        out_shape=[jax.ShapeDtypeStruct((T, D), F32), jax.ShapeDtypeStruct((T, D // 2), U32),
                   jax.ShapeDtypeStruct((T, LANES), F32), jax.ShapeDtypeStruct((8, T), F32),
                   jax.ShapeDtypeStruct((8, LANES), F32)],
        compiler_params=_cparams("arbitrary"), name="out_router",
    )(dn, proj, norm_w.reshape(1, DN_HEAD_DIM), cv, h, w_out, g.reshape(1, D), b.reshape(1, D), w_router, b_router)


def _onehot_t(routet_ref, tr):
    sub = lax.broadcasted_iota(I32, (2 * N_EXPERTS, tr), 0)
    e1 = routet_ref[0:1, :].astype(I32)
    e2 = routet_ref[1:2, :].astype(I32)
    return sub == jnp.where(sub < N_EXPERTS, e1, e2 + N_EXPERTS)


def _place_kernel(routet_ref, base_ref, d1_ref, d2_ref, carry, *, tr):
    @pl.when(pl.program_id(0) == 0)
    def _():
        carry[...] = jnp.zeros_like(carry)
    oh = _onehot_t(routet_ref, tr)
    ohb = oh.astype(F32).astype(BF16)
    ri = lax.broadcasted_iota(I32, (tr, tr), 0)
    ci = lax.broadcasted_iota(I32, (tr, tr), 1)
    before = (ri < ci).astype(F32).astype(BF16)
    excl = jnp.dot(ohb, before, preferred_element_type=F32)
    pos = jnp.where(oh, excl + carry[...] + base_ref[...], 0.0)
    d1 = jnp.sum(pos[:N_EXPERTS], axis=0, keepdims=True)
    d2 = jnp.sum(pos[N_EXPERTS:], axis=0, keepdims=True)
    d1_ref[...] = d1.astype(I32)
    d2_ref[...] = d2.astype(I32)
    carry[...] += jnp.sum(oh.astype(F32), axis=1, keepdims=True)


def _placement(routet, cnt, tr=512):
    T = routet.shape[1]
    E = N_EXPERTS
    rt = pl.BlockSpec((8, tr), lambda i: (0, i))
    col = pl.BlockSpec((2 * E, 1), lambda i: (0, 0))
    c1 = cnt[0, :E].astype(I32)
    c2 = cnt[0, E:].astype(I32)
    padded = ((c1 + c2 + FFN_BLOCK - 1) // FFN_BLOCK) * FFN_BLOCK
    pend = jnp.cumsum(padded)
    pstart = pend - padded
    base = jnp.concatenate([pstart, pstart + c1]).astype(F32).reshape(2 * E, 1)
    lane_row = pl.BlockSpec((1, tr), lambda i: (0, i))
    d1, d2 = pl.pallas_call(
        functools.partial(_place_kernel, tr=tr), grid=(T // tr,), in_specs=[rt, col], out_specs=[lane_row, lane_row],
        out_shape=[jax.ShapeDtypeStruct((1, T), I32)] * 2,
        scratch_shapes=[pltpu.VMEM((2 * E, 1), F32)],
        compiler_params=_cparams("arbitrary"), name="moe_place")(routet, base)
    n_blocks = (T * TOP_K + E * (FFN_BLOCK - 1) + FFN_BLOCK - 1) // FFN_BLOCK
    starts = jnp.arange(n_blocks, dtype=I32) * FFN_BLOCK
    blk_e = jnp.minimum(jnp.sum((pend[None, :] <= starts[:, None]).astype(I32), axis=1), E - 1)
    blk_first = jnp.concatenate([jnp.ones((1,), I32), (blk_e[1:] != blk_e[:-1]).astype(I32)])
    n_used = (pend[-1] // FFN_BLOCK).astype(I32).reshape(1)
    ids = jnp.arange(E, dtype=I32)
    owns = padded > 0
    later = owns[None, :] & (ids[None, :] > ids[:, None])
    next_tab = jnp.min(jnp.where(later, ids[None, :], E), axis=1)
    next_tab = jnp.concatenate([next_tab, jnp.full((1,), E, I32)])
    next2_tab = next_tab[next_tab[:E]]
    hide = lambda t: jnp.where(t == E, -1, t)
    slot_tab = (jnp.cumsum(owns.astype(I32)) - 1) % 2
    steer = (blk_e, blk_first, slot_tab[blk_e], hide(next_tab[:E])[blk_e], hide(next2_tab)[blk_e])
    return d1.reshape(T), d2.reshape(T), pend.astype(I32), steer, n_used, n_blocks


def _dispatch_kernel(d1_ref, d2_ref, pend_ref, src_ref, xs_ref, zbuf, sem, zsem, *, tb):
    @pl.when(pl.program_id(0) == 0)
    def _():
        zbuf[...] = jnp.zeros_like(zbuf)
        tails = []
        for e in range(N_EXPERTS):
            end = pend_ref[e]
            nonempty = end > (pend_ref[e - 1] if e else 0)
            start = pl.multiple_of(jnp.maximum(end - FFN_BLOCK, 0), FFN_BLOCK)
            tails.append((nonempty, pltpu.make_async_copy(zbuf, xs_ref.at[pl.ds(start, FFN_BLOCK), :], zsem)))
        for nonempty, copy in tails:
            pl.when(nonempty)(copy.start)

        def spare(b):
            return pltpu.make_async_copy(
                zbuf, xs_ref.at[pl.ds(pl.multiple_of(b * FFN_BLOCK, FFN_BLOCK), FFN_BLOCK), :], zsem)
        first_spare = pend_ref[N_EXPERTS - 1] // FFN_BLOCK
        n_blocks = xs_ref.shape[0] // FFN_BLOCK
        lax.fori_loop(first_spare, n_blocks, lambda b, c: (spare(b).start(), c)[1], 0)
        for nonempty, copy in tails:
            pl.when(nonempty)(copy.wait)
        lax.fori_loop(first_spare, n_blocks, lambda b, c: (spare(b).wait(), c)[1], 0)

    def issue(t, _):
        row = src_ref.at[pl.ds(t, 1), :]
        pltpu.make_async_copy(row, xs_ref.at[pl.ds(d1_ref[t], 1), :], sem).start(priority=0)
        pltpu.make_async_copy(row, xs_ref.at[pl.ds(d2_ref[t], 1), :], sem).start(priority=1)
        return 0

    lax.fori_loop(0, tb, issue, 0, unroll=8)
    pltpu.make_async_copy(src_ref, xs_ref.at[pl.ds(0, tb), :], sem).wait()
    pltpu.make_async_copy(src_ref, xs_ref.at[pl.ds(0, tb), :], sem).wait()


def _dispatch(h1p, d1, d2, pend, n_rows, tb=2048):
    T, W = h1p.shape
    idx = pl.BlockSpec((tb,), lambda i: (i,), memory_space=pltpu.SMEM)
    return pl.pallas_call(
        functools.partial(_dispatch_kernel, tb=tb), grid=(T // tb,),
        in_specs=[idx, idx, pl.BlockSpec((N_EXPERTS,), lambda i: (0,), memory_space=pltpu.SMEM),
                  pl.BlockSpec((tb, W), lambda i: (i, 0))],
        out_specs=pl.BlockSpec(memory_space=pl.ANY),
        out_shape=jax.ShapeDtypeStruct((n_rows, W), U32),
        scratch_shapes=[pltpu.VMEM((FFN_BLOCK, W), U32), pltpu.SemaphoreType.DMA(()), pltpu.SemaphoreType.DMA(())],
        compiler_params=_cparams("arbitrary"), name="moe_dispatch",
    )(d1, d2, pend, h1p)


def _cast_rows(src, dst):
    rows = CAST_CHUNK_ELEMS // src.shape[1]

    def body(i, c):
        r = pl.ds(pl.multiple_of(i * rows, rows), rows)
        dst[r, :] = src[r, :].astype(BF16)
        return c

    lax.fori_loop(0, src.shape[0] // rows, body, 0, unroll=4)


def _ffn_kernel(blk_e_ref, blk_first_ref, blk_slot_ref, blk_next_ref, blk_next2_ref, n_used_ref,
                xs_ref, wgu_hbm, wdn_hbm, y_ref, wgu_f, wdn_f, wgu_s, wdn_s, sem, *, ff, layer):

    def weight_copies(e, slot):
        copies = []
        for k, (src, dst) in enumerate(((wgu_hbm, wgu_f), (wdn_hbm, wdn_f))):
            rows = dst.shape[1] // FFN_WEIGHT_CHUNKS
            for c in range(FFN_WEIGHT_CHUNKS):
                r = pl.ds(c * rows, rows)
                copies.append(pltpu.make_async_copy(src.at[layer, e, r, :], dst.at[slot, r, :],
                                                    sem.at[slot, k * FFN_WEIGHT_CHUNKS + c]))
        return copies

    def start(e, slot):
        for c in weight_copies(e, slot):
            c.start(priority=1)

    @pl.when(pl.program_id(0) == 0)
    def _():
        start(blk_e_ref[0], 0)

        @pl.when(blk_next_ref[0] >= 0)
        def _():
            start(blk_next_ref[0], 1)

    for j in range(FFN_BLOCKS_PER_STEP):
        i = pl.program_id(0) * FFN_BLOCKS_PER_STEP + j
        rows = pl.ds(j * FFN_BLOCK, FFN_BLOCK)
        used = i < n_used_ref[0]

        @pl.when(used & (blk_first_ref[i] == 1))
        def _():
            slot = blk_slot_ref[i]
            for c in weight_copies(blk_e_ref[i], slot):
                c.wait()
            _cast_rows(wgu_f.at[slot], wgu_s)
            _cast_rows(wdn_f.at[slot], wdn_s)

            @pl.when(blk_next2_ref[i] >= 0)
            def _():
                start(blk_next2_ref[i], slot)

        @pl.when(used)
        def _():
            lo, hi = _unpack_halves(xs_ref[rows, :])
            x = jnp.concatenate([lo, hi], axis=1).astype(BF16)
            gu = jnp.dot(x, wgu_s[...], preferred_element_type=F32)
            act = (_silu(gu[:, :ff]) * gu[:, ff:]).astype(BF16)
            y_ref[rows, :] = _pack_halves(jnp.dot(act, wdn_s[...], preferred_element_type=F32))

        @pl.when(jnp.logical_not(used))
        def _():
            y_ref[rows, :] = jnp.zeros((FFN_BLOCK, y_ref.shape[1]), y_ref.dtype)


def _ffn(xs, w_gu, w_dn, layer, blk_e, blk_first, blk_slot, blk_next, blk_next2, n_used, n_blocks):
    _, E, D, FF2 = w_gu.shape
    ff = FF2 // 2
    W = xs.shape[1]
    assert n_blocks % FFN_BLOCKS_PER_STEP == 0
    step_rows = FFN_BLOCK * FFN_BLOCKS_PER_STEP
    any_spec = pl.BlockSpec(memory_space=pl.ANY)
    last_used_step = lambda nu: (nu[0] - 1) // FFN_BLOCKS_PER_STEP
    grid_spec = pltpu.PrefetchScalarGridSpec(
        num_scalar_prefetch=6, grid=(n_blocks // FFN_BLOCKS_PER_STEP,),
        in_specs=[pl.BlockSpec((step_rows, W), lambda i, be, bf, bs, bn, bn2, nu: (jnp.minimum(i, last_used_step(nu)), 0)),
                  any_spec, any_spec],
        out_specs=pl.BlockSpec((step_rows, W), lambda i, *_: (i, 0)),
        scratch_shapes=[pltpu.VMEM((2, D, FF2), F32), pltpu.VMEM((2, ff, D), F32),
                        pltpu.VMEM((D, FF2), BF16), pltpu.VMEM((ff, D), BF16),
                        pltpu.SemaphoreType.DMA((2, 2 * FFN_WEIGHT_CHUNKS))])
    return pl.pallas_call(
        functools.partial(_ffn_kernel, ff=ff, layer=layer), grid_spec=grid_spec,
        out_shape=jax.ShapeDtypeStruct(xs.shape, U32),
        compiler_params=_cparams("arbitrary"), name="moe_ffn",
    )(blk_e, blk_first, blk_slot, blk_next, blk_next2, n_used, xs, w_gu, w_dn)


def _combine_kernel(d1_ref, d2_ref, d1n_ref, d2n_ref, y_ref, h1_ref, route_ref, g_ref, b_ref, h2_ref, h2b_ref,
                    ybuf, sem, *, tc, alpha):
    i = pl.program_id(0)
    slot = i % 2

    def row_copies(r1, r2, s, t):
        return (pltpu.make_async_copy(y_ref.at[pl.ds(r1[t], 1), :], ybuf.at[s, 0, pl.ds(t, 1), :], sem.at[s]),
                pltpu.make_async_copy(y_ref.at[pl.ds(r2[t], 1), :], ybuf.at[s, 1, pl.ds(t, 1), :], sem.at[s]))

    def wait_slot(s):
        for j in range(TOP_K):
            pltpu.make_async_copy(y_ref.at[pl.ds(0, tc), :], ybuf.at[s, j], sem.at[s]).wait()

    @pl.when(i == 0)
    def _():
        def issue(t, _):
            for j, c in enumerate(row_copies(d1_ref, d2_ref, 0, t)):
                c.start(priority=j)
            return 0
        lax.fori_loop(0, tc, issue, 0, unroll=8)

    wait_slot(slot)
    for t in range(tc):
        for j, c in enumerate(row_copies(d1n_ref, d2n_ref, 1 - slot, t)):
            c.start(priority=j)

    route = route_ref[...]
    g1 = route[:, 2:3]
    g2 = route[:, 3:4]
    lo1, hi1 = _unpack_halves(ybuf[slot, 0])
    lo2, hi2 = _unpack_halves(ybuf[slot, 1])
    ffn = jnp.concatenate([g1 * lo1 + g2 * lo2, g1 * hi1 + g2 * hi2], axis=1)
    h2 = _layer_norm(alpha * h1_ref[...] + ffn, g_ref[...], b_ref[...])
    h2_ref[...] = h2
    h2b_ref[...] = h2.astype(BF16)

    @pl.when(i == pl.num_programs(0) - 1)
    def _():
        wait_slot(1 - slot)


def _combine(y, d1, d2, h1, route, g, b, *, alpha, tc=256):
    T, D = h1.shape
    W = y.shape[1]
    n = T // tc
    idx = pl.BlockSpec((tc,), lambda i: (i,), memory_space=pltpu.SMEM)
    idx_next = pl.BlockSpec((tc,), lambda i: (jnp.minimum(i + 1, n - 1),), memory_space=pltpu.SMEM)
    row = pl.BlockSpec((tc, D), lambda i: (i, 0))
    vec = pl.BlockSpec((1, D), lambda i: (0, 0))
    return pl.pallas_call(
        functools.partial(_combine_kernel, tc=tc, alpha=alpha), grid=(n,),
        in_specs=[idx, idx, idx_next, idx_next, pl.BlockSpec(memory_space=pl.ANY), row,
                  pl.BlockSpec((tc, LANES), lambda i: (i, 0)), vec, vec],
        out_specs=[row, row],
        out_shape=[jax.ShapeDtypeStruct((T, D), F32), jax.ShapeDtypeStruct((T, D), BF16)],
        scratch_shapes=[pltpu.VMEM((2, TOP_K, tc, W), U32), pltpu.SemaphoreType.DMA((2,))],
        compiler_params=_cparams("arbitrary"), name="moe_combine",
    )(d1, d2, d1, d2, y, h1, route, g.reshape(1, D), b.reshape(1, D))


def kernel(x, emb_ln_g, emb_ln_b, w_in, short_conv_w, a_log, dt_bias, dn_norm_w, dw_conv_w, dw_conv_b,
           conv_ln_g, conv_ln_b, w_out, ln1_g, ln1_b, w_group, b_group, w_expert, b_expert, w_gate_up,
           w_down, ln2_g, ln2_b):
    B, L, D = x.shape
    T = B * L
    depth = w_in.shape[0]
    H = a_log.shape[2]
    dn_w = H * DN_HEAD_DIM
    cv_w = dw_conv_w.shape[2]
    n_ab = 4 * H
    alpha = (2 * depth) ** 0.25
    C = DN_CHUNK

    h, hb = _emb_ln(x.reshape(T, D), emb_ln_g, emb_ln_b)
    for l in range(depth):
        w = w_in[l]
        w_main = jnp.concatenate([w[:, :4 * dn_w], w[:, 4 * dn_w + n_ab:]], axis=1).astype(BF16)
        proj, gt = _in_proj(hb, w_main, w[:, 4 * dn_w:4 * dn_w + n_ab], a_log[l], dt_bias[l], B=B, L=L)
        gt = gt.reshape(B, n_ab, L // C, C)
        dn = _deltanet(proj, gt, short_conv_w[l], B=B, L=L, H=H)
        cv = _conformer(proj, dw_conv_w[l], dw_conv_b[l], conv_ln_g[l], conv_ln_b[l],
                        B=B, L=L, col0=4 * dn_w, width=cv_w)
        n_r = N_GROUPS + N_EXPERTS
        w_router = jnp.pad(jnp.concatenate([w_group[l], w_expert[l]], axis=1),
                           ((0, 0), (0, LANES - n_r))).astype(BF16)
        b_router = jnp.pad(jnp.concatenate([b_group[l], b_expert[l]]), (0, LANES - n_r)).reshape(1, LANES)
        h1, h1p, route, routet, cnt = _out_router(
            dn.reshape(T, dn_w), proj, 3 * H, dn_norm_w[l], cv.reshape(T, cv_w), h, w_out[l].astype(BF16),
            ln1_g[l], ln1_b[l],
            w_router, b_router, alpha=alpha)
        d1, d2, pend, steer, n_used, n_blocks = _placement(routet, cnt)
        xs = _dispatch(h1p, d1, d2, pend, n_blocks * FFN_BLOCK)
        y = _ffn(xs, w_gate_up, w_down, l, *steer, n_used, n_blocks)
        h, hb = _combine(y, d1, d2, h1, route, ln2_g[l], ln2_b[l], alpha=alpha)
    return h.reshape(B, L, D)
```
